```python
import jax, jax.numpy as jnp
from jax import lax
import numpy as np

D_MODEL = 1024
BATCH = 4
SEQ = 4096
DEPTH = 2
DEC_BATCH = 32
DEC_SEQ = 4
PAST_LEN = 8192
PAGE_SIZE = 128

HEAD_DIM = 64
A_WIDTH = D_MODEL // 2
MOBA_HEADS = (D_MODEL // 2) // HEAD_DIM
C_WIDTH = D_MODEL // 2
C_GROUPS = 8
SB_HEADS = (D_MODEL // 2) // HEAD_DIM
CONV_W = 3
MOBA_BLOCK = 256
MOBA_TOPK = 3
MOBA_QB = 32
GMLP_CHUNK = 128
SB_QB = 128
N_EVEN = (DEPTH + 1) // 2
N_ODD = DEPTH // 2
RMS_EPS = 1e-6
NEG = -1e30

kernel_name = "hybrid_conv_moba_gmlp_stickbreak_step"


def rmsnorm(x, g):
    x32 = x.astype(jnp.float32)
    y = x32 * lax.rsqrt(jnp.mean(x32 * x32, axis=-1, keepdims=True) + RMS_EPS)
    return (y * g.astype(jnp.float32)).astype(x.dtype)


def _pad_rows(x, n, axis=1):
    if n == 0:
        return x
    widths = [(0, 0)] * x.ndim
    widths[axis] = (0, n)
    return jnp.pad(x, widths)


def _split(x, sizes):
    return jnp.split(x, list(np.cumsum(sizes)[:-1]), axis=-1)


def moba_attend(q, qpos, k, v):
    bsz, lq, nh, dh = q.shape
    t = k.shape[1]
    nb = max(-(-t // MOBA_BLOCK), MOBA_TOPK)
    kp = _pad_rows(k, nb * MOBA_BLOCK - t)
    vp = _pad_rows(v, nb * MOBA_BLOCK - t)
    kmean = kp.astype(jnp.float32).reshape(bsz, nb, MOBA_BLOCK, nh, dh).mean(axis=2)
    qb = min(lq, MOBA_QB)
    nq = -(-lq // qb)
    lp = nq * qb
    q_items = _pad_rows(q, lp - lq).reshape(bsz * nq, qb, nh, dh)
    p_items = jnp.tile(_pad_rows(qpos, lp - lq, axis=0).reshape(nq, qb), (bsz, 1))
    b_items = jnp.repeat(jnp.arange(bsz, dtype=jnp.int32), nq)
    blk_off = jnp.arange(MOBA_BLOCK, dtype=jnp.int32)
    head_idx = jnp.arange(nh, dtype=jnp.int32)[:, None, None]
    block_ids = jnp.arange(nb, dtype=jnp.int32)
    slot_ids = jnp.arange(MOBA_TOPK, dtype=jnp.int32)
    scale = dh ** -0.5

    def one_query(bi, qt, pt):
        qblk = pt // MOBA_BLOCK
        gate = jnp.einsum('hd,nhd->hn', qt.astype(jnp.float32), kmean[bi])
        gate = jnp.where(block_ids[None, :] < qblk, gate, NEG)
        _, sel = lax.top_k(gate, MOBA_TOPK)
        blocks = jnp.concatenate(
            [sel.astype(jnp.int32), jnp.full((nh, 1), qblk, jnp.int32)], axis=1)
        kidx = blocks[:, :, None] * MOBA_BLOCK + blk_off
        kg = kp[bi, kidx, head_idx]
        vg = vp[bi, kidx, head_idx]
        slot_ok = jnp.concatenate([slot_ids < qblk, jnp.ones((1,), bool)])
        mask = slot_ok[None, :, None] & (kidx <= pt)
        logits = jnp.einsum('hd,hskd->hsk', qt, kg).astype(jnp.float32) * scale
        logits = jnp.where(mask, logits, NEG).reshape(nh, -1)
        p = jax.nn.softmax(logits, axis=-1).reshape(kidx.shape)
        return jnp.einsum('hsk,hskd->hd', p.astype(vg.dtype), vg)

    def chunk(item):
        bi, qc, pc = item
        return jax.vmap(one_query, in_axes=(None, 0, 0))(bi, qc, pc)

    out = lax.map(chunk, (b_items, q_items, p_items))
    return out.reshape(bsz, lp, nh, dh)[:, :lq]


def stick_breaking_attend(q, qpos, k, v):
    bsz, lq, nh, dh = q.shape
    t = k.shape[1]
    qb = min(lq, SB_QB)
    nq = -(-lq // qb)
    lp = nq * qb
    qc = _pad_rows(q, lp - lq).reshape(bsz, nq, qb, nh, dh).swapaxes(0, 1)
    pc = _pad_rows(qpos, lp - lq, axis=0).reshape(nq, qb)
    kpos = jnp.arange(t, dtype=jnp.int32)
    scale = dh ** -0.5

    def block(item):
        qi, pi = item
        z = jnp.einsum('bqhd,bkhd->bhqk', qi, k).astype(jnp.float32) * scale
        strict = (kpos[None, :] < pi[:, None])[None, None]
        log_keep = jnp.where(strict, jax.nn.log_sigmoid(-z), 0.0)
        between = lax.cumsum(log_keep, axis=3, reverse=True) - log_keep
        w = jnp.where(strict, jnp.exp(jax.nn.log_sigmoid(z) + between), 0.0)
        return jnp.einsum('bhqk,bkhd->bqhd', w.astype(v.dtype), v)

    out = lax.map(block, (qc, pc))
    return out.swapaxes(0, 1).reshape(bsz, lp, nh, dh)[:, :lq]


def chunk_spatial_gate(v, w_s, b_s):
    bsz, l, cw = v.shape
    cs = min(l, GMLP_CHUNK)
    nc = -(-l // cs)
    lp = nc * cs
    vr = _pad_rows(v, lp - l).reshape(bsz, nc, cs, C_GROUPS, cw // C_GROUPS)
    tri = jnp.tril(jnp.ones((cs, cs), dtype=bool))
    wm = jnp.where(tri[None], w_s[:, :cs, :cs], 0)
    mixed = jnp.einsum('gts,bnsgc->bntgc', wm, vr) + b_s[:, :cs].T[None, None, :, :, None]
    return mixed.reshape(bsz, lp, cw)[:, :l]


def even_layer(x, qpos, conv_prev, k_past, v_past, g_pre, g_post, w_in, conv_w, w_out):
    bsz, l, _ = x.shape
    xn = rmsnorm(x, g_pre)
    bw = MOBA_HEADS * HEAD_DIM
    a_b, a_c, a_h, a_z, q, k, v, b_z = _split(
        xn @ w_in, [A_WIDTH] * 4 + [bw] * 4)
    u = a_c * a_h
    ue = jnp.concatenate([conv_prev.astype(u.dtype), u], axis=1)
    conv = ue[:, 0:l] * conv_w[0]
    for j in range(1, CONV_W):
        conv = conv + ue[:, j:j + l] * conv_w[j]
    y_a = a_b * conv * jax.nn.silu(a_z)
    qh = q.reshape(bsz, l, MOBA_HEADS, HEAD_DIM)
    kh = k.reshape(bsz, l, MOBA_HEADS, HEAD_DIM)
    vh = v.reshape(bsz, l, MOBA_HEADS, HEAD_DIM)
    if k_past is None:
        k_all, v_all = kh, vh
    else:
        k_all = jnp.concatenate([k_past.astype(kh.dtype), kh], axis=1)
        v_all = jnp.concatenate([v_past.astype(vh.dtype), vh], axis=1)
    y_b = moba_attend(qh, qpos, k_all, v_all).reshape(bsz, l, bw) * jax.nn.silu(b_z)
    out = jnp.concatenate([y_a, y_b], axis=-1) @ w_out
    return x + rmsnorm(out, g_post), ue[:, -(CONV_W - 1):], kh, vh


def odd_layer(x, qpos, k_past, v_past, g_pre, g_post, w_in, w_s, b_s, w_out):
    bsz, l, _ = x.shape
    xn = rmsnorm(x, g_pre)
    dw = SB_HEADS * HEAD_DIM
    c_u, c_v, c_z, q, k, v, d_z = _split(
        xn @ w_in, [C_WIDTH] * 3 + [dw] * 4)
    y_c = c_u * chunk_spatial_gate(c_v, w_s, b_s) * jax.nn.silu(c_z)
    qh = q.reshape(bsz, l, SB_HEADS, HEAD_DIM)
    kh = k.reshape(bsz, l, SB_HEADS, HEAD_DIM)
    vh = v.reshape(bsz, l, SB_HEADS, HEAD_DIM)
    if k_past is None:
        k_all, v_all = kh, vh
    else:
        k_all = jnp.concatenate([k_past.astype(kh.dtype), kh], axis=1)
        v_all = jnp.concatenate([v_past.astype(vh.dtype), vh], axis=1)
    y_d = stick_breaking_attend(qh, qpos, k_all, v_all).reshape(bsz, l, dw) * jax.nn.silu(d_z)
    out = jnp.concatenate([y_c, y_d], axis=-1) @ w_out
    open_start = ((l - 1) // GMLP_CHUNK) * GMLP_CHUNK
    return x + rmsnorm(out, g_post), kh, vh, c_v[:, open_start:]


def setup_inputs(seed: int = 0) -> dict:
    key = jax.random.key(seed)
    ks = jax.random.split(key, 20)
    f32 = jnp.float32
    n_pages = PAST_LEN // PAGE_SIZE
    n_used = DEC_BATCH * n_pages
    n_phys = n_used + max(1, n_used // 4)
    bw = MOBA_HEADS * HEAD_DIM
    dw = SB_HEADS * HEAD_DIM
    in_e = 4 * A_WIDTH + 4 * bw
    in_o = 3 * C_WIDTH + 4 * dw

    def nrm(k, shape, s=1.0):
        return jax.random.normal(k, shape, f32) * s

    page_table = jax.random.permutation(ks[7], n_phys)[:n_used].reshape(
        DEC_BATCH, n_pages).astype(jnp.int32)
    return {
        "x_prompt": nrm(ks[0], (BATCH, SEQ, D_MODEL)),
        "x_sample": nrm(ks[1], (DEC_BATCH, DEC_SEQ, D_MODEL)),
        "state_conv": nrm(ks[2], (N_EVEN, DEC_BATCH, CONV_W - 1, A_WIDTH)),
        "cache_k_moba": nrm(ks[3], (N_EVEN, n_phys, PAGE_SIZE, MOBA_HEADS, HEAD_DIM)),
        "cache_v_moba": nrm(ks[4], (N_EVEN, n_phys, PAGE_SIZE, MOBA_HEADS, HEAD_DIM)),
        "cache_k_sb": nrm(ks[5], (N_ODD, n_phys, PAGE_SIZE, SB_HEADS, HEAD_DIM)),
        "cache_v_sb": nrm(ks[6], (N_ODD, n_phys, PAGE_SIZE, SB_HEADS, HEAD_DIM)),
        "page_table": page_table,
        "norm_pre_e": 1.0 + nrm(ks[8], (N_EVEN, D_MODEL), 0.1),
        "norm_post_e": 1.0 + nrm(ks[9], (N_EVEN, D_MODEL), 0.1),
        "w_in_e": nrm(ks[10], (N_EVEN, D_MODEL, in_e), D_MODEL ** -0.5),
        "conv_w": nrm(ks[11], (N_EVEN, CONV_W, A_WIDTH), CONV_W ** -0.5),
        "w_out_e": nrm(ks[12], (N_EVEN, A_WIDTH + bw, D_MODEL), (A_WIDTH + bw) ** -0.5),
        "norm_pre_o": 1.0 + nrm(ks[13], (N_ODD, D_MODEL), 0.1),
        "norm_post_o": 1.0 + nrm(ks[14], (N_ODD, D_MODEL), 0.1),
        "w_in_o": nrm(ks[15], (N_ODD, D_MODEL, in_o), D_MODEL ** -0.5),
        "gmlp_w": nrm(ks[16], (N_ODD, C_GROUPS, GMLP_CHUNK, GMLP_CHUNK), GMLP_CHUNK ** -0.5),
        "gmlp_b": nrm(ks[17], (N_ODD, C_GROUPS, GMLP_CHUNK), 0.1),
        "w_out_o": nrm(ks[18], (N_ODD, C_WIDTH + dw, D_MODEL), (C_WIDTH + dw) ** -0.5),
    }


def reference(x_prompt, x_sample, state_conv, cache_k_moba, cache_v_moba, cache_k_sb, cache_v_sb,
              page_table, norm_pre_e, norm_post_e, w_in_e, conv_w, w_out_e,
              norm_pre_o, norm_post_o, w_in_o, gmlp_w, gmlp_b, w_out_o):
    n_dec, n_pages = page_table.shape
    past_len = n_pages * cache_k_moba.shape[2]
    pos_p = jnp.arange(x_prompt.shape[1], dtype=jnp.int32)
    pos_s = past_len + jnp.arange(x_sample.shape[1], dtype=jnp.int32)

    def gather_past(cache):
        return cache[page_table].reshape(n_dec, past_len, cache.shape[2], cache.shape[3])[:, :, :, :] if False else \
            cache[page_table].reshape(n_dec, past_len, cache.shape[2], cache.shape[3])

    hp, hs = x_prompt, x_sample
    conv_p, conv_s, kmb_p, vmb_p, kmb_s, vmb_s = [], [], [], [], [], []
    ksb_p, vsb_p, ksb_s, vsb_s, gv_p, gv_s = [], [], [], [], [], []
    for i in range(DEPTH):
        j = i // 2
        if i % 2 == 0:
            zero_conv = jnp.zeros((hp.shape[0], CONV_W - 1, A_WIDTH), hp.dtype)
            hp, c1, k1, v1 = even_layer(hp, pos_p, zero_conv, None, None, norm_pre_e[j],
                                        norm_post_e[j], w_in_e[j], conv_w[j], w_out_e[j])
            hs, c2, k2, v2 = even_layer(hs, pos_s, state_conv[j], gather_past(cache_k_moba[j]),
                                        gather_past(cache_v_moba[j]), norm_pre_e[j],
                                        norm_post_e[j], w_in_e[j], conv_w[j], w_out_e[j])
            conv_p.append(c1); conv_s.append(c2)
            kmb_p.append(k1); vmb_p.append(v1); kmb_s.append(k2); vmb_s.append(v2)
        else:
            hp, k1, v1, g1 = odd_layer(hp, pos_p, None, None, norm_pre_o[j], norm_post_o[j],
                                       w_in_o[j], gmlp_w[j], gmlp_b[j], w_out_o[j])
            hs, k2, v2, g2 = odd_layer(hs, pos_s, gather_past(cache_k_sb[j]),
                                       gather_past(cache_v_sb[j]), norm_pre_o[j], norm_post_o[j],
                                       w_in_o[j], gmlp_w[j], gmlp_b[j], w_out_o[j])
            ksb_p.append(k1); vsb_p.append(v1); ksb_s.append(k2); vsb_s.append(v2)
            gv_p.append(g1); gv_s.append(g2)

    y_prompt, y_sample = hp, hs
    conv_prompt, conv_sample = jnp.stack(conv_p), jnp.stack(conv_s)
    k_moba_prompt, v_moba_prompt = jnp.stack(kmb_p), jnp.stack(vmb_p)
    k_moba_sample, v_moba_sample = jnp.stack(kmb_s), jnp.stack(vmb_s)
    k_sb_prompt, v_sb_prompt = jnp.stack(ksb_p), jnp.stack(vsb_p)
    k_sb_sample, v_sb_sample = jnp.stack(ksb_s), jnp.stack(vsb_s)
    gmlp_v_prompt, gmlp_v_sample = jnp.stack(gv_p), jnp.stack(gv_s)
    return (y_prompt, y_sample, conv_prompt, conv_sample, k_moba_prompt, v_moba_prompt,
            k_moba_sample, v_moba_sample, k_sb_prompt, v_sb_prompt, k_sb_sample, v_sb_sample,
            gmlp_v_prompt, gmlp_v_sample)
```

```python
import functools

import jax
import jax.numpy as jnp
from jax import lax
from jax.experimental import pallas as pl
from jax.experimental.pallas import tpu as pltpu

F32 = jnp.float32
BF16 = jnp.bfloat16

HEAD_DIM = 64
LANES = 128
BRANCH_W = 512
MOBA_BLOCK = 256
MOBA_TOPK = 3
GMLP_CHUNK = 128
CONV_W = 3
RMS_EPS = 1e-6
NEG = -1e30
ATTN_SCALE = HEAD_DIM ** -0.5
SB_TILE = 256
SB_DEAD = -110.0
VMEM_LIMIT = 56 * 1024 * 1024

_NT = (((1,), (1,)), ((), ()))


def _cparams(sem):
    return pltpu.CompilerParams(dimension_semantics=sem, vmem_limit_bytes=VMEM_LIMIT)


def _rms(x, g):
    ms = jnp.mean(x * x, axis=-1, keepdims=True)
    return x * lax.rsqrt(ms + RMS_EPS) * g


def _silu(x):
    return x / (1.0 + jnp.exp(-x))


def _softplus(z):
    return jnp.maximum(z, 0.0) + jnp.log1p(jnp.exp(-jnp.abs(z)))


def _split_bf16(x):
    hi = x.astype(BF16)
    lo = (x - hi.astype(F32)).astype(BF16)
    return hi, lo


def _proj_kernel(x_ref, g_ref, w_ref, o_ref, ob_ref, xn_ref, *, qkv_first):
    j = pl.program_id(1)

    @pl.when(j == 0)
    def _():
        xn_ref[...] = _rms(x_ref[...], g_ref[...]).astype(BF16)

    r = jnp.dot(xn_ref[...], w_ref[...], preferred_element_type=F32)
    o_ref[...] = r

    @pl.when(j == qkv_first)
    def _():
        ob_ref[...] = (r * ATTN_SCALE).astype(BF16)

    @pl.when((j > qkv_first) & (j < qkv_first + 3))
    def _():
        ob_ref[...] = r.astype(BF16)


def _proj(x, g, w_bf, qkv_first, tm):
    m, d = x.shape
    n = w_bf.shape[1]
    nj = n // BRANCH_W
    return pl.pallas_call(
        functools.partial(_proj_kernel, qkv_first=qkv_first),
        grid=(m // tm, nj),
        in_specs=[
            pl.BlockSpec((tm, d), lambda i, j: (i, 0)),
            pl.BlockSpec((1, d), lambda i, j: (0, 0)),
            pl.BlockSpec((d, BRANCH_W), lambda i, j: (0, j)),
        ],
        out_specs=[
            pl.BlockSpec((tm, BRANCH_W), lambda i, j: (i, j)),
            pl.BlockSpec((tm, BRANCH_W), lambda i, j: (i, jnp.clip(j - qkv_first, 0, 2))),
        ],
        out_shape=[
            jax.ShapeDtypeStruct((m, n), F32),
            jax.ShapeDtypeStruct((m, 3 * BRANCH_W), BF16),
        ],
        scratch_shapes=[pltpu.VMEM((tm, d), BF16)],
        compiler_params=_cparams(("parallel", "arbitrary")),
    )(x, g.reshape(1, d), w_bf)


def _moba_prompt_kernel(q_ref, k_ref, v_ref, o_ref, km_ref, *, nblk):
    i = pl.program_id(2)
    lane = lax.broadcasted_iota(jnp.int32, (1, LANES), 1)
    low = lane < HEAD_DIM
    nbp = km_ref.shape[0]

    @pl.when(i == 0)
    def _():
        km_ref[...] = jnp.zeros_like(km_ref)
        for j in range(nblk):
            blk = k_ref[j * MOBA_BLOCK:(j + 1) * MOBA_BLOCK, :].astype(F32)
            km_ref[j:j + 1, :] = jnp.sum(blk, axis=0, keepdims=True) * (1.0 / MOBA_BLOCK)

    q = q_ref[...]
    km = km_ref[...]
    brow = lax.broadcasted_iota(jnp.int32, (nbp, MOBA_BLOCK), 0)
    r_i = lax.broadcasted_iota(jnp.int32, (MOBA_BLOCK, MOBA_BLOCK), 0)
    c_i = lax.broadcasted_iota(jnp.int32, (MOBA_BLOCK, MOBA_BLOCK), 1)
    start = pl.multiple_of(i * MOBA_BLOCK, MOBA_BLOCK)
    k_own = k_ref[pl.ds(start, MOBA_BLOCK), :]
    v_own = v_ref[pl.ds(start, MOBA_BLOCK), :]

    outs = []
    for h in range(2):
        mine = low if h == 0 else jnp.logical_not(low)
        km_hi, km_lo = _split_bf16(jnp.where(mine, km, 0.0))
        gate = (lax.dot_general(km_hi, q, _NT, preferred_element_type=F32)
                + lax.dot_general(km_lo, q, _NT, preferred_element_type=F32))
        gate = jnp.where(brow < i, gate, -jnp.inf)
        bias = jnp.full((nbp, MOBA_BLOCK), NEG, F32)
        for _ in range(MOBA_TOPK):
            mx = jnp.max(gate, axis=0, keepdims=True)
            first = jnp.min(jnp.where(gate == mx, brow, nbp), axis=0, keepdims=True)
            pick = (brow == first) & (mx > -jnp.inf)
            bias = jnp.where(pick, 0.0, bias)
            gate = jnp.where(pick, -jnp.inf, gate)
        off = HEAD_DIM if h == 0 else 0
        pieces = []
        if off:
            pieces.append(jnp.zeros((off, MOBA_BLOCK), F32))
        pieces.append(bias)
        pieces.append(jnp.zeros((LANES - off - nbp, MOBA_BLOCK), F32))
        bias_t = jnp.concatenate(pieces, axis=0).T
        q_own = jnp.where(mine, q, jnp.zeros_like(q))
        q_past = jnp.where(mine, q, bias_t.astype(BF16))

        s = lax.dot_general(q_own, k_own, _NT, preferred_element_type=F32)
        s = jnp.where(c_i <= r_i, s, NEG)
        m0 = jnp.max(s, axis=-1, keepdims=True)
        p = jnp.exp(s - m0)
        l0 = jnp.sum(p, axis=-1, keepdims=True)
        acc0 = jnp.dot(p.astype(BF16), v_own, preferred_element_type=F32)

        def body(j, carry, mine=mine, off=off, q_past=q_past):
            m, l, acc = carry
            st = pl.multiple_of(j * MOBA_BLOCK, MOBA_BLOCK)
            kj = k_ref[pl.ds(st, MOBA_BLOCK), :]
            vj = v_ref[pl.ds(st, MOBA_BLOCK), :]
            onehot = jnp.where(lane - off == j, 1.0, 0.0).astype(BF16)
            kj = jnp.where(mine, kj, onehot)
            sj = lax.dot_general(q_past, kj, _NT, preferred_element_type=F32)
            m_new = jnp.maximum(m, jnp.max(sj, axis=-1, keepdims=True))
            alpha = jnp.exp(m - m_new)
            pj = jnp.exp(sj - m_new)
            l = alpha * l + jnp.sum(pj, axis=-1, keepdims=True)
            acc = alpha * acc + jnp.dot(pj.astype(BF16), vj, preferred_element_type=F32)
            return m_new, l, acc

        _, l, acc = lax.fori_loop(0, i, body, (m0, l0, acc0))
        outs.append(acc / l)
    o_ref[...] = jnp.where(low, outs[0], outs[1])


def _moba_prompt(qkv, bsz, seq):
    nblk = seq // MOBA_BLOCK
    npair = BRANCH_W // LANES
    nbp = -(-nblk // 8) * 8
    return pl.pallas_call(
        functools.partial(_moba_prompt_kernel, nblk=nblk),
        grid=(bsz, npair, nblk),
        in_specs=[
            pl.BlockSpec((MOBA_BLOCK, LANES), lambda b, p, i: (b * nblk + i, p)),
            pl.BlockSpec((seq, LANES), lambda b, p, i: (b, npair + p)),
            pl.BlockSpec((seq, LANES), lambda b, p, i: (b, 2 * npair + p)),
        ],
        out_specs=pl.BlockSpec((MOBA_BLOCK, LANES), lambda b, p, i: (b * nblk + i, p)),
        out_shape=jax.ShapeDtypeStruct((bsz * seq, BRANCH_W), F32),
        scratch_shapes=[pltpu.VMEM((nbp, LANES), F32)],
        compiler_params=_cparams(("parallel", "parallel", "arbitrary")),
    )(qkv, qkv, qkv)


def _sb_tile(q_h, kj, vj, tri, carry, acc, strict):
    z = lax.dot_general(q_h, kj, _NT, preferred_element_type=F32)
    lk = -_softplus(z)
    if strict is not None:
        lk = jnp.where(strict, lk, 0.0)
    hi, lo = _split_bf16(lk)
    cum = (jnp.dot(hi, tri, preferred_element_type=F32)
           + jnp.dot(lo, tri, preferred_element_type=F32)) + carry
    w = jnp.exp(z + cum)
    if strict is not None:
        w = jnp.where(strict, w, 0.0)
    acc = acc + jnp.dot(w.astype(BF16), vj, preferred_element_type=F32)
    return cum[:, 0:1], acc


def _sb_prompt_kernel(q_ref, k_ref, v_ref, o_ref):
    i = pl.program_id(2)
    lane = lax.broadcasted_iota(jnp.int32, (1, LANES), 1)
    low = lane < HEAD_DIM
    r_i = lax.broadcasted_iota(jnp.int32, (SB_TILE, SB_TILE), 0)
    c_i = lax.broadcasted_iota(jnp.int32, (SB_TILE, SB_TILE), 1)
    tri = jnp.where(r_i >= c_i, 1.0, 0.0).astype(BF16)
    strict = c_i < r_i
    q = q_ref[...]
    start = pl.multiple_of(i * SB_TILE, SB_TILE)
    k_own = k_ref[pl.ds(start, SB_TILE), :]
    v_own = v_ref[pl.ds(start, SB_TILE), :]

    outs = []
    for h in range(2):
        mine = low if h == 0 else jnp.logical_not(low)
        q_h = jnp.where(mine, q, jnp.zeros_like(q))
        carry, acc = _sb_tile(q_h, k_own, v_own, tri,
                              jnp.zeros((SB_TILE, 1), F32),
                              jnp.zeros((SB_TILE, LANES), F32), strict)

        def cond(st):
            j, live, _, _ = st
            return (j >= 0) & (live > SB_DEAD)

        def body(st, q_h=q_h):
            j, _, carry, acc = st
            s0 = pl.multiple_of(j * SB_TILE, SB_TILE)
            kj = k_ref[pl.ds(s0, SB_TILE), :]
            vj = v_ref[pl.ds(s0, SB_TILE), :]
            carry, acc = _sb_tile(q_h, kj, vj, tri, carry, acc, None)
            return j - 1, jnp.max(carry), carry, acc

        _, _, _, acc = lax.while_loop(cond, body, (i - 1, jnp.max(carry), carry, acc))
        outs.append(acc)
    o_ref[...] = jnp.where(low, outs[0], outs[1])


def _sb_prompt(qkv, bsz, seq):
    nt = seq // SB_TILE
    npair = BRANCH_W // LANES
    return pl.pallas_call(
        _sb_prompt_kernel,
        grid=(bsz, npair, nt),
        in_specs=[
            pl.BlockSpec((SB_TILE, LANES), lambda b, p, i: (b * nt + i, p)),
            pl.BlockSpec((seq, LANES), lambda b, p, i: (b, npair + p)),
            pl.BlockSpec((seq, LANES), lambda b, p, i: (b, 2 * npair + p)),
        ],
        out_specs=pl.BlockSpec((SB_TILE, LANES), lambda b, p, i: (b * nt + i, p)),
        out_shape=jax.ShapeDtypeStruct((bsz * seq, BRANCH_W), F32),
        compiler_params=_cparams(("parallel", "parallel", "arbitrary")),
    )(qkv, qkv, qkv)


def _paged_kernel(pt_ref, qrep_ref, knew_ref, vnew_ref, *rest, mode, group, nchunk, nheads):
    del pt_ref
    k_refs = rest[:group]
    v_refs = rest[group:2 * group]
    o_ref = rest[2 * group]
    s_ref, p_ref, pn_ref, l_ref, acc_ref = rest[2 * group + 1:]
    s = pl.program_id(1)
    nrow, width = qrep_ref.shape[1], qrep_ref.shape[2]
    nq = nrow // nheads
    page = k_refs[0].shape[1]
    cw = group * page
    row_w = lax.broadcasted_iota(jnp.int32, (nrow, width), 0)
    lane_w = lax.broadcasted_iota(jnp.int32, (nrow, width), 1)
    head_of_lane = lane_w // HEAD_DIM
    head_mask = head_of_lane == (row_w % nheads)
    qrows = jnp.where(head_mask, qrep_ref[0], jnp.zeros_like(qrep_ref[0]))
    row = lax.broadcasted_iota(jnp.int32, (nrow, LANES), 0)
    lane = lax.broadcasted_iota(jnp.int32, (nrow, LANES), 1)
    qidx = row // nheads

    @pl.when(s < nchunk)
    def _():
        for g in range(group):
            s_ref[s, :, g * page:(g + 1) * page] = jnp.dot(
                qrows, k_refs[g][...].astype(BF16), preferred_element_type=F32)

    @pl.when(s == nchunk - 1)
    def _():
        zn = lax.dot_general(qrows, knew_ref[0], _NT, preferred_element_type=F32)
        if mode == "moba":
            bpc = cw // MOBA_BLOCK
            nblk = nchunk * bpc
            gates = jnp.zeros((nrow, LANES), F32)
            for c in range(nchunk):
                for bl in range(bpc):
                    sb = s_ref[c, :, bl * MOBA_BLOCK:(bl + 1) * MOBA_BLOCK]
                    gates = jnp.where(lane == c * bpc + bl,
                                      jnp.sum(sb, axis=1, keepdims=True), gates)
            gates = jnp.where(lane < nblk, gates, -jnp.inf)
            sel = jnp.zeros((nrow, LANES), F32)
            for _ in range(MOBA_TOPK):
                mx = jnp.max(gates, axis=1, keepdims=True)
                first = jnp.min(jnp.where(gates == mx, lane, LANES), axis=1, keepdims=True)
                pick = (lane == first) & (mx > -jnp.inf)
                sel = jnp.where(pick, 1.0, sel)
                gates = jnp.where(pick, -jnp.inf, gates)
            zn = jnp.where(lane <= qidx, zn, NEG)
            m = jnp.max(zn, axis=1, keepdims=True)
            cols = []
            for c in range(nchunk):
                for bl in range(bpc):
                    col = jnp.max(jnp.where(lane == c * bpc + bl, sel, 0.0),
                                  axis=1, keepdims=True) > 0.5
                    cols.append(col)
                    sb = s_ref[c, :, bl * MOBA_BLOCK:(bl + 1) * MOBA_BLOCK]
                    m = jnp.maximum(m, jnp.max(jnp.where(col, sb, NEG), axis=1, keepdims=True))
            pn = jnp.exp(zn - m)
            l = jnp.sum(pn, axis=1, keepdims=True)
            pn_ref[...] = pn
            for c in range(nchunk):
                for bl in range(bpc):
                    sb = s_ref[c, :, bl * MOBA_BLOCK:(bl + 1) * MOBA_BLOCK]
                    p = jnp.exp(jnp.where(cols[c * bpc + bl], sb, NEG) - m)
                    l = l + jnp.sum(p, axis=1, keepdims=True)
                    p_ref[c, :, bl * MOBA_BLOCK:(bl + 1) * MOBA_BLOCK] = p.astype(BF16)
            l_ref[...] = jnp.broadcast_to(l, l_ref.shape)
        else:
            r_n = lax.broadcasted_iota(jnp.int32, (LANES, LANES), 0)
            c_n = lax.broadcasted_iota(jnp.int32, (LANES, LANES), 1)
            tri_n = jnp.where(r_n >= c_n, 1.0, 0.0).astype(BF16)
            r_t = lax.broadcasted_iota(jnp.int32, (SB_TILE, SB_TILE), 0)
            c_t = lax.broadcasted_iota(jnp.int32, (SB_TILE, SB_TILE), 1)
            tri = jnp.where(r_t >= c_t, 1.0, 0.0).astype(BF16)
            strict = lane < qidx
            lkn = jnp.where(strict, -_softplus(zn), 0.0)
            hi, lo = _split_bf16(lkn)
            cum = (jnp.dot(hi, tri_n, preferred_element_type=F32)
                   + jnp.dot(lo, tri_n, preferred_element_type=F32))
            pn_ref[...] = jnp.where(strict, jnp.exp(zn + cum), 0.0)
            carry = cum[:, 0:1]
            tpc = cw // SB_TILE
            for c in reversed(range(nchunk)):
                for bl in reversed(range(tpc)):
                    z = s_ref[c, :, bl * SB_TILE:(bl + 1) * SB_TILE]
                    hi, lo = _split_bf16(-_softplus(z))
                    cum = (jnp.dot(hi, tri, preferred_element_type=F32)
                           + jnp.dot(lo, tri, preferred_element_type=F32)) + carry
                    p_ref[c, :, bl * SB_TILE:(bl + 1) * SB_TILE] = jnp.exp(z + cum).astype(BF16)
                    carry = cum[:, 0:1]
            l_ref[...] = jnp.ones_like(l_ref)

    @pl.when(s == nchunk)
    def _():
        acc_ref[...] = jnp.dot(pn_ref[...].astype(BF16), vnew_ref[0], preferred_element_type=F32)

    @pl.when(s >= nchunk)
    def _():
        c = s - nchunk
        acc = acc_ref[...]
        for g in range(group):
            acc = acc + lax.dot_general(p_ref[c, :, g * page:(g + 1) * page],
                                        v_refs[g][...].astype(BF16), _NT,
                                        preferred_element_type=F32)
        acc_ref[...] = acc

    @pl.when(s == 2 * nchunk - 1)
    def _():
        y = acc_ref[...] / l_ref[:, 0:1]
        y = jnp.where(head_mask, y, 0.0)
        o_ref[0] = jnp.sum(y.reshape(nq, nheads, width), axis=1)


def _paged_attention(mode, q_bf, knew_bf, vnew_bf, cache_k, cache_v, page_table, group):
    nb, nq, width = q_bf.shape
    nheads = width // HEAD_DIM
    nrow = nq * nheads
    nphys, page = cache_k.shape[0], cache_k.shape[1]
    npages = page_table.shape[1]
    nchunk = npages // group
    ck = cache_k.transpose(0, 2, 3, 1).reshape(nphys, width, page)
    cv = cache_v.transpose(0, 2, 3, 1).reshape(nphys, width, page)
    qrep = jnp.repeat(q_bf, nheads, axis=1)
    pad = ((0, 0), (0, LANES - nq), (0, 0))
    knew = jnp.pad(knew_bf, pad)
    vnew = jnp.pad(vnew_bf, pad)
    pt = page_table.reshape(-1).astype(jnp.int32)

    def k_map(g):
        return lambda b, s, pt: (pt[b * npages + jnp.minimum(s, nchunk - 1) * group + g], 0, 0)

    def v_map(g):
        return lambda b, s, pt: (pt[b * npages + jnp.maximum(s - nchunk, 0) * group + g], 0, 0)

    per_b = lambda b, s, pt: (b, 0, 0)
    in_specs = [
        pl.BlockSpec((1, nrow, width), per_b),
        pl.BlockSpec((1, LANES, width), per_b),
        pl.BlockSpec((1, LANES, width), per_b),
    ]
    in_specs += [pl.BlockSpec((None, width, page), k_map(g)) for g in range(group)]
    in_specs += [pl.BlockSpec((None, width, page), v_map(g)) for g in range(group)]
    cw = group * page
    grid_spec = pltpu.PrefetchScalarGridSpec(
        num_scalar_prefetch=1,
        grid=(nb, 2 * nchunk),
        in_specs=in_specs,
        out_specs=pl.BlockSpec((1, nq, width), per_b),
        scratch_shapes=[
            pltpu.VMEM((nchunk, nrow, cw), F32),
            pltpu.VMEM((nchunk, nrow, cw), BF16),
            pltpu.VMEM((nrow, LANES), F32),
            pltpu.VMEM((nrow, LANES), F32),
            pltpu.VMEM((nrow, width), F32),
        ],
    )
    return pl.pallas_call(
        functools.partial(_paged_kernel, mode=mode, group=group, nchunk=nchunk, nheads=nheads),
        grid_spec=grid_spec,
        out_shape=jax.ShapeDtypeStruct((nb, nq, width), F32),
        compiler_params=_cparams(("parallel", "arbitrary")),
    )(pt, qrep, knew, vnew, *([ck] * group), *([cv] * group))


def _finish(y_first, y_second, w_ref, g_ref, x_ref, o_ref):
    half = y_first.shape[1]
    out = (jnp.dot(y_first.astype(BF16), w_ref[0:half, :], preferred_element_type=F32)
           + jnp.dot(y_second.astype(BF16), w_ref[half:2 * half, :], preferred_element_type=F32))
    o_ref[...] = x_ref[...] + _rms(out, g_ref[...])


def _even_out_prompt_kernel(ab_ref, ac_ref, ah_ref, az_ref, bz_ref, hc_ref, hh_ref, yb_ref, x_ref,
                            w_ref, cw_ref, g_ref, o_ref, tail_ref, ue_ref, *, tiles_per_seq):
    i = pl.program_id(0)
    tm = ab_ref.shape[0]
    u = ac_ref[...] * ah_ref[...]
    first = (i % tiles_per_seq) == 0
    ue_ref[0:8, :] = jnp.where(first, 0.0, hc_ref[...] * hh_ref[...])
    ue_ref[8:8 + tm, :] = u
    conv = (ue_ref[6:6 + tm, :] * cw_ref[0:1, :] + ue_ref[7:7 + tm, :] * cw_ref[1:2, :]
            + u * cw_ref[2:3, :])
    y_a = ab_ref[...] * conv * _silu(az_ref[...])
    y_b = yb_ref[...] * _silu(bz_ref[...])
    tail_ref[...] = u[tm - 8:tm, :]
    _finish(y_a, y_b, w_ref, g_ref, x_ref, o_ref)


def _even_out_prompt(proj, y_b, x, w_out_bf, conv_w, g_post, seq, tm):
    m, d = x.shape
    nt = m // tm
    col = lambda c: pl.BlockSpec((tm, BRANCH_W), lambda i, c=c: (i, c))
    halo = lambda c: pl.BlockSpec((8, BRANCH_W),
                                  lambda i, c=c: (jnp.maximum(i * (tm // 8) - 1, 0), c))
    return pl.pallas_call(
        functools.partial(_even_out_prompt_kernel, tiles_per_seq=seq // tm),
        grid=(nt,),
        in_specs=[col(0), col(1), col(2), col(3), col(7), halo(1), halo(2),
                  pl.BlockSpec((tm, BRANCH_W), lambda i: (i, 0)),
                  pl.BlockSpec((tm, d), lambda i: (i, 0)),
                  pl.BlockSpec(w_out_bf.shape, lambda i: (0, 0)),
                  pl.BlockSpec(conv_w.shape, lambda i: (0, 0)),
                  pl.BlockSpec((1, d), lambda i: (0, 0))],
        out_specs=[pl.BlockSpec((tm, d), lambda i: (i, 0)),
                   pl.BlockSpec((8, BRANCH_W), lambda i: (i, 0))],
        out_shape=[jax.ShapeDtypeStruct((m, d), F32),
                   jax.ShapeDtypeStruct((nt * 8, BRANCH_W), F32)],
        scratch_shapes=[pltpu.VMEM((tm + 8, BRANCH_W), F32)],
        compiler_params=_cparams(("parallel",)),
    )(proj, proj, proj, proj, proj, proj, proj, y_b, x, w_out_bf, conv_w, g_post.reshape(1, d))


def _even_out_sample_kernel(proj_ref, st_ref, yb_ref, x_ref, w_ref, cw_ref, g_ref,
                            o_ref, tail_ref, *, nb):
    w = BRANCH_W
    rows = proj_ref.shape[0]
    u = proj_ref[:, w:2 * w] * proj_ref[:, 2 * w:3 * w]
    ue = jnp.concatenate([st_ref[...], u], axis=0)
    conv = (ue[0:rows] * cw_ref[0:1, :] + ue[nb:nb + rows] * cw_ref[1:2, :]
            + ue[2 * nb:2 * nb + rows] * cw_ref[2:3, :])
    y_a = proj_ref[:, 0:w] * conv * _silu(proj_ref[:, 3 * w:4 * w])
    y_b = yb_ref[...] * _silu(proj_ref[:, 7 * w:8 * w])
    tail_ref[...] = ue[rows:rows + 2 * nb]
    _finish(y_a, y_b, w_ref, g_ref, x_ref, o_ref)


def _even_out_sample(proj, state_tm, y_b, x, w_out_bf, conv_w, g_post, nb):
    m, d = x.shape
    return pl.pallas_call(
        functools.partial(_even_out_sample_kernel, nb=nb),
        out_shape=[jax.ShapeDtypeStruct((m, d), F32),
                   jax.ShapeDtypeStruct(((CONV_W - 1) * nb, BRANCH_W), F32)],
        compiler_params=pltpu.CompilerParams(vmem_limit_bytes=VMEM_LIMIT),
    )(proj, state_tm, y_b, x, w_out_bf, conv_w, g_post.reshape(1, d))


def _gmlp_mix(v, gw_ref, low):
    r_i = lax.broadcasted_iota(jnp.int32, (GMLP_CHUNK, GMLP_CHUNK), 0)
    c_i = lax.broadcasted_iota(jnp.int32, (GMLP_CHUNK, GMLP_CHUNK), 1)
    tril = c_i <= r_i
    parts = []
    for p in range(v.shape[1] // LANES):
        vp = v[:, p * LANES:(p + 1) * LANES]
        v_lo = jnp.where(low, vp, 0.0).astype(BF16)
        v_hi = jnp.where(low, 0.0, vp).astype(BF16)
        w_lo = jnp.where(tril, gw_ref[2 * p], 0.0).astype(BF16)
        w_hi = jnp.where(tril, gw_ref[2 * p + 1], 0.0).astype(BF16)
        parts.append(jnp.dot(w_lo, v_lo, preferred_element_type=F32)
                     + jnp.dot(w_hi, v_hi, preferred_element_type=F32))
    return jnp.concatenate(parts, axis=1)


def _odd_out_prompt_kernel(cu_ref, cv_ref, cz_ref, dz_ref, yd_ref, x_ref, w_ref, gw_ref, gb_ref,
                           g_ref, o_ref, yc_ref):
    tm = cu_ref.shape[0]
    low = lax.broadcasted_iota(jnp.int32, (1, LANES), 1) < HEAD_DIM
    for c in range(tm // GMLP_CHUNK):
        rows = slice(c * GMLP_CHUNK, (c + 1) * GMLP_CHUNK)
        mixed = _gmlp_mix(cv_ref[rows, :], gw_ref, low) + gb_ref[...]
        yc_ref[rows, :] = cu_ref[rows, :] * mixed * _silu(cz_ref[rows, :])
    y_d = yd_ref[...] * _silu(dz_ref[...])
    _finish(yc_ref[...], y_d, w_ref, g_ref, x_ref, o_ref)


def _odd_out_prompt(proj, y_d, x, w_out_bf, gmlp_w, gb_full, g_post, tm):
    m, d = x.shape
    col = lambda c: pl.BlockSpec((tm, BRANCH_W), lambda i, c=c: (i, c))
    return pl.pallas_call(
        _odd_out_prompt_kernel,
        grid=(m // tm,),
        in_specs=[col(0), col(1), col(2), col(6),
                  pl.BlockSpec((tm, BRANCH_W), lambda i: (i, 0)),
                  pl.BlockSpec((tm, d), lambda i: (i, 0)),
                  pl.BlockSpec(w_out_bf.shape, lambda i: (0, 0)),
                  pl.BlockSpec(gmlp_w.shape, lambda i: (0, 0, 0)),
                  pl.BlockSpec(gb_full.shape, lambda i: (0, 0)),
                  pl.BlockSpec((1, d), lambda i: (0, 0))],
        out_specs=pl.BlockSpec((tm, d), lambda i: (i, 0)),
        out_shape=jax.ShapeDtypeStruct((m, d), F32),
        scratch_shapes=[pltpu.VMEM((tm, BRANCH_W), F32)],
        compiler_params=_cparams(("parallel",)),
    )(proj, proj, proj, proj, y_d, x, w_out_bf, gmlp_w, gb_full, g_post.reshape(1, d))


def _odd_out_sample_kernel(proj_ref, yd_ref, x_ref, w_ref, w4_ref, b4_ref, g_ref, o_ref, *, nb, nq):
    w = BRANCH_W
    parts = []
    for t in range(nq):
        mixed = jnp.broadcast_to(b4_ref[t:t + 1, :], (nb, w))
        for s in range(t + 1):
            mixed = mixed + w4_ref[t * nq + s:t * nq + s + 1, :] * proj_ref[s * nb:(s + 1) * nb, w:2 * w]
        parts.append(mixed)
    mixed = jnp.concatenate(parts, axis=0)
    y_c = proj_ref[:, 0:w] * mixed * _silu(proj_ref[:, 2 * w:3 * w])
    y_d = yd_ref[...] * _silu(proj_ref[:, 6 * w:7 * w])
    _finish(y_c, y_d, w_ref, g_ref, x_ref, o_ref)


def _odd_out_sample(proj, y_d, x, w_out_bf, w4, b4, g_post, nb, nq):
    m, d = x.shape
    return pl.pallas_call(
        functools.partial(_odd_out_sample_kernel, nb=nb, nq=nq),
        out_shape=jax.ShapeDtypeStruct((m, d), F32),
        compiler_params=pltpu.CompilerParams(vmem_limit_bytes=VMEM_LIMIT),
    )(proj, y_d, x, w_out_bf, w4, b4, g_post.reshape(1, d))


def _to_token_major(a):
    nb, nq, w = a.shape
    return a.transpose(1, 0, 2).reshape(nq * nb, w)


def _to_batch_major(a, nb):
    w = a.shape[1]
    return a.reshape(-1, nb, w).transpose(1, 0, 2)


def _even_layer(hp, hs, dims, state_conv, cache_k, cache_v, page_table,
                g_pre, g_post, w_in, conv_w, w_out, tm, group):
    bsz, seq, nb, nq = dims
    w = BRANCH_W
    w_in_bf = w_in.astype(BF16)
    w_out_bf = w_out.astype(BF16)
    proj_p, qkv_p = _proj(hp, g_pre, w_in_bf, 4, tm)
    proj_s, qkv_s = _proj(hs, g_pre, w_in_bf, 4, hs.shape[0])
    yb_p = _moba_prompt(qkv_p, bsz, seq)
    q_s, k_s, v_s = (_to_batch_major(qkv_s[:, c * w:(c + 1) * w], nb) for c in range(3))
    yb_s = _to_token_major(_paged_attention("moba", q_s, k_s, v_s, cache_k, cache_v, page_table, group))
    hp, tail_p = _even_out_prompt(proj_p, yb_p, hp, w_out_bf, conv_w, g_post, seq, tm)
    state_tm = _to_token_major(state_conv)
    hs, tail_s = _even_out_sample(proj_s, state_tm, yb_s, hs, w_out_bf, conv_w, g_post, nb)
    conv_p = tail_p.reshape(bsz, seq // tm, 8, w)[:, -1, 8 - (CONV_W - 1):, :]
    conv_s = _to_batch_major(tail_s, nb)
    heads = w // HEAD_DIM
    k_p = proj_p[:, 5 * w:6 * w].reshape(bsz, seq, heads, HEAD_DIM)
    v_p = proj_p[:, 6 * w:7 * w].reshape(bsz, seq, heads, HEAD_DIM)
    k_sn = _to_batch_major(proj_s[:, 5 * w:6 * w], nb).reshape(nb, nq, heads, HEAD_DIM)
    v_sn = _to_batch_major(proj_s[:, 6 * w:7 * w], nb).reshape(nb, nq, heads, HEAD_DIM)
    return hp, hs, conv_p, conv_s, k_p, v_p, k_sn, v_sn


def _odd_layer(hp, hs, dims, cache_k, cache_v, page_table,
               g_pre, g_post, w_in, gmlp_w, gmlp_b, w_out, tm, group):
    bsz, seq, nb, nq = dims
    w = BRANCH_W
    w_in_bf = w_in.astype(BF16)
    w_out_bf = w_out.astype(BF16)
    ngroups = gmlp_w.shape[0]
    cpg = w // ngroups
    proj_p, qkv_p = _proj(hp, g_pre, w_in_bf, 3, tm)
    proj_s, qkv_s = _proj(hs, g_pre, w_in_bf, 3, hs.shape[0])
    yd_p = _sb_prompt(qkv_p, bsz, seq)
    q_s, k_s, v_s = (_to_batch_major(qkv_s[:, c * w:(c + 1) * w], nb) for c in range(3))
    yd_s = _to_token_major(_paged_attention("sb", q_s, k_s, v_s, cache_k, cache_v, page_table, group))
    gb_full = jnp.repeat(gmlp_b.T, cpg, axis=1)
    hp = _odd_out_prompt(proj_p, yd_p, hp, w_out_bf, gmlp_w, gb_full, g_post, tm)
    w4 = jnp.repeat(gmlp_w[:, :nq, :nq].transpose(1, 2, 0), cpg, axis=2).reshape(nq * nq, w)
    b4 = jnp.repeat(gmlp_b[:, :nq].T, cpg, axis=1)
    hs = _odd_out_sample(proj_s, yd_s, hs, w_out_bf, w4, b4, g_post, nb, nq)
    heads = w // HEAD_DIM
    k_p = proj_p[:, 4 * w:5 * w].reshape(bsz, seq, heads, HEAD_DIM)
    v_p = proj_p[:, 5 * w:6 * w].reshape(bsz, seq, heads, HEAD_DIM)
    k_sn = _to_batch_major(proj_s[:, 4 * w:5 * w], nb).reshape(nb, nq, heads, HEAD_DIM)
    v_sn = _to_batch_major(proj_s[:, 5 * w:6 * w], nb).reshape(nb, nq, heads, HEAD_DIM)
    open_start = ((seq - 1) // GMLP_CHUNK) * GMLP_CHUNK
    gv_p = proj_p[:, w:2 * w].reshape(bsz, seq, w)[:, open_start:, :]
    gv_s = _to_batch_major(proj_s[:, w:2 * w], nb)
    return hp, hs, k_p, v_p, k_sn, v_sn, gv_p, gv_s


def kernel(x_prompt, x_sample, state_conv, cache_k_moba, cache_v_moba, cache_k_sb, cache_v_sb,
           page_table, norm_pre_e, norm_post_e, w_in_e, conv_w, w_out_e,
           norm_pre_o, norm_post_o, w_in_o, gmlp_w, gmlp_b, w_out_o):
    bsz, seq, d = x_prompt.shape
    nb, nq, _ = x_sample.shape
    depth = norm_pre_e.shape[0] + norm_pre_o.shape[0]
    npages, page = page_table.shape[1], cache_k_moba.shape[2]
    assert seq % MOBA_BLOCK == 0 and (npages * page) % MOBA_BLOCK == 0 and nq <= GMLP_CHUNK
    assert (nq - 1) // MOBA_BLOCK == 0 and w_in_e.shape[2] == 8 * BRANCH_W and w_in_o.shape[2] == 7 * BRANCH_W
    dims = (bsz, seq, nb, nq)
    tm = min(512, seq)
    group = min(8, npages)
    hp = x_prompt.reshape(bsz * seq, d)
    hs = _to_token_major(x_sample)
    ev, od = [], []
    for i in range(depth):
        j = i // 2
        if i % 2 == 0:
            hp, hs, *rest = _even_layer(
                hp, hs, dims, state_conv[j], cache_k_moba[j], cache_v_moba[j], page_table,
                norm_pre_e[j], norm_post_e[j], w_in_e[j], conv_w[j], w_out_e[j], tm, group)
            ev.append(rest)
        else:
            hp, hs, *rest = _odd_layer(
                hp, hs, dims, cache_k_sb[j], cache_v_sb[j], page_table,
                norm_pre_o[j], norm_post_o[j], w_in_o[j], gmlp_w[j], gmlp_b[j], w_out_o[j], tm, group)
            od.append(rest)
    y_prompt = hp.reshape(bsz, seq, d)
    y_sample = _to_batch_major(hs, nb)
    ev_out = [jnp.stack([r[k] for r in ev]) for k in range(6)]
    od_out = [jnp.stack([r[k] for r in od]) for k in range(6)]
    conv_p, conv_s, kmp, vmp, kms, vms = ev_out
    ksp, vsp, kss, vss, gvp, gvs = od_out
    return (y_prompt, y_sample, conv_p, conv_s, kmp, vmp, kms, vms, ksp, vsp, kss, vss, gvp, gvs)
```

```python
import functools

import jax
import jax.numpy as jnp
from jax import lax
from jax.experimental import pallas as pl
from jax.experimental.pallas import tpu as pltpu

F32 = jnp.float32
BF16 = jnp.bfloat16

HEAD_DIM = 64
LANES = 128
BRANCH_W = 512
MOBA_BLOCK = 256
MOBA_TOPK = 3
GMLP_CHUNK = 128
CONV_W = 3
RMS_EPS = 1e-6
NEG = -1e30
LOG2E = 1.4426950408889634
Q_SCALE = HEAD_DIM ** -0.5 * LOG2E
SB_TILE = 256
SB_DEAD = -160.0
VMEM_LIMIT = 56 * 1024 * 1024

_NT = (((1,), (1,)), ((), ()))


def _cparams(sem):
    return pltpu.CompilerParams(dimension_semantics=sem, vmem_limit_bytes=VMEM_LIMIT)


def _rms(x, g):
    ms = jnp.mean(x * x, axis=-1, keepdims=True)
    return x * lax.rsqrt(ms + RMS_EPS) * g


def _silu(x):
    return x / (1.0 + jnp.exp(-x))


def _softplus2(z2):
    return jnp.maximum(z2, 0.0) + jnp.log2(1.0 + jnp.exp2(-jnp.abs(z2)))


def _split_bf16(x):
    hi = x.astype(BF16)
    lo = (x - hi.astype(F32)).astype(BF16)
    return hi, lo


def _tri(n):
    r = lax.broadcasted_iota(jnp.int32, (n, n), 0)
    c = lax.broadcasted_iota(jnp.int32, (n, n), 1)
    return jnp.where(r >= c, 1.0, 0.0).astype(BF16)


def _proj_kernel(x_ref, g_ref, w_ref, rest_ref, qk_ref, kt_ref, vt_ref, vtb_ref, xn_ref,
                 *, qkv_first, blk):
    j = pl.program_id(2)
    tm = x_ref.shape[0]

    @pl.when(j == 0)
    def _():
        xn_ref[...] = _rms(x_ref[...], g_ref[...]).astype(BF16)

    r = jnp.dot(xn_ref[...], w_ref[...], preferred_element_type=F32)

    @pl.when((j < qkv_first) | (j >= qkv_first + 3))
    def _():
        rest_ref[...] = r

    @pl.when(j == qkv_first)
    def _():
        qk_ref[...] = (r * Q_SCALE).astype(BF16)

    @pl.when(j == qkv_first + 1)
    def _():
        qk_ref[...] = r.astype(BF16)
        for c in range(tm // blk):
            kt_ref[:, c * blk:(c + 1) * blk] = r[c * blk:(c + 1) * blk, :].T

    @pl.when(j == qkv_first + 2)
    def _():
        for c in range(tm // blk):
            t = r[c * blk:(c + 1) * blk, :].T
            vt_ref[:, c * blk:(c + 1) * blk] = t
            vtb_ref[c] = t.astype(BF16)


def _proj(x, g, w_bf, qkv_first, bsz, seq, tm):
    m, d = x.shape
    n = w_bf.shape[1]
    nj = n // BRANCH_W
    tps = seq // tm
    blk = min(MOBA_BLOCK, tm)
    w = BRANCH_W

    def rest_col(j):
        return jnp.where(j < qkv_first, j, jnp.where(j < qkv_first + 3, qkv_first - 1, j - 3))

    return pl.pallas_call(
        functools.partial(_proj_kernel, qkv_first=qkv_first, blk=blk),
        grid=(bsz, tps, nj),
        in_specs=[
            pl.BlockSpec((tm, d), lambda b, t, j: (b * tps + t, 0)),
            pl.BlockSpec((1, d), lambda b, t, j: (0, 0)),
            pl.BlockSpec((d, w), lambda b, t, j: (0, j)),
        ],
        out_specs=[
            pl.BlockSpec((tm, w), lambda b, t, j: (b * tps + t, rest_col(j))),
            pl.BlockSpec((tm, w), lambda b, t, j: (b * tps + t, jnp.clip(j - qkv_first, 0, 1))),
            pl.BlockSpec((None, w, tm), lambda b, t, j: (b, 0, t)),
            pl.BlockSpec((None, w, tm), lambda b, t, j: (b, 0, t)),
            pl.BlockSpec((None, tm // blk, w, blk), lambda b, t, j: (b, t, 0, 0)),
        ],
        out_shape=[
            jax.ShapeDtypeStruct((m, n - 3 * w), F32),
            jax.ShapeDtypeStruct((m, 2 * w), BF16),
            jax.ShapeDtypeStruct((bsz, w, seq), F32),
            jax.ShapeDtypeStruct((bsz, w, seq), F32),
            jax.ShapeDtypeStruct((bsz, seq // blk, w, blk), BF16),
        ],
        scratch_shapes=[pltpu.VMEM((tm, d), BF16)],
        compiler_params=_cparams(("parallel", "parallel", "arbitrary")),
    )(x, g.reshape(1, d), w_bf)


def _moba_prompt_kernel(q_ref, k_ref, vt_ref, o_ref, km_ref, *, nblk):
    i = pl.program_id(2)
    lane = lax.broadcasted_iota(jnp.int32, (1, LANES), 1)
    low = lane < HEAD_DIM
    nbp = km_ref.shape[0]
    blk = MOBA_BLOCK
    nheads = q_ref.shape[1] // HEAD_DIM

    @pl.when(i == 0)
    def _():
        km_ref[...] = jnp.zeros_like(km_ref)
        for j in range(nblk):
            rows = k_ref[j * blk:(j + 1) * blk, :].astype(F32)
            km_ref[j:j + 1, :] = jnp.sum(rows, axis=0, keepdims=True) * (1.0 / blk)

    brow = lax.broadcasted_iota(jnp.int32, (nbp, blk), 0)
    key_i = lax.broadcasted_iota(jnp.int32, (blk, blk), 0)
    qry_i = lax.broadcasted_iota(jnp.int32, (blk, blk), 1)
    start = pl.multiple_of(i * blk, blk)

    ones_rows = jnp.ones((2 * 8, blk), BF16)

    def v_rows(j, hh):
        return jnp.concatenate([vt_ref[j, hh * HEAD_DIM:(hh + 1) * HEAD_DIM, :], ones_rows], axis=0)

    q_owns, gates, own_scores = [], [], []
    for hh in range(nheads):
        cols = slice((hh // 2) * LANES, (hh // 2 + 1) * LANES)
        q = q_ref[:, cols]
        km = km_ref[:, cols]
        mine = low if hh % 2 == 0 else jnp.logical_not(low)
        km_hi, km_lo = _split_bf16(jnp.where(mine, km, 0.0))
        gates.append(lax.dot_general(km_hi, q, _NT, preferred_element_type=F32)
                     + lax.dot_general(km_lo, q, _NT, preferred_element_type=F32))
        q_own = jnp.where(mine, q, jnp.zeros_like(q))
        q_owns.append(q_own)
        own_scores.append(lax.dot_general(k_ref[pl.ds(start, blk), cols], q_own, _NT,
                                          preferred_element_type=F32))

    q_pasts, init = [], []
    for hh in range(nheads):
        gate = jnp.where(brow < i, gates[hh], -jnp.inf)
        bias = jnp.full((nbp, blk), NEG, F32)
        for _ in range(MOBA_TOPK):
            mx = jnp.max(gate, axis=0, keepdims=True)
            first = jnp.min(jnp.where(gate == mx, brow, nbp), axis=0, keepdims=True)
            pick = (brow == first) & (mx > -jnp.inf)
            bias = jnp.where(pick, 0.0, bias)
            gate = jnp.where(pick, -jnp.inf, gate)
        bias_t = jnp.concatenate([bias, jnp.zeros((LANES - nbp, blk), F32)], axis=0).T
        q_pasts.append(jnp.concatenate([q_owns[hh], bias_t.astype(BF16)], axis=1))

        s = jnp.where(key_i <= qry_i, own_scores[hh], NEG)
        m0 = jnp.max(s, axis=0, keepdims=True)
        p = jnp.exp2(s - m0)
        acc0 = jnp.dot(v_rows(i, hh), p.astype(BF16), preferred_element_type=F32)
        init.append((m0, acc0))

    def body(j, carry):
        st = pl.multiple_of(j * blk, blk)
        onehot = jnp.broadcast_to(jnp.where(lane == j, 1.0, 0.0).astype(BF16), (blk, LANES))
        scores = []
        for hh in range(nheads):
            kj = k_ref[pl.ds(st, blk), (hh // 2) * LANES:(hh // 2 + 1) * LANES]
            scores.append(lax.dot_general(jnp.concatenate([kj, onehot], axis=1), q_pasts[hh], _NT,
                                          preferred_element_type=F32))
        new = []
        for hh in range(nheads):
            m, acc = carry[hh]
            sj = scores[hh]
            m_new = jnp.maximum(m, jnp.max(sj, axis=0, keepdims=True))
            alpha = jnp.exp2(m - m_new)
            pj = jnp.exp2(sj - m_new)
            acc = alpha * acc + jnp.dot(v_rows(j, hh), pj.astype(BF16), preferred_element_type=F32)
            new.append((m_new, acc))
        return tuple(new)

    fin = lax.fori_loop(0, i, body, tuple(init))
    out_t = jnp.concatenate([acc[:HEAD_DIM] / acc[HEAD_DIM:HEAD_DIM + 1] for _, acc in fin],
                            axis=0)
    o_ref[...] = out_t.T


def _attn_prompt_call(kernel_fn, qk, vtb, bsz, seq, scratch, wid):
    nblk = seq // MOBA_BLOCK
    ngrp = BRANCH_W // wid
    return pl.pallas_call(
        kernel_fn,
        grid=(bsz, ngrp, nblk),
        in_specs=[
            pl.BlockSpec((MOBA_BLOCK, wid), lambda b, p, i: (b * nblk + i, p)),
            pl.BlockSpec((seq, wid), lambda b, p, i: (b, ngrp + p)),
            pl.BlockSpec((None, nblk, wid, MOBA_BLOCK), lambda b, p, i: (b, 0, p, 0)),
        ],
        out_specs=pl.BlockSpec((MOBA_BLOCK, wid), lambda b, p, i: (b * nblk + i, p)),
        out_shape=jax.ShapeDtypeStruct((bsz * seq, BRANCH_W), F32),
        scratch_shapes=scratch,
        compiler_params=_cparams(("parallel", "parallel", "arbitrary")),
    )(qk, qk, vtb)


MOBA_STEP_W = 512


def _moba_prompt(qk, vtb, bsz, seq):
    nblk = seq // MOBA_BLOCK
    nbp = -(-nblk // 8) * 8
    return _attn_prompt_call(functools.partial(_moba_prompt_kernel, nblk=nblk), qk, vtb, bsz, seq,
                             [pltpu.VMEM((nbp, MOBA_STEP_W), F32)], MOBA_STEP_W)


def _sb_prompt_kernel(q_ref, k_ref, vt_ref, o_ref):
    i = pl.program_id(2)
    nheads = q_ref.shape[1] // HEAD_DIM
    tile = SB_TILE
    lane = lax.broadcasted_iota(jnp.int32, (1, LANES), 1)
    low = lane < HEAD_DIM
    key_i = lax.broadcasted_iota(jnp.int32, (tile, tile), 0)
    qry_i = lax.broadcasted_iota(jnp.int32, (tile, tile), 1)
    tri_t = jnp.where(qry_i >= key_i, 1.0, 0.0).astype(BF16)
    strict = key_i < qry_i

    q_hs = []
    for hh in range(nheads):
        q = q_ref[:, (hh // 2) * LANES:(hh // 2 + 1) * LANES]
        mine = low if hh % 2 == 0 else jnp.logical_not(low)
        q_hs.append(jnp.where(mine, q, jnp.zeros_like(q)))

    def sweep(j, state, diag):
        st = pl.multiple_of(j * tile, tile)
        zs = [lax.dot_general(k_ref[pl.ds(st, tile), (hh // 2) * LANES:(hh // 2 + 1) * LANES],
                              q_hs[hh], _NT, preferred_element_type=F32)
              for hh in range(nheads)]
        cums = []
        for hh in range(nheads):
            lk = -_softplus2(zs[hh])
            if diag:
                lk = jnp.where(strict, lk, 0.0)
            hi, lo = _split_bf16(lk)
            cums.append(jnp.dot(tri_t, hi, preferred_element_type=F32)
                        + jnp.dot(tri_t, lo, preferred_element_type=F32) + state[hh][0])
        new = []
        for hh in range(nheads):
            w = jnp.exp2(zs[hh] + cums[hh])
            if diag:
                w = jnp.where(strict, w, 0.0)
            acc = state[hh][1] + jnp.dot(vt_ref[j, hh * HEAD_DIM:(hh + 1) * HEAD_DIM, :],
                                         w.astype(BF16), preferred_element_type=F32)
            new.append((cums[hh][0:1, :], acc))
        return tuple(new)

    def live_of(state):
        live = jnp.max(state[0][0])
        for hh in range(1, nheads):
            live = jnp.maximum(live, jnp.max(state[hh][0]))
        return live

    def cond(st):
        j, live, _ = st
        return (j >= 0) & (live > SB_DEAD)

    def body(st):
        j, _, state = st
        new = sweep(j, state, False)
        return j - 1, live_of(new), new

    zero = tuple((jnp.zeros((1, tile), F32), jnp.zeros((HEAD_DIM, tile), F32))
                 for _ in range(nheads))
    init = sweep(i, zero, True)
    _, _, fin = lax.while_loop(cond, body, (i - 1, live_of(init), init))
    o_ref[...] = jnp.concatenate([acc for _, acc in fin], axis=0).T


SB_STEP_W = 512


def _sb_prompt(qk, vtb, bsz, seq):
    return _attn_prompt_call(_sb_prompt_kernel, qk, vtb, bsz, seq, [], SB_STEP_W)


def _query_rows(qrep_ref, nheads):
    nrow, width = qrep_ref.shape[1], qrep_ref.shape[2]
    row_w = lax.broadcasted_iota(jnp.int32, (nrow, width), 0)
    lane_w = lax.broadcasted_iota(jnp.int32, (nrow, width), 1)
    head_mask = (lane_w // HEAD_DIM) == (row_w % nheads)
    qrows = jnp.where(head_mask, qrep_ref[0], jnp.zeros_like(qrep_ref[0]))
    return qrows, head_mask


def _pick_heads(y, head_mask, nheads):
    nrow, width = y.shape
    y = jnp.where(head_mask, y, 0.0)
    return jnp.sum(y.reshape(nrow // nheads, nheads, width), axis=1)


def _moba_paged_kernel(pt_ref, qrep_ref, knew_ref, vnew_ref, *rest, group, nchunk, nheads):
    del pt_ref
    k_refs = rest[:group]
    v_refs = rest[group:2 * group]
    o_ref = rest[2 * group]
    s_ref, p_ref, pn_ref, l_ref, acc_ref = rest[2 * group + 1:]
    s = pl.program_id(1)
    nrow = qrep_ref.shape[1]
    page = k_refs[0].shape[1]
    cw = group * page
    qrows, head_mask = _query_rows(qrep_ref, nheads)
    row = lax.broadcasted_iota(jnp.int32, (nrow, LANES), 0)
    lane = lax.broadcasted_iota(jnp.int32, (nrow, LANES), 1)
    qidx = row // nheads

    @pl.when(s < nchunk)
    def _():
        for g in range(group):
            s_ref[s, :, g * page:(g + 1) * page] = jnp.dot(
                qrows, k_refs[g][...].astype(BF16), preferred_element_type=F32)

    @pl.when(s == nchunk - 1)
    def _():
        zn = lax.dot_general(qrows, knew_ref[0], _NT, preferred_element_type=F32)
        bpc = cw // MOBA_BLOCK
        nblk = nchunk * bpc
        gates = jnp.zeros((nrow, LANES), F32)
        for c in range(nchunk):
            for bl in range(bpc):
                sb = s_ref[c, :, bl * MOBA_BLOCK:(bl + 1) * MOBA_BLOCK]
                gates = jnp.where(lane == c * bpc + bl,
                                  jnp.sum(sb, axis=1, keepdims=True), gates)
        gates = jnp.where(lane < nblk, gates, -jnp.inf)
        sel = jnp.zeros((nrow, LANES), F32)
        for _ in range(MOBA_TOPK):
            mx = jnp.max(gates, axis=1, keepdims=True)
            first = jnp.min(jnp.where(gates == mx, lane, LANES), axis=1, keepdims=True)
            pick = (lane == first) & (mx > -jnp.inf)
            sel = jnp.where(pick, 1.0, sel)
            gates = jnp.where(pick, -jnp.inf, gates)
        zn = jnp.where(lane <= qidx, zn, NEG)
        m = jnp.max(zn, axis=1, keepdims=True)
        cols = []
        for c in range(nchunk):
            for bl in range(bpc):
                col = jnp.max(jnp.where(lane == c * bpc + bl, sel, 0.0),
                              axis=1, keepdims=True) > 0.5
                cols.append(col)
                sb = s_ref[c, :, bl * MOBA_BLOCK:(bl + 1) * MOBA_BLOCK]
                m = jnp.maximum(m, jnp.max(jnp.where(col, sb, NEG), axis=1, keepdims=True))
        pn = jnp.exp2(zn - m)
        l = jnp.sum(pn, axis=1, keepdims=True)
        pn_ref[...] = pn
        for c in range(nchunk):
            for bl in range(bpc):
                sb = s_ref[c, :, bl * MOBA_BLOCK:(bl + 1) * MOBA_BLOCK]
                p = jnp.exp2(jnp.where(cols[c * bpc + bl], sb, NEG) - m)
                l = l + jnp.sum(p, axis=1, keepdims=True)
                p_ref[c, :, bl * MOBA_BLOCK:(bl + 1) * MOBA_BLOCK] = p.astype(BF16)
        l_ref[...] = jnp.broadcast_to(l, l_ref.shape)

    @pl.when(s == nchunk)
    def _():
        acc_ref[...] = jnp.dot(pn_ref[...].astype(BF16), vnew_ref[0], preferred_element_type=F32)

    @pl.when(s >= nchunk)
    def _():
        c = s - nchunk
        acc = acc_ref[...]
        for g in range(group):
            acc = acc + lax.dot_general(p_ref[c, :, g * page:(g + 1) * page],
                                        v_refs[g][...].astype(BF16), _NT,
                                        preferred_element_type=F32)
        acc_ref[...] = acc

    @pl.when(s == 2 * nchunk - 1)
    def _():
        o_ref[0] = _pick_heads(acc_ref[...] / l_ref[:, 0:1], head_mask, nheads)


def _paged_operands(q_bf, knew_bf, vnew_bf, cache_k, cache_v, page_table):
    nb, nq, width = q_bf.shape
    nheads = width // HEAD_DIM
    nphys, page = cache_k.shape[0], cache_k.shape[1]
    ck = cache_k.transpose(0, 2, 3, 1).reshape(nphys, width, page)
    cv = cache_v.transpose(0, 2, 3, 1).reshape(nphys, width, page)
    qrep = jnp.repeat(q_bf, nheads, axis=1)
    pad = ((0, 0), (0, LANES - nq), (0, 0))
    knew = jnp.pad(knew_bf, pad)
    vnew = jnp.pad(vnew_bf, pad)
    pt = page_table.reshape(-1).astype(jnp.int32)
    return qrep, knew, vnew, ck, cv, pt, nheads, page


def _moba_paged(q_bf, knew_bf, vnew_bf, cache_k, cache_v, page_table, group):
    nb, nq, width = q_bf.shape
    qrep, knew, vnew, ck, cv, pt, nheads, page = _paged_operands(
        q_bf, knew_bf, vnew_bf, cache_k, cache_v, page_table)
    nrow = nq * nheads
    npages = page_table.shape[1]
    nchunk = npages // group

    def k_map(g):
        return lambda b, s, pt: (pt[b * npages + jnp.minimum(s, nchunk - 1) * group + g], 0, 0)

    def v_map(g):
        return lambda b, s, pt: (pt[b * npages + jnp.maximum(s - nchunk, 0) * group + g], 0, 0)

    per_b = lambda b, s, pt: (b, 0, 0)
    in_specs = [
        pl.BlockSpec((1, nrow, width), per_b),
        pl.BlockSpec((1, LANES, width), per_b),
        pl.BlockSpec((1, LANES, width), per_b),
    ]
    in_specs += [pl.BlockSpec((None, width, page), k_map(g)) for g in range(group)]
    in_specs += [pl.BlockSpec((None, width, page), v_map(g)) for g in range(group)]
    cw = group * page
    grid_spec = pltpu.PrefetchScalarGridSpec(
        num_scalar_prefetch=1,
        grid=(nb, 2 * nchunk),
        in_specs=in_specs,
        out_specs=pl.BlockSpec((1, nq, width), per_b),
        scratch_shapes=[
            pltpu.VMEM((nchunk, nrow, cw), F32),
            pltpu.VMEM((nchunk, nrow, cw), BF16),
            pltpu.VMEM((nrow, LANES), F32),
            pltpu.VMEM((nrow, LANES), F32),
            pltpu.VMEM((nrow, width), F32),
        ],
    )
    return pl.pallas_call(
        functools.partial(_moba_paged_kernel, group=group, nchunk=nchunk, nheads=nheads),
        grid_spec=grid_spec,
        out_shape=jax.ShapeDtypeStruct((nb, nq, width), F32),
        compiler_params=_cparams(("parallel", "arbitrary")),
    )(pt, qrep, knew, vnew, *([ck] * group), *([cv] * group))


def _sb_paged_kernel(pt_ref, live_ref, qrep_ref, knew_ref, vnew_ref, cin_ref, ain_ref, *rest,
                     group, nheads, first):
    del pt_ref
    k_refs = rest[:group]
    v_refs = rest[group:2 * group]
    y_ref, acc_ref, carry_ref = rest[2 * group:2 * group + 3]
    b = pl.program_id(0)
    s = pl.program_id(1)
    nrow = qrep_ref.shape[1]
    page = k_refs[0].shape[1]
    ppt = SB_TILE // page
    qrows, head_mask = _query_rows(qrep_ref, nheads)
    tri = _tri(SB_TILE)

    @pl.when(s == 0)
    def _():
        if first:
            row = lax.broadcasted_iota(jnp.int32, (nrow, LANES), 0)
            lane = lax.broadcasted_iota(jnp.int32, (nrow, LANES), 1)
            strict = lane < row // nheads
            zn = lax.dot_general(qrows, knew_ref[0], _NT, preferred_element_type=F32)
            hi, lo = _split_bf16(jnp.where(strict, -_softplus2(zn), 0.0))
            tri_n = _tri(LANES)
            cum = (jnp.dot(hi, tri_n, preferred_element_type=F32)
                   + jnp.dot(lo, tri_n, preferred_element_type=F32))
            wn = jnp.where(strict, jnp.exp2(zn + cum), 0.0)
            acc_ref[0] = jnp.dot(wn.astype(BF16), vnew_ref[0], preferred_element_type=F32)
            carry_ref[0] = jnp.broadcast_to(cum[:, 0:1], carry_ref.shape[1:])
        else:
            acc_ref[0] = ain_ref[0]
            carry_ref[0] = cin_ref[0]

    @pl.when(live_ref[b] > 0)
    def _():
        carry = carry_ref[0][:, 0:1]
        acc = acc_ref[0]
        for t in reversed(range(group // ppt)):
            kt = jnp.concatenate([k_refs[t * ppt + u][...] for u in range(ppt)], axis=1).astype(BF16)
            vt = jnp.concatenate([v_refs[t * ppt + u][...] for u in range(ppt)], axis=1).astype(BF16)
            z = jnp.dot(qrows, kt, preferred_element_type=F32)
            hi, lo = _split_bf16(-_softplus2(z))
            cum = (jnp.dot(hi, tri, preferred_element_type=F32)
                   + jnp.dot(lo, tri, preferred_element_type=F32)) + carry
            w = jnp.exp2(z + cum).astype(BF16)
            acc = acc + lax.dot_general(w, vt, _NT, preferred_element_type=F32)
            carry = cum[:, 0:1]
        acc_ref[0] = acc
        carry_ref[0] = jnp.broadcast_to(carry, carry_ref.shape[1:])

    @pl.when(s == pl.num_programs(1) - 1)
    def _():
        y_ref[0] = _pick_heads(acc_ref[0], head_mask, nheads)


def _sb_paged_call(first, pt, live, qrep, knew, vnew, carry_in, acc_in, ck, cv,
                   nq, npages, group, chunk_lo, nsteps, nheads):
    nb, nrow, width = qrep.shape
    page = ck.shape[2]

    def page_map(g):
        def index(b, s, pt, live):
            chunk_page = (chunk_lo + nsteps - 1 - s) * group + g
            return (jnp.where(live[b] > 0, pt[b * npages + chunk_page], 0), 0, 0)
        return index

    per_b = lambda b, s, pt, live: (b, 0, 0)
    in_specs = [
        pl.BlockSpec((1, nrow, width), per_b),
        pl.BlockSpec((1, LANES, width), per_b),
        pl.BlockSpec((1, LANES, width), per_b),
        pl.BlockSpec((1, nrow, LANES), per_b),
        pl.BlockSpec((1, nrow, width), per_b),
    ]
    in_specs += [pl.BlockSpec((None, width, page), page_map(g)) for g in range(group)] * 2
    grid_spec = pltpu.PrefetchScalarGridSpec(
        num_scalar_prefetch=2,
        grid=(nb, nsteps),
        in_specs=in_specs,
        out_specs=[pl.BlockSpec((1, nq, width), per_b),
                   pl.BlockSpec((1, nrow, width), per_b),
                   pl.BlockSpec((1, nrow, LANES), per_b)],
    )
    return pl.pallas_call(
        functools.partial(_sb_paged_kernel, group=group, nheads=nheads, first=first),
        grid_spec=grid_spec,
        out_shape=[jax.ShapeDtypeStruct((nb, nq, width), F32),
                   jax.ShapeDtypeStruct((nb, nrow, width), F32),
                   jax.ShapeDtypeStruct((nb, nrow, LANES), F32)],
        compiler_params=_cparams(("parallel", "arbitrary")),
    )(pt, live, qrep, knew, vnew, carry_in, acc_in, *([ck] * group), *([cv] * group))


def _sb_paged(q_bf, knew_bf, vnew_bf, cache_k, cache_v, page_table, group):
    nb, nq, width = q_bf.shape
    qrep, knew, vnew, ck, cv, pt, nheads, page = _paged_operands(
        q_bf, knew_bf, vnew_bf, cache_k, cache_v, page_table)
    nrow = nq * nheads
    npages = page_table.shape[1]
    nchunk = npages // group
    zc = jnp.zeros((nb, nrow, LANES), F32)
    za = jnp.zeros((nb, nrow, width), F32)
    all_live = jnp.ones((nb,), jnp.int32)
    y, acc, carry = _sb_paged_call(True, pt, all_live, qrep, knew, vnew, zc, za, ck, cv,
                                   nq, npages, group, nchunk - 1, 1, nheads)
    if nchunk == 1:
        return y
    live = (jnp.max(carry, axis=(1, 2)) > SB_DEAD).astype(jnp.int32)

    def older(_):
        return _sb_paged_call(False, pt, live, qrep, knew, vnew, carry, acc, ck, cv,
                              nq, npages, group, 0, nchunk - 1, nheads)[0]

    return lax.cond(jnp.any(live > 0), older, lambda _: y, None)


def _finish(y_first, y_second, w_ref, g_ref, x_ref, o_ref):
    half = y_first.shape[1]
    out = (jnp.dot(y_first.astype(BF16), w_ref[0:half, :], preferred_element_type=F32)
           + jnp.dot(y_second.astype(BF16), w_ref[half:2 * half, :], preferred_element_type=F32))
    o_ref[...] = x_ref[...] + _rms(out, g_ref[...])


def _even_out_prompt_kernel(ab_ref, ac_ref, ah_ref, az_ref, bz_ref, hc_ref, hh_ref, yb_ref, x_ref,
                            w_ref, cw_ref, g_ref, o_ref, tail_ref, ue_ref, *, tiles_per_seq):
    i = pl.program_id(0)
    tm = ab_ref.shape[0]
    u = ac_ref[...] * ah_ref[...]
    first = (i % tiles_per_seq) == 0
    ue_ref[0:8, :] = jnp.where(first, 0.0, hc_ref[...] * hh_ref[...])
    ue_ref[8:8 + tm, :] = u
    conv = (ue_ref[6:6 + tm, :] * cw_ref[0:1, :] + ue_ref[7:7 + tm, :] * cw_ref[1:2, :]
            + u * cw_ref[2:3, :])
    y_a = ab_ref[...] * conv * _silu(az_ref[...])
    y_b = yb_ref[...] * _silu(bz_ref[...])
    tail_ref[...] = u[tm - 8:tm, :]
    _finish(y_a, y_b, w_ref, g_ref, x_ref, o_ref)


def _even_out_prompt(rest, y_b, x, w_out_bf, conv_w, g_post, seq, tm):
    m, d = x.shape
    nt = m // tm
    col = lambda c: pl.BlockSpec((tm, BRANCH_W), lambda i, c=c: (i, c))
    halo = lambda c: pl.BlockSpec((8, BRANCH_W),
                                  lambda i, c=c: (jnp.maximum(i * (tm // 8) - 1, 0), c))
    return pl.pallas_call(
        functools.partial(_even_out_prompt_kernel, tiles_per_seq=seq // tm),
        grid=(nt,),
        in_specs=[col(0), col(1), col(2), col(3), col(4), halo(1), halo(2),
                  pl.BlockSpec((tm, BRANCH_W), lambda i: (i, 0)),
                  pl.BlockSpec((tm, d), lambda i: (i, 0)),
                  pl.BlockSpec(w_out_bf.shape, lambda i: (0, 0)),
                  pl.BlockSpec(conv_w.shape, lambda i: (0, 0)),
                  pl.BlockSpec((1, d), lambda i: (0, 0))],
        out_specs=[pl.BlockSpec((tm, d), lambda i: (i, 0)),
                   pl.BlockSpec((8, BRANCH_W), lambda i: (i, 0))],
        out_shape=[jax.ShapeDtypeStruct((m, d), F32),
                   jax.ShapeDtypeStruct((nt * 8, BRANCH_W), F32)],
        scratch_shapes=[pltpu.VMEM((tm + 8, BRANCH_W), F32)],
        compiler_params=_cparams(("parallel",)),
    )(rest, rest, rest, rest, rest, rest, rest, y_b, x, w_out_bf, conv_w, g_post.reshape(1, d))


def _even_out_sample_kernel(rest_ref, st_ref, yb_ref, x_ref, w_ref, cw_ref, g_ref,
                            o_ref, tail_ref, *, nb):
    w = BRANCH_W
    rows = rest_ref.shape[0]
    u = rest_ref[:, w:2 * w] * rest_ref[:, 2 * w:3 * w]
    ue = jnp.concatenate([st_ref[...], u], axis=0)
    conv = (ue[0:rows] * cw_ref[0:1, :] + ue[nb:nb + rows] * cw_ref[1:2, :]
            + ue[2 * nb:2 * nb + rows] * cw_ref[2:3, :])
    y_a = rest_ref[:, 0:w] * conv * _silu(rest_ref[:, 3 * w:4 * w])
    y_b = yb_ref[...] * _silu(rest_ref[:, 4 * w:5 * w])
    tail_ref[...] = ue[rows:rows + 2 * nb]
    _finish(y_a, y_b, w_ref, g_ref, x_ref, o_ref)


def _even_out_sample(rest, state_tm, y_b, x, w_out_bf, conv_w, g_post, nb):
    m, d = x.shape
    return pl.pallas_call(
        functools.partial(_even_out_sample_kernel, nb=nb),
        out_shape=[jax.ShapeDtypeStruct((m, d), F32),
                   jax.ShapeDtypeStruct(((CONV_W - 1) * nb, BRANCH_W), F32)],
        compiler_params=pltpu.CompilerParams(vmem_limit_bytes=VMEM_LIMIT),
    )(rest, state_tm, y_b, x, w_out_bf, conv_w, g_post.reshape(1, d))


def _gmlp_mix(v, gw_ref, low):
    r_i = lax.broadcasted_iota(jnp.int32, (GMLP_CHUNK, GMLP_CHUNK), 0)
    c_i = lax.broadcasted_iota(jnp.int32, (GMLP_CHUNK, GMLP_CHUNK), 1)
    tril = c_i <= r_i
    parts = []
    for p in range(v.shape[1] // LANES):
        vp = v[:, p * LANES:(p + 1) * LANES]
        v_lo = jnp.where(low, vp, 0.0).astype(BF16)
        v_hi = jnp.where(low, 0.0, vp).astype(BF16)
        w_lo = jnp.where(tril, gw_ref[2 * p], 0.0).astype(BF16)
        w_hi = jnp.where(tril, gw_ref[2 * p + 1], 0.0).astype(BF16)
        parts.append(jnp.dot(w_lo, v_lo, preferred_element_type=F32)
                     + jnp.dot(w_hi, v_hi, preferred_element_type=F32))
    return jnp.concatenate(parts, axis=1)


def _odd_out_prompt_kernel(cu_ref, cv_ref, cz_ref, dz_ref, yd_ref, x_ref, w_ref, gw_ref, gb_ref,
                           g_ref, o_ref, yc_ref):
    tm = cu_ref.shape[0]
    low = lax.broadcasted_iota(jnp.int32, (1, LANES), 1) < HEAD_DIM
    for c in range(tm // GMLP_CHUNK):
        rows = slice(c * GMLP_CHUNK, (c + 1) * GMLP_CHUNK)
        mixed = _gmlp_mix(cv_ref[rows, :], gw_ref, low) + gb_ref[...]
        yc_ref[rows, :] = cu_ref[rows, :] * mixed * _silu(cz_ref[rows, :])
    y_d = yd_ref[...] * _silu(dz_ref[...])
    _finish(yc_ref[...], y_d, w_ref, g_ref, x_ref, o_ref)


def _odd_out_prompt(rest, y_d, x, w_out_bf, gmlp_w, gb_full, g_post, tm):
    m, d = x.shape
    col = lambda c: pl.BlockSpec((tm, BRANCH_W), lambda i, c=c: (i, c))
    return pl.pallas_call(
        _odd_out_prompt_kernel,
        grid=(m // tm,),
        in_specs=[col(0), col(1), col(2), col(3),
                  pl.BlockSpec((tm, BRANCH_W), lambda i: (i, 0)),
                  pl.BlockSpec((tm, d), lambda i: (i, 0)),
                  pl.BlockSpec(w_out_bf.shape, lambda i: (0, 0)),
                  pl.BlockSpec(gmlp_w.shape, lambda i: (0, 0, 0)),
                  pl.BlockSpec(gb_full.shape, lambda i: (0, 0)),
                  pl.BlockSpec((1, d), lambda i: (0, 0))],
        out_specs=pl.BlockSpec((tm, d), lambda i: (i, 0)),
        out_shape=jax.ShapeDtypeStruct((m, d), F32),
        scratch_shapes=[pltpu.VMEM((tm, BRANCH_W), F32)],
        compiler_params=_cparams(("parallel",)),
    )(rest, rest, rest, rest, y_d, x, w_out_bf, gmlp_w, gb_full, g_post.reshape(1, d))


def _odd_out_sample_kernel(rest_ref, yd_ref, x_ref, w_ref, w4_ref, b4_ref, g_ref, o_ref, *, nb, nq):
    w = BRANCH_W
    parts = []
    for t in range(nq):
        mixed = jnp.broadcast_to(b4_ref[t:t + 1, :], (nb, w))
        for s in range(t + 1):
            mixed = mixed + w4_ref[t * nq + s:t * nq + s + 1, :] * rest_ref[s * nb:(s + 1) * nb, w:2 * w]
        parts.append(mixed)
    mixed = jnp.concatenate(parts, axis=0)
    y_c = rest_ref[:, 0:w] * mixed * _silu(rest_ref[:, 2 * w:3 * w])
    y_d = yd_ref[...] * _silu(rest_ref[:, 3 * w:4 * w])
    _finish(y_c, y_d, w_ref, g_ref, x_ref, o_ref)


def _odd_out_sample(rest, y_d, x, w_out_bf, w4, b4, g_post, nb, nq):
    m, d = x.shape
    return pl.pallas_call(
        functools.partial(_odd_out_sample_kernel, nb=nb, nq=nq),
        out_shape=jax.ShapeDtypeStruct((m, d), F32),
        compiler_params=pltpu.CompilerParams(vmem_limit_bytes=VMEM_LIMIT),
    )(rest, y_d, x, w_out_bf, w4, b4, g_post.reshape(1, d))


def _to_token_major(a):
    nb, nq, w = a.shape
    return a.transpose(1, 0, 2).reshape(nq * nb, w)


def _to_batch_major(a, nb):
    w = a.shape[1]
    return a.reshape(-1, nb, w).transpose(1, 0, 2)


def _heads_prompt(t, heads):
    bsz, w, seq = t.shape
    return t.reshape(bsz, heads, w // heads, seq).transpose(0, 3, 1, 2)


def _project_both(hp, hs, dims, g_pre, w_in, qkv_first, tm):
    bsz, seq, nb, nq = dims
    w_in_bf = w_in.astype(BF16)
    prompt = _proj(hp, g_pre, w_in_bf, qkv_first, bsz, seq, tm)
    rest_s, qk_s, kt_s, vt_s, _ = _proj(hs, g_pre, w_in_bf, qkv_first, 1, nq * nb, nq * nb)
    k_s = _to_batch_major(kt_s[0].T, nb)
    v_s = _to_batch_major(vt_s[0].T, nb)
    q_s = _to_batch_major(qk_s[:, :BRANCH_W], nb)
    return prompt, rest_s, q_s, k_s, v_s


def _even_layer(hp, hs, dims, state_conv, cache_k, cache_v, page_table,
                g_pre, g_post, w_in, conv_w, w_out, tiles, group):
    bsz, seq, nb, nq = dims
    tm, tm_proj = tiles
    w = BRANCH_W
    heads = w // HEAD_DIM
    w_out_bf = w_out.astype(BF16)
    (rest_p, qk_p, kt_p, vt_p, vtb_p), rest_s, q_s, k_s, v_s = _project_both(
        hp, hs, dims, g_pre, w_in, 4, tm_proj)
    yb_p = _moba_prompt(qk_p, vtb_p, bsz, seq)
    yb_s = _to_token_major(_moba_paged(q_s, k_s.astype(BF16), v_s.astype(BF16),
                                       cache_k, cache_v, page_table, group))
    hp, tail_p = _even_out_prompt(rest_p, yb_p, hp, w_out_bf, conv_w, g_post, seq, tm)
    state_tm = _to_token_major(state_conv)
    hs, tail_s = _even_out_sample(rest_s, state_tm, yb_s, hs, w_out_bf, conv_w, g_post, nb)
    conv_p = tail_p.reshape(bsz, seq // tm, 8, w)[:, -1, 8 - (CONV_W - 1):, :]
    conv_s = _to_batch_major(tail_s, nb)
    return (hp, hs, conv_p, conv_s, _heads_prompt(kt_p, heads), _heads_prompt(vt_p, heads),
            k_s.reshape(nb, nq, heads, HEAD_DIM), v_s.reshape(nb, nq, heads, HEAD_DIM))


def _odd_layer(hp, hs, dims, cache_k, cache_v, page_table,
               g_pre, g_post, w_in, gmlp_w, gmlp_b, w_out, tiles, group):
    bsz, seq, nb, nq = dims
    tm, tm_proj = tiles
    w = BRANCH_W
    heads = w // HEAD_DIM
    w_out_bf = w_out.astype(BF16)
    ngroups = gmlp_w.shape[0]
    cpg = w // ngroups
    (rest_p, qk_p, kt_p, vt_p, vtb_p), rest_s, q_s, k_s, v_s = _project_both(
        hp, hs, dims, g_pre, w_in, 3, tm_proj)
    yd_p = _sb_prompt(qk_p, vtb_p, bsz, seq)
    yd_s = _to_token_major(_sb_paged(q_s, k_s.astype(BF16), v_s.astype(BF16),
                                     cache_k, cache_v, page_table, group))
    gb_full = jnp.repeat(gmlp_b.T, cpg, axis=1)
    hp = _odd_out_prompt(rest_p, yd_p, hp, w_out_bf, gmlp_w, gb_full, g_post, tm)
    w4 = jnp.repeat(gmlp_w[:, :nq, :nq].transpose(1, 2, 0), cpg, axis=2).reshape(nq * nq, w)
    b4 = jnp.repeat(gmlp_b[:, :nq].T, cpg, axis=1)
    hs = _odd_out_sample(rest_s, yd_s, hs, w_out_bf, w4, b4, g_post, nb, nq)
    open_start = ((seq - 1) // GMLP_CHUNK) * GMLP_CHUNK
    gv_p = rest_p[:, w:2 * w].reshape(bsz, seq, w)[:, open_start:, :]
    gv_s = _to_batch_major(rest_s[:, w:2 * w], nb)
    return (hp, hs, _heads_prompt(kt_p, heads), _heads_prompt(vt_p, heads),
            k_s.reshape(nb, nq, heads, HEAD_DIM), v_s.reshape(nb, nq, heads, HEAD_DIM), gv_p, gv_s)


def kernel(x_prompt, x_sample, state_conv, cache_k_moba, cache_v_moba, cache_k_sb, cache_v_sb,
           page_table, norm_pre_e, norm_post_e, w_in_e, conv_w, w_out_e,
           norm_pre_o, norm_post_o, w_in_o, gmlp_w, gmlp_b, w_out_o):
    bsz, seq, d = x_prompt.shape
    nb, nq, _ = x_sample.shape
    depth = norm_pre_e.shape[0] + norm_pre_o.shape[0]
    npages, page = page_table.shape[1], cache_k_moba.shape[2]
    assert seq % MOBA_BLOCK == 0 and (npages * page) % MOBA_BLOCK == 0 and nq <= GMLP_CHUNK
    assert (nq - 1) // MOBA_BLOCK == 0 and w_in_e.shape[2] == 8 * BRANCH_W and w_in_o.shape[2] == 7 * BRANCH_W
    dims = (bsz, seq, nb, nq)
    tm = min(512, seq)
    tm_proj = min(1024, seq)
    group_moba = min(16, npages)
    group_sb = min(8, npages)
    hp = x_prompt.reshape(bsz * seq, d)
    hs = _to_token_major(x_sample)
    ev, od = [], []
    for i in range(depth):
        j = i // 2
        if i % 2 == 0:
            hp, hs, *rest = _even_layer(
                hp, hs, dims, state_conv[j], cache_k_moba[j], cache_v_moba[j], page_table,
                norm_pre_e[j], norm_post_e[j], w_in_e[j], conv_w[j], w_out_e[j],
                (tm, tm_proj), group_moba)
            ev.append(rest)
        else:
            hp, hs, *rest = _odd_layer(
                hp, hs, dims, cache_k_sb[j], cache_v_sb[j], page_table,
                norm_pre_o[j], norm_post_o[j], w_in_o[j], gmlp_w[j], gmlp_b[j], w_out_o[j],
                (tm, tm_proj), group_sb)
            od.append(rest)
    y_prompt = hp.reshape(bsz, seq, d)
    y_sample = _to_batch_major(hs, nb)
    ev_out = [jnp.stack([r[k] for r in ev]) for k in range(6)]
    od_out = [jnp.stack([r[k] for r in od]) for k in range(6)]
    conv_p, conv_s, kmp, vmp, kms, vms = ev_out
    ksp, vsp, kss, vss, gvp, gvs = od_out
    return (y_prompt, y_sample, conv_p, conv_s, kmp, vmp, kms, vms, ksp, vsp, kss, vss, gvp, gvs)
```

```python
import functools

import jax
import jax.numpy as jnp
from jax import lax
from jax.experimental import pallas as pl
from jax.experimental.pallas import tpu as pltpu

F32 = jnp.float32
BF16 = jnp.bfloat16

HEAD_DIM = 64
LANES = 128
BRANCH_W = 512
MOBA_BLOCK = 256
MOBA_TOPK = 3
MOBA_UNROLL = 4
GMLP_CHUNK = 128
CONV_W = 3
RMS_EPS = 1e-6
NEG = -1e30
LOG2E = 1.4426950408889634
Q_SCALE = HEAD_DIM ** -0.5 * LOG2E
SB_TILE = 256
SB_DEAD = -160.0
VMEM_LIMIT = 56 * 1024 * 1024
PAGE_BUFFERS = 2

_NT = (((1,), (1,)), ((), ()))


def _cparams(sem):
    return pltpu.CompilerParams(dimension_semantics=sem, vmem_limit_bytes=VMEM_LIMIT)


def _rms(x, g):
    ms = jnp.mean(x * x, axis=-1, keepdims=True)
    return x * lax.rsqrt(ms + RMS_EPS) * g


def _silu(x):
    return x / (1.0 + jnp.exp(-x))


def _softplus2(z2):
    return jnp.maximum(z2, 0.0) + jnp.log2(1.0 + jnp.exp2(-jnp.abs(z2)))


def _split_bf16(x):
    hi = x.astype(BF16)
    lo = (x - hi.astype(F32)).astype(BF16)
    return hi, lo


def _tri(n):
    r = lax.broadcasted_iota(jnp.int32, (n, n), 0)
    c = lax.broadcasted_iota(jnp.int32, (n, n), 1)
    return jnp.where(r >= c, 1.0, 0.0).astype(BF16)


def _proj_kernel(x_ref, g_ref, w_ref, rest_ref, qk_ref, kt_ref, vt_ref, vtb_ref, xn_ref,
                 *, qkv_first, blk):
    tm = x_ref.shape[0]
    w = BRANCH_W
    xn_ref[...] = _rms(x_ref[...], g_ref[...]).astype(BF16)
    for j in range(w_ref.shape[1] // w):
        r = jnp.dot(xn_ref[...], w_ref[:, j * w:(j + 1) * w], preferred_element_type=F32)
        if j == qkv_first:
            qk_ref[:, 0:w] = (r * Q_SCALE).astype(BF16)
        elif j == qkv_first + 1:
            qk_ref[:, w:2 * w] = r.astype(BF16)
            for c in range(tm // blk):
                kt_ref[:, c * blk:(c + 1) * blk] = r[c * blk:(c + 1) * blk, :].T
        elif j == qkv_first + 2:
            for c in range(tm // blk):
                t = r[c * blk:(c + 1) * blk, :].T
                vt_ref[:, c * blk:(c + 1) * blk] = t
                vtb_ref[c] = t.astype(BF16)
        else:
            jr = j if j < qkv_first else j - 3
            rest_ref[:, jr * w:(jr + 1) * w] = r


def _proj(x, g, w_bf, qkv_first, bsz, seq, tm):
    m, d = x.shape
    n = w_bf.shape[1]
    tps = seq // tm
    blk = min(MOBA_BLOCK, tm)
    w = BRANCH_W
    return pl.pallas_call(
        functools.partial(_proj_kernel, qkv_first=qkv_first, blk=blk),
        grid=(bsz, tps),
        in_specs=[
            pl.BlockSpec((tm, d), lambda b, t: (b * tps + t, 0)),
            pl.BlockSpec((1, d), lambda b, t: (0, 0)),
            pl.BlockSpec((d, n), lambda b, t: (0, 0), pipeline_mode=pl.Buffered(1)),
        ],
        out_specs=[
            pl.BlockSpec((tm, n - 3 * w), lambda b, t: (b * tps + t, 0)),
            pl.BlockSpec((tm, 2 * w), lambda b, t: (b * tps + t, 0)),
            pl.BlockSpec((None, w, tm), lambda b, t: (b, 0, t)),
            pl.BlockSpec((None, w, tm), lambda b, t: (b, 0, t)),
            pl.BlockSpec((None, tm // blk, w, blk), lambda b, t: (b, t, 0, 0)),
        ],
        out_shape=[
            jax.ShapeDtypeStruct((m, n - 3 * w), F32),
            jax.ShapeDtypeStruct((m, 2 * w), BF16),
            jax.ShapeDtypeStruct((bsz, w, seq), F32),
            jax.ShapeDtypeStruct((bsz, w, seq), F32),
            jax.ShapeDtypeStruct((bsz, seq // blk, w, blk), BF16),
        ],
        scratch_shapes=[pltpu.VMEM((tm, d), BF16)],
        compiler_params=_cparams(("parallel", "parallel")),
    )(x, g.reshape(1, d), w_bf)


def _moba_prompt_kernel(q_ref, k_ref, vt_ref, o_ref, km_ref, *, nblk):
    i = pl.program_id(2)
    lane = lax.broadcasted_iota(jnp.int32, (1, LANES), 1)
    low = lane < HEAD_DIM
    nbp = km_ref.shape[0]
    blk = MOBA_BLOCK
    nheads = q_ref.shape[1] // HEAD_DIM

    @pl.when(i == 0)
    def _():
        km_ref[...] = jnp.zeros_like(km_ref)
        for j in range(nblk):
            rows = k_ref[j * blk:(j + 1) * blk, :].astype(F32)
            km_ref[j:j + 1, :] = jnp.sum(rows, axis=0, keepdims=True) * (1.0 / blk)

    brow = lax.broadcasted_iota(jnp.int32, (nbp, blk), 0)
    key_i = lax.broadcasted_iota(jnp.int32, (blk, blk), 0)
    qry_i = lax.broadcasted_iota(jnp.int32, (blk, blk), 1)
    start = pl.multiple_of(i * blk, blk)

    ones_rows = jnp.ones((2 * 8, blk), BF16)

    def v_rows(j, hh):
        return jnp.concatenate([vt_ref[j, hh * HEAD_DIM:(hh + 1) * HEAD_DIM, :], ones_rows], axis=0)

    q_owns, gates, own_scores = [], [], []
    for hh in range(nheads):
        cols = slice((hh // 2) * LANES, (hh // 2 + 1) * LANES)
        q = q_ref[:, cols]
        km = km_ref[:, cols]
        mine = low if hh % 2 == 0 else jnp.logical_not(low)
        km_hi, km_lo = _split_bf16(jnp.where(mine, km, 0.0))
        gates.append(lax.dot_general(km_hi, q, _NT, preferred_element_type=F32)
                     + lax.dot_general(km_lo, q, _NT, preferred_element_type=F32))
        q_own = jnp.where(mine, q, jnp.zeros_like(q))
        q_owns.append(q_own)
        own_scores.append(lax.dot_general(k_ref[pl.ds(start, blk), cols], q_own, _NT,
                                          preferred_element_type=F32))

    q_pasts, init = [], []
    for hh in range(nheads):
        gate = jnp.where(brow < i, gates[hh], -jnp.inf)
        bias = jnp.full((nbp, blk), NEG, F32)
        for _ in range(MOBA_TOPK):
            mx = jnp.max(gate, axis=0, keepdims=True)
            first = jnp.min(jnp.where(gate == mx, brow, nbp), axis=0, keepdims=True)
            pick = (brow == first) & (mx > -jnp.inf)
            bias = jnp.where(pick, 0.0, bias)
            gate = jnp.where(pick, -jnp.inf, gate)
        bias_t = jnp.concatenate([bias, jnp.zeros((LANES - nbp, blk), F32)], axis=0).T
        q_pasts.append(jnp.concatenate([q_owns[hh], bias_t.astype(BF16)], axis=1))

        s = jnp.where(key_i <= qry_i, own_scores[hh], NEG)
        m0 = jnp.max(s, axis=0, keepdims=True)
        p = jnp.exp2(s - m0)
        acc0 = jnp.dot(v_rows(i, hh), p.astype(BF16), preferred_element_type=F32)
        init.append((m0, acc0))

    def stage(js, carry):
        scores = []
        for j in js:
            st = pl.multiple_of(j * blk, blk)
            onehot = jnp.broadcast_to(jnp.where(lane == j, 1.0, 0.0).astype(BF16), (blk, LANES))
            for hh in range(nheads):
                kj = k_ref[pl.ds(st, blk), (hh // 2) * LANES:(hh // 2 + 1) * LANES]
                scores.append(lax.dot_general(jnp.concatenate([kj, onehot], axis=1), q_pasts[hh],
                                              _NT, preferred_element_type=F32))
        carry = list(carry)
        for n, j in enumerate(js):
            for hh in range(nheads):
                m, acc = carry[hh]
                sj = scores[n * nheads + hh]
                m_new = jnp.maximum(m, jnp.max(sj, axis=0, keepdims=True))
                alpha = jnp.exp2(m - m_new)
                pj = jnp.exp2(sj - m_new)
                acc = alpha * acc + jnp.dot(v_rows(j, hh), pj.astype(BF16),
                                            preferred_element_type=F32)
                carry[hh] = (m_new, acc)
        return tuple(carry)

    fin = tuple(init)
    base = 0
    size = MOBA_UNROLL
    while size >= 1:
        trips = (i - base) // size if size == MOBA_UNROLL else ((i - base) // size) % 2
        fin = lax.fori_loop(
            0, trips,
            lambda t, c, base=base, size=size: stage([base + t * size + u for u in range(size)], c),
            fin)
        base = base + trips * size
        size //= 2
    out_t = jnp.concatenate([acc[:HEAD_DIM] / acc[HEAD_DIM:HEAD_DIM + 1] for _, acc in fin],
                            axis=0)
    o_ref[...] = out_t.T


def _attn_prompt_call(kernel_fn, qk, vtb, bsz, seq, scratch, wid):
    nblk = seq // MOBA_BLOCK
    ngrp = BRANCH_W // wid
    return pl.pallas_call(
        kernel_fn,
        grid=(bsz, ngrp, nblk),
        in_specs=[
            pl.BlockSpec((MOBA_BLOCK, wid), lambda b, p, i: (b * nblk + i, p)),
            pl.BlockSpec((seq, wid), lambda b, p, i: (b, ngrp + p)),
            pl.BlockSpec((None, nblk, wid, MOBA_BLOCK), lambda b, p, i: (b, 0, p, 0)),
        ],
        out_specs=pl.BlockSpec((MOBA_BLOCK, wid), lambda b, p, i: (b * nblk + i, p)),
        out_shape=jax.ShapeDtypeStruct((bsz * seq, BRANCH_W), F32),
        scratch_shapes=scratch,
        compiler_params=_cparams(("parallel", "parallel", "arbitrary")),
    )(qk, qk, vtb)


MOBA_STEP_W = 512


def _moba_prompt(qk, vtb, bsz, seq):
    nblk = seq // MOBA_BLOCK
    nbp = -(-nblk // 8) * 8
    return _attn_prompt_call(functools.partial(_moba_prompt_kernel, nblk=nblk), qk, vtb, bsz, seq,
                             [pltpu.VMEM((nbp, MOBA_STEP_W), F32)], MOBA_STEP_W)


def _sb_prompt_kernel(q_ref, k_ref, vt_ref, o_ref):
    i = pl.program_id(2)
    nheads = q_ref.shape[1] // HEAD_DIM
    tile = SB_TILE
    lane = lax.broadcasted_iota(jnp.int32, (1, LANES), 1)
    low = lane < HEAD_DIM
    key_i = lax.broadcasted_iota(jnp.int32, (tile, tile), 0)
    qry_i = lax.broadcasted_iota(jnp.int32, (tile, tile), 1)
    tri_t = jnp.where(qry_i >= key_i, 1.0, 0.0).astype(BF16)
    strict = key_i < qry_i

    q_hs = []
    for hh in range(nheads):
        q = q_ref[:, (hh // 2) * LANES:(hh // 2 + 1) * LANES]
        mine = low if hh % 2 == 0 else jnp.logical_not(low)
        q_hs.append(jnp.where(mine, q, jnp.zeros_like(q)))

    def sweep(j, state, diag):
        st = pl.multiple_of(j * tile, tile)
        zs = [lax.dot_general(k_ref[pl.ds(st, tile), (hh // 2) * LANES:(hh // 2 + 1) * LANES],
                              q_hs[hh], _NT, preferred_element_type=F32)
              for hh in range(nheads)]
        cums = []
        for hh in range(nheads):
            lk = -_softplus2(zs[hh])
            if diag:
                lk = jnp.where(strict, lk, 0.0)
            hi, lo = _split_bf16(lk)
            cums.append(jnp.dot(tri_t, hi, preferred_element_type=F32)
                        + jnp.dot(tri_t, lo, preferred_element_type=F32) + state[hh][0])
        new = []
        for hh in range(nheads):
            w = jnp.exp2(zs[hh] + cums[hh])
            if diag:
                w = jnp.where(strict, w, 0.0)
            acc = state[hh][1] + jnp.dot(vt_ref[j, hh * HEAD_DIM:(hh + 1) * HEAD_DIM, :],
                                         w.astype(BF16), preferred_element_type=F32)
            new.append((cums[hh][0:1, :], acc))
        return tuple(new)

    def live_of(state):
        live = jnp.max(state[0][0])
        for hh in range(1, nheads):
            live = jnp.maximum(live, jnp.max(state[hh][0]))
        return live

    def cond(st):
        j, live, _ = st
        return (j >= 0) & (live > SB_DEAD)

    def body(st):
        j, _, state = st
        new = sweep(j, state, False)
        return j - 1, live_of(new), new

    zero = tuple((jnp.zeros((1, tile), F32), jnp.zeros((HEAD_DIM, tile), F32))
                 for _ in range(nheads))
    init = sweep(i, zero, True)
    _, _, fin = lax.while_loop(cond, body, (i - 1, live_of(init), init))
    o_ref[...] = jnp.concatenate([acc for _, acc in fin], axis=0).T


SB_STEP_W = 512


def _sb_prompt(qk, vtb, bsz, seq):
    return _attn_prompt_call(_sb_prompt_kernel, qk, vtb, bsz, seq, [], SB_STEP_W)


def _query_rows(qrep_ref, nheads):
    nrow, width = qrep_ref.shape[1], qrep_ref.shape[2]
    row_w = lax.broadcasted_iota(jnp.int32, (nrow, width), 0)
    lane_w = lax.broadcasted_iota(jnp.int32, (nrow, width), 1)
    head_mask = (lane_w // HEAD_DIM) == (row_w % nheads)
    qrows = jnp.where(head_mask, qrep_ref[0], jnp.zeros_like(qrep_ref[0]))
    return qrows, head_mask


def _pick_heads(y, head_mask, nheads):
    nrow, width = y.shape
    y = jnp.where(head_mask, y, 0.0)
    return jnp.sum(y.reshape(nrow // nheads, nheads, width), axis=1)


def _moba_paged_kernel(pt_ref, qrep_ref, knew_ref, vnew_ref, *rest, group, nchunk, nheads):
    del pt_ref
    k_refs = rest[:group]
    v_refs = rest[group:2 * group]
    o_ref = rest[2 * group]
    s_ref, p_ref, pn_ref, l_ref, acc_ref = rest[2 * group + 1:]
    s = pl.program_id(1)
    nrow = qrep_ref.shape[1]
    page = k_refs[0].shape[1]
    cw = group * page
    qrows, head_mask = _query_rows(qrep_ref, nheads)
    row = lax.broadcasted_iota(jnp.int32, (nrow, LANES), 0)
    lane = lax.broadcasted_iota(jnp.int32, (nrow, LANES), 1)
    qidx = row // nheads

    @pl.when(s < nchunk)
    def _():
        for g in range(group):
            s_ref[s, :, g * page:(g + 1) * page] = jnp.dot(
                qrows, k_refs[g][...].astype(BF16), preferred_element_type=F32)

    @pl.when(s == nchunk - 1)
    def _():
        zn = lax.dot_general(qrows, knew_ref[0], _NT, preferred_element_type=F32)
        bpc = cw // MOBA_BLOCK
        nblk = nchunk * bpc
        gates = jnp.zeros((nrow, LANES), F32)
        for c in range(nchunk):
            for bl in range(bpc):
                sb = s_ref[c, :, bl * MOBA_BLOCK:(bl + 1) * MOBA_BLOCK]
                gates = jnp.where(lane == c * bpc + bl,
                                  jnp.sum(sb, axis=1, keepdims=True), gates)
        gates = jnp.where(lane < nblk, gates, -jnp.inf)
        sel = jnp.zeros((nrow, LANES), F32)
        for _ in range(MOBA_TOPK):
            mx = jnp.max(gates, axis=1, keepdims=True)
            first = jnp.min(jnp.where(gates == mx, lane, LANES), axis=1, keepdims=True)
            pick = (lane == first) & (mx > -jnp.inf)
            sel = jnp.where(pick, 1.0, sel)
            gates = jnp.where(pick, -jnp.inf, gates)
        zn = jnp.where(lane <= qidx, zn, NEG)
        m = jnp.max(zn, axis=1, keepdims=True)
        cols = []
        for c in range(nchunk):
            for bl in range(bpc):
                col = jnp.max(jnp.where(lane == c * bpc + bl, sel, 0.0),
                              axis=1, keepdims=True) > 0.5
                cols.append(col)
                sb = s_ref[c, :, bl * MOBA_BLOCK:(bl + 1) * MOBA_BLOCK]
                m = jnp.maximum(m, jnp.max(jnp.where(col, sb, NEG), axis=1, keepdims=True))
        pn = jnp.exp2(zn - m)
        l = jnp.sum(pn, axis=1, keepdims=True)
        pn_ref[...] = pn
        for c in range(nchunk):
            for bl in range(bpc):
                sb = s_ref[c, :, bl * MOBA_BLOCK:(bl + 1) * MOBA_BLOCK]
                p = jnp.exp2(jnp.where(cols[c * bpc + bl], sb, NEG) - m)
                l = l + jnp.sum(p, axis=1, keepdims=True)
                p_ref[c, :, bl * MOBA_BLOCK:(bl + 1) * MOBA_BLOCK] = p.astype(BF16)
        l_ref[...] = jnp.broadcast_to(l, l_ref.shape)

    @pl.when(s == nchunk)
    def _():
        acc_ref[...] = jnp.dot(pn_ref[...].astype(BF16), vnew_ref[0], preferred_element_type=F32)

    @pl.when(s >= nchunk)
    def _():
        c = s - nchunk
        acc = acc_ref[...]
        for g in range(group):
            acc = acc + lax.dot_general(p_ref[c, :, g * page:(g + 1) * page],
                                        v_refs[g][...].astype(BF16), _NT,
                                        preferred_element_type=F32)
        acc_ref[...] = acc

    @pl.when(s == 2 * nchunk - 1)
    def _():
        o_ref[0] = _pick_heads(acc_ref[...] / l_ref[:, 0:1], head_mask, nheads)


def _paged_operands(q_bf, knew_bf, vnew_bf, cache_k, cache_v, page_table):
    nb, nq, width = q_bf.shape
    nheads = width // HEAD_DIM
    nphys, page = cache_k.shape[0], cache_k.shape[1]
    ck = cache_k.transpose(0, 2, 3, 1).reshape(nphys, width, page)
    cv = cache_v.transpose(0, 2, 3, 1).reshape(nphys, width, page)
    qrep = jnp.repeat(q_bf, nheads, axis=1)
    pad = ((0, 0), (0, LANES - nq), (0, 0))
    knew = jnp.pad(knew_bf, pad)
    vnew = jnp.pad(vnew_bf, pad)
    pt = page_table.reshape(-1).astype(jnp.int32)
    return qrep, knew, vnew, ck, cv, pt, nheads, page


def _moba_paged(q_bf, knew_bf, vnew_bf, cache_k, cache_v, page_table, group):
    nb, nq, width = q_bf.shape
    qrep, knew, vnew, ck, cv, pt, nheads, page = _paged_operands(
        q_bf, knew_bf, vnew_bf, cache_k, cache_v, page_table)
    nrow = nq * nheads
    npages = page_table.shape[1]
    nchunk = npages // group

    def k_map(g):
        return lambda b, s, pt: (pt[b * npages + jnp.minimum(s, nchunk - 1) * group + g], 0, 0)

    def v_map(g):
        return lambda b, s, pt: (pt[b * npages + jnp.maximum(s - nchunk, 0) * group + g], 0, 0)

    per_b = lambda b, s, pt: (b, 0, 0)
    in_specs = [
        pl.BlockSpec((1, nrow, width), per_b),
        pl.BlockSpec((1, LANES, width), per_b),
        pl.BlockSpec((1, LANES, width), per_b),
    ]
    deep = pl.Buffered(PAGE_BUFFERS)
    in_specs += [pl.BlockSpec((None, width, page), k_map(g), pipeline_mode=deep) for g in range(group)]
    in_specs += [pl.BlockSpec((None, width, page), v_map(g), pipeline_mode=deep) for g in range(group)]
    cw = group * page
    grid_spec = pltpu.PrefetchScalarGridSpec(
        num_scalar_prefetch=1,
        grid=(nb, 2 * nchunk),
        in_specs=in_specs,
        out_specs=pl.BlockSpec((1, nq, width), per_b),
        scratch_shapes=[
            pltpu.VMEM((nchunk, nrow, cw), F32),
            pltpu.VMEM((nchunk, nrow, cw), BF16),
            pltpu.VMEM((nrow, LANES), F32),
            pltpu.VMEM((nrow, LANES), F32),
            pltpu.VMEM((nrow, width), F32),
        ],
    )
    return pl.pallas_call(
        functools.partial(_moba_paged_kernel, group=group, nchunk=nchunk, nheads=nheads),
        grid_spec=grid_spec,
        out_shape=jax.ShapeDtypeStruct((nb, nq, width), F32),
        compiler_params=_cparams(("parallel", "arbitrary")),
    )(pt, qrep, knew, vnew, *([ck] * group), *([cv] * group))


def _sb_paged_kernel(pt_ref, live_ref, qrep_ref, knew_ref, vnew_ref, cin_ref, ain_ref, *rest,
                     group, nheads, first):
    del pt_ref
    k_refs = rest[:group]
    v_refs = rest[group:2 * group]
    y_ref, acc_ref, carry_ref = rest[2 * group:2 * group + 3]
    b = pl.program_id(0)
    s = pl.program_id(1)
    nrow = qrep_ref.shape[1]
    page = k_refs[0].shape[1]
    ppt = SB_TILE // page
    qrows, head_mask = _query_rows(qrep_ref, nheads)
    tri = _tri(SB_TILE)

    @pl.when(s == 0)
    def _():
        if first:
            row = lax.broadcasted_iota(jnp.int32, (nrow, LANES), 0)
            lane = lax.broadcasted_iota(jnp.int32, (nrow, LANES), 1)
            strict = lane < row // nheads
            zn = lax.dot_general(qrows, knew_ref[0], _NT, preferred_element_type=F32)
            hi, lo = _split_bf16(jnp.where(strict, -_softplus2(zn), 0.0))
            tri_n = _tri(LANES)
            cum = (jnp.dot(hi, tri_n, preferred_element_type=F32)
                   + jnp.dot(lo, tri_n, preferred_element_type=F32))
            wn = jnp.where(strict, jnp.exp2(zn + cum), 0.0)
            acc_ref[0] = jnp.dot(wn.astype(BF16), vnew_ref[0], preferred_element_type=F32)
            carry_ref[0] = jnp.broadcast_to(cum[:, 0:1], carry_ref.shape[1:])
        else:
            acc_ref[0] = ain_ref[0]
            carry_ref[0] = cin_ref[0]

    @pl.when(live_ref[b] > 0)
    def _():
        carry = carry_ref[0][:, 0:1]
        acc = acc_ref[0]
        for t in reversed(range(group // ppt)):
            kt = jnp.concatenate([k_refs[t * ppt + u][...] for u in range(ppt)], axis=1).astype(BF16)
            vt = jnp.concatenate([v_refs[t * ppt + u][...] for u in range(ppt)], axis=1).astype(BF16)
            z = jnp.dot(qrows, kt, preferred_element_type=F32)
            hi, lo = _split_bf16(-_softplus2(z))
            cum = (jnp.dot(hi, tri, preferred_element_type=F32)
                   + jnp.dot(lo, tri, preferred_element_type=F32)) + carry
            w = jnp.exp2(z + cum).astype(BF16)
            acc = acc + lax.dot_general(w, vt, _NT, preferred_element_type=F32)
            carry = cum[:, 0:1]
        acc_ref[0] = acc
        carry_ref[0] = jnp.broadcast_to(carry, carry_ref.shape[1:])

    @pl.when(s == pl.num_programs(1) - 1)
    def _():
        y_ref[0] = _pick_heads(acc_ref[0], head_mask, nheads)


def _sb_paged_call(first, pt, live, qrep, knew, vnew, carry_in, acc_in, ck, cv,
                   nq, npages, group, chunk_lo, nsteps, nheads):
    nb, nrow, width = qrep.shape
    page = ck.shape[2]

    def page_map(g):
        def index(b, s, pt, live):
            chunk_page = (chunk_lo + nsteps - 1 - s) * group + g
            return (jnp.where(live[b] > 0, pt[b * npages + chunk_page], 0), 0, 0)
        return index

    per_b = lambda b, s, pt, live: (b, 0, 0)
    in_specs = [
        pl.BlockSpec((1, nrow, width), per_b),
        pl.BlockSpec((1, LANES, width), per_b),
        pl.BlockSpec((1, LANES, width), per_b),
        pl.BlockSpec((1, nrow, LANES), per_b),
        pl.BlockSpec((1, nrow, width), per_b),
    ]
    in_specs += [pl.BlockSpec((None, width, page), page_map(g), pipeline_mode=pl.Buffered(PAGE_BUFFERS))
                 for g in range(group)] * 2
    grid_spec = pltpu.PrefetchScalarGridSpec(
        num_scalar_prefetch=2,
        grid=(nb, nsteps),
        in_specs=in_specs,
        out_specs=[pl.BlockSpec((1, nq, width), per_b),
                   pl.BlockSpec((1, nrow, width), per_b),
                   pl.BlockSpec((1, nrow, LANES), per_b)],
    )
    return pl.pallas_call(
        functools.partial(_sb_paged_kernel, group=group, nheads=nheads, first=first),
        grid_spec=grid_spec,
        out_shape=[jax.ShapeDtypeStruct((nb, nq, width), F32),
                   jax.ShapeDtypeStruct((nb, nrow, width), F32),
                   jax.ShapeDtypeStruct((nb, nrow, LANES), F32)],
        compiler_params=_cparams(("parallel", "arbitrary")),
    )(pt, live, qrep, knew, vnew, carry_in, acc_in, *([ck] * group), *([cv] * group))


def _sb_paged(q_bf, knew_bf, vnew_bf, cache_k, cache_v, page_table, group):
    nb, nq, width = q_bf.shape
    qrep, knew, vnew, ck, cv, pt, nheads, page = _paged_operands(
        q_bf, knew_bf, vnew_bf, cache_k, cache_v, page_table)
    nrow = nq * nheads
    npages = page_table.shape[1]
    nchunk = npages // group
    zc = jnp.zeros((nb, nrow, LANES), F32)
    za = jnp.zeros((nb, nrow, width), F32)
    all_live = jnp.ones((nb,), jnp.int32)
    y, acc, carry = _sb_paged_call(True, pt, all_live, qrep, knew, vnew, zc, za, ck, cv,
                                   nq, npages, group, nchunk - 1, 1, nheads)
    if nchunk == 1:
        return y
    live = (jnp.max(carry, axis=(1, 2)) > SB_DEAD).astype(jnp.int32)

    def older(_):
        return _sb_paged_call(False, pt, live, qrep, knew, vnew, carry, acc, ck, cv,
                              nq, npages, group, 0, nchunk - 1, nheads)[0]

    return lax.cond(jnp.any(live > 0), older, lambda _: y, None)


def _finish(y_first, y_second, w_ref, g_ref, x_ref, o_ref):
    half = y_first.shape[1]
    out = (jnp.dot(y_first.astype(BF16), w_ref[0:half, :], preferred_element_type=F32)
           + jnp.dot(y_second.astype(BF16), w_ref[half:2 * half, :], preferred_element_type=F32))
    o_ref[...] = x_ref[...] + _rms(out, g_ref[...])


def _even_out_prompt_kernel(ab_ref, ac_ref, ah_ref, az_ref, bz_ref, hc_ref, hh_ref, yb_ref, x_ref,
                            w_ref, cw_ref, g_ref, o_ref, tail_ref, ue_ref, *, tiles_per_seq):
    i = pl.program_id(0)
    tm = ab_ref.shape[0]
    u = ac_ref[...] * ah_ref[...]
    first = (i % tiles_per_seq) == 0
    ue_ref[0:8, :] = jnp.where(first, 0.0, hc_ref[...] * hh_ref[...])
    ue_ref[8:8 + tm, :] = u
    conv = (ue_ref[6:6 + tm, :] * cw_ref[0:1, :] + ue_ref[7:7 + tm, :] * cw_ref[1:2, :]
            + u * cw_ref[2:3, :])
    y_a = ab_ref[...] * conv * _silu(az_ref[...])
    y_b = yb_ref[...] * _silu(bz_ref[...])
    tail_ref[...] = u[tm - 8:tm, :]
    _finish(y_a, y_b, w_ref, g_ref, x_ref, o_ref)


def _even_out_prompt(rest, y_b, x, w_out_bf, conv_w, g_post, seq, tm):
    m, d = x.shape
    nt = m // tm
    col = lambda c: pl.BlockSpec((tm, BRANCH_W), lambda i, c=c: (i, c))
    halo = lambda c: pl.BlockSpec((8, BRANCH_W),
                                  lambda i, c=c: (jnp.maximum(i * (tm // 8) - 1, 0), c))
    return pl.pallas_call(
        functools.partial(_even_out_prompt_kernel, tiles_per_seq=seq // tm),
        grid=(nt,),
        in_specs=[col(0), col(1), col(2), col(3), col(4), halo(1), halo(2),
                  pl.BlockSpec((tm, BRANCH_W), lambda i: (i, 0)),
                  pl.BlockSpec((tm, d), lambda i: (i, 0)),
                  pl.BlockSpec(w_out_bf.shape, lambda i: (0, 0)),
                  pl.BlockSpec(conv_w.shape, lambda i: (0, 0)),
                  pl.BlockSpec((1, d), lambda i: (0, 0))],
        out_specs=[pl.BlockSpec((tm, d), lambda i: (i, 0)),
                   pl.BlockSpec((8, BRANCH_W), lambda i: (i, 0))],
        out_shape=[jax.ShapeDtypeStruct((m, d), F32),
                   jax.ShapeDtypeStruct((nt * 8, BRANCH_W), F32)],
        scratch_shapes=[pltpu.VMEM((tm + 8, BRANCH_W), F32)],
        compiler_params=_cparams(("parallel",)),
    )(rest, rest, rest, rest, rest, rest, rest, y_b, x, w_out_bf, conv_w, g_post.reshape(1, d))


def _even_out_sample_kernel(rest_ref, st_ref, yb_ref, x_ref, w_ref, cw_ref, g_ref,
                            o_ref, tail_ref, *, nb):
    w = BRANCH_W
    rows = rest_ref.shape[0]
    u = rest_ref[:, w:2 * w] * rest_ref[:, 2 * w:3 * w]
    ue = jnp.concatenate([st_ref[...], u], axis=0)
    conv = (ue[0:rows] * cw_ref[0:1, :] + ue[nb:nb + rows] * cw_ref[1:2, :]
            + ue[2 * nb:2 * nb + rows] * cw_ref[2:3, :])
    y_a = rest_ref[:, 0:w] * conv * _silu(rest_ref[:, 3 * w:4 * w])
    y_b = yb_ref[...] * _silu(rest_ref[:, 4 * w:5 * w])
    tail_ref[...] = ue[rows:rows + 2 * nb]
    _finish(y_a, y_b, w_ref, g_ref, x_ref, o_ref)


def _even_out_sample(rest, state_tm, y_b, x, w_out_bf, conv_w, g_post, nb):
    m, d = x.shape
    return pl.pallas_call(
        functools.partial(_even_out_sample_kernel, nb=nb),
        out_shape=[jax.ShapeDtypeStruct((m, d), F32),
                   jax.ShapeDtypeStruct(((CONV_W - 1) * nb, BRANCH_W), F32)],
        compiler_params=pltpu.CompilerParams(vmem_limit_bytes=VMEM_LIMIT),
    )(rest, state_tm, y_b, x, w_out_bf, conv_w, g_post.reshape(1, d))


def _gmlp_mix(v, gw_ref, low):
    r_i = lax.broadcasted_iota(jnp.int32, (GMLP_CHUNK, GMLP_CHUNK), 0)
    c_i = lax.broadcasted_iota(jnp.int32, (GMLP_CHUNK, GMLP_CHUNK), 1)
    tril = c_i <= r_i
    parts = []
    for p in range(v.shape[1] // LANES):
        vp = v[:, p * LANES:(p + 1) * LANES]
        v_lo = jnp.where(low, vp, 0.0).astype(BF16)
        v_hi = jnp.where(low, 0.0, vp).astype(BF16)
        w_lo = jnp.where(tril, gw_ref[2 * p], 0.0).astype(BF16)
        w_hi = jnp.where(tril, gw_ref[2 * p + 1], 0.0).astype(BF16)
        parts.append(jnp.dot(w_lo, v_lo, preferred_element_type=F32)
                     + jnp.dot(w_hi, v_hi, preferred_element_type=F32))
    return jnp.concatenate(parts, axis=1)


def _odd_out_prompt_kernel(cu_ref, cv_ref, cz_ref, dz_ref, yd_ref, x_ref, w_ref, gw_ref, gb_ref,
                           g_ref, o_ref, yc_ref):
    tm = cu_ref.shape[0]
    low = lax.broadcasted_iota(jnp.int32, (1, LANES), 1) < HEAD_DIM
    for c in range(tm // GMLP_CHUNK):
        rows = slice(c * GMLP_CHUNK, (c + 1) * GMLP_CHUNK)
        mixed = _gmlp_mix(cv_ref[rows, :], gw_ref, low) + gb_ref[...]
        yc_ref[rows, :] = cu_ref[rows, :] * mixed * _silu(cz_ref[rows, :])
    y_d = yd_ref[...] * _silu(dz_ref[...])
    _finish(yc_ref[...], y_d, w_ref, g_ref, x_ref, o_ref)


def _odd_out_prompt(rest, y_d, x, w_out_bf, gmlp_w, gb_full, g_post, tm):
    m, d = x.shape
    col = lambda c: pl.BlockSpec((tm, BRANCH_W), lambda i, c=c: (i, c))
    return pl.pallas_call(
        _odd_out_prompt_kernel,
        grid=(m // tm,),
        in_specs=[col(0), col(1), col(2), col(3),
                  pl.BlockSpec((tm, BRANCH_W), lambda i: (i, 0)),
                  pl.BlockSpec((tm, d), lambda i: (i, 0)),
                  pl.BlockSpec(w_out_bf.shape, lambda i: (0, 0)),
                  pl.BlockSpec(gmlp_w.shape, lambda i: (0, 0, 0)),
                  pl.BlockSpec(gb_full.shape, lambda i: (0, 0)),
                  pl.BlockSpec((1, d), lambda i: (0, 0))],
        out_specs=pl.BlockSpec((tm, d), lambda i: (i, 0)),
        out_shape=jax.ShapeDtypeStruct((m, d), F32),
        scratch_shapes=[pltpu.VMEM((tm, BRANCH_W), F32)],
        compiler_params=_cparams(("parallel",)),
    )(rest, rest, rest, rest, y_d, x, w_out_bf, gmlp_w, gb_full, g_post.reshape(1, d))


def _odd_out_sample_kernel(rest_ref, yd_ref, x_ref, w_ref, w4_ref, b4_ref, g_ref, o_ref, *, nb, nq):
    w = BRANCH_W
    parts = []
    for t in range(nq):
        mixed = jnp.broadcast_to(b4_ref[t:t + 1, :], (nb, w))
        for s in range(t + 1):
            mixed = mixed + w4_ref[t * nq + s:t * nq + s + 1, :] * rest_ref[s * nb:(s + 1) * nb, w:2 * w]
        parts.append(mixed)
    mixed = jnp.concatenate(parts, axis=0)
    y_c = rest_ref[:, 0:w] * mixed * _silu(rest_ref[:, 2 * w:3 * w])
    y_d = yd_ref[...] * _silu(rest_ref[:, 3 * w:4 * w])
    _finish(y_c, y_d, w_ref, g_ref, x_ref, o_ref)


def _odd_out_sample(rest, y_d, x, w_out_bf, w4, b4, g_post, nb, nq):
    m, d = x.shape
    return pl.pallas_call(
        functools.partial(_odd_out_sample_kernel, nb=nb, nq=nq),
        out_shape=jax.ShapeDtypeStruct((m, d), F32),
        compiler_params=pltpu.CompilerParams(vmem_limit_bytes=VMEM_LIMIT),
    )(rest, y_d, x, w_out_bf, w4, b4, g_post.reshape(1, d))


def _to_token_major(a):
    nb, nq, w = a.shape
    return a.transpose(1, 0, 2).reshape(nq * nb, w)


def _to_batch_major(a, nb):
    w = a.shape[1]
    return a.reshape(-1, nb, w).transpose(1, 0, 2)


def _heads_prompt(t, heads):
    bsz, w, seq = t.shape
    return t.reshape(bsz, heads, w // heads, seq).transpose(0, 3, 1, 2)


def _project_both(hp, hs, dims, g_pre, w_in, qkv_first, tm):
    bsz, seq, nb, nq = dims
    w_in_bf = w_in.astype(BF16)
    prompt = _proj(hp, g_pre, w_in_bf, qkv_first, bsz, seq, tm)
    ns = nq * nb
    rows = max(ns, tm)
    hs_pad = jnp.pad(hs, ((0, rows - ns), (0, 0)))
    rest_s, qk_s, kt_s, vt_s, _ = _proj(hs_pad, g_pre, w_in_bf, qkv_first, 1, rows, rows)
    rest_s = rest_s[:ns]
    k_s = _to_batch_major(kt_s[0, :, :ns].T, nb)
    v_s = _to_batch_major(vt_s[0, :, :ns].T, nb)
    q_s = _to_batch_major(qk_s[:ns, :BRANCH_W], nb)
    return prompt, rest_s, q_s, k_s, v_s


def _even_layer(hp, hs, dims, state_conv, cache_k, cache_v, page_table,
                g_pre, g_post, w_in, conv_w, w_out, tiles, group):
    bsz, seq, nb, nq = dims
    tm, tm_proj = tiles
    w = BRANCH_W
    heads = w // HEAD_DIM
    w_out_bf = w_out.astype(BF16)
    (rest_p, qk_p, kt_p, vt_p, vtb_p), rest_s, q_s, k_s, v_s = _project_both(
        hp, hs, dims, g_pre, w_in, 4, tm_proj)
    yb_p = _moba_prompt(qk_p, vtb_p, bsz, seq)
    yb_s = _to_token_major(_moba_paged(q_s, k_s.astype(BF16), v_s.astype(BF16),
                                       cache_k, cache_v, page_table, group))
    hp, tail_p = _even_out_prompt(rest_p, yb_p, hp, w_out_bf, conv_w, g_post, seq, tm)
    state_tm = _to_token_major(state_conv)
    hs, tail_s = _even_out_sample(rest_s, state_tm, yb_s, hs, w_out_bf, conv_w, g_post, nb)
    conv_p = tail_p.reshape(bsz, seq // tm, 8, w)[:, -1, 8 - (CONV_W - 1):, :]
    conv_s = _to_batch_major(tail_s, nb)
    return (hp, hs, conv_p, conv_s, _heads_prompt(kt_p, heads), _heads_prompt(vt_p, heads),
            k_s.reshape(nb, nq, heads, HEAD_DIM), v_s.reshape(nb, nq, heads, HEAD_DIM))


def _odd_layer(hp, hs, dims, cache_k, cache_v, page_table,
               g_pre, g_post, w_in, gmlp_w, gmlp_b, w_out, tiles, group):
    bsz, seq, nb, nq = dims
    tm, tm_proj = tiles
    w = BRANCH_W
    heads = w // HEAD_DIM
    w_out_bf = w_out.astype(BF16)
    ngroups = gmlp_w.shape[0]
    cpg = w // ngroups
    (rest_p, qk_p, kt_p, vt_p, vtb_p), rest_s, q_s, k_s, v_s = _project_both(
        hp, hs, dims, g_pre, w_in, 3, tm_proj)
    yd_p = _sb_prompt(qk_p, vtb_p, bsz, seq)
    yd_s = _to_token_major(_sb_paged(q_s, k_s.astype(BF16), v_s.astype(BF16),
                                     cache_k, cache_v, page_table, group))
    gb_full = jnp.repeat(gmlp_b.T, cpg, axis=1)
    hp = _odd_out_prompt(rest_p, yd_p, hp, w_out_bf, gmlp_w, gb_full, g_post, tm)
    w4 = jnp.repeat(gmlp_w[:, :nq, :nq].transpose(1, 2, 0), cpg, axis=2).reshape(nq * nq, w)
    b4 = jnp.repeat(gmlp_b[:, :nq].T, cpg, axis=1)
    hs = _odd_out_sample(rest_s, yd_s, hs, w_out_bf, w4, b4, g_post, nb, nq)
    open_start = ((seq - 1) // GMLP_CHUNK) * GMLP_CHUNK
    gv_p = rest_p[:, w:2 * w].reshape(bsz, seq, w)[:, open_start:, :]
    gv_s = _to_batch_major(rest_s[:, w:2 * w], nb)
    return (hp, hs, _heads_prompt(kt_p, heads), _heads_prompt(vt_p, heads),
            k_s.reshape(nb, nq, heads, HEAD_DIM), v_s.reshape(nb, nq, heads, HEAD_DIM), gv_p, gv_s)


def kernel(x_prompt, x_sample, state_conv, cache_k_moba, cache_v_moba, cache_k_sb, cache_v_sb,
           page_table, norm_pre_e, norm_post_e, w_in_e, conv_w, w_out_e,
           norm_pre_o, norm_post_o, w_in_o, gmlp_w, gmlp_b, w_out_o):
    bsz, seq, d = x_prompt.shape
    nb, nq, _ = x_sample.shape
    depth = norm_pre_e.shape[0] + norm_pre_o.shape[0]
    npages, page = page_table.shape[1], cache_k_moba.shape[2]
    assert seq % MOBA_BLOCK == 0 and (npages * page) % MOBA_BLOCK == 0 and nq <= GMLP_CHUNK
    assert (nq - 1) // MOBA_BLOCK == 0 and w_in_e.shape[2] == 8 * BRANCH_W and w_in_o.shape[2] == 7 * BRANCH_W
    dims = (bsz, seq, nb, nq)
    tm = min(512, seq)
    tm_proj = min(512, seq)
    group_moba = min(32, npages)
    group_sb = min(4, npages)
    hp = x_prompt.reshape(bsz * seq, d)
    hs = _to_token_major(x_sample)
    ev, od = [], []
    for i in range(depth):
        j = i // 2
        if i % 2 == 0:
            hp, hs, *rest = _even_layer(
                hp, hs, dims, state_conv[j], cache_k_moba[j], cache_v_moba[j], page_table,
                norm_pre_e[j], norm_post_e[j], w_in_e[j], conv_w[j], w_out_e[j],
                (tm, tm_proj), group_moba)
            ev.append(rest)
        else:
            hp, hs, *rest = _odd_layer(
                hp, hs, dims, cache_k_sb[j], cache_v_sb[j], page_table,
                norm_pre_o[j], norm_post_o[j], w_in_o[j], gmlp_w[j], gmlp_b[j], w_out_o[j],
                (tm, tm_proj), group_sb)
            od.append(rest)
    y_prompt = hp.reshape(bsz, seq, d)
    y_sample = _to_batch_major(hs, nb)
    ev_out = [jnp.stack([r[k] for r in ev]) for k in range(6)]
    od_out = [jnp.stack([r[k] for r in od]) for k in range(6)]
    conv_p, conv_s, kmp, vmp, kms, vms = ev_out
    ksp, vsp, kss, vss, gvp, gvs = od_out
    return (y_prompt, y_sample, conv_p, conv_s, kmp, vmp, kms, vms, ksp, vsp, kss, vss, gvp, gvs)
```

```python
import functools

import jax
import jax.numpy as jnp
from jax import lax
from jax.experimental import pallas as pl
from jax.experimental.pallas import tpu as pltpu

F32 = jnp.float32
BF16 = jnp.bfloat16

HEAD_DIM = 64
LANES = 128
BRANCH_W = 512
MOBA_BLOCK = 256
MOBA_TOPK = 3
MOBA_UNROLL = 4
GMLP_CHUNK = 128
CONV_W = 3
RMS_EPS = 1e-6
NEG = -1e30
LOG2E = 1.4426950408889634
Q_SCALE = HEAD_DIM ** -0.5 * LOG2E
SB_TILE = 256
SB_DEAD = -160.0
VMEM_LIMIT = 56 * 1024 * 1024
PAGE_BUFFERS = 2
PAGE_RING = 32
PAGE_GROUP = 8

_NT = (((1,), (1,)), ((), ()))


def _cparams(sem):
    return pltpu.CompilerParams(dimension_semantics=sem, vmem_limit_bytes=VMEM_LIMIT)


def _rms(x, g):
    ms = jnp.mean(x * x, axis=-1, keepdims=True)
    return x * lax.rsqrt(ms + RMS_EPS) * g


def _silu(x):
    return x / (1.0 + jnp.exp(-x))


def _softplus2(z2):
    return jnp.maximum(z2, 0.0) + jnp.log2(1.0 + jnp.exp2(-jnp.abs(z2)))


def _split_bf16(x):
    hi = x.astype(BF16)
    lo = (x - hi.astype(F32)).astype(BF16)
    return hi, lo


def _tri(n):
    r = lax.broadcasted_iota(jnp.int32, (n, n), 0)
    c = lax.broadcasted_iota(jnp.int32, (n, n), 1)
    return jnp.where(r >= c, 1.0, 0.0).astype(BF16)


def _proj_kernel(x_ref, g_ref, w_ref, rest_ref, qk_ref, kt_ref, vt_ref, vtb_ref, keep_ref, xn_ref,
                 *, qkv_first, blk, keep):
    tm = x_ref.shape[0]
    w = BRANCH_W
    keep_tile, keep_row = keep
    xn_ref[...] = _rms(x_ref[...], g_ref[...]).astype(BF16)
    for j in range(w_ref.shape[1] // w):
        r = jnp.dot(xn_ref[...], w_ref[:, j * w:(j + 1) * w], preferred_element_type=F32)
        if j == qkv_first:
            qk_ref[:, 0:w] = (r * Q_SCALE).astype(BF16)
        elif j == qkv_first + 1:
            qk_ref[:, w:2 * w] = r.astype(BF16)
            for c in range(tm // blk):
                kt_ref[:, c * blk:(c + 1) * blk] = r[c * blk:(c + 1) * blk, :].T
        elif j == qkv_first + 2:
            for c in range(tm // blk):
                t = r[c * blk:(c + 1) * blk, :].T
                vt_ref[:, c * blk:(c + 1) * blk] = t
                vtb_ref[c] = t.astype(BF16)
        else:
            jr = j if j < qkv_first else j - 3
            rest_ref[:, jr * w:(jr + 1) * w] = r.astype(BF16)
            if jr == 1:
                @pl.when(pl.program_id(1) == keep_tile)
                def _():
                    keep_ref[...] = r[keep_row:keep_row + keep_ref.shape[0], :]


def _proj(x, g, w_bf, qkv_first, bsz, seq, tm, keep_rows):
    m, d = x.shape
    n = w_bf.shape[1]
    tps = seq // tm
    blk = min(MOBA_BLOCK, tm)
    w = BRANCH_W
    keep_first, keep_n = keep_rows
    keep = (keep_first // tm, keep_first % tm)
    assert keep[1] + keep_n <= tm and keep_n % 8 == 0
    return pl.pallas_call(
        functools.partial(_proj_kernel, qkv_first=qkv_first, blk=blk, keep=keep),
        grid=(bsz, tps),
        in_specs=[
            pl.BlockSpec((tm, d), lambda b, t: (b * tps + t, 0)),
            pl.BlockSpec((1, d), lambda b, t: (0, 0)),
            pl.BlockSpec((d, n), lambda b, t: (0, 0), pipeline_mode=pl.Buffered(1)),
        ],
        out_specs=[
            pl.BlockSpec((tm, n - 3 * w), lambda b, t: (b * tps + t, 0)),
            pl.BlockSpec((tm, 2 * w), lambda b, t: (b * tps + t, 0)),
            pl.BlockSpec((None, w, tm), lambda b, t: (b, 0, t)),
            pl.BlockSpec((None, w, tm), lambda b, t: (b, 0, t)),
            pl.BlockSpec((None, tm // blk, w, blk), lambda b, t: (b, t, 0, 0)),
            pl.BlockSpec((None, keep_n, w), lambda b, t: (b, 0, 0)),
        ],
        out_shape=[
            jax.ShapeDtypeStruct((m, n - 3 * w), BF16),
            jax.ShapeDtypeStruct((m, 2 * w), BF16),
            jax.ShapeDtypeStruct((bsz, w, seq), F32),
            jax.ShapeDtypeStruct((bsz, w, seq), F32),
            jax.ShapeDtypeStruct((bsz, seq // blk, w, blk), BF16),
            jax.ShapeDtypeStruct((bsz, keep_n, w), F32),
        ],
        scratch_shapes=[pltpu.VMEM((tm, d), BF16)],
        compiler_params=_cparams(("parallel", "arbitrary")),
    )(x, g.reshape(1, d), w_bf)


def _moba_prompt_kernel(q_ref, k_ref, vt_ref, o_ref, km_ref, *, nblk):
    i = pl.program_id(2)
    lane = lax.broadcasted_iota(jnp.int32, (1, LANES), 1)
    low = lane < HEAD_DIM
    nbp = km_ref.shape[0]
    blk = MOBA_BLOCK
    nheads = q_ref.shape[1] // HEAD_DIM

    @pl.when(i == 0)
    def _():
        km_ref[...] = jnp.zeros_like(km_ref)
        for j in range(nblk):
            rows = k_ref[j * blk:(j + 1) * blk, :].astype(F32)
            km_ref[j:j + 1, :] = jnp.sum(rows, axis=0, keepdims=True) * (1.0 / blk)

    brow = lax.broadcasted_iota(jnp.int32, (nbp, blk), 0)
    key_i = lax.broadcasted_iota(jnp.int32, (blk, blk), 0)
    qry_i = lax.broadcasted_iota(jnp.int32, (blk, blk), 1)
    start = pl.multiple_of(i * blk, blk)

    ones_rows = jnp.ones((2 * 8, blk), BF16)

    def v_rows(j, hh):
        return jnp.concatenate([vt_ref[j, hh * HEAD_DIM:(hh + 1) * HEAD_DIM, :], ones_rows], axis=0)

    q_owns, gates, own_scores = [], [], []
    for hh in range(nheads):
        cols = slice((hh // 2) * LANES, (hh // 2 + 1) * LANES)
        q = q_ref[:, cols]
        km = km_ref[:, cols]
        mine = low if hh % 2 == 0 else jnp.logical_not(low)
        km_hi, km_lo = _split_bf16(jnp.where(mine, km, 0.0))
        gates.append(lax.dot_general(km_hi, q, _NT, preferred_element_type=F32)
                     + lax.dot_general(km_lo, q, _NT, preferred_element_type=F32))
        q_own = jnp.where(mine, q, jnp.zeros_like(q))
        q_owns.append(q_own)
        own_scores.append(lax.dot_general(k_ref[pl.ds(start, blk), cols], q_own, _NT,
                                          preferred_element_type=F32))

    q_pasts, init = [], []
    for hh in range(nheads):
        gate = jnp.where(brow < i, gates[hh], -jnp.inf)
        bias = jnp.full((nbp, blk), NEG, F32)
        for _ in range(MOBA_TOPK):
            mx = jnp.max(gate, axis=0, keepdims=True)
            first = jnp.min(jnp.where(gate == mx, brow, nbp), axis=0, keepdims=True)
            pick = (brow == first) & (mx > -jnp.inf)
            bias = jnp.where(pick, 0.0, bias)
            gate = jnp.where(pick, -jnp.inf, gate)
        bias_t = jnp.concatenate([bias, jnp.zeros((LANES - nbp, blk), F32)], axis=0).T
        q_pasts.append(jnp.concatenate([q_owns[hh], bias_t.astype(BF16)], axis=1))

        s = jnp.where(key_i <= qry_i, own_scores[hh], NEG)
        m0 = jnp.max(s, axis=0, keepdims=True)
        p = jnp.exp2(s - m0)
        acc0 = jnp.dot(v_rows(i, hh), p.astype(BF16), preferred_element_type=F32)
        init.append((m0, acc0))

    def stage(js, carry):
        scores = []
        for j in js:
            st = pl.multiple_of(j * blk, blk)
            onehot = jnp.broadcast_to(jnp.where(lane == j, 1.0, 0.0).astype(BF16), (blk, LANES))
            for hh in range(nheads):
                kj = k_ref[pl.ds(st, blk), (hh // 2) * LANES:(hh // 2 + 1) * LANES]
                scores.append(lax.dot_general(jnp.concatenate([kj, onehot], axis=1), q_pasts[hh],
                                              _NT, preferred_element_type=F32))
        carry = list(carry)
        for n, j in enumerate(js):
            for hh in range(nheads):
                m, acc = carry[hh]
                sj = scores[n * nheads + hh]
                m_new = jnp.maximum(m, jnp.max(sj, axis=0, keepdims=True))
                alpha = jnp.exp2(m - m_new)
                pj = jnp.exp2(sj - m_new)
                acc = alpha * acc + jnp.dot(v_rows(j, hh), pj.astype(BF16),
                                            preferred_element_type=F32)
                carry[hh] = (m_new, acc)
        return tuple(carry)

    fin = tuple(init)
    base = 0
    size = MOBA_UNROLL
    while size >= 1:
        trips = (i - base) // size if size == MOBA_UNROLL else ((i - base) // size) % 2
        fin = lax.fori_loop(
            0, trips,
            lambda t, c, base=base, size=size: stage([base + t * size + u for u in range(size)], c),
            fin)
        base = base + trips * size
        size //= 2
    out_t = jnp.concatenate([acc[:HEAD_DIM] / acc[HEAD_DIM:HEAD_DIM + 1] for _, acc in fin],
                            axis=0)
    o_ref[...] = out_t.T


def _attn_prompt_call(kernel_fn, qk, vtb, bsz, seq, scratch, wid):
    nblk = seq // MOBA_BLOCK
    ngrp = BRANCH_W // wid
    return pl.pallas_call(
        kernel_fn,
        grid=(bsz, ngrp, nblk),
        in_specs=[
            pl.BlockSpec((MOBA_BLOCK, wid), lambda b, p, i: (b * nblk + i, p)),
            pl.BlockSpec((seq, wid), lambda b, p, i: (b, ngrp + p)),
            pl.BlockSpec((None, nblk, wid, MOBA_BLOCK), lambda b, p, i: (b, 0, p, 0)),
        ],
        out_specs=pl.BlockSpec((MOBA_BLOCK, wid), lambda b, p, i: (b * nblk + i, p)),
        out_shape=jax.ShapeDtypeStruct((bsz * seq, BRANCH_W), F32),
        scratch_shapes=scratch,
        compiler_params=_cparams(("parallel", "parallel", "arbitrary")),
    )(qk, qk, vtb)


MOBA_STEP_W = 512


def _moba_prompt(qk, vtb, bsz, seq):
    nblk = seq // MOBA_BLOCK
    nbp = -(-nblk // 8) * 8
    return _attn_prompt_call(functools.partial(_moba_prompt_kernel, nblk=nblk), qk, vtb, bsz, seq,
                             [pltpu.VMEM((nbp, MOBA_STEP_W), F32)], MOBA_STEP_W)


def _sb_prompt_kernel(q_ref, k_ref, vt_ref, o_ref):
    i = pl.program_id(2)
    nheads = q_ref.shape[1] // HEAD_DIM
    tile = SB_TILE
    lane = lax.broadcasted_iota(jnp.int32, (1, LANES), 1)
    low = lane < HEAD_DIM
    key_i = lax.broadcasted_iota(jnp.int32, (tile, tile), 0)
    qry_i = lax.broadcasted_iota(jnp.int32, (tile, tile), 1)
    tri_t = jnp.where(qry_i >= key_i, 1.0, 0.0).astype(BF16)
    strict = key_i < qry_i

    q_hs = []
    for hh in range(nheads):
        q = q_ref[:, (hh // 2) * LANES:(hh // 2 + 1) * LANES]
        mine = low if hh % 2 == 0 else jnp.logical_not(low)
        q_hs.append(jnp.where(mine, q, jnp.zeros_like(q)))

    def sweep(j, state, diag):
        st = pl.multiple_of(j * tile, tile)
        zs = [lax.dot_general(k_ref[pl.ds(st, tile), (hh // 2) * LANES:(hh // 2 + 1) * LANES],
                              q_hs[hh], _NT, preferred_element_type=F32)
              for hh in range(nheads)]
        cums = []
        for hh in range(nheads):
            lk = -_softplus2(zs[hh])
            if diag:
                lk = jnp.where(strict, lk, 0.0)
            hi, lo = _split_bf16(lk)
            cums.append(jnp.dot(tri_t, hi, preferred_element_type=F32)
                        + jnp.dot(tri_t, lo, preferred_element_type=F32) + state[hh][0])
        new = []
        for hh in range(nheads):
            w = jnp.exp2(zs[hh] + cums[hh])
            if diag:
                w = jnp.where(strict, w, 0.0)
            acc = state[hh][1] + jnp.dot(vt_ref[j, hh * HEAD_DIM:(hh + 1) * HEAD_DIM, :],
                                         w.astype(BF16), preferred_element_type=F32)
            new.append((cums[hh][0:1, :], acc))
        return tuple(new)

    def live_of(state):
        live = jnp.max(state[0][0])
        for hh in range(1, nheads):
            live = jnp.maximum(live, jnp.max(state[hh][0]))
        return live

    def cond(st):
        j, live, _ = st
        return (j >= 0) & (live > SB_DEAD)

    def body(st):
        j, _, state = st
        new = sweep(j, state, False)
        return j - 1, live_of(new), new

    zero = tuple((jnp.zeros((1, tile), F32), jnp.zeros((HEAD_DIM, tile), F32))
                 for _ in range(nheads))
    init = sweep(i, zero, True)
    _, _, fin = lax.while_loop(cond, body, (i - 1, live_of(init), init))
    o_ref[...] = jnp.concatenate([acc for _, acc in fin], axis=0).T


SB_STEP_W = 512


def _sb_prompt(qk, vtb, bsz, seq):
    return _attn_prompt_call(_sb_prompt_kernel, qk, vtb, bsz, seq, [], SB_STEP_W)


def _query_rows(qrep_ref, nheads):
    nrow, width = qrep_ref.shape[1], qrep_ref.shape[2]
    row_w = lax.broadcasted_iota(jnp.int32, (nrow, width), 0)
    lane_w = lax.broadcasted_iota(jnp.int32, (nrow, width), 1)
    head_mask = (lane_w // HEAD_DIM) == (row_w % nheads)
    qrows = jnp.where(head_mask, qrep_ref[0], jnp.zeros_like(qrep_ref[0]))
    return qrows, head_mask


def _pick_heads(y, head_mask, nheads):
    nrow, width = y.shape
    y = jnp.where(head_mask, y, 0.0)
    return jnp.sum(y.reshape(nrow // nheads, nheads, width), axis=1)


def _moba_paged_kernel(pt_ref, qrep_ref, knew_ref, vnew_ref, k_hbm, v_hbm, o_ref,
                       kbuf, vbuf, ksem, vsem, s_ref, p_ref, *, npages, nheads):
    b = pl.program_id(0)
    nrow = qrep_ref.shape[1]
    ring = kbuf.shape[0]
    page = kbuf.shape[2]
    ppb = MOBA_BLOCK // page
    nblk = npages // ppb
    qrows, head_mask = _query_rows(qrep_ref, nheads)
    row = lax.broadcasted_iota(jnp.int32, (nrow, LANES), 0)
    lane = lax.broadcasted_iota(jnp.int32, (nrow, LANES), 1)
    qidx = row // nheads

    def k_copy(seq, p, slot):
        return pltpu.make_async_copy(k_hbm.at[pt_ref[seq * npages + p]], kbuf.at[slot], ksem.at[slot])

    def v_copy(p, slot):
        return pltpu.make_async_copy(v_hbm.at[pt_ref[b * npages + p]], vbuf.at[slot], vsem.at[slot])

    @pl.when(b == 0)
    def _():
        for p in range(ring):
            k_copy(0, p, p).start()

    grp = min(PAGE_GROUP, ring)

    def k_group(gi, carry):
        base = gi * grp
        slot0 = (gi % (ring // grp)) * grp
        for u in range(grp):
            k_copy(b, base + u, slot0 + u).wait()
        for u in range(grp):
            s_ref[gi * (grp // ppb) + u // ppb, :, (u % ppb) * page:(u % ppb + 1) * page] = jnp.dot(
                qrows, kbuf[slot0 + u].astype(BF16), preferred_element_type=F32)

        @pl.when(base + ring < npages)
        def _():
            for u in range(grp):
                k_copy(b, base + ring + u, slot0 + u).start()
        return carry

    lax.fori_loop(0, npages // grp, k_group, 0)
    for p in range(ring):
        v_copy(p, p).start()

    @pl.when(b + 1 < pl.num_programs(0))
    def _():
        for p in range(ring):
            k_copy(b + 1, p, p).start()

    zn = lax.dot_general(qrows, knew_ref[0], _NT, preferred_element_type=F32)
    gates = jnp.zeros((nrow, LANES), F32)
    for bl in range(nblk):
        gates = jnp.where(lane == bl, jnp.sum(s_ref[bl], axis=1, keepdims=True), gates)
    gates = jnp.where(lane < nblk, gates, -jnp.inf)
    sel = jnp.zeros((nrow, LANES), F32)
    for _ in range(MOBA_TOPK):
        mx = jnp.max(gates, axis=1, keepdims=True)
        first = jnp.min(jnp.where(gates == mx, lane, LANES), axis=1, keepdims=True)
        pick = (lane == first) & (mx > -jnp.inf)
        sel = jnp.where(pick, 1.0, sel)
        gates = jnp.where(pick, -jnp.inf, gates)
    zn = jnp.where(lane <= qidx, zn, NEG)
    m = jnp.max(zn, axis=1, keepdims=True)
    cols = []
    for bl in range(nblk):
        col = jnp.max(jnp.where(lane == bl, sel, 0.0), axis=1, keepdims=True) > 0.5
        cols.append(col)
        m = jnp.maximum(m, jnp.max(jnp.where(col, s_ref[bl], NEG), axis=1, keepdims=True))
    pn = jnp.exp2(zn - m)
    l = jnp.sum(pn, axis=1, keepdims=True)
    for bl in range(nblk):
        p = jnp.exp2(jnp.where(cols[bl], s_ref[bl], NEG) - m)
        l = l + jnp.sum(p, axis=1, keepdims=True)
        p_ref[bl] = p.astype(BF16)
    acc0 = jnp.dot(pn.astype(BF16), vnew_ref[0], preferred_element_type=F32)

    def v_group(gi, acc):
        base = gi * grp
        slot0 = (gi % (ring // grp)) * grp
        for u in range(grp):
            v_copy(base + u, slot0 + u).wait()
        for u in range(grp):
            pw = p_ref[gi * (grp // ppb) + u // ppb, :, (u % ppb) * page:(u % ppb + 1) * page]
            acc = acc + lax.dot_general(pw, vbuf[slot0 + u].astype(BF16), _NT,
                                        preferred_element_type=F32)

        @pl.when(base + ring < npages)
        def _():
            for u in range(grp):
                v_copy(base + ring + u, slot0 + u).start()
        return acc

    acc = lax.fori_loop(0, npages // grp, v_group, acc0)
    o_ref[0] = _pick_heads(acc / l, head_mask, nheads)


def _paged_operands(q_bf, knew_bf, vnew_bf, cache_k, cache_v, page_table):
    nb, nq, width = q_bf.shape
    nheads = width // HEAD_DIM
    nphys, page = cache_k.shape[0], cache_k.shape[1]
    ck = cache_k.transpose(0, 2, 3, 1).reshape(nphys, width, page)
    cv = cache_v.transpose(0, 2, 3, 1).reshape(nphys, width, page)
    qrep = jnp.repeat(q_bf, nheads, axis=1)
    pad = ((0, 0), (0, LANES - nq), (0, 0))
    knew = jnp.pad(knew_bf, pad)
    vnew = jnp.pad(vnew_bf, pad)
    pt = page_table.reshape(-1).astype(jnp.int32)
    return qrep, knew, vnew, ck, cv, pt, nheads, page


def _moba_paged(q_bf, knew_bf, vnew_bf, cache_k, cache_v, page_table):
    nb, nq, width = q_bf.shape
    qrep, knew, vnew, ck, cv, pt, nheads, page = _paged_operands(
        q_bf, knew_bf, vnew_bf, cache_k, cache_v, page_table)
    nrow = nq * nheads
    npages = page_table.shape[1]
    ppb = MOBA_BLOCK // page
    ring = min(PAGE_RING, npages)
    grp = min(PAGE_GROUP, ring)
    assert grp % ppb == 0 and ring % grp == 0 and npages % ring == 0
    per_b = lambda b, pt: (b, 0, 0)
    grid_spec = pltpu.PrefetchScalarGridSpec(
        num_scalar_prefetch=1,
        grid=(nb,),
        in_specs=[
            pl.BlockSpec((1, nrow, width), per_b),
            pl.BlockSpec((1, LANES, width), per_b),
            pl.BlockSpec((1, LANES, width), per_b),
            pl.BlockSpec(memory_space=pl.ANY),
            pl.BlockSpec(memory_space=pl.ANY),
        ],
        out_specs=pl.BlockSpec((1, nq, width), per_b),
        scratch_shapes=[
            pltpu.VMEM((ring, width, page), F32),
            pltpu.VMEM((ring, width, page), F32),
            pltpu.SemaphoreType.DMA((ring,)),
            pltpu.SemaphoreType.DMA((ring,)),
            pltpu.VMEM((npages // ppb, nrow, MOBA_BLOCK), F32),
            pltpu.VMEM((npages // ppb, nrow, MOBA_BLOCK), BF16),
        ],
    )
    return pl.pallas_call(
        functools.partial(_moba_paged_kernel, npages=npages, nheads=nheads),
        grid_spec=grid_spec,
        out_shape=jax.ShapeDtypeStruct((nb, nq, width), F32),
        compiler_params=_cparams(("arbitrary",)),
    )(pt, qrep, knew, vnew, ck, cv)


def _sb_paged_kernel(pt_ref, live_ref, qrep_ref, knew_ref, vnew_ref, cin_ref, ain_ref, *rest,
                     group, nheads, first):
    del pt_ref
    k_refs = rest[:group]
    v_refs = rest[group:2 * group]
    y_ref, acc_ref, carry_ref = rest[2 * group:2 * group + 3]
    b = pl.program_id(0)
    s = pl.program_id(1)
    nrow = qrep_ref.shape[1]
    page = k_refs[0].shape[1]
    ppt = SB_TILE // page
    qrows, head_mask = _query_rows(qrep_ref, nheads)
    tri = _tri(SB_TILE)

    @pl.when(s == 0)
    def _():
        if first:
            row = lax.broadcasted_iota(jnp.int32, (nrow, LANES), 0)
            lane = lax.broadcasted_iota(jnp.int32, (nrow, LANES), 1)
            strict = lane < row // nheads
            zn = lax.dot_general(qrows, knew_ref[0], _NT, preferred_element_type=F32)
            hi, lo = _split_bf16(jnp.where(strict, -_softplus2(zn), 0.0))
            tri_n = _tri(LANES)
            cum = (jnp.dot(hi, tri_n, preferred_element_type=F32)
                   + jnp.dot(lo, tri_n, preferred_element_type=F32))
            wn = jnp.where(strict, jnp.exp2(zn + cum), 0.0)
            acc_ref[0] = jnp.dot(wn.astype(BF16), vnew_ref[0], preferred_element_type=F32)
            carry_ref[0] = jnp.broadcast_to(cum[:, 0:1], carry_ref.shape[1:])
        else:
            acc_ref[0] = ain_ref[0]
            carry_ref[0] = cin_ref[0]

    @pl.when(live_ref[b] > 0)
    def _():
        carry = carry_ref[0][:, 0:1]
        acc = acc_ref[0]
        for t in reversed(range(group // ppt)):
            kt = jnp.concatenate([k_refs[t * ppt + u][...] for u in range(ppt)], axis=1).astype(BF16)
            vt = jnp.concatenate([v_refs[t * ppt + u][...] for u in range(ppt)], axis=1).astype(BF16)
            z = jnp.dot(qrows, kt, preferred_element_type=F32)
            hi, lo = _split_bf16(-_softplus2(z))
            cum = (jnp.dot(hi, tri, preferred_element_type=F32)
                   + jnp.dot(lo, tri, preferred_element_type=F32)) + carry
            w = jnp.exp2(z + cum).astype(BF16)
            acc = acc + lax.dot_general(w, vt, _NT, preferred_element_type=F32)
            carry = cum[:, 0:1]
        acc_ref[0] = acc
        carry_ref[0] = jnp.broadcast_to(carry, carry_ref.shape[1:])

    @pl.when(s == pl.num_programs(1) - 1)
    def _():
        y_ref[0] = _pick_heads(acc_ref[0], head_mask, nheads)


def _sb_paged_call(first, pt, live, qrep, knew, vnew, carry_in, acc_in, ck, cv,
                   nq, npages, group, chunk_lo, nsteps, nheads):
    nb, nrow, width = qrep.shape
    page = ck.shape[2]

    def page_map(g):
        def index(b, s, pt, live):
            chunk_page = (chunk_lo + nsteps - 1 - s) * group + g
            return (jnp.where(live[b] > 0, pt[b * npages + chunk_page], 0), 0, 0)
        return index

    per_b = lambda b, s, pt, live: (b, 0, 0)
    in_specs = [
        pl.BlockSpec((1, nrow, width), per_b),
        pl.BlockSpec((1, LANES, width), per_b),
        pl.BlockSpec((1, LANES, width), per_b),
        pl.BlockSpec((1, nrow, LANES), per_b),
        pl.BlockSpec((1, nrow, width), per_b),
    ]
    in_specs += [pl.BlockSpec((None, width, page), page_map(g), pipeline_mode=pl.Buffered(PAGE_BUFFERS))
                 for g in range(group)] * 2
    grid_spec = pltpu.PrefetchScalarGridSpec(
        num_scalar_prefetch=2,
        grid=(nb, nsteps),
        in_specs=in_specs,
        out_specs=[pl.BlockSpec((1, nq, width), per_b),
                   pl.BlockSpec((1, nrow, width), per_b),
                   pl.BlockSpec((1, nrow, LANES), per_b)],
    )
    return pl.pallas_call(
        functools.partial(_sb_paged_kernel, group=group, nheads=nheads, first=first),
        grid_spec=grid_spec,
        out_shape=[jax.ShapeDtypeStruct((nb, nq, width), F32),
                   jax.ShapeDtypeStruct((nb, nrow, width), F32),
                   jax.ShapeDtypeStruct((nb, nrow, LANES), F32)],
        compiler_params=_cparams(("parallel", "arbitrary")),
    )(pt, live, qrep, knew, vnew, carry_in, acc_in, *([ck] * group), *([cv] * group))


def _sb_paged(q_bf, knew_bf, vnew_bf, cache_k, cache_v, page_table, group):
    nb, nq, width = q_bf.shape
    qrep, knew, vnew, ck, cv, pt, nheads, page = _paged_operands(
        q_bf, knew_bf, vnew_bf, cache_k, cache_v, page_table)
    nrow = nq * nheads
    npages = page_table.shape[1]
    nchunk = npages // group
    zc = jnp.zeros((nb, nrow, LANES), F32)
    za = jnp.zeros((nb, nrow, width), F32)
    all_live = jnp.ones((nb,), jnp.int32)
    y, acc, carry = _sb_paged_call(True, pt, all_live, qrep, knew, vnew, zc, za, ck, cv,
                                   nq, npages, group, nchunk - 1, 1, nheads)
    if nchunk == 1:
        return y
    live = (jnp.max(carry, axis=(1, 2)) > SB_DEAD).astype(jnp.int32)

    def older(_):
        return _sb_paged_call(False, pt, live, qrep, knew, vnew, carry, acc, ck, cv,
                              nq, npages, group, 0, nchunk - 1, nheads)[0]

    return lax.cond(jnp.any(live > 0), older, lambda _: y, None)


def _finish(y_first, y_second, w_ref, g_ref, x_ref, o_ref):
    half = y_first.shape[1]
    out = (jnp.dot(y_first.astype(BF16), w_ref[0:half, :], preferred_element_type=F32)
           + jnp.dot(y_second.astype(BF16), w_ref[half:2 * half, :], preferred_element_type=F32))
    o_ref[...] = x_ref[...] + _rms(out, g_ref[...])


def _even_out_prompt_kernel(ab_ref, ac_ref, ah_ref, az_ref, bz_ref, hc_ref, hh_ref, yb_ref, x_ref,
                            w_ref, cw_ref, g_ref, o_ref, tail_ref, ue_ref, *, tiles_per_seq):
    i = pl.program_id(0)
    tm = ab_ref.shape[0]
    hrows = hc_ref.shape[0]
    u = ac_ref[...].astype(F32) * ah_ref[...].astype(F32)
    first = (i % tiles_per_seq) == 0
    halo = hc_ref[...].astype(F32) * hh_ref[...].astype(F32)
    ue_ref[0:8, :] = jnp.where(first, 0.0, halo[hrows - 8:hrows, :])
    ue_ref[8:8 + tm, :] = u
    conv = (ue_ref[6:6 + tm, :] * cw_ref[0:1, :] + ue_ref[7:7 + tm, :] * cw_ref[1:2, :]
            + u * cw_ref[2:3, :])
    y_a = ab_ref[...].astype(F32) * conv * _silu(az_ref[...].astype(F32))
    y_b = yb_ref[...] * _silu(bz_ref[...].astype(F32))
    tail_ref[...] = u[tm - 8:tm, :]
    _finish(y_a, y_b, w_ref, g_ref, x_ref, o_ref)


def _even_out_prompt(rest, y_b, x, w_out_bf, conv_w, g_post, seq, tm):
    m, d = x.shape
    nt = m // tm
    col = lambda c: pl.BlockSpec((tm, BRANCH_W), lambda i, c=c: (i, c))
    hrows = 16
    halo = lambda c: pl.BlockSpec((hrows, BRANCH_W),
                                  lambda i, c=c: (jnp.maximum(i * (tm // hrows) - 1, 0), c))
    return pl.pallas_call(
        functools.partial(_even_out_prompt_kernel, tiles_per_seq=seq // tm),
        grid=(nt,),
        in_specs=[col(0), col(1), col(2), col(3), col(4), halo(1), halo(2),
                  pl.BlockSpec((tm, BRANCH_W), lambda i: (i, 0)),
                  pl.BlockSpec((tm, d), lambda i: (i, 0)),
                  pl.BlockSpec(w_out_bf.shape, lambda i: (0, 0)),
                  pl.BlockSpec(conv_w.shape, lambda i: (0, 0)),
                  pl.BlockSpec((1, d), lambda i: (0, 0))],
        out_specs=[pl.BlockSpec((tm, d), lambda i: (i, 0)),
                   pl.BlockSpec((8, BRANCH_W), lambda i: (i, 0))],
        out_shape=[jax.ShapeDtypeStruct((m, d), F32),
                   jax.ShapeDtypeStruct((nt * 8, BRANCH_W), F32)],
        scratch_shapes=[pltpu.VMEM((tm + 8, BRANCH_W), F32)],
        compiler_params=_cparams(("parallel",)),
    )(rest, rest, rest, rest, rest, rest, rest, y_b, x, w_out_bf, conv_w, g_post.reshape(1, d))


def _even_out_sample_kernel(rest_ref, st_ref, yb_ref, x_ref, w_ref, cw_ref, g_ref,
                            o_ref, tail_ref, *, nb):
    w = BRANCH_W
    rows = rest_ref.shape[0]
    rest = rest_ref[...].astype(F32)
    u = rest[:, w:2 * w] * rest[:, 2 * w:3 * w]
    ue = jnp.concatenate([st_ref[...], u], axis=0)
    conv = (ue[0:rows] * cw_ref[0:1, :] + ue[nb:nb + rows] * cw_ref[1:2, :]
            + ue[2 * nb:2 * nb + rows] * cw_ref[2:3, :])
    y_a = rest[:, 0:w] * conv * _silu(rest[:, 3 * w:4 * w])
    y_b = yb_ref[...] * _silu(rest[:, 4 * w:5 * w])
    tail_ref[...] = ue[rows:rows + 2 * nb]
    _finish(y_a, y_b, w_ref, g_ref, x_ref, o_ref)


def _even_out_sample(rest, state_tm, y_b, x, w_out_bf, conv_w, g_post, nb):
    m, d = x.shape
    return pl.pallas_call(
        functools.partial(_even_out_sample_kernel, nb=nb),
        out_shape=[jax.ShapeDtypeStruct((m, d), F32),
                   jax.ShapeDtypeStruct(((CONV_W - 1) * nb, BRANCH_W), F32)],
        compiler_params=pltpu.CompilerParams(vmem_limit_bytes=VMEM_LIMIT),
    )(rest, state_tm, y_b, x, w_out_bf, conv_w, g_post.reshape(1, d))


def _gmlp_mix(v, gw_ref, low):
    r_i = lax.broadcasted_iota(jnp.int32, (GMLP_CHUNK, GMLP_CHUNK), 0)
    c_i = lax.broadcasted_iota(jnp.int32, (GMLP_CHUNK, GMLP_CHUNK), 1)
    tril = c_i <= r_i
    parts = []
    for p in range(v.shape[1] // LANES):
        vp = v[:, p * LANES:(p + 1) * LANES]
        v_lo = jnp.where(low, vp, jnp.zeros_like(vp)).astype(BF16)
        v_hi = jnp.where(low, jnp.zeros_like(vp), vp).astype(BF16)
        w_lo = jnp.where(tril, gw_ref[2 * p], 0.0).astype(BF16)
        w_hi = jnp.where(tril, gw_ref[2 * p + 1], 0.0).astype(BF16)
        parts.append(jnp.dot(w_lo, v_lo, preferred_element_type=F32)
                     + jnp.dot(w_hi, v_hi, preferred_element_type=F32))
    return jnp.concatenate(parts, axis=1)


def _odd_out_prompt_kernel(cu_ref, cv_ref, cz_ref, dz_ref, yd_ref, x_ref, w_ref, gw_ref, gb_ref,
                           g_ref, o_ref, yc_ref):
    tm = cu_ref.shape[0]
    low = lax.broadcasted_iota(jnp.int32, (1, LANES), 1) < HEAD_DIM
    for c in range(tm // GMLP_CHUNK):
        rows = slice(c * GMLP_CHUNK, (c + 1) * GMLP_CHUNK)
        mixed = _gmlp_mix(cv_ref[rows, :], gw_ref, low) + gb_ref[...]
        yc_ref[rows, :] = (cu_ref[rows, :].astype(F32) * mixed
                           * _silu(cz_ref[rows, :].astype(F32)))
    y_d = yd_ref[...] * _silu(dz_ref[...].astype(F32))
    _finish(yc_ref[...], y_d, w_ref, g_ref, x_ref, o_ref)


def _odd_out_prompt(rest, y_d, x, w_out_bf, gmlp_w, gb_full, g_post, tm):
    m, d = x.shape
    col = lambda c: pl.BlockSpec((tm, BRANCH_W), lambda i, c=c: (i, c))
    return pl.pallas_call(
        _odd_out_prompt_kernel,
        grid=(m // tm,),
        in_specs=[col(0), col(1), col(2), col(3),
                  pl.BlockSpec((tm, BRANCH_W), lambda i: (i, 0)),
                  pl.BlockSpec((tm, d), lambda i: (i, 0)),
                  pl.BlockSpec(w_out_bf.shape, lambda i: (0, 0)),
                  pl.BlockSpec(gmlp_w.shape, lambda i: (0, 0, 0)),
                  pl.BlockSpec(gb_full.shape, lambda i: (0, 0)),
                  pl.BlockSpec((1, d), lambda i: (0, 0))],
        out_specs=pl.BlockSpec((tm, d), lambda i: (i, 0)),
        out_shape=jax.ShapeDtypeStruct((m, d), F32),
        scratch_shapes=[pltpu.VMEM((tm, BRANCH_W), F32)],
        compiler_params=_cparams(("parallel",)),
    )(rest, rest, rest, rest, y_d, x, w_out_bf, gmlp_w, gb_full, g_post.reshape(1, d))


def _odd_out_sample_kernel(rest_ref, yd_ref, x_ref, w_ref, w4_ref, b4_ref, g_ref, o_ref, *, nb, nq):
    w = BRANCH_W
    rest = rest_ref[...].astype(F32)
    parts = []
    for t in range(nq):
        mixed = jnp.broadcast_to(b4_ref[t:t + 1, :], (nb, w))
        for s in range(t + 1):
            mixed = mixed + w4_ref[t * nq + s:t * nq + s + 1, :] * rest[s * nb:(s + 1) * nb, w:2 * w]
        parts.append(mixed)
    mixed = jnp.concatenate(parts, axis=0)
    y_c = rest[:, 0:w] * mixed * _silu(rest[:, 2 * w:3 * w])
    y_d = yd_ref[...] * _silu(rest[:, 3 * w:4 * w])
    _finish(y_c, y_d, w_ref, g_ref, x_ref, o_ref)


def _odd_out_sample(rest, y_d, x, w_out_bf, w4, b4, g_post, nb, nq):
    m, d = x.shape
    return pl.pallas_call(
        functools.partial(_odd_out_sample_kernel, nb=nb, nq=nq),
        out_shape=jax.ShapeDtypeStruct((m, d), F32),
        compiler_params=pltpu.CompilerParams(vmem_limit_bytes=VMEM_LIMIT),
    )(rest, y_d, x, w_out_bf, w4, b4, g_post.reshape(1, d))


def _to_token_major(a):
    nb, nq, w = a.shape
    return a.transpose(1, 0, 2).reshape(nq * nb, w)


def _to_batch_major(a, nb):
    w = a.shape[1]
    return a.reshape(-1, nb, w).transpose(1, 0, 2)


def _heads_prompt(t, heads):
    bsz, w, seq = t.shape
    return t.reshape(bsz, heads, w // heads, seq).transpose(0, 3, 1, 2)


def _project_both(hp, hs, dims, g_pre, w_in, qkv_first, tm):
    bsz, seq, nb, nq = dims
    w_in_bf = w_in.astype(BF16)
    open_start = ((seq - 1) // GMLP_CHUNK) * GMLP_CHUNK
    prompt = _proj(hp, g_pre, w_in_bf, qkv_first, bsz, seq, tm, (open_start, seq - open_start))
    ns = nq * nb
    rows = max(ns, tm)
    hs_pad = jnp.pad(hs, ((0, rows - ns), (0, 0)))
    rest_s, qk_s, kt_s, vt_s, _, keep_s = _proj(hs_pad, g_pre, w_in_bf, qkv_first, 1, rows, rows,
                                                (0, ns))
    rest_s = rest_s[:ns]
    k_s = _to_batch_major(kt_s[0, :, :ns].T, nb)
    v_s = _to_batch_major(vt_s[0, :, :ns].T, nb)
    q_s = _to_batch_major(qk_s[:ns, :BRANCH_W], nb)
    return prompt, rest_s, q_s, k_s, v_s, keep_s[0]


def _even_layer(hp, hs, dims, state_conv, cache_k, cache_v, page_table,
                g_pre, g_post, w_in, conv_w, w_out, tiles):
    bsz, seq, nb, nq = dims
    tm, tm_proj = tiles
    w = BRANCH_W
    heads = w // HEAD_DIM
    w_out_bf = w_out.astype(BF16)
    (rest_p, qk_p, kt_p, vt_p, vtb_p, _), rest_s, q_s, k_s, v_s, _ = _project_both(
        hp, hs, dims, g_pre, w_in, 4, tm_proj)
    yb_p = _moba_prompt(qk_p, vtb_p, bsz, seq)
    yb_s = _to_token_major(_moba_paged(q_s, k_s.astype(BF16), v_s.astype(BF16),
                                       cache_k, cache_v, page_table))
    hp, tail_p = _even_out_prompt(rest_p, yb_p, hp, w_out_bf, conv_w, g_post, seq, tm)
    state_tm = _to_token_major(state_conv)
    hs, tail_s = _even_out_sample(rest_s, state_tm, yb_s, hs, w_out_bf, conv_w, g_post, nb)
    conv_p = tail_p.reshape(bsz, seq // tm, 8, w)[:, -1, 8 - (CONV_W - 1):, :]
    conv_s = _to_batch_major(tail_s, nb)
    return (hp, hs, conv_p, conv_s, _heads_prompt(kt_p, heads), _heads_prompt(vt_p, heads),
            k_s.reshape(nb, nq, heads, HEAD_DIM), v_s.reshape(nb, nq, heads, HEAD_DIM))


def _odd_layer(hp, hs, dims, cache_k, cache_v, page_table,
               g_pre, g_post, w_in, gmlp_w, gmlp_b, w_out, tiles, group):
    bsz, seq, nb, nq = dims
    tm, tm_proj = tiles
    w = BRANCH_W
    heads = w // HEAD_DIM
    w_out_bf = w_out.astype(BF16)
    ngroups = gmlp_w.shape[0]
    cpg = w // ngroups
    (rest_p, qk_p, kt_p, vt_p, vtb_p, gv_p), rest_s, q_s, k_s, v_s, keep_s = _project_both(
        hp, hs, dims, g_pre, w_in, 3, tm_proj)
    yd_p = _sb_prompt(qk_p, vtb_p, bsz, seq)
    yd_s = _to_token_major(_sb_paged(q_s, k_s.astype(BF16), v_s.astype(BF16),
                                     cache_k, cache_v, page_table, group))
    gb_full = jnp.repeat(gmlp_b.T, cpg, axis=1)
    hp = _odd_out_prompt(rest_p, yd_p, hp, w_out_bf, gmlp_w, gb_full, g_post, tm)
    w4 = jnp.repeat(gmlp_w[:, :nq, :nq].transpose(1, 2, 0), cpg, axis=2).reshape(nq * nq, w)
    b4 = jnp.repeat(gmlp_b[:, :nq].T, cpg, axis=1)
    hs = _odd_out_sample(rest_s, yd_s, hs, w_out_bf, w4, b4, g_post, nb, nq)
    gv_s = _to_batch_major(keep_s, nb)
    return (hp, hs, _heads_prompt(kt_p, heads), _heads_prompt(vt_p, heads),
            k_s.reshape(nb, nq, heads, HEAD_DIM), v_s.reshape(nb, nq, heads, HEAD_DIM), gv_p, gv_s)


def kernel(x_prompt, x_sample, state_conv, cache_k_moba, cache_v_moba, cache_k_sb, cache_v_sb,
           page_table, norm_pre_e, norm_post_e, w_in_e, conv_w, w_out_e,
           norm_pre_o, norm_post_o, w_in_o, gmlp_w, gmlp_b, w_out_o):
    bsz, seq, d = x_prompt.shape
    nb, nq, _ = x_sample.shape
    depth = norm_pre_e.shape[0] + norm_pre_o.shape[0]
    npages, page = page_table.shape[1], cache_k_moba.shape[2]
    assert seq % MOBA_BLOCK == 0 and (npages * page) % MOBA_BLOCK == 0 and nq <= GMLP_CHUNK
    assert (nq - 1) // MOBA_BLOCK == 0 and w_in_e.shape[2] == 8 * BRANCH_W and w_in_o.shape[2] == 7 * BRANCH_W
    dims = (bsz, seq, nb, nq)
    tm = min(512, seq)
    tm_proj = min(512, seq)
    group_sb = min(4, npages)
    hp = x_prompt.reshape(bsz * seq, d)
    hs = _to_token_major(x_sample)
    ev, od = [], []
    for i in range(depth):
        j = i // 2
        if i % 2 == 0:
            hp, hs, *rest = _even_layer(
                hp, hs, dims, state_conv[j], cache_k_moba[j], cache_v_moba[j], page_table,
                norm_pre_e[j], norm_post_e[j], w_in_e[j], conv_w[j], w_out_e[j],
                (tm, tm_proj))
            ev.append(rest)
        else:
            hp, hs, *rest = _odd_layer(
                hp, hs, dims, cache_k_sb[j], cache_v_sb[j], page_table,
                norm_pre_o[j], norm_post_o[j], w_in_o[j], gmlp_w[j], gmlp_b[j], w_out_o[j],
                (tm, tm_proj), group_sb)
            od.append(rest)
    y_prompt = hp.reshape(bsz, seq, d)
    y_sample = _to_batch_major(hs, nb)
    ev_out = [jnp.stack([r[k] for r in ev]) for k in range(6)]
    od_out = [jnp.stack([r[k] for r in od]) for k in range(6)]
    conv_p, conv_s, kmp, vmp, kms, vms = ev_out
    ksp, vsp, kss, vss, gvp, gvs = od_out
    return (y_prompt, y_sample, conv_p, conv_s, kmp, vmp, kms, vms, ksp, vsp, kss, vss, gvp, gvs)
```

```python
import functools

import jax
import jax.numpy as jnp
from jax import lax
from jax.experimental import pallas as pl
from jax.experimental.pallas import tpu as pltpu

F32 = jnp.float32
BF16 = jnp.bfloat16

HEAD_DIM = 64
LANES = 128
BRANCH_W = 512
MOBA_BLOCK = 256
MOBA_TOPK = 3
MOBA_UNROLL = 4
GMLP_CHUNK = 128
CONV_W = 3
RMS_EPS = 1e-6
NEG = -1e30
LOG2E = 1.4426950408889634
Q_SCALE = HEAD_DIM ** -0.5 * LOG2E
SB_TILE = 256
SB_DEAD = -160.0
VMEM_LIMIT = 56 * 1024 * 1024
PAGE_BUFFERS = 2
PAGE_RING = 32
PAGE_GROUP = 8

_NT = (((1,), (1,)), ((), ()))


def _cparams(sem):
    return pltpu.CompilerParams(dimension_semantics=sem, vmem_limit_bytes=VMEM_LIMIT)


def _rms(x, g):
    ms = jnp.mean(x * x, axis=-1, keepdims=True)
    return x * lax.rsqrt(ms + RMS_EPS) * g


def _silu(x):
    return x / (1.0 + jnp.exp(-x))


def _softplus2(z2):
    return jnp.maximum(z2, 0.0) + jnp.log2(1.0 + jnp.exp2(-jnp.abs(z2)))


def _split_bf16(x):
    hi = x.astype(BF16)
    lo = (x - hi.astype(F32)).astype(BF16)
    return hi, lo


def _tri(n):
    r = lax.broadcasted_iota(jnp.int32, (n, n), 0)
    c = lax.broadcasted_iota(jnp.int32, (n, n), 1)
    return jnp.where(r >= c, 1.0, 0.0).astype(BF16)


def _proj_kernel(x_ref, g_ref, w_ref, rest_ref, qk_ref, kt_ref, vt_ref, vtb_ref, keep_ref, xn_ref,
                 *, qkv_first, blk, keep):
    tm = x_ref.shape[0]
    w = BRANCH_W
    keep_tile, keep_row = keep
    xn_ref[...] = _rms(x_ref[...], g_ref[...]).astype(BF16)
    for j in range(w_ref.shape[1] // w):
        r = jnp.dot(xn_ref[...], w_ref[:, j * w:(j + 1) * w], preferred_element_type=F32)
        if j == qkv_first:
            qk_ref[:, 0:w] = (r * Q_SCALE).astype(BF16)
        elif j == qkv_first + 1:
            qk_ref[:, w:2 * w] = r.astype(BF16)
            for c in range(tm // blk):
                kt_ref[:, c * blk:(c + 1) * blk] = r[c * blk:(c + 1) * blk, :].T
        elif j == qkv_first + 2:
            for c in range(tm // blk):
                t = r[c * blk:(c + 1) * blk, :].T
                vt_ref[:, c * blk:(c + 1) * blk] = t
                vtb_ref[c] = t.astype(BF16)
        else:
            jr = j if j < qkv_first else j - 3
            rest_ref[:, jr * w:(jr + 1) * w] = r.astype(BF16)
            if jr == 1:
                @pl.when(pl.program_id(1) == keep_tile)
                def _():
                    keep_ref[...] = r[keep_row:keep_row + keep_ref.shape[0], :]


def _proj(x, g, w_bf, qkv_first, bsz, seq, tm, keep_rows):
    m, d = x.shape
    n = w_bf.shape[1]
    tps = seq // tm
    blk = min(MOBA_BLOCK, tm)
    w = BRANCH_W
    keep_first, keep_n = keep_rows
    keep = (keep_first // tm, keep_first % tm)
    assert keep[1] + keep_n <= tm and keep_n % 8 == 0
    return pl.pallas_call(
        functools.partial(_proj_kernel, qkv_first=qkv_first, blk=blk, keep=keep),
        grid=(bsz, tps),
        in_specs=[
            pl.BlockSpec((tm, d), lambda b, t: (b * tps + t, 0)),
            pl.BlockSpec((1, d), lambda b, t: (0, 0)),
            pl.BlockSpec((d, n), lambda b, t: (0, 0), pipeline_mode=pl.Buffered(1)),
        ],
        out_specs=[
            pl.BlockSpec((tm, n - 3 * w), lambda b, t: (b * tps + t, 0)),
            pl.BlockSpec((tm, 2 * w), lambda b, t: (b * tps + t, 0)),
            pl.BlockSpec((None, w, tm), lambda b, t: (b, 0, t)),
            pl.BlockSpec((None, w, tm), lambda b, t: (b, 0, t)),
            pl.BlockSpec((None, tm // blk, w, blk), lambda b, t: (b, t, 0, 0)),
            pl.BlockSpec((None, keep_n, w), lambda b, t: (b, 0, 0)),
        ],
        out_shape=[
            jax.ShapeDtypeStruct((m, n - 3 * w), BF16),
            jax.ShapeDtypeStruct((m, 2 * w), BF16),
            jax.ShapeDtypeStruct((bsz, w, seq), F32),
            jax.ShapeDtypeStruct((bsz, w, seq), F32),
            jax.ShapeDtypeStruct((bsz, seq // blk, w, blk), BF16),
            jax.ShapeDtypeStruct((bsz, keep_n, w), F32),
        ],
        scratch_shapes=[pltpu.VMEM((tm, d), BF16)],
        compiler_params=_cparams(("parallel", "arbitrary")),
    )(x, g.reshape(1, d), w_bf)


def _moba_prompt_kernel(q_ref, k_ref, vt_ref, o_ref, km_ref, *, nblk):
    _moba_prompt_block(pl.program_id(2), q_ref, k_ref, vt_ref, o_ref, km_ref, nblk)


def _moba_prompt_block(i, q_ref, k_ref, vt_ref, o_ref, km_ref, nblk):
    lane = lax.broadcasted_iota(jnp.int32, (1, LANES), 1)
    low = lane < HEAD_DIM
    nbp = km_ref.shape[0]
    blk = MOBA_BLOCK
    nheads = q_ref.shape[1] // HEAD_DIM

    @pl.when(i == 0)
    def _():
        km_ref[...] = jnp.zeros_like(km_ref)
        for j in range(nblk):
            rows = k_ref[j * blk:(j + 1) * blk, :].astype(F32)
            km_ref[j:j + 1, :] = jnp.sum(rows, axis=0, keepdims=True) * (1.0 / blk)

    brow = lax.broadcasted_iota(jnp.int32, (nbp, blk), 0)
    key_i = lax.broadcasted_iota(jnp.int32, (blk, blk), 0)
    qry_i = lax.broadcasted_iota(jnp.int32, (blk, blk), 1)
    start = pl.multiple_of(i * blk, blk)

    ones_rows = jnp.ones((2 * 8, blk), BF16)

    def v_rows(j, hh):
        return jnp.concatenate([vt_ref[j, hh * HEAD_DIM:(hh + 1) * HEAD_DIM, :], ones_rows], axis=0)

    q_owns, gates, own_scores = [], [], []
    for hh in range(nheads):
        cols = slice((hh // 2) * LANES, (hh // 2 + 1) * LANES)
        q = q_ref[:, cols]
        km = km_ref[:, cols]
        mine = low if hh % 2 == 0 else jnp.logical_not(low)
        km_hi, km_lo = _split_bf16(jnp.where(mine, km, 0.0))
        gates.append(lax.dot_general(km_hi, q, _NT, preferred_element_type=F32)
                     + lax.dot_general(km_lo, q, _NT, preferred_element_type=F32))
        q_own = jnp.where(mine, q, jnp.zeros_like(q))
        q_owns.append(q_own)
        own_scores.append(lax.dot_general(k_ref[pl.ds(start, blk), cols], q_own, _NT,
                                          preferred_element_type=F32))

    q_pasts, init = [], []
    for hh in range(nheads):
        gate = jnp.where(brow < i, gates[hh], -jnp.inf)
        bias = jnp.full((nbp, blk), NEG, F32)
        for _ in range(MOBA_TOPK):
            mx = jnp.max(gate, axis=0, keepdims=True)
            first = jnp.min(jnp.where(gate == mx, brow, nbp), axis=0, keepdims=True)
            pick = (brow == first) & (mx > -jnp.inf)
            bias = jnp.where(pick, 0.0, bias)
            gate = jnp.where(pick, -jnp.inf, gate)
        bias_t = jnp.concatenate([bias, jnp.zeros((LANES - nbp, blk), F32)], axis=0).T
        q_pasts.append(jnp.concatenate([q_owns[hh], bias_t.astype(BF16)], axis=1))

        s = jnp.where(key_i <= qry_i, own_scores[hh], NEG)
        m0 = jnp.max(s, axis=0, keepdims=True)
        p = jnp.exp2(s - m0)
        acc0 = jnp.dot(v_rows(i, hh), p.astype(BF16), preferred_element_type=F32)
        init.append((m0, acc0))

    def stage(js, carry):
        scores = []
        for j in js:
            st = pl.multiple_of(j * blk, blk)
            onehot = jnp.broadcast_to(jnp.where(lane == j, 1.0, 0.0).astype(BF16), (blk, LANES))
            for hh in range(nheads):
                kj = k_ref[pl.ds(st, blk), (hh // 2) * LANES:(hh // 2 + 1) * LANES]
                scores.append(lax.dot_general(jnp.concatenate([kj, onehot], axis=1), q_pasts[hh],
                                              _NT, preferred_element_type=F32))
        carry = list(carry)
        for n, j in enumerate(js):
            for hh in range(nheads):
                m, acc = carry[hh]
                sj = scores[n * nheads + hh]
                m_new = jnp.maximum(m, jnp.max(sj, axis=0, keepdims=True))
                alpha = jnp.exp2(m - m_new)
                pj = jnp.exp2(sj - m_new)
                acc = alpha * acc + jnp.dot(v_rows(j, hh), pj.astype(BF16),
                                            preferred_element_type=F32)
                carry[hh] = (m_new, acc)
        return tuple(carry)

    fin = tuple(init)
    base = 0
    size = MOBA_UNROLL
    while size >= 1:
        trips = (i - base) // size if size == MOBA_UNROLL else ((i - base) // size) % 2
        fin = lax.fori_loop(
            0, trips,
            lambda t, c, base=base, size=size: stage([base + t * size + u for u in range(size)], c),
            fin)
        base = base + trips * size
        size //= 2
    out_t = jnp.concatenate([acc[:HEAD_DIM] / acc[HEAD_DIM:HEAD_DIM + 1] for _, acc in fin],
                            axis=0)
    o_ref[...] = out_t.T


def _attn_prompt_call(kernel_fn, qk, vtb, bsz, seq, scratch, wid):
    nblk = seq // MOBA_BLOCK
    ngrp = BRANCH_W // wid
    return pl.pallas_call(
        kernel_fn,
        grid=(bsz, ngrp, nblk),
        in_specs=[
            pl.BlockSpec((MOBA_BLOCK, wid), lambda b, p, i: (b * nblk + i, p)),
            pl.BlockSpec((seq, wid), lambda b, p, i: (b, ngrp + p)),
            pl.BlockSpec((None, nblk, wid, MOBA_BLOCK), lambda b, p, i: (b, 0, p, 0)),
        ],
        out_specs=pl.BlockSpec((MOBA_BLOCK, wid), lambda b, p, i: (b * nblk + i, p)),
        out_shape=jax.ShapeDtypeStruct((bsz * seq, BRANCH_W), F32),
        scratch_shapes=scratch,
        compiler_params=_cparams(("parallel", "parallel", "arbitrary")),
    )(qk, qk, vtb)


MOBA_STEP_W = 512


def _moba_prompt(qk, vtb, bsz, seq):
    nblk = seq // MOBA_BLOCK
    nbp = -(-nblk // 8) * 8
    return _attn_prompt_call(functools.partial(_moba_prompt_kernel, nblk=nblk), qk, vtb, bsz, seq,
                             [pltpu.VMEM((nbp, MOBA_STEP_W), F32)], MOBA_STEP_W)


def _sb_prompt_kernel(q_ref, k_ref, vt_ref, o_ref):
    i = pl.program_id(2)
    nheads = q_ref.shape[1] // HEAD_DIM
    tile = SB_TILE
    lane = lax.broadcasted_iota(jnp.int32, (1, LANES), 1)
    low = lane < HEAD_DIM
    key_i = lax.broadcasted_iota(jnp.int32, (tile, tile), 0)
    qry_i = lax.broadcasted_iota(jnp.int32, (tile, tile), 1)
    tri_t = jnp.where(qry_i >= key_i, 1.0, 0.0).astype(BF16)
    strict = key_i < qry_i

    q_hs = []
    for hh in range(nheads):
        q = q_ref[:, (hh // 2) * LANES:(hh // 2 + 1) * LANES]
        mine = low if hh % 2 == 0 else jnp.logical_not(low)
        q_hs.append(jnp.where(mine, q, jnp.zeros_like(q)))

    def sweep(j, state, diag):
        st = pl.multiple_of(j * tile, tile)
        zs = [lax.dot_general(k_ref[pl.ds(st, tile), (hh // 2) * LANES:(hh // 2 + 1) * LANES],
                              q_hs[hh], _NT, preferred_element_type=F32)
              for hh in range(nheads)]
        cums = []
        for hh in range(nheads):
            lk = -_softplus2(zs[hh])
            if diag:
                lk = jnp.where(strict, lk, 0.0)
            hi, lo = _split_bf16(lk)
            cums.append(jnp.dot(tri_t, hi, preferred_element_type=F32)
                        + jnp.dot(tri_t, lo, preferred_element_type=F32) + state[hh][0])
        new = []
        for hh in range(nheads):
            w = jnp.exp2(zs[hh] + cums[hh])
            if diag:
                w = jnp.where(strict, w, 0.0)
            acc = state[hh][1] + jnp.dot(vt_ref[j, hh * HEAD_DIM:(hh + 1) * HEAD_DIM, :],
                                         w.astype(BF16), preferred_element_type=F32)
            new.append((cums[hh][0:1, :], acc))
        return tuple(new)

    def live_of(state):
        live = jnp.max(state[0][0])
        for hh in range(1, nheads):
            live = jnp.maximum(live, jnp.max(state[hh][0]))
        return live

    def cond(st):
        j, live, _ = st
        return (j >= 0) & (live > SB_DEAD)

    def body(st):
        j, _, state = st
        new = sweep(j, state, False)
        return j - 1, live_of(new), new

    zero = tuple((jnp.zeros((1, tile), F32), jnp.zeros((HEAD_DIM, tile), F32))
                 for _ in range(nheads))
    init = sweep(i, zero, True)
    _, _, fin = lax.while_loop(cond, body, (i - 1, live_of(init), init))
    o_ref[...] = jnp.concatenate([acc for _, acc in fin], axis=0).T


SB_STEP_W = 512


def _sb_prompt(qk, vtb, bsz, seq):
    return _attn_prompt_call(_sb_prompt_kernel, qk, vtb, bsz, seq, [], SB_STEP_W)


def _query_rows(qrep_ref, nheads):
    nrow, width = qrep_ref.shape[1], qrep_ref.shape[2]
    row_w = lax.broadcasted_iota(jnp.int32, (nrow, width), 0)
    lane_w = lax.broadcasted_iota(jnp.int32, (nrow, width), 1)
    head_mask = (lane_w // HEAD_DIM) == (row_w % nheads)
    qrows = jnp.where(head_mask, qrep_ref[0], jnp.zeros_like(qrep_ref[0]))
    return qrows, head_mask


def _pick_heads(y, head_mask, nheads):
    nrow, width = y.shape
    y = jnp.where(head_mask, y, 0.0)
    return jnp.sum(y.reshape(nrow // nheads, nheads, width), axis=1)


def _moba_sample_weights(qrows, knew_ref, s_ref, p_ref, nheads):
    nblk, nrow = s_ref.shape[0], s_ref.shape[1]
    row = lax.broadcasted_iota(jnp.int32, (nrow, LANES), 0)
    lane = lax.broadcasted_iota(jnp.int32, (nrow, LANES), 1)
    qidx = row // nheads
    zn = lax.dot_general(qrows, knew_ref[0], _NT, preferred_element_type=F32)
    gates = jnp.zeros((nrow, LANES), F32)
    for bl in range(nblk):
        gates = jnp.where(lane == bl, jnp.sum(s_ref[bl], axis=1, keepdims=True), gates)
    gates = jnp.where(lane < nblk, gates, -jnp.inf)
    sel = jnp.zeros((nrow, LANES), F32)
    for _ in range(MOBA_TOPK):
        mx = jnp.max(gates, axis=1, keepdims=True)
        first = jnp.min(jnp.where(gates == mx, lane, LANES), axis=1, keepdims=True)
        pick = (lane == first) & (mx > -jnp.inf)
        sel = jnp.where(pick, 1.0, sel)
        gates = jnp.where(pick, -jnp.inf, gates)
    zn = jnp.where(lane <= qidx, zn, NEG)
    m = jnp.max(zn, axis=1, keepdims=True)
    cols = []
    for bl in range(nblk):
        col = jnp.max(jnp.where(lane == bl, sel, 0.0), axis=1, keepdims=True) > 0.5
        cols.append(col)
        m = jnp.maximum(m, jnp.max(jnp.where(col, s_ref[bl], NEG), axis=1, keepdims=True))
    pn = jnp.exp2(zn - m)
    l = jnp.sum(pn, axis=1, keepdims=True)
    for bl in range(nblk):
        p = jnp.exp2(jnp.where(cols[bl], s_ref[bl], NEG) - m)
        l = l + jnp.sum(p, axis=1, keepdims=True)
        p_ref[bl] = p.astype(BF16)
    return pn, l


def _moba_paged_kernel(pt_ref, qrep_ref, knew_ref, vnew_ref, k_hbm, v_hbm, o_ref,
                       kbuf, vbuf, ksem, vsem, s_ref, p_ref, *, npages, nheads):
    b = pl.program_id(0)
    ring = kbuf.shape[0]
    page = kbuf.shape[2]
    ppb = MOBA_BLOCK // page
    qrows, head_mask = _query_rows(qrep_ref, nheads)

    def k_copy(seq, p, slot):
        return pltpu.make_async_copy(k_hbm.at[pt_ref[seq * npages + p]], kbuf.at[slot], ksem.at[slot])

    def v_copy(p, slot):
        return pltpu.make_async_copy(v_hbm.at[pt_ref[b * npages + p]], vbuf.at[slot], vsem.at[slot])

    @pl.when(b == 0)
    def _():
        for p in range(ring):
            k_copy(0, p, p).start()

    grp = min(PAGE_GROUP, ring)

    def k_group(gi, carry):
        base = gi * grp
        slot0 = (gi % (ring // grp)) * grp
        for u in range(grp):
            k_copy(b, base + u, slot0 + u).wait()
        for u in range(grp):
            s_ref[gi * (grp // ppb) + u // ppb, :, (u % ppb) * page:(u % ppb + 1) * page] = jnp.dot(
                qrows, kbuf[slot0 + u].astype(BF16), preferred_element_type=F32)

        @pl.when(base + ring < npages)
        def _():
            for u in range(grp):
                k_copy(b, base + ring + u, slot0 + u).start()
        return carry

    lax.fori_loop(0, npages // grp, k_group, 0)
    for p in range(ring):
        v_copy(p, p).start()

    @pl.when(b + 1 < pl.num_programs(0))
    def _():
        for p in range(ring):
            k_copy(b + 1, p, p).start()

    pn, l = _moba_sample_weights(qrows, knew_ref, s_ref, p_ref, nheads)
    acc0 = jnp.dot(pn.astype(BF16), vnew_ref[0], preferred_element_type=F32)

    def v_group(gi, acc):
        base = gi * grp
        slot0 = (gi % (ring // grp)) * grp
        for u in range(grp):
            v_copy(base + u, slot0 + u).wait()
        for u in range(grp):
            pw = p_ref[gi * (grp // ppb) + u // ppb, :, (u % ppb) * page:(u % ppb + 1) * page]
            acc = acc + lax.dot_general(pw, vbuf[slot0 + u].astype(BF16), _NT,
                                        preferred_element_type=F32)

        @pl.when(base + ring < npages)
        def _():
            for u in range(grp):
                v_copy(base + ring + u, slot0 + u).start()
        return acc

    acc = lax.fori_loop(0, npages // grp, v_group, acc0)
    o_ref[0] = _pick_heads(acc / l, head_mask, nheads)


def _paged_operands(q_bf, knew_bf, vnew_bf, cache_k, cache_v, page_table):
    nb, nq, width = q_bf.shape
    nheads = width // HEAD_DIM
    nphys, page = cache_k.shape[0], cache_k.shape[1]
    ck = cache_k.transpose(0, 2, 3, 1).reshape(nphys, width, page)
    cv = cache_v.transpose(0, 2, 3, 1).reshape(nphys, width, page)
    qrep = jnp.repeat(q_bf, nheads, axis=1)
    pad = ((0, 0), (0, LANES - nq), (0, 0))
    knew = jnp.pad(knew_bf, pad)
    vnew = jnp.pad(vnew_bf, pad)
    pt = page_table.reshape(-1).astype(jnp.int32)
    return qrep, knew, vnew, ck, cv, pt, nheads, page


def _moba_paged(q_bf, knew_bf, vnew_bf, cache_k, cache_v, page_table):
    nb, nq, width = q_bf.shape
    qrep, knew, vnew, ck, cv, pt, nheads, page = _paged_operands(
        q_bf, knew_bf, vnew_bf, cache_k, cache_v, page_table)
    nrow = nq * nheads
    npages = page_table.shape[1]
    ppb = MOBA_BLOCK // page
    ring = min(PAGE_RING, npages)
    grp = min(PAGE_GROUP, ring)
    assert grp % ppb == 0 and ring % grp == 0 and npages % ring == 0
    per_b = lambda b, pt: (b, 0, 0)
    grid_spec = pltpu.PrefetchScalarGridSpec(
        num_scalar_prefetch=1,
        grid=(nb,),
        in_specs=[
            pl.BlockSpec((1, nrow, width), per_b),
            pl.BlockSpec((1, LANES, width), per_b),
            pl.BlockSpec((1, LANES, width), per_b),
            pl.BlockSpec(memory_space=pl.ANY),
            pl.BlockSpec(memory_space=pl.ANY),
        ],
        out_specs=pl.BlockSpec((1, nq, width), per_b),
        scratch_shapes=[
            pltpu.VMEM((ring, width, page), F32),
            pltpu.VMEM((ring, width, page), F32),
            pltpu.SemaphoreType.DMA((ring,)),
            pltpu.SemaphoreType.DMA((ring,)),
            pltpu.VMEM((npages // ppb, nrow, MOBA_BLOCK), F32),
            pltpu.VMEM((npages // ppb, nrow, MOBA_BLOCK), BF16),
        ],
    )
    return pl.pallas_call(
        functools.partial(_moba_paged_kernel, npages=npages, nheads=nheads),
        grid_spec=grid_spec,
        out_shape=jax.ShapeDtypeStruct((nb, nq, width), F32),
        compiler_params=_cparams(("arbitrary",)),
    )(pt, qrep, knew, vnew, ck, cv)


def _moba_fused_kernel(pt_ref, q_ref, k_ref, vt_ref, qrep_ref, knew_ref, vnew_ref, kc_hbm, vc_hbm,
                       o_ref, os_ref, km_ref, pbuf, psem, s_ref, p_ref, *, nblk, npages, nheads):
    i = pl.program_id(2)
    step = pl.program_id(0) * nblk + i
    seq = step // 2
    page = pbuf.shape[2]
    ppb = MOBA_BLOCK // page
    grp = min(PAGE_GROUP, npages)

    def page_copy(cache_hbm, sq, p):
        return pltpu.make_async_copy(cache_hbm.at[pt_ref[sq * npages + p]], pbuf.at[p], psem.at[p])

    @pl.when(step == 0)
    def _():
        for p in range(npages):
            page_copy(kc_hbm, 0, p).start()

    _moba_prompt_block(i, q_ref, k_ref, vt_ref, o_ref, km_ref, nblk)

    qrows, head_mask = _query_rows(qrep_ref, nheads)

    @pl.when(step % 2 == 0)
    def _():
        for p in range(npages):
            page_copy(kc_hbm, seq, p).wait()

        def k_group(gi, carry):
            for u in range(grp):
                s_ref[gi * (grp // ppb) + u // ppb, :, (u % ppb) * page:(u % ppb + 1) * page] = (
                    jnp.dot(qrows, pbuf[gi * grp + u].astype(BF16), preferred_element_type=F32))
            return carry

        lax.fori_loop(0, npages // grp, k_group, 0)
        for p in range(npages):
            page_copy(vc_hbm, seq, p).start()

    @pl.when(step % 2 == 1)
    def _():
        pn, l = _moba_sample_weights(qrows, knew_ref, s_ref, p_ref, nheads)
        acc0 = jnp.dot(pn.astype(BF16), vnew_ref[0], preferred_element_type=F32)
        for p in range(npages):
            page_copy(vc_hbm, seq, p).wait()

        def v_group(gi, acc):
            for u in range(grp):
                pw = p_ref[gi * (grp // ppb) + u // ppb, :, (u % ppb) * page:(u % ppb + 1) * page]
                acc = acc + lax.dot_general(pw, pbuf[gi * grp + u].astype(BF16), _NT,
                                            preferred_element_type=F32)
            return acc

        acc = lax.fori_loop(0, npages // grp, v_group, acc0)
        os_ref[0] = _pick_heads(acc / l, head_mask, nheads)

        @pl.when(step + 1 < pl.num_programs(0) * nblk)
        def _():
            for p in range(npages):
                page_copy(kc_hbm, seq + 1, p).start()


def _moba_fused(qk, vtb, bsz, seq, q_bf, knew_bf, vnew_bf, cache_k, cache_v, page_table):
    nb, nq, width = q_bf.shape
    qrep, knew, vnew, ck, cv, pt, nheads, page = _paged_operands(
        q_bf, knew_bf, vnew_bf, cache_k, cache_v, page_table)
    nrow = nq * nheads
    npages = page_table.shape[1]
    nblk = seq // MOBA_BLOCK
    nbp = -(-nblk // 8) * 8
    wid = MOBA_STEP_W
    ppb = MOBA_BLOCK // page
    grp = min(PAGE_GROUP, npages)
    assert bsz * nblk == 2 * nb and wid == BRANCH_W and grp % ppb == 0 and npages % grp == 0
    per_seq = lambda b, p, i, pt: ((b * nblk + i) // 2, 0, 0)
    grid_spec = pltpu.PrefetchScalarGridSpec(
        num_scalar_prefetch=1,
        grid=(bsz, 1, nblk),
        in_specs=[
            pl.BlockSpec((MOBA_BLOCK, wid), lambda b, p, i, pt: (b * nblk + i, 0)),
            pl.BlockSpec((seq, wid), lambda b, p, i, pt: (b, 1)),
            pl.BlockSpec((None, nblk, wid, MOBA_BLOCK), lambda b, p, i, pt: (b, 0, 0, 0)),
            pl.BlockSpec((1, nrow, width), per_seq),
            pl.BlockSpec((1, LANES, width), per_seq),
            pl.BlockSpec((1, LANES, width), per_seq),
            pl.BlockSpec(memory_space=pl.ANY),
            pl.BlockSpec(memory_space=pl.ANY),
        ],
        out_specs=[
            pl.BlockSpec((MOBA_BLOCK, wid), lambda b, p, i, pt: (b * nblk + i, 0)),
            pl.BlockSpec((1, nq, width), per_seq),
        ],
        scratch_shapes=[
            pltpu.VMEM((nbp, wid), F32),
            pltpu.VMEM((npages, width, page), F32),
            pltpu.SemaphoreType.DMA((npages,)),
            pltpu.VMEM((npages // ppb, nrow, MOBA_BLOCK), F32),
            pltpu.VMEM((npages // ppb, nrow, MOBA_BLOCK), BF16),
        ],
    )
    return pl.pallas_call(
        functools.partial(_moba_fused_kernel, nblk=nblk, npages=npages, nheads=nheads),
        grid_spec=grid_spec,
        out_shape=[jax.ShapeDtypeStruct((bsz * seq, BRANCH_W), F32),
                   jax.ShapeDtypeStruct((nb, nq, width), F32)],
        compiler_params=_cparams(("arbitrary", "arbitrary", "arbitrary")),
    )(pt, qk, qk, vtb, qrep, knew, vnew, ck, cv)


def _sb_paged_kernel(pt_ref, live_ref, qrep_ref, knew_ref, vnew_ref, cin_ref, ain_ref, *rest,
                     group, nheads, first):
    del pt_ref
    k_refs = rest[:group]
    v_refs = rest[group:2 * group]
    y_ref, acc_ref, carry_ref = rest[2 * group:2 * group + 3]
    b = pl.program_id(0)
    s = pl.program_id(1)
    nrow = qrep_ref.shape[1]
    page = k_refs[0].shape[1]
    ppt = SB_TILE // page
    qrows, head_mask = _query_rows(qrep_ref, nheads)
    tri = _tri(SB_TILE)

    @pl.when(s == 0)
    def _():
        if first:
            row = lax.broadcasted_iota(jnp.int32, (nrow, LANES), 0)
            lane = lax.broadcasted_iota(jnp.int32, (nrow, LANES), 1)
            strict = lane < row // nheads
            zn = lax.dot_general(qrows, knew_ref[0], _NT, preferred_element_type=F32)
            hi, lo = _split_bf16(jnp.where(strict, -_softplus2(zn), 0.0))
            tri_n = _tri(LANES)
            cum = (jnp.dot(hi, tri_n, preferred_element_type=F32)
                   + jnp.dot(lo, tri_n, preferred_element_type=F32))
            wn = jnp.where(strict, jnp.exp2(zn + cum), 0.0)
            acc_ref[0] = jnp.dot(wn.astype(BF16), vnew_ref[0], preferred_element_type=F32)
            carry_ref[0] = jnp.broadcast_to(cum[:, 0:1], carry_ref.shape[1:])
        else:
            acc_ref[0] = ain_ref[0]
            carry_ref[0] = cin_ref[0]

    @pl.when(live_ref[b] > 0)
    def _():
        carry = carry_ref[0][:, 0:1]
        acc = acc_ref[0]
        for t in reversed(range(group // ppt)):
            kt = jnp.concatenate([k_refs[t * ppt + u][...] for u in range(ppt)], axis=1).astype(BF16)
            vt = jnp.concatenate([v_refs[t * ppt + u][...] for u in range(ppt)], axis=1).astype(BF16)
            z = jnp.dot(qrows, kt, preferred_element_type=F32)
            hi, lo = _split_bf16(-_softplus2(z))
            cum = (jnp.dot(hi, tri, preferred_element_type=F32)
                   + jnp.dot(lo, tri, preferred_element_type=F32)) + carry
            w = jnp.exp2(z + cum).astype(BF16)
            acc = acc + lax.dot_general(w, vt, _NT, preferred_element_type=F32)
            carry = cum[:, 0:1]
        acc_ref[0] = acc
        carry_ref[0] = jnp.broadcast_to(carry, carry_ref.shape[1:])

    @pl.when(s == pl.num_programs(1) - 1)
    def _():
        y_ref[0] = _pick_heads(acc_ref[0], head_mask, nheads)


def _sb_paged_call(first, pt, live, qrep, knew, vnew, carry_in, acc_in, ck, cv,
                   nq, npages, group, chunk_lo, nsteps, nheads):
    nb, nrow, width = qrep.shape
    page = ck.shape[2]

    def page_map(g):
        def index(b, s, pt, live):
            chunk_page = (chunk_lo + nsteps - 1 - s) * group + g
            return (jnp.where(live[b] > 0, pt[b * npages + chunk_page], 0), 0, 0)
        return index

    per_b = lambda b, s, pt, live: (b, 0, 0)
    in_specs = [
        pl.BlockSpec((1, nrow, width), per_b),
        pl.BlockSpec((1, LANES, width), per_b),
        pl.BlockSpec((1, LANES, width), per_b),
        pl.BlockSpec((1, nrow, LANES), per_b),
        pl.BlockSpec((1, nrow, width), per_b),
    ]
    in_specs += [pl.BlockSpec((None, width, page), page_map(g), pipeline_mode=pl.Buffered(PAGE_BUFFERS))
                 for g in range(group)] * 2
    grid_spec = pltpu.PrefetchScalarGridSpec(
        num_scalar_prefetch=2,
        grid=(nb, nsteps),
        in_specs=in_specs,
        out_specs=[pl.BlockSpec((1, nq, width), per_b),
                   pl.BlockSpec((1, nrow, width), per_b),
                   pl.BlockSpec((1, nrow, LANES), per_b)],
    )
    return pl.pallas_call(
        functools.partial(_sb_paged_kernel, group=group, nheads=nheads, first=first),
        grid_spec=grid_spec,
        out_shape=[jax.ShapeDtypeStruct((nb, nq, width), F32),
                   jax.ShapeDtypeStruct((nb, nrow, width), F32),
                   jax.ShapeDtypeStruct((nb, nrow, LANES), F32)],
        compiler_params=_cparams(("parallel", "arbitrary")),
    )(pt, live, qrep, knew, vnew, carry_in, acc_in, *([ck] * group), *([cv] * group))


def _sb_paged(q_bf, knew_bf, vnew_bf, cache_k, cache_v, page_table, group):
    nb, nq, width = q_bf.shape
    qrep, knew, vnew, ck, cv, pt, nheads, page = _paged_operands(
        q_bf, knew_bf, vnew_bf, cache_k, cache_v, page_table)
    nrow = nq * nheads
    npages = page_table.shape[1]
    nchunk = npages // group
    zc = jnp.zeros((nb, nrow, LANES), F32)
    za = jnp.zeros((nb, nrow, width), F32)
    all_live = jnp.ones((nb,), jnp.int32)
    y, acc, carry = _sb_paged_call(True, pt, all_live, qrep, knew, vnew, zc, za, ck, cv,
                                   nq, npages, group, nchunk - 1, 1, nheads)
    if nchunk == 1:
        return y
    live = (jnp.max(carry, axis=(1, 2)) > SB_DEAD).astype(jnp.int32)

    def older(_):
        return _sb_paged_call(False, pt, live, qrep, knew, vnew, carry, acc, ck, cv,
                              nq, npages, group, 0, nchunk - 1, nheads)[0]

    return lax.cond(jnp.any(live > 0), older, lambda _: y, None)


def _finish(y_first, y_second, w_ref, g_ref, x_ref, o_ref):
    half = y_first.shape[1]
    out = (jnp.dot(y_first.astype(BF16), w_ref[0:half, :], preferred_element_type=F32)
           + jnp.dot(y_second.astype(BF16), w_ref[half:2 * half, :], preferred_element_type=F32))
    o_ref[...] = x_ref[...] + _rms(out, g_ref[...])


def _even_out_prompt_kernel(ab_ref, ac_ref, ah_ref, az_ref, bz_ref, hc_ref, hh_ref, yb_ref, x_ref,
                            w_ref, cw_ref, g_ref, o_ref, tail_ref, ue_ref, *, tiles_per_seq):
    i = pl.program_id(0)
    tm = ab_ref.shape[0]
    hrows = hc_ref.shape[0]
    u = ac_ref[...].astype(F32) * ah_ref[...].astype(F32)
    first = (i % tiles_per_seq) == 0
    halo = hc_ref[...].astype(F32) * hh_ref[...].astype(F32)
    ue_ref[0:8, :] = jnp.where(first, 0.0, halo[hrows - 8:hrows, :])
    ue_ref[8:8 + tm, :] = u
    conv = (ue_ref[6:6 + tm, :] * cw_ref[0:1, :] + ue_ref[7:7 + tm, :] * cw_ref[1:2, :]
            + u * cw_ref[2:3, :])
    y_a = ab_ref[...].astype(F32) * conv * _silu(az_ref[...].astype(F32))
    y_b = yb_ref[...] * _silu(bz_ref[...].astype(F32))
    tail_ref[...] = u[tm - 8:tm, :]
    _finish(y_a, y_b, w_ref, g_ref, x_ref, o_ref)


def _even_out_prompt(rest, y_b, x, w_out_bf, conv_w, g_post, seq, tm):
    m, d = x.shape
    nt = m // tm
    col = lambda c: pl.BlockSpec((tm, BRANCH_W), lambda i, c=c: (i, c))
    hrows = 16
    halo = lambda c: pl.BlockSpec((hrows, BRANCH_W),
                                  lambda i, c=c: (jnp.maximum(i * (tm // hrows) - 1, 0), c))
    return pl.pallas_call(
        functools.partial(_even_out_prompt_kernel, tiles_per_seq=seq // tm),
        grid=(nt,),
        in_specs=[col(0), col(1), col(2), col(3), col(4), halo(1), halo(2),
                  pl.BlockSpec((tm, BRANCH_W), lambda i: (i, 0)),
                  pl.BlockSpec((tm, d), lambda i: (i, 0)),
                  pl.BlockSpec(w_out_bf.shape, lambda i: (0, 0)),
                  pl.BlockSpec(conv_w.shape, lambda i: (0, 0)),
                  pl.BlockSpec((1, d), lambda i: (0, 0))],
        out_specs=[pl.BlockSpec((tm, d), lambda i: (i, 0)),
                   pl.BlockSpec((8, BRANCH_W), lambda i: (i, 0))],
        out_shape=[jax.ShapeDtypeStruct((m, d), F32),
                   jax.ShapeDtypeStruct((nt * 8, BRANCH_W), F32)],
        scratch_shapes=[pltpu.VMEM((tm + 8, BRANCH_W), F32)],
        compiler_params=_cparams(("parallel",)),
    )(rest, rest, rest, rest, rest, rest, rest, y_b, x, w_out_bf, conv_w, g_post.reshape(1, d))


def _even_out_sample_kernel(rest_ref, st_ref, yb_ref, x_ref, w_ref, cw_ref, g_ref,
                            o_ref, tail_ref, *, nb):
    w = BRANCH_W
    rows = rest_ref.shape[0]
    rest = rest_ref[...].astype(F32)
    u = rest[:, w:2 * w] * rest[:, 2 * w:3 * w]
    ue = jnp.concatenate([st_ref[...], u], axis=0)
    conv = (ue[0:rows] * cw_ref[0:1, :] + ue[nb:nb + rows] * cw_ref[1:2, :]
            + ue[2 * nb:2 * nb + rows] * cw_ref[2:3, :])
    y_a = rest[:, 0:w] * conv * _silu(rest[:, 3 * w:4 * w])
    y_b = yb_ref[...] * _silu(rest[:, 4 * w:5 * w])
    tail_ref[...] = ue[rows:rows + 2 * nb]
    _finish(y_a, y_b, w_ref, g_ref, x_ref, o_ref)


def _even_out_sample(rest, state_tm, y_b, x, w_out_bf, conv_w, g_post, nb):
    m, d = x.shape
    return pl.pallas_call(
        functools.partial(_even_out_sample_kernel, nb=nb),
        out_shape=[jax.ShapeDtypeStruct((m, d), F32),
                   jax.ShapeDtypeStruct(((CONV_W - 1) * nb, BRANCH_W), F32)],
        compiler_params=pltpu.CompilerParams(vmem_limit_bytes=VMEM_LIMIT),
    )(rest, state_tm, y_b, x, w_out_bf, conv_w, g_post.reshape(1, d))


def _gmlp_mix(v, gw_ref, low):
    r_i = lax.broadcasted_iota(jnp.int32, (GMLP_CHUNK, GMLP_CHUNK), 0)
    c_i = lax.broadcasted_iota(jnp.int32, (GMLP_CHUNK, GMLP_CHUNK), 1)
    tril = c_i <= r_i
    parts = []
    for p in range(v.shape[1] // LANES):
        vp = v[:, p * LANES:(p + 1) * LANES]
        v_lo = jnp.where(low, vp, jnp.zeros_like(vp)).astype(BF16)
        v_hi = jnp.where(low, jnp.zeros_like(vp), vp).astype(BF16)
        w_lo = jnp.where(tril, gw_ref[2 * p], 0.0).astype(BF16)
        w_hi = jnp.where(tril, gw_ref[2 * p + 1], 0.0).astype(BF16)
        parts.append(jnp.dot(w_lo, v_lo, preferred_element_type=F32)
                     + jnp.dot(w_hi, v_hi, preferred_element_type=F32))
    return jnp.concatenate(parts, axis=1)


def _odd_out_prompt_kernel(cu_ref, cv_ref, cz_ref, dz_ref, yd_ref, x_ref, w_ref, gw_ref, gb_ref,
                           g_ref, o_ref, yc_ref):
    tm = cu_ref.shape[0]
    low = lax.broadcasted_iota(jnp.int32, (1, LANES), 1) < HEAD_DIM
    for c in range(tm // GMLP_CHUNK):
        rows = slice(c * GMLP_CHUNK, (c + 1) * GMLP_CHUNK)
        mixed = _gmlp_mix(cv_ref[rows, :], gw_ref, low) + gb_ref[...]
        yc_ref[rows, :] = (cu_ref[rows, :].astype(F32) * mixed
                           * _silu(cz_ref[rows, :].astype(F32)))
    y_d = yd_ref[...] * _silu(dz_ref[...].astype(F32))
    _finish(yc_ref[...], y_d, w_ref, g_ref, x_ref, o_ref)


def _odd_out_prompt(rest, y_d, x, w_out_bf, gmlp_w, gb_full, g_post, tm):
    m, d = x.shape
    col = lambda c: pl.BlockSpec((tm, BRANCH_W), lambda i, c=c: (i, c))
    return pl.pallas_call(
        _odd_out_prompt_kernel,
        grid=(m // tm,),
        in_specs=[col(0), col(1), col(2), col(3),
                  pl.BlockSpec((tm, BRANCH_W), lambda i: (i, 0)),
                  pl.BlockSpec((tm, d), lambda i: (i, 0)),
                  pl.BlockSpec(w_out_bf.shape, lambda i: (0, 0)),
                  pl.BlockSpec(gmlp_w.shape, lambda i: (0, 0, 0)),
                  pl.BlockSpec(gb_full.shape, lambda i: (0, 0)),
                  pl.BlockSpec((1, d), lambda i: (0, 0))],
        out_specs=pl.BlockSpec((tm, d), lambda i: (i, 0)),
        out_shape=jax.ShapeDtypeStruct((m, d), F32),
        scratch_shapes=[pltpu.VMEM((tm, BRANCH_W), F32)],
        compiler_params=_cparams(("parallel",)),
    )(rest, rest, rest, rest, y_d, x, w_out_bf, gmlp_w, gb_full, g_post.reshape(1, d))


def _odd_out_sample_kernel(rest_ref, yd_ref, x_ref, w_ref, w4_ref, b4_ref, g_ref, o_ref, *, nb, nq):
    w = BRANCH_W
    rest = rest_ref[...].astype(F32)
    parts = []
    for t in range(nq):
        mixed = jnp.broadcast_to(b4_ref[t:t + 1, :], (nb, w))
        for s in range(t + 1):
            mixed = mixed + w4_ref[t * nq + s:t * nq + s + 1, :] * rest[s * nb:(s + 1) * nb, w:2 * w]
        parts.append(mixed)
    mixed = jnp.concatenate(parts, axis=0)
    y_c = rest[:, 0:w] * mixed * _silu(rest[:, 2 * w:3 * w])
    y_d = yd_ref[...] * _silu(rest[:, 3 * w:4 * w])
    _finish(y_c, y_d, w_ref, g_ref, x_ref, o_ref)


def _odd_out_sample(rest, y_d, x, w_out_bf, w4, b4, g_post, nb, nq):
    m, d = x.shape
    return pl.pallas_call(
        functools.partial(_odd_out_sample_kernel, nb=nb, nq=nq),
        out_shape=jax.ShapeDtypeStruct((m, d), F32),
        compiler_params=pltpu.CompilerParams(vmem_limit_bytes=VMEM_LIMIT),
    )(rest, y_d, x, w_out_bf, w4, b4, g_post.reshape(1, d))


def _to_token_major(a):
    nb, nq, w = a.shape
    return a.transpose(1, 0, 2).reshape(nq * nb, w)


def _to_batch_major(a, nb):
    w = a.shape[1]
    return a.reshape(-1, nb, w).transpose(1, 0, 2)


def _heads_prompt(t, heads):
    bsz, w, seq = t.shape
    return t.reshape(bsz, heads, w // heads, seq).transpose(0, 3, 1, 2)


def _project_both(hp, hs, dims, g_pre, w_in, qkv_first, tm):
    bsz, seq, nb, nq = dims
    w_in_bf = w_in.astype(BF16)
    open_start = ((seq - 1) // GMLP_CHUNK) * GMLP_CHUNK
    prompt = _proj(hp, g_pre, w_in_bf, qkv_first, bsz, seq, tm, (open_start, seq - open_start))
    ns = nq * nb
    rows = max(ns, tm)
    hs_pad = jnp.pad(hs, ((0, rows - ns), (0, 0)))
    rest_s, qk_s, kt_s, vt_s, _, keep_s = _proj(hs_pad, g_pre, w_in_bf, qkv_first, 1, rows, rows,
                                                (0, ns))
    rest_s = rest_s[:ns]
    k_s = _to_batch_major(kt_s[0, :, :ns].T, nb)
    v_s = _to_batch_major(vt_s[0, :, :ns].T, nb)
    q_s = _to_batch_major(qk_s[:ns, :BRANCH_W], nb)
    return prompt, rest_s, q_s, k_s, v_s, keep_s[0]


def _even_layer(hp, hs, dims, state_conv, cache_k, cache_v, page_table,
                g_pre, g_post, w_in, conv_w, w_out, tiles):
    bsz, seq, nb, nq = dims
    tm, tm_proj = tiles
    w = BRANCH_W
    heads = w // HEAD_DIM
    w_out_bf = w_out.astype(BF16)
    (rest_p, qk_p, kt_p, vt_p, vtb_p, _), rest_s, q_s, k_s, v_s, _ = _project_both(
        hp, hs, dims, g_pre, w_in, 4, tm_proj)
    sample_qkv = (q_s, k_s.astype(BF16), v_s.astype(BF16))
    if bsz * (seq // MOBA_BLOCK) == 2 * nb:
        yb_p, yb_s = _moba_fused(qk_p, vtb_p, bsz, seq, *sample_qkv, cache_k, cache_v, page_table)
    else:
        yb_p = _moba_prompt(qk_p, vtb_p, bsz, seq)
        yb_s = _moba_paged(*sample_qkv, cache_k, cache_v, page_table)
    yb_s = _to_token_major(yb_s)
    hp, tail_p = _even_out_prompt(rest_p, yb_p, hp, w_out_bf, conv_w, g_post, seq, tm)
    state_tm = _to_token_major(state_conv)
    hs, tail_s = _even_out_sample(rest_s, state_tm, yb_s, hs, w_out_bf, conv_w, g_post, nb)
    conv_p = tail_p.reshape(bsz, seq // tm, 8, w)[:, -1, 8 - (CONV_W - 1):, :]
    conv_s = _to_batch_major(tail_s, nb)
    return (hp, hs, conv_p, conv_s, _heads_prompt(kt_p, heads), _heads_prompt(vt_p, heads),
            k_s.reshape(nb, nq, heads, HEAD_DIM), v_s.reshape(nb, nq, heads, HEAD_DIM))


def _odd_layer(hp, hs, dims, cache_k, cache_v, page_table,
               g_pre, g_post, w_in, gmlp_w, gmlp_b, w_out, tiles, group):
    bsz, seq, nb, nq = dims
    tm, tm_proj = tiles
    w = BRANCH_W
    heads = w // HEAD_DIM
    w_out_bf = w_out.astype(BF16)
    ngroups = gmlp_w.shape[0]
    cpg = w // ngroups
    (rest_p, qk_p, kt_p, vt_p, vtb_p, gv_p), rest_s, q_s, k_s, v_s, keep_s = _project_both(
        hp, hs, dims, g_pre, w_in, 3, tm_proj)
    yd_p = _sb_prompt(qk_p, vtb_p, bsz, seq)
    yd_s = _to_token_major(_sb_paged(q_s, k_s.astype(BF16), v_s.astype(BF16),
                                     cache_k, cache_v, page_table, group))
    gb_full = jnp.repeat(gmlp_b.T, cpg, axis=1)
    hp = _odd_out_prompt(rest_p, yd_p, hp, w_out_bf, gmlp_w, gb_full, g_post, tm)
    w4 = jnp.repeat(gmlp_w[:, :nq, :nq].transpose(1, 2, 0), cpg, axis=2).reshape(nq * nq, w)
    b4 = jnp.repeat(gmlp_b[:, :nq].T, cpg, axis=1)
    hs = _odd_out_sample(rest_s, yd_s, hs, w_out_bf, w4, b4, g_post, nb, nq)
    gv_s = _to_batch_major(keep_s, nb)
    return (hp, hs, _heads_prompt(kt_p, heads), _heads_prompt(vt_p, heads),
            k_s.reshape(nb, nq, heads, HEAD_DIM), v_s.reshape(nb, nq, heads, HEAD_DIM), gv_p, gv_s)


def kernel(x_prompt, x_sample, state_conv, cache_k_moba, cache_v_moba, cache_k_sb, cache_v_sb,
           page_table, norm_pre_e, norm_post_e, w_in_e, conv_w, w_out_e,
           norm_pre_o, norm_post_o, w_in_o, gmlp_w, gmlp_b, w_out_o):
    bsz, seq, d = x_prompt.shape
    nb, nq, _ = x_sample.shape
    depth = norm_pre_e.shape[0] + norm_pre_o.shape[0]
    npages, page = page_table.shape[1], cache_k_moba.shape[2]
    assert seq % MOBA_BLOCK == 0 and (npages * page) % MOBA_BLOCK == 0 and nq <= GMLP_CHUNK
    assert (nq - 1) // MOBA_BLOCK == 0 and w_in_e.shape[2] == 8 * BRANCH_W and w_in_o.shape[2] == 7 * BRANCH_W
    dims = (bsz, seq, nb, nq)
    tm = min(512, seq)
    tm_proj = min(512, seq)
    group_sb = min(4, npages)
    hp = x_prompt.reshape(bsz * seq, d)
    hs = _to_token_major(x_sample)
    ev, od = [], []
    for i in range(depth):
        j = i // 2
        if i % 2 == 0:
            hp, hs, *rest = _even_layer(
                hp, hs, dims, state_conv[j], cache_k_moba[j], cache_v_moba[j], page_table,
                norm_pre_e[j], norm_post_e[j], w_in_e[j], conv_w[j], w_out_e[j],
                (tm, tm_proj))
            ev.append(rest)
        else:
            hp, hs, *rest = _odd_layer(
                hp, hs, dims, cache_k_sb[j], cache_v_sb[j], page_table,
                norm_pre_o[j], norm_post_o[j], w_in_o[j], gmlp_w[j], gmlp_b[j], w_out_o[j],
                (tm, tm_proj), group_sb)
            od.append(rest)
    y_prompt = hp.reshape(bsz, seq, d)
    y_sample = _to_batch_major(hs, nb)
    ev_out = [jnp.stack([r[k] for r in ev]) for k in range(6)]
    od_out = [jnp.stack([r[k] for r in od]) for k in range(6)]
    conv_p, conv_s, kmp, vmp, kms, vms = ev_out
    ksp, vsp, kss, vss, gvp, gvs = od_out
    return (y_prompt, y_sample, conv_p, conv_s, kmp, vmp, kms, vms, ksp, vsp, kss, vss, gvp, gvs)
```

```python
import functools

import jax
import jax.numpy as jnp
from jax import lax
from jax.experimental import pallas as pl
from jax.experimental.pallas import tpu as pltpu

F32 = jnp.float32
BF16 = jnp.bfloat16

HEAD_DIM = 64
LANES = 128
BRANCH_W = 512
MOBA_BLOCK = 256
MOBA_TOPK = 3
MOBA_UNROLL = 4
GMLP_CHUNK = 128
CONV_W = 3
RMS_EPS = 1e-6
NEG = -1e30
LOG2E = 1.4426950408889634
Q_SCALE = HEAD_DIM ** -0.5 * LOG2E
SB_TILE = 256
SB_DEAD = -160.0
VMEM_LIMIT = 56 * 1024 * 1024
PAGE_BUFFERS = 2
PAGE_RING = 32
PAGE_GROUP = 8

_NT = (((1,), (1,)), ((), ()))


def _cparams(sem):
    return pltpu.CompilerParams(dimension_semantics=sem, vmem_limit_bytes=VMEM_LIMIT)


def _rms(x, g):
    ms = jnp.mean(x * x, axis=-1, keepdims=True)
    return x * lax.rsqrt(ms + RMS_EPS) * g


def _silu(x):
    return x / (1.0 + jnp.exp(-x))


def _softplus2(z2):
    return jnp.maximum(z2, 0.0) + jnp.log2(1.0 + jnp.exp2(-jnp.abs(z2)))


def _split_bf16(x):
    hi = x.astype(BF16)
    lo = (x - hi.astype(F32)).astype(BF16)
    return hi, lo


def _tri(n):
    r = lax.broadcasted_iota(jnp.int32, (n, n), 0)
    c = lax.broadcasted_iota(jnp.int32, (n, n), 1)
    return jnp.where(r >= c, 1.0, 0.0).astype(BF16)


def _proj_kernel(x_ref, g_ref, w_ref, rest_ref, qk_ref, kt_ref, vt_ref, vtb_ref, keep_ref, xn_ref,
                 *, qkv_first, blk, keep):
    tm = x_ref.shape[0]
    w = BRANCH_W
    keep_tile, keep_row = keep
    xn_ref[...] = _rms(x_ref[...], g_ref[...]).astype(BF16)
    for j in range(w_ref.shape[1] // w):
        r = jnp.dot(xn_ref[...], w_ref[:, j * w:(j + 1) * w], preferred_element_type=F32)
        if j == qkv_first:
            qk_ref[:, 0:w] = (r * Q_SCALE).astype(BF16)
        elif j == qkv_first + 1:
            qk_ref[:, w:2 * w] = r.astype(BF16)
            for c in range(tm // blk):
                kt_ref[:, c * blk:(c + 1) * blk] = r[c * blk:(c + 1) * blk, :].T
        elif j == qkv_first + 2:
            for c in range(tm // blk):
                t = r[c * blk:(c + 1) * blk, :].T
                vt_ref[:, c * blk:(c + 1) * blk] = t
                vtb_ref[c] = t.astype(BF16)
        else:
            jr = j if j < qkv_first else j - 3
            rest_ref[:, jr * w:(jr + 1) * w] = r.astype(BF16)
            if jr == 1:
                @pl.when(pl.program_id(1) == keep_tile)
                def _():
                    keep_ref[...] = r[keep_row:keep_row + keep_ref.shape[0], :]


def _proj(x, g, w_bf, qkv_first, bsz, seq, tm, keep_rows):
    m, d = x.shape
    n = w_bf.shape[1]
    tps = seq // tm
    blk = min(MOBA_BLOCK, tm)
    w = BRANCH_W
    keep_first, keep_n = keep_rows
    keep = (keep_first // tm, keep_first % tm)
    assert keep[1] + keep_n <= tm and keep_n % 8 == 0
    return pl.pallas_call(
        functools.partial(_proj_kernel, qkv_first=qkv_first, blk=blk, keep=keep),
        grid=(bsz, tps),
        in_specs=[
            pl.BlockSpec((tm, d), lambda b, t: (b * tps + t, 0)),
            pl.BlockSpec((1, d), lambda b, t: (0, 0)),
            pl.BlockSpec((d, n), lambda b, t: (0, 0), pipeline_mode=pl.Buffered(1)),
        ],
        out_specs=[
            pl.BlockSpec((tm, n - 3 * w), lambda b, t: (b * tps + t, 0)),
            pl.BlockSpec((tm, 2 * w), lambda b, t: (b * tps + t, 0)),
            pl.BlockSpec((None, w, tm), lambda b, t: (b, 0, t)),
            pl.BlockSpec((None, w, tm), lambda b, t: (b, 0, t)),
            pl.BlockSpec((None, tm // blk, w, blk), lambda b, t: (b, t, 0, 0)),
            pl.BlockSpec((None, keep_n, w), lambda b, t: (b, 0, 0)),
        ],
        out_shape=[
            jax.ShapeDtypeStruct((m, n - 3 * w), BF16),
            jax.ShapeDtypeStruct((m, 2 * w), BF16),
            jax.ShapeDtypeStruct((bsz, w, seq), F32),
            jax.ShapeDtypeStruct((bsz, w, seq), F32),
            jax.ShapeDtypeStruct((bsz, seq // blk, w, blk), BF16),
            jax.ShapeDtypeStruct((bsz, keep_n, w), F32),
        ],
        scratch_shapes=[pltpu.VMEM((tm, d), BF16)],
        compiler_params=_cparams(("parallel", "arbitrary")),
    )(x, g.reshape(1, d), w_bf)


def _moba_prompt_kernel(q_ref, k_ref, vt_ref, o_ref, km_ref, *, nblk):
    _moba_prompt_block(pl.program_id(2), q_ref, k_ref, vt_ref, o_ref, km_ref, nblk)


def _moba_prompt_block(i, q_ref, k_ref, vt_ref, o_ref, km_ref, nblk):
    lane = lax.broadcasted_iota(jnp.int32, (1, LANES), 1)
    low = lane < HEAD_DIM
    nbp = km_ref.shape[0]
    blk = MOBA_BLOCK
    nheads = q_ref.shape[1] // HEAD_DIM

    @pl.when(i == 0)
    def _():
        km_ref[...] = jnp.zeros_like(km_ref)
        for j in range(nblk):
            rows = k_ref[j * blk:(j + 1) * blk, :].astype(F32)
            km_ref[j:j + 1, :] = jnp.sum(rows, axis=0, keepdims=True) * (1.0 / blk)

    brow = lax.broadcasted_iota(jnp.int32, (nbp, blk), 0)
    key_i = lax.broadcasted_iota(jnp.int32, (blk, blk), 0)
    qry_i = lax.broadcasted_iota(jnp.int32, (blk, blk), 1)
    start = pl.multiple_of(i * blk, blk)

    ones_rows = jnp.ones((2 * 8, blk), BF16)

    def v_rows(j, hh):
        return jnp.concatenate([vt_ref[j, hh * HEAD_DIM:(hh + 1) * HEAD_DIM, :], ones_rows], axis=0)

    q_owns, gates, own_scores = [], [], []
    for hh in range(nheads):
        cols = slice((hh // 2) * LANES, (hh // 2 + 1) * LANES)
        q = q_ref[:, cols]
        km = km_ref[:, cols]
        mine = low if hh % 2 == 0 else jnp.logical_not(low)
        km_hi, km_lo = _split_bf16(jnp.where(mine, km, 0.0))
        gates.append(lax.dot_general(km_hi, q, _NT, preferred_element_type=F32)
                     + lax.dot_general(km_lo, q, _NT, preferred_element_type=F32))
        q_own = jnp.where(mine, q, jnp.zeros_like(q))
        q_owns.append(q_own)
        own_scores.append(lax.dot_general(k_ref[pl.ds(start, blk), cols], q_own, _NT,
                                          preferred_element_type=F32))

    q_pasts, init = [], []
    for hh in range(nheads):
        gate = jnp.where(brow < i, gates[hh], -jnp.inf)
        bias = jnp.full((nbp, blk), NEG, F32)
        for _ in range(MOBA_TOPK):
            mx = jnp.max(gate, axis=0, keepdims=True)
            first = jnp.min(jnp.where(gate == mx, brow, nbp), axis=0, keepdims=True)
            pick = (brow == first) & (mx > -jnp.inf)
            bias = jnp.where(pick, 0.0, bias)
            gate = jnp.where(pick, -jnp.inf, gate)
        bias_t = jnp.concatenate([bias, jnp.zeros((LANES - nbp, blk), F32)], axis=0).T
        q_pasts.append(jnp.concatenate([q_owns[hh], bias_t.astype(BF16)], axis=1))

        s = jnp.where(key_i <= qry_i, own_scores[hh], NEG)
        m0 = jnp.max(s, axis=0, keepdims=True)
        p = jnp.exp2(s - m0)
        acc0 = jnp.dot(v_rows(i, hh), p.astype(BF16), preferred_element_type=F32)
        init.append((m0, acc0))

    def stage(js, carry):
        scores = []
        for j in js:
            st = pl.multiple_of(j * blk, blk)
            onehot = jnp.broadcast_to(jnp.where(lane == j, 1.0, 0.0).astype(BF16), (blk, LANES))
            for hh in range(nheads):
                kj = k_ref[pl.ds(st, blk), (hh // 2) * LANES:(hh // 2 + 1) * LANES]
                scores.append(lax.dot_general(jnp.concatenate([kj, onehot], axis=1), q_pasts[hh],
                                              _NT, preferred_element_type=F32))
        carry = list(carry)
        for n, j in enumerate(js):
            for hh in range(nheads):
                m, acc = carry[hh]
                sj = scores[n * nheads + hh]
                m_new = jnp.maximum(m, jnp.max(sj, axis=0, keepdims=True))
                alpha = jnp.exp2(m - m_new)
                pj = jnp.exp2(sj - m_new)
                acc = alpha * acc + jnp.dot(v_rows(j, hh), pj.astype(BF16),
                                            preferred_element_type=F32)
                carry[hh] = (m_new, acc)
        return tuple(carry)

    fin = tuple(init)
    base = 0
    size = MOBA_UNROLL
    while size >= 1:
        trips = (i - base) // size if size == MOBA_UNROLL else ((i - base) // size) % 2
        fin = lax.fori_loop(
            0, trips,
            lambda t, c, base=base, size=size: stage([base + t * size + u for u in range(size)], c),
            fin)
        base = base + trips * size
        size //= 2
    out_t = jnp.concatenate([acc[:HEAD_DIM] / acc[HEAD_DIM:HEAD_DIM + 1] for _, acc in fin],
                            axis=0)
    o_ref[...] = out_t.T


def _attn_prompt_call(kernel_fn, qk, vtb, bsz, seq, scratch, wid):
    nblk = seq // MOBA_BLOCK
    ngrp = BRANCH_W // wid
    return pl.pallas_call(
        kernel_fn,
        grid=(bsz, ngrp, nblk),
        in_specs=[
            pl.BlockSpec((MOBA_BLOCK, wid), lambda b, p, i: (b * nblk + i, p)),
            pl.BlockSpec((seq, wid), lambda b, p, i: (b, ngrp + p)),
            pl.BlockSpec((None, nblk, wid, MOBA_BLOCK), lambda b, p, i: (b, 0, p, 0)),
        ],
        out_specs=pl.BlockSpec((MOBA_BLOCK, wid), lambda b, p, i: (b * nblk + i, p)),
        out_shape=jax.ShapeDtypeStruct((bsz * seq, BRANCH_W), F32),
        scratch_shapes=scratch,
        compiler_params=_cparams(("parallel", "parallel", "arbitrary")),
    )(qk, qk, vtb)


MOBA_STEP_W = 512


def _moba_prompt(qk, vtb, bsz, seq):
    nblk = seq // MOBA_BLOCK
    nbp = -(-nblk // 8) * 8
    return _attn_prompt_call(functools.partial(_moba_prompt_kernel, nblk=nblk), qk, vtb, bsz, seq,
                             [pltpu.VMEM((nbp, MOBA_STEP_W), F32)], MOBA_STEP_W)


def _sb_prompt_kernel(q_ref, k_ref, vt_ref, o_ref):
    i = pl.program_id(2)
    nheads = q_ref.shape[1] // HEAD_DIM
    tile = SB_TILE
    lane = lax.broadcasted_iota(jnp.int32, (1, LANES), 1)
    low = lane < HEAD_DIM
    key_i = lax.broadcasted_iota(jnp.int32, (tile, tile), 0)
    qry_i = lax.broadcasted_iota(jnp.int32, (tile, tile), 1)
    tri_t = jnp.where(qry_i >= key_i, 1.0, 0.0).astype(BF16)
    tri_2 = jnp.concatenate([tri_t, tri_t], axis=1)
    strict = key_i < qry_i

    q_hs = []
    for hh in range(nheads):
        q = q_ref[:, (hh // 2) * LANES:(hh // 2 + 1) * LANES]
        mine = low if hh % 2 == 0 else jnp.logical_not(low)
        q_hs.append(jnp.where(mine, q, jnp.zeros_like(q)))

    def sweep(j, state, diag):
        st = pl.multiple_of(j * tile, tile)
        zs = [lax.dot_general(k_ref[pl.ds(st, tile), (hh // 2) * LANES:(hh // 2 + 1) * LANES],
                              q_hs[hh], _NT, preferred_element_type=F32)
              for hh in range(nheads)]
        cums = []
        for hh in range(nheads):
            sp = _softplus2(zs[hh])
            if diag:
                sp = jnp.where(strict, sp, 0.0)
            hi, lo = _split_bf16(sp)
            cums.append(jnp.dot(tri_2, jnp.concatenate([hi, lo], axis=0),
                                preferred_element_type=F32) + state[hh][0])
        new = []
        for hh in range(nheads):
            w = jnp.exp2(zs[hh] - cums[hh])
            if diag:
                w = jnp.where(strict, w, 0.0)
            acc = state[hh][1] + jnp.dot(vt_ref[j, hh * HEAD_DIM:(hh + 1) * HEAD_DIM, :],
                                         w.astype(BF16), preferred_element_type=F32)
            new.append((cums[hh][0:1, :], acc))
        return tuple(new)

    def live_of(state):
        spent = jnp.min(state[0][0])
        for hh in range(1, nheads):
            spent = jnp.minimum(spent, jnp.min(state[hh][0]))
        return -spent

    def cond(st):
        j, live, _ = st
        return (j >= 0) & (live > SB_DEAD)

    def body(st):
        j, _, state = st
        new = sweep(j, state, False)
        return j - 1, live_of(new), new

    zero = tuple((jnp.zeros((1, tile), F32), jnp.zeros((HEAD_DIM, tile), F32))
                 for _ in range(nheads))
    init = sweep(i, zero, True)
    _, _, fin = lax.while_loop(cond, body, (i - 1, live_of(init), init))
    o_ref[...] = jnp.concatenate([acc for _, acc in fin], axis=0).T


SB_STEP_W = 512


def _sb_prompt(qk, vtb, bsz, seq):
    return _attn_prompt_call(_sb_prompt_kernel, qk, vtb, bsz, seq, [], SB_STEP_W)


def _query_rows(qrep_ref, nheads):
    nrow, width = qrep_ref.shape[1], qrep_ref.shape[2]
    row_w = lax.broadcasted_iota(jnp.int32, (nrow, width), 0)
    lane_w = lax.broadcasted_iota(jnp.int32, (nrow, width), 1)
    head_mask = (lane_w // HEAD_DIM) == (row_w % nheads)
    qrows = jnp.where(head_mask, qrep_ref[0], jnp.zeros_like(qrep_ref[0]))
    return qrows, head_mask


def _pick_heads(y, head_mask, nheads):
    nrow, width = y.shape
    y = jnp.where(head_mask, y, 0.0)
    return jnp.sum(y.reshape(nrow // nheads, nheads, width), axis=1)


def _moba_sample_weights(qrows, knew_ref, s_ref, p_ref, nheads):
    nblk, nrow = s_ref.shape[0], s_ref.shape[1]
    row = lax.broadcasted_iota(jnp.int32, (nrow, LANES), 0)
    lane = lax.broadcasted_iota(jnp.int32, (nrow, LANES), 1)
    qidx = row // nheads
    zn = lax.dot_general(qrows, knew_ref[0], _NT, preferred_element_type=F32)
    gates = jnp.zeros((nrow, LANES), F32)
    for bl in range(nblk):
        gates = jnp.where(lane == bl, jnp.sum(s_ref[bl], axis=1, keepdims=True), gates)
    gates = jnp.where(lane < nblk, gates, -jnp.inf)
    sel = jnp.zeros((nrow, LANES), F32)
    for _ in range(MOBA_TOPK):
        mx = jnp.max(gates, axis=1, keepdims=True)
        first = jnp.min(jnp.where(gates == mx, lane, LANES), axis=1, keepdims=True)
        pick = (lane == first) & (mx > -jnp.inf)
        sel = jnp.where(pick, 1.0, sel)
        gates = jnp.where(pick, -jnp.inf, gates)
    zn = jnp.where(lane <= qidx, zn, NEG)
    m = jnp.max(zn, axis=1, keepdims=True)
    cols = []
    for bl in range(nblk):
        col = jnp.max(jnp.where(lane == bl, sel, 0.0), axis=1, keepdims=True) > 0.5
        cols.append(col)
        m = jnp.maximum(m, jnp.max(jnp.where(col, s_ref[bl], NEG), axis=1, keepdims=True))
    pn = jnp.exp2(zn - m)
    l = jnp.sum(pn, axis=1, keepdims=True)
    for bl in range(nblk):
        p = jnp.exp2(jnp.where(cols[bl], s_ref[bl], NEG) - m)
        l = l + jnp.sum(p, axis=1, keepdims=True)
        p_ref[bl] = p.astype(BF16)
    return pn, l


def _moba_paged_kernel(pt_ref, qrep_ref, knew_ref, vnew_ref, k_hbm, v_hbm, o_ref,
                       kbuf, vbuf, ksem, vsem, s_ref, p_ref, *, npages, nheads):
    b = pl.program_id(0)
    ring = kbuf.shape[0]
    page = kbuf.shape[2]
    ppb = MOBA_BLOCK // page
    qrows, head_mask = _query_rows(qrep_ref, nheads)

    def k_copy(seq, p, slot):
        return pltpu.make_async_copy(k_hbm.at[pt_ref[seq * npages + p]], kbuf.at[slot], ksem.at[slot])

    def v_copy(p, slot):
        return pltpu.make_async_copy(v_hbm.at[pt_ref[b * npages + p]], vbuf.at[slot], vsem.at[slot])

    @pl.when(b == 0)
    def _():
        for p in range(ring):
            k_copy(0, p, p).start()

    grp = min(PAGE_GROUP, ring)

    def k_group(gi, carry):
        base = gi * grp
        slot0 = (gi % (ring // grp)) * grp
        for u in range(grp):
            k_copy(b, base + u, slot0 + u).wait()
        for u in range(grp):
            s_ref[gi * (grp // ppb) + u // ppb, :, (u % ppb) * page:(u % ppb + 1) * page] = jnp.dot(
                qrows, kbuf[slot0 + u].astype(BF16), preferred_element_type=F32)

        @pl.when(base + ring < npages)
        def _():
            for u in range(grp):
                k_copy(b, base + ring + u, slot0 + u).start()
        return carry

    lax.fori_loop(0, npages // grp, k_group, 0)
    for p in range(ring):
        v_copy(p, p).start()

    @pl.when(b + 1 < pl.num_programs(0))
    def _():
        for p in range(ring):
            k_copy(b + 1, p, p).start()

    pn, l = _moba_sample_weights(qrows, knew_ref, s_ref, p_ref, nheads)
    acc0 = jnp.dot(pn.astype(BF16), vnew_ref[0], preferred_element_type=F32)

    def v_group(gi, acc):
        base = gi * grp
        slot0 = (gi % (ring // grp)) * grp
        for u in range(grp):
            v_copy(base + u, slot0 + u).wait()
        for u in range(grp):
            pw = p_ref[gi * (grp // ppb) + u // ppb, :, (u % ppb) * page:(u % ppb + 1) * page]
            acc = acc + lax.dot_general(pw, vbuf[slot0 + u].astype(BF16), _NT,
                                        preferred_element_type=F32)

        @pl.when(base + ring < npages)
        def _():
            for u in range(grp):
                v_copy(base + ring + u, slot0 + u).start()
        return acc

    acc = lax.fori_loop(0, npages // grp, v_group, acc0)
    o_ref[0] = _pick_heads(acc / l, head_mask, nheads)


def _paged_operands(q_bf, knew_bf, vnew_bf, cache_k, cache_v, page_table):
    nb, nq, width = q_bf.shape
    nheads = width // HEAD_DIM
    nphys, page = cache_k.shape[0], cache_k.shape[1]
    ck = cache_k.transpose(0, 2, 3, 1).reshape(nphys, width, page)
    cv = cache_v.transpose(0, 2, 3, 1).reshape(nphys, width, page)
    qrep = jnp.repeat(q_bf, nheads, axis=1)
    pad = ((0, 0), (0, LANES - nq), (0, 0))
    knew = jnp.pad(knew_bf, pad)
    vnew = jnp.pad(vnew_bf, pad)
    pt = page_table.reshape(-1).astype(jnp.int32)
    return qrep, knew, vnew, ck, cv, pt, nheads, page


def _moba_paged(q_bf, knew_bf, vnew_bf, cache_k, cache_v, page_table):
    nb, nq, width = q_bf.shape
    qrep, knew, vnew, ck, cv, pt, nheads, page = _paged_operands(
        q_bf, knew_bf, vnew_bf, cache_k, cache_v, page_table)
    nrow = nq * nheads
    npages = page_table.shape[1]
    ppb = MOBA_BLOCK // page
    ring = min(PAGE_RING, npages)
    grp = min(PAGE_GROUP, ring)
    assert grp % ppb == 0 and ring % grp == 0 and npages % ring == 0
    per_b = lambda b, pt: (b, 0, 0)
    grid_spec = pltpu.PrefetchScalarGridSpec(
        num_scalar_prefetch=1,
        grid=(nb,),
        in_specs=[
            pl.BlockSpec((1, nrow, width), per_b),
            pl.BlockSpec((1, LANES, width), per_b),
            pl.BlockSpec((1, LANES, width), per_b),
            pl.BlockSpec(memory_space=pl.ANY),
            pl.BlockSpec(memory_space=pl.ANY),
        ],
        out_specs=pl.BlockSpec((1, nq, width), per_b),
        scratch_shapes=[
            pltpu.VMEM((ring, width, page), F32),
            pltpu.VMEM((ring, width, page), F32),
            pltpu.SemaphoreType.DMA((ring,)),
            pltpu.SemaphoreType.DMA((ring,)),
            pltpu.VMEM((npages // ppb, nrow, MOBA_BLOCK), F32),
            pltpu.VMEM((npages // ppb, nrow, MOBA_BLOCK), BF16),
        ],
    )
    return pl.pallas_call(
        functools.partial(_moba_paged_kernel, npages=npages, nheads=nheads),
        grid_spec=grid_spec,
        out_shape=jax.ShapeDtypeStruct((nb, nq, width), F32),
        compiler_params=_cparams(("arbitrary",)),
    )(pt, qrep, knew, vnew, ck, cv)


def _moba_fused_kernel(pt_ref, q_ref, k_ref, vt_ref, qrep_ref, knew_ref, vnew_ref, kc_hbm, vc_hbm,
                       o_ref, os_ref, km_ref, pbuf, psem, s_ref, p_ref, acc_ref, l_ref,
                       *, nblk, npages, nheads):
    i = pl.program_id(2)
    step = pl.program_id(0) * nblk + i
    seq = step // 2
    page = pbuf.shape[2]
    ppb = MOBA_BLOCK // page
    grp = min(PAGE_GROUP, npages)

    half = npages // 2
    gph = half // grp
    even = step % 2 == 0
    odd = step % 2 == 1
    has_next = step + 1 < pl.num_programs(0) * nblk

    def page_copy(cache_hbm, sq, p):
        return pltpu.make_async_copy(cache_hbm.at[pt_ref[sq * npages + p]], pbuf.at[p], psem.at[p])

    def start_half(cache_hbm, sq, h):
        for p in range(h * half, (h + 1) * half):
            page_copy(cache_hbm, sq, p).start()

    def wait_half(cache_hbm, sq, h):
        for p in range(h * half, (h + 1) * half):
            page_copy(cache_hbm, sq, p).wait()

    def scores_half(h):
        def k_group(gi, carry):
            for u in range(grp):
                s_ref[gi * (grp // ppb) + u // ppb, :, (u % ppb) * page:(u % ppb + 1) * page] = (
                    jnp.dot(qrows, pbuf[gi * grp + u].astype(BF16), preferred_element_type=F32))
            return carry
        lax.fori_loop(h * gph, (h + 1) * gph, k_group, 0)

    def values_half(h, acc):
        def v_group(gi, acc):
            for u in range(grp):
                pw = p_ref[gi * (grp // ppb) + u // ppb, :, (u % ppb) * page:(u % ppb + 1) * page]
                acc = acc + lax.dot_general(pw, pbuf[gi * grp + u].astype(BF16), _NT,
                                            preferred_element_type=F32)
            return acc
        return lax.fori_loop(h * gph, (h + 1) * gph, v_group, acc)

    qrows, head_mask = _query_rows(qrep_ref, nheads)

    @pl.when(step == 0)
    def _():
        start_half(kc_hbm, 0, 0)
        start_half(kc_hbm, 0, 1)

    @pl.when(even)
    def _():
        wait_half(kc_hbm, seq, 0)
        scores_half(0)
        start_half(vc_hbm, seq, 0)

    @pl.when(odd)
    def _():
        pn, l = _moba_sample_weights(qrows, knew_ref, s_ref, p_ref, nheads)
        l_ref[...] = jnp.broadcast_to(l, l_ref.shape)
        acc0 = jnp.dot(pn.astype(BF16), vnew_ref[0], preferred_element_type=F32)
        wait_half(vc_hbm, seq, 0)
        acc_ref[...] = values_half(0, acc0)

        @pl.when(has_next)
        def _():
            start_half(kc_hbm, seq + 1, 0)

    _moba_prompt_block(i, q_ref, k_ref, vt_ref, o_ref, km_ref, nblk)

    @pl.when(even)
    def _():
        wait_half(kc_hbm, seq, 1)
        scores_half(1)
        start_half(vc_hbm, seq, 1)

    @pl.when(odd)
    def _():
        wait_half(vc_hbm, seq, 1)
        acc = values_half(1, acc_ref[...])
        os_ref[0] = _pick_heads(acc / l_ref[:, 0:1], head_mask, nheads)

        @pl.when(has_next)
        def _():
            start_half(kc_hbm, seq + 1, 1)


def _moba_fused(qk, vtb, bsz, seq, q_bf, knew_bf, vnew_bf, cache_k, cache_v, page_table):
    nb, nq, width = q_bf.shape
    qrep, knew, vnew, ck, cv, pt, nheads, page = _paged_operands(
        q_bf, knew_bf, vnew_bf, cache_k, cache_v, page_table)
    nrow = nq * nheads
    npages = page_table.shape[1]
    nblk = seq // MOBA_BLOCK
    nbp = -(-nblk // 8) * 8
    wid = MOBA_STEP_W
    ppb = MOBA_BLOCK // page
    grp = min(PAGE_GROUP, npages)
    assert bsz * nblk == 2 * nb and wid == BRANCH_W and grp % ppb == 0 and npages % (2 * grp) == 0
    per_seq = lambda b, p, i, pt: ((b * nblk + i) // 2, 0, 0)
    grid_spec = pltpu.PrefetchScalarGridSpec(
        num_scalar_prefetch=1,
        grid=(bsz, 1, nblk),
        in_specs=[
            pl.BlockSpec((MOBA_BLOCK, wid), lambda b, p, i, pt: (b * nblk + i, 0)),
            pl.BlockSpec((seq, wid), lambda b, p, i, pt: (b, 1)),
            pl.BlockSpec((None, nblk, wid, MOBA_BLOCK), lambda b, p, i, pt: (b, 0, 0, 0)),
            pl.BlockSpec((1, nrow, width), per_seq),
            pl.BlockSpec((1, LANES, width), per_seq),
            pl.BlockSpec((1, LANES, width), per_seq),
            pl.BlockSpec(memory_space=pl.ANY),
            pl.BlockSpec(memory_space=pl.ANY),
        ],
        out_specs=[
            pl.BlockSpec((MOBA_BLOCK, wid), lambda b, p, i, pt: (b * nblk + i, 0)),
            pl.BlockSpec((1, nq, width), per_seq),
        ],
        scratch_shapes=[
            pltpu.VMEM((nbp, wid), F32),
            pltpu.VMEM((npages, width, page), F32),
            pltpu.SemaphoreType.DMA((npages,)),
            pltpu.VMEM((npages // ppb, nrow, MOBA_BLOCK), F32),
            pltpu.VMEM((npages // ppb, nrow, MOBA_BLOCK), BF16),
            pltpu.VMEM((nrow, width), F32),
            pltpu.VMEM((nrow, LANES), F32),
        ],
    )
    return pl.pallas_call(
        functools.partial(_moba_fused_kernel, nblk=nblk, npages=npages, nheads=nheads),
        grid_spec=grid_spec,
        out_shape=[jax.ShapeDtypeStruct((bsz * seq, BRANCH_W), F32),
                   jax.ShapeDtypeStruct((nb, nq, width), F32)],
        compiler_params=_cparams(("arbitrary", "arbitrary", "arbitrary")),
    )(pt, qk, qk, vtb, qrep, knew, vnew, ck, cv)


def _sb_paged_kernel(pt_ref, live_ref, qrep_ref, knew_ref, vnew_ref, cin_ref, ain_ref, *rest,
                     group, nheads, first):
    del pt_ref
    k_refs = rest[:group]
    v_refs = rest[group:2 * group]
    y_ref, acc_ref, carry_ref = rest[2 * group:2 * group + 3]
    b = pl.program_id(0)
    s = pl.program_id(1)
    nrow = qrep_ref.shape[1]
    page = k_refs[0].shape[1]
    ppt = SB_TILE // page
    qrows, head_mask = _query_rows(qrep_ref, nheads)
    tri = _tri(SB_TILE)

    @pl.when(s == 0)
    def _():
        if first:
            row = lax.broadcasted_iota(jnp.int32, (nrow, LANES), 0)
            lane = lax.broadcasted_iota(jnp.int32, (nrow, LANES), 1)
            strict = lane < row // nheads
            zn = lax.dot_general(qrows, knew_ref[0], _NT, preferred_element_type=F32)
            hi, lo = _split_bf16(jnp.where(strict, -_softplus2(zn), 0.0))
            tri_n = _tri(LANES)
            cum = (jnp.dot(hi, tri_n, preferred_element_type=F32)
                   + jnp.dot(lo, tri_n, preferred_element_type=F32))
            wn = jnp.where(strict, jnp.exp2(zn + cum), 0.0)
            acc_ref[0] = jnp.dot(wn.astype(BF16), vnew_ref[0], preferred_element_type=F32)
            carry_ref[0] = jnp.broadcast_to(cum[:, 0:1], carry_ref.shape[1:])
        else:
            acc_ref[0] = ain_ref[0]
            carry_ref[0] = cin_ref[0]

    @pl.when(live_ref[b] > 0)
    def _():
        carry = carry_ref[0][:, 0:1]
        acc = acc_ref[0]
        for t in reversed(range(group // ppt)):
            kt = jnp.concatenate([k_refs[t * ppt + u][...] for u in range(ppt)], axis=1).astype(BF16)
            vt = jnp.concatenate([v_refs[t * ppt + u][...] for u in range(ppt)], axis=1).astype(BF16)
            z = jnp.dot(qrows, kt, preferred_element_type=F32)
            hi, lo = _split_bf16(-_softplus2(z))
            cum = (jnp.dot(hi, tri, preferred_element_type=F32)
                   + jnp.dot(lo, tri, preferred_element_type=F32)) + carry
            w = jnp.exp2(z + cum).astype(BF16)
            acc = acc + lax.dot_general(w, vt, _NT, preferred_element_type=F32)
            carry = cum[:, 0:1]
        acc_ref[0] = acc
        carry_ref[0] = jnp.broadcast_to(carry, carry_ref.shape[1:])

    @pl.when(s == pl.num_programs(1) - 1)
    def _():
        y_ref[0] = _pick_heads(acc_ref[0], head_mask, nheads)


def _sb_paged_call(first, pt, live, qrep, knew, vnew, carry_in, acc_in, ck, cv,
                   nq, npages, group, chunk_lo, nsteps, nheads):
    nb, nrow, width = qrep.shape
    page = ck.shape[2]

    def page_map(g):
        def index(b, s, pt, live):
            chunk_page = (chunk_lo + nsteps - 1 - s) * group + g
            return (jnp.where(live[b] > 0, pt[b * npages + chunk_page], 0), 0, 0)
        return index

    per_b = lambda b, s, pt, live: (b, 0, 0)
    in_specs = [
        pl.BlockSpec((1, nrow, width), per_b),
        pl.BlockSpec((1, LANES, width), per_b),
        pl.BlockSpec((1, LANES, width), per_b),
        pl.BlockSpec((1, nrow, LANES), per_b),
        pl.BlockSpec((1, nrow, width), per_b),
    ]
    in_specs += [pl.BlockSpec((None, width, page), page_map(g), pipeline_mode=pl.Buffered(PAGE_BUFFERS))
                 for g in range(group)] * 2
    grid_spec = pltpu.PrefetchScalarGridSpec(
        num_scalar_prefetch=2,
        grid=(nb, nsteps),
        in_specs=in_specs,
        out_specs=[pl.BlockSpec((1, nq, width), per_b),
                   pl.BlockSpec((1, nrow, width), per_b),
                   pl.BlockSpec((1, nrow, LANES), per_b)],
    )
    return pl.pallas_call(
        functools.partial(_sb_paged_kernel, group=group, nheads=nheads, first=first),
        grid_spec=grid_spec,
        out_shape=[jax.ShapeDtypeStruct((nb, nq, width), F32),
                   jax.ShapeDtypeStruct((nb, nrow, width), F32),
                   jax.ShapeDtypeStruct((nb, nrow, LANES), F32)],
        compiler_params=_cparams(("parallel", "arbitrary")),
    )(pt, live, qrep, knew, vnew, carry_in, acc_in, *([ck] * group), *([cv] * group))


def _sb_paged(q_bf, knew_bf, vnew_bf, cache_k, cache_v, page_table, group):
    nb, nq, width = q_bf.shape
    qrep, knew, vnew, ck, cv, pt, nheads, page = _paged_operands(
        q_bf, knew_bf, vnew_bf, cache_k, cache_v, page_table)
    nrow = nq * nheads
    npages = page_table.shape[1]
    nchunk = npages // group
    zc = jnp.zeros((nb, nrow, LANES), F32)
    za = jnp.zeros((nb, nrow, width), F32)
    all_live = jnp.ones((nb,), jnp.int32)
    y, acc, carry = _sb_paged_call(True, pt, all_live, qrep, knew, vnew, zc, za, ck, cv,
                                   nq, npages, group, nchunk - 1, 1, nheads)
    if nchunk == 1:
        return y
    live = (jnp.max(carry, axis=(1, 2)) > SB_DEAD).astype(jnp.int32)

    def older(_):
        return _sb_paged_call(False, pt, live, qrep, knew, vnew, carry, acc, ck, cv,
                              nq, npages, group, 0, nchunk - 1, nheads)[0]

    return lax.cond(jnp.any(live > 0), older, lambda _: y, None)


def _finish(y_first, y_second, w_ref, g_ref, x_ref, o_ref):
    half = y_first.shape[1]
    out = (jnp.dot(y_first.astype(BF16), w_ref[0:half, :], preferred_element_type=F32)
           + jnp.dot(y_second.astype(BF16), w_ref[half:2 * half, :], preferred_element_type=F32))
    o_ref[...] = x_ref[...] + _rms(out, g_ref[...])


def _even_out_prompt_kernel(ab_ref, ac_ref, ah_ref, az_ref, bz_ref, hc_ref, hh_ref, yb_ref, x_ref,
                            w_ref, cw_ref, g_ref, o_ref, tail_ref, ue_ref, *, tiles_per_seq):
    i = pl.program_id(0)
    tm = ab_ref.shape[0]
    hrows = hc_ref.shape[0]
    u = ac_ref[...].astype(F32) * ah_ref[...].astype(F32)
    first = (i % tiles_per_seq) == 0
    halo = hc_ref[...].astype(F32) * hh_ref[...].astype(F32)
    ue_ref[0:8, :] = jnp.where(first, 0.0, halo[hrows - 8:hrows, :])
    ue_ref[8:8 + tm, :] = u
    conv = (ue_ref[6:6 + tm, :] * cw_ref[0:1, :] + ue_ref[7:7 + tm, :] * cw_ref[1:2, :]
            + u * cw_ref[2:3, :])
    y_a = ab_ref[...].astype(F32) * conv * _silu(az_ref[...].astype(F32))
    y_b = yb_ref[...] * _silu(bz_ref[...].astype(F32))
    tail_ref[...] = u[tm - 8:tm, :]
    _finish(y_a, y_b, w_ref, g_ref, x_ref, o_ref)


def _even_out_prompt(rest, y_b, x, w_out_bf, conv_w, g_post, seq, tm):
    m, d = x.shape
    nt = m // tm
    col = lambda c: pl.BlockSpec((tm, BRANCH_W), lambda i, c=c: (i, c))
    hrows = 16
    halo = lambda c: pl.BlockSpec((hrows, BRANCH_W),
                                  lambda i, c=c: (jnp.maximum(i * (tm // hrows) - 1, 0), c))
    return pl.pallas_call(
        functools.partial(_even_out_prompt_kernel, tiles_per_seq=seq // tm),
        grid=(nt,),
        in_specs=[col(0), col(1), col(2), col(3), col(4), halo(1), halo(2),
                  pl.BlockSpec((tm, BRANCH_W), lambda i: (i, 0)),
                  pl.BlockSpec((tm, d), lambda i: (i, 0)),
                  pl.BlockSpec(w_out_bf.shape, lambda i: (0, 0)),
                  pl.BlockSpec(conv_w.shape, lambda i: (0, 0)),
                  pl.BlockSpec((1, d), lambda i: (0, 0))],
        out_specs=[pl.BlockSpec((tm, d), lambda i: (i, 0)),
                   pl.BlockSpec((8, BRANCH_W), lambda i: (i, 0))],
        out_shape=[jax.ShapeDtypeStruct((m, d), F32),
                   jax.ShapeDtypeStruct((nt * 8, BRANCH_W), F32)],
        scratch_shapes=[pltpu.VMEM((tm + 8, BRANCH_W), F32)],
        compiler_params=_cparams(("parallel",)),
    )(rest, rest, rest, rest, rest, rest, rest, y_b, x, w_out_bf, conv_w, g_post.reshape(1, d))


def _even_out_sample_kernel(rest_ref, st_ref, yb_ref, x_ref, w_ref, cw_ref, g_ref,
                            o_ref, tail_ref, *, nb):
    w = BRANCH_W
    rows = rest_ref.shape[0]
    rest = rest_ref[...].astype(F32)
    u = rest[:, w:2 * w] * rest[:, 2 * w:3 * w]
    ue = jnp.concatenate([st_ref[...], u], axis=0)
    conv = (ue[0:rows] * cw_ref[0:1, :] + ue[nb:nb + rows] * cw_ref[1:2, :]
            + ue[2 * nb:2 * nb + rows] * cw_ref[2:3, :])
    y_a = rest[:, 0:w] * conv * _silu(rest[:, 3 * w:4 * w])
    y_b = yb_ref[...] * _silu(rest[:, 4 * w:5 * w])
    tail_ref[...] = ue[rows:rows + 2 * nb]
    _finish(y_a, y_b, w_ref, g_ref, x_ref, o_ref)


def _even_out_sample(rest, state_tm, y_b, x, w_out_bf, conv_w, g_post, nb):
    m, d = x.shape
    return pl.pallas_call(
        functools.partial(_even_out_sample_kernel, nb=nb),
        out_shape=[jax.ShapeDtypeStruct((m, d), F32),
                   jax.ShapeDtypeStruct(((CONV_W - 1) * nb, BRANCH_W), F32)],
        compiler_params=pltpu.CompilerParams(vmem_limit_bytes=VMEM_LIMIT),
    )(rest, state_tm, y_b, x, w_out_bf, conv_w, g_post.reshape(1, d))


def _gmlp_mix(v, gw_ref, low):
    r_i = lax.broadcasted_iota(jnp.int32, (GMLP_CHUNK, GMLP_CHUNK), 0)
    c_i = lax.broadcasted_iota(jnp.int32, (GMLP_CHUNK, GMLP_CHUNK), 1)
    tril = c_i <= r_i
    parts = []
    for p in range(v.shape[1] // LANES):
        vp = v[:, p * LANES:(p + 1) * LANES]
        v_lo = jnp.where(low, vp, jnp.zeros_like(vp)).astype(BF16)
        v_hi = jnp.where(low, jnp.zeros_like(vp), vp).astype(BF16)
        w_lo = jnp.where(tril, gw_ref[2 * p], 0.0).astype(BF16)
        w_hi = jnp.where(tril, gw_ref[2 * p + 1], 0.0).astype(BF16)
        parts.append(jnp.dot(w_lo, v_lo, preferred_element_type=F32)
                     + jnp.dot(w_hi, v_hi, preferred_element_type=F32))
    return jnp.concatenate(parts, axis=1)


def _odd_out_prompt_kernel(cu_ref, cv_ref, cz_ref, dz_ref, yd_ref, x_ref, w_ref, gw_ref, gb_ref,
                           g_ref, o_ref, yc_ref):
    tm = cu_ref.shape[0]
    low = lax.broadcasted_iota(jnp.int32, (1, LANES), 1) < HEAD_DIM
    for c in range(tm // GMLP_CHUNK):
        rows = slice(c * GMLP_CHUNK, (c + 1) * GMLP_CHUNK)
        mixed = _gmlp_mix(cv_ref[rows, :], gw_ref, low) + gb_ref[...]
        yc_ref[rows, :] = (cu_ref[rows, :].astype(F32) * mixed
                           * _silu(cz_ref[rows, :].astype(F32)))
    y_d = yd_ref[...] * _silu(dz_ref[...].astype(F32))
    _finish(yc_ref[...], y_d, w_ref, g_ref, x_ref, o_ref)


def _odd_out_prompt(rest, y_d, x, w_out_bf, gmlp_w, gb_full, g_post, tm):
    m, d = x.shape
    col = lambda c: pl.BlockSpec((tm, BRANCH_W), lambda i, c=c: (i, c))
    return pl.pallas_call(
        _odd_out_prompt_kernel,
        grid=(m // tm,),
        in_specs=[col(0), col(1), col(2), col(3),
                  pl.BlockSpec((tm, BRANCH_W), lambda i: (i, 0)),
                  pl.BlockSpec((tm, d), lambda i: (i, 0)),
                  pl.BlockSpec(w_out_bf.shape, lambda i: (0, 0)),
                  pl.BlockSpec(gmlp_w.shape, lambda i: (0, 0, 0)),
                  pl.BlockSpec(gb_full.shape, lambda i: (0, 0)),
                  pl.BlockSpec((1, d), lambda i: (0, 0))],
        out_specs=pl.BlockSpec((tm, d), lambda i: (i, 0)),
        out_shape=jax.ShapeDtypeStruct((m, d), F32),
        scratch_shapes=[pltpu.VMEM((tm, BRANCH_W), F32)],
        compiler_params=_cparams(("parallel",)),
    )(rest, rest, rest, rest, y_d, x, w_out_bf, gmlp_w, gb_full, g_post.reshape(1, d))


def _odd_out_sample_kernel(rest_ref, yd_ref, x_ref, w_ref, w4_ref, b4_ref, g_ref, o_ref, *, nb, nq):
    w = BRANCH_W
    rest = rest_ref[...].astype(F32)
    parts = []
    for t in range(nq):
        mixed = jnp.broadcast_to(b4_ref[t:t + 1, :], (nb, w))
        for s in range(t + 1):
            mixed = mixed + w4_ref[t * nq + s:t * nq + s + 1, :] * rest[s * nb:(s + 1) * nb, w:2 * w]
        parts.append(mixed)
    mixed = jnp.concatenate(parts, axis=0)
    y_c = rest[:, 0:w] * mixed * _silu(rest[:, 2 * w:3 * w])
    y_d = yd_ref[...] * _silu(rest[:, 3 * w:4 * w])
    _finish(y_c, y_d, w_ref, g_ref, x_ref, o_ref)


def _odd_out_sample(rest, y_d, x, w_out_bf, w4, b4, g_post, nb, nq):
    m, d = x.shape
    return pl.pallas_call(
        functools.partial(_odd_out_sample_kernel, nb=nb, nq=nq),
        out_shape=jax.ShapeDtypeStruct((m, d), F32),
        compiler_params=pltpu.CompilerParams(vmem_limit_bytes=VMEM_LIMIT),
    )(rest, y_d, x, w_out_bf, w4, b4, g_post.reshape(1, d))


def _to_token_major(a):
    nb, nq, w = a.shape
    return a.transpose(1, 0, 2).reshape(nq * nb, w)


def _to_batch_major(a, nb):
    w = a.shape[1]
    return a.reshape(-1, nb, w).transpose(1, 0, 2)


def _heads_prompt(t, heads):
    bsz, w, seq = t.shape
    return t.reshape(bsz, heads, w // heads, seq).transpose(0, 3, 1, 2)


def _project_both(hp, hs, dims, g_pre, w_in, qkv_first, tm):
    bsz, seq, nb, nq = dims
    w_in_bf = w_in.astype(BF16)
    open_start = ((seq - 1) // GMLP_CHUNK) * GMLP_CHUNK
    prompt = _proj(hp, g_pre, w_in_bf, qkv_first, bsz, seq, tm, (open_start, seq - open_start))
    ns = nq * nb
    rows = max(ns, tm)
    hs_pad = jnp.pad(hs, ((0, rows - ns), (0, 0)))
    rest_s, qk_s, kt_s, vt_s, _, keep_s = _proj(hs_pad, g_pre, w_in_bf, qkv_first, 1, rows, rows,
                                                (0, ns))
    rest_s = rest_s[:ns]
    k_s = _to_batch_major(kt_s[0, :, :ns].T, nb)
    v_s = _to_batch_major(vt_s[0, :, :ns].T, nb)
    q_s = _to_batch_major(qk_s[:ns, :BRANCH_W], nb)
    return prompt, rest_s, q_s, k_s, v_s, keep_s[0]


def _even_layer(hp, hs, dims, state_conv, cache_k, cache_v, page_table,
                g_pre, g_post, w_in, conv_w, w_out, tiles):
    bsz, seq, nb, nq = dims
    tm, tm_proj = tiles
    w = BRANCH_W
    heads = w // HEAD_DIM
    w_out_bf = w_out.astype(BF16)
    (rest_p, qk_p, kt_p, vt_p, vtb_p, _), rest_s, q_s, k_s, v_s, _ = _project_both(
        hp, hs, dims, g_pre, w_in, 4, tm_proj)
    sample_qkv = (q_s, k_s.astype(BF16), v_s.astype(BF16))
    if bsz * (seq // MOBA_BLOCK) == 2 * nb:
        yb_p, yb_s = _moba_fused(qk_p, vtb_p, bsz, seq, *sample_qkv, cache_k, cache_v, page_table)
    else:
        yb_p = _moba_prompt(qk_p, vtb_p, bsz, seq)
        yb_s = _moba_paged(*sample_qkv, cache_k, cache_v, page_table)
    yb_s = _to_token_major(yb_s)
    hp, tail_p = _even_out_prompt(rest_p, yb_p, hp, w_out_bf, conv_w, g_post, seq, tm)
    state_tm = _to_token_major(state_conv)
    hs, tail_s = _even_out_sample(rest_s, state_tm, yb_s, hs, w_out_bf, conv_w, g_post, nb)
    conv_p = tail_p.reshape(bsz, seq // tm, 8, w)[:, -1, 8 - (CONV_W - 1):, :]
    conv_s = _to_batch_major(tail_s, nb)
    return (hp, hs, conv_p, conv_s, _heads_prompt(kt_p, heads), _heads_prompt(vt_p, heads),
            k_s.reshape(nb, nq, heads, HEAD_DIM), v_s.reshape(nb, nq, heads, HEAD_DIM))


def _odd_layer(hp, hs, dims, cache_k, cache_v, page_table,
               g_pre, g_post, w_in, gmlp_w, gmlp_b, w_out, tiles, group):
    bsz, seq, nb, nq = dims
    tm, tm_proj = tiles
    w = BRANCH_W
    heads = w // HEAD_DIM
    w_out_bf = w_out.astype(BF16)
    ngroups = gmlp_w.shape[0]
    cpg = w // ngroups
    (rest_p, qk_p, kt_p, vt_p, vtb_p, gv_p), rest_s, q_s, k_s, v_s, keep_s = _project_both(
        hp, hs, dims, g_pre, w_in, 3, tm_proj)
    yd_p = _sb_prompt(qk_p, vtb_p, bsz, seq)
    yd_s = _to_token_major(_sb_paged(q_s, k_s.astype(BF16), v_s.astype(BF16),
                                     cache_k, cache_v, page_table, group))
    gb_full = jnp.repeat(gmlp_b.T, cpg, axis=1)
    hp = _odd_out_prompt(rest_p, yd_p, hp, w_out_bf, gmlp_w, gb_full, g_post, tm)
    w4 = jnp.repeat(gmlp_w[:, :nq, :nq].transpose(1, 2, 0), cpg, axis=2).reshape(nq * nq, w)
    b4 = jnp.repeat(gmlp_b[:, :nq].T, cpg, axis=1)
    hs = _odd_out_sample(rest_s, yd_s, hs, w_out_bf, w4, b4, g_post, nb, nq)
    gv_s = _to_batch_major(keep_s, nb)
    return (hp, hs, _heads_prompt(kt_p, heads), _heads_prompt(vt_p, heads),
            k_s.reshape(nb, nq, heads, HEAD_DIM), v_s.reshape(nb, nq, heads, HEAD_DIM), gv_p, gv_s)


def kernel(x_prompt, x_sample, state_conv, cache_k_moba, cache_v_moba, cache_k_sb, cache_v_sb,
           page_table, norm_pre_e, norm_post_e, w_in_e, conv_w, w_out_e,
           norm_pre_o, norm_post_o, w_in_o, gmlp_w, gmlp_b, w_out_o):
    bsz, seq, d = x_prompt.shape
    nb, nq, _ = x_sample.shape
    depth = norm_pre_e.shape[0] + norm_pre_o.shape[0]
    npages, page = page_table.shape[1], cache_k_moba.shape[2]
    assert seq % MOBA_BLOCK == 0 and (npages * page) % MOBA_BLOCK == 0 and nq <= GMLP_CHUNK
    assert (nq - 1) // MOBA_BLOCK == 0 and w_in_e.shape[2] == 8 * BRANCH_W and w_in_o.shape[2] == 7 * BRANCH_W
    dims = (bsz, seq, nb, nq)
    tm = min(512, seq)
    tm_proj = min(512, seq)
    group_sb = min(4, npages)
    hp = x_prompt.reshape(bsz * seq, d)
    hs = _to_token_major(x_sample)
    ev, od = [], []
    for i in range(depth):
        j = i // 2
        if i % 2 == 0:
            hp, hs, *rest = _even_layer(
                hp, hs, dims, state_conv[j], cache_k_moba[j], cache_v_moba[j], page_table,
                norm_pre_e[j], norm_post_e[j], w_in_e[j], conv_w[j], w_out_e[j],
                (tm, tm_proj))
            ev.append(rest)
        else:
            hp, hs, *rest = _odd_layer(
                hp, hs, dims, cache_k_sb[j], cache_v_sb[j], page_table,
                norm_pre_o[j], norm_post_o[j], w_in_o[j], gmlp_w[j], gmlp_b[j], w_out_o[j],
                (tm, tm_proj), group_sb)
            od.append(rest)
    y_prompt = hp.reshape(bsz, seq, d)
    y_sample = _to_batch_major(hs, nb)
    ev_out = [jnp.stack([r[k] for r in ev]) for k in range(6)]
    od_out = [jnp.stack([r[k] for r in od]) for k in range(6)]
    conv_p, conv_s, kmp, vmp, kms, vms = ev_out
    ksp, vsp, kss, vss, gvp, gvs = od_out
    return (y_prompt, y_sample, conv_p, conv_s, kmp, vmp, kms, vms, ksp, vsp, kss, vss, gvp, gvs)
```

```python
import functools

import jax
import jax.numpy as jnp
from jax import lax
from jax.experimental import pallas as pl
from jax.experimental.pallas import tpu as pltpu

F32 = jnp.float32
BF16 = jnp.bfloat16

HEAD_DIM = 64
LANES = 128
BRANCH_W = 512
MOBA_BLOCK = 256
MOBA_TOPK = 3
MOBA_UNROLL = 4
GMLP_CHUNK = 128
CONV_W = 3
RMS_EPS = 1e-6
NEG = -1e30
LOG2E = 1.4426950408889634
Q_SCALE = HEAD_DIM ** -0.5 * LOG2E
SB_TILE = 256
SB_DEAD = -160.0
VMEM_LIMIT = 56 * 1024 * 1024
PAGE_BUFFERS = 2
PAGE_RING = 32
PAGE_GROUP = 8

_NT = (((1,), (1,)), ((), ()))


def _cparams(sem):
    return pltpu.CompilerParams(dimension_semantics=sem, vmem_limit_bytes=VMEM_LIMIT)


def _rms(x, g):
    ms = jnp.mean(x * x, axis=-1, keepdims=True)
    return x * lax.rsqrt(ms + RMS_EPS) * g


def _silu(x):
    return x / (1.0 + jnp.exp(-x))


def _softplus2(z2):
    return jnp.maximum(z2, 0.0) + jnp.log2(1.0 + jnp.exp2(-jnp.abs(z2)))


def _split_bf16(x):
    hi = x.astype(BF16)
    lo = (x - hi.astype(F32)).astype(BF16)
    return hi, lo


def _tri(n):
    r = lax.broadcasted_iota(jnp.int32, (n, n), 0)
    c = lax.broadcasted_iota(jnp.int32, (n, n), 1)
    return jnp.where(r >= c, 1.0, 0.0).astype(BF16)


def _proj_kernel(x_ref, g_ref, w_ref, rest_ref, qk_ref, kt_ref, vt_ref, vtb_ref, keep_ref, xn_ref,
                 *, qkv_first, blk, keep):
    xn_ref[...] = _rms(x_ref[...], g_ref[...]).astype(BF16)
    _proj_tile(xn_ref, w_ref, rest_ref, qk_ref, kt_ref, vt_ref, vtb_ref, keep_ref, qkv_first, blk, keep)


def _proj_tile(xn_ref, w_ref, rest_ref, qk_ref, kt_ref, vt_ref, vtb_ref, keep_ref, qkv_first, blk, keep):
    tm = xn_ref.shape[0]
    w = BRANCH_W
    keep_tile, keep_row = keep
    for j in range(w_ref.shape[1] // w):
        r = jnp.dot(xn_ref[...], w_ref[:, j * w:(j + 1) * w], preferred_element_type=F32)
        if j == qkv_first:
            qk_ref[:, 0:w] = (r * Q_SCALE).astype(BF16)
        elif j == qkv_first + 1:
            qk_ref[:, w:2 * w] = r.astype(BF16)
            for c in range(tm // blk):
                kt_ref[:, c * blk:(c + 1) * blk] = r[c * blk:(c + 1) * blk, :].T
        elif j == qkv_first + 2:
            for c in range(tm // blk):
                t = r[c * blk:(c + 1) * blk, :].T
                vt_ref[:, c * blk:(c + 1) * blk] = t
                vtb_ref[c] = t.astype(BF16)
        else:
            jr = j if j < qkv_first else j - 3
            rest_ref[:, jr * w:(jr + 1) * w] = r.astype(BF16)
            if jr == 1:
                @pl.when(pl.program_id(1) == keep_tile)
                def _():
                    keep_ref[...] = r[keep_row:keep_row + keep_ref.shape[0], :]


def _proj_outputs(n, bsz, seq, tm, keep_rows):
    tps = seq // tm
    blk = min(MOBA_BLOCK, tm)
    w = BRANCH_W
    m = bsz * seq
    keep_first, keep_n = keep_rows
    keep = (keep_first // tm, keep_first % tm)
    assert keep[1] + keep_n <= tm and keep_n % 8 == 0
    specs = [
        pl.BlockSpec((tm, n - 3 * w), lambda b, t: (b * tps + t, 0)),
        pl.BlockSpec((tm, 2 * w), lambda b, t: (b * tps + t, 0)),
        pl.BlockSpec((None, w, tm), lambda b, t: (b, 0, t)),
        pl.BlockSpec((None, w, tm), lambda b, t: (b, 0, t)),
        pl.BlockSpec((None, tm // blk, w, blk), lambda b, t: (b, t, 0, 0)),
        pl.BlockSpec((None, keep_n, w), lambda b, t: (b, 0, 0)),
    ]
    shapes = [
        jax.ShapeDtypeStruct((m, n - 3 * w), BF16),
        jax.ShapeDtypeStruct((m, 2 * w), BF16),
        jax.ShapeDtypeStruct((bsz, w, seq), F32),
        jax.ShapeDtypeStruct((bsz, w, seq), F32),
        jax.ShapeDtypeStruct((bsz, seq // blk, w, blk), BF16),
        jax.ShapeDtypeStruct((bsz, keep_n, w), F32),
    ]
    return specs, shapes, blk, keep


def _proj(x, g, w_bf, qkv_first, bsz, seq, tm, keep_rows):
    m, d = x.shape
    n = w_bf.shape[1]
    tps = seq // tm
    out_specs, out_shape, blk, keep = _proj_outputs(n, bsz, seq, tm, keep_rows)
    return pl.pallas_call(
        functools.partial(_proj_kernel, qkv_first=qkv_first, blk=blk, keep=keep),
        grid=(bsz, tps),
        in_specs=[
            pl.BlockSpec((tm, d), lambda b, t: (b * tps + t, 0)),
            pl.BlockSpec((1, d), lambda b, t: (0, 0)),
            pl.BlockSpec((d, n), lambda b, t: (0, 0), pipeline_mode=pl.Buffered(1)),
        ],
        out_specs=out_specs,
        out_shape=out_shape,
        scratch_shapes=[pltpu.VMEM((tm, d), BF16)],
        compiler_params=_cparams(("parallel", "arbitrary")),
    )(x, g.reshape(1, d), w_bf)


def _moba_prompt_kernel(q_ref, k_ref, vt_ref, o_ref, km_ref, *, nblk):
    _moba_prompt_block(pl.program_id(2), q_ref, k_ref, vt_ref, o_ref, km_ref, nblk)


def _moba_prompt_block(i, q_ref, k_ref, vt_ref, o_ref, km_ref, nblk):
    lane = lax.broadcasted_iota(jnp.int32, (1, LANES), 1)
    low = lane < HEAD_DIM
    nbp = km_ref.shape[0]
    blk = MOBA_BLOCK
    nheads = q_ref.shape[1] // HEAD_DIM

    @pl.when(i == 0)
    def _():
        km_ref[...] = jnp.zeros_like(km_ref)
        for j in range(nblk):
            rows = k_ref[j * blk:(j + 1) * blk, :].astype(F32)
            km_ref[j:j + 1, :] = jnp.sum(rows, axis=0, keepdims=True) * (1.0 / blk)

    brow = lax.broadcasted_iota(jnp.int32, (nbp, blk), 0)
    key_i = lax.broadcasted_iota(jnp.int32, (blk, blk), 0)
    qry_i = lax.broadcasted_iota(jnp.int32, (blk, blk), 1)
    start = pl.multiple_of(i * blk, blk)

    ones_rows = jnp.ones((2 * 8, blk), BF16)

    def v_rows(j, hh):
        return jnp.concatenate([vt_ref[j, hh * HEAD_DIM:(hh + 1) * HEAD_DIM, :], ones_rows], axis=0)

    q_owns, gates, own_scores = [], [], []
    for hh in range(nheads):
        cols = slice((hh // 2) * LANES, (hh // 2 + 1) * LANES)
        q = q_ref[:, cols]
        km = km_ref[:, cols]
        mine = low if hh % 2 == 0 else jnp.logical_not(low)
        km_hi, km_lo = _split_bf16(jnp.where(mine, km, 0.0))
        gates.append(lax.dot_general(km_hi, q, _NT, preferred_element_type=F32)
                     + lax.dot_general(km_lo, q, _NT, preferred_element_type=F32))
        q_own = jnp.where(mine, q, jnp.zeros_like(q))
        q_owns.append(q_own)
        own_scores.append(lax.dot_general(k_ref[pl.ds(start, blk), cols], q_own, _NT,
                                          preferred_element_type=F32))

    q_pasts, init = [], []
    for hh in range(nheads):
        gate = jnp.where(brow < i, gates[hh], -jnp.inf)
        bias = jnp.full((nbp, blk), NEG, F32)
        for _ in range(MOBA_TOPK):
            mx = jnp.max(gate, axis=0, keepdims=True)
            first = jnp.min(jnp.where(gate == mx, brow, nbp), axis=0, keepdims=True)
            pick = (brow == first) & (mx > -jnp.inf)
            bias = jnp.where(pick, 0.0, bias)
            gate = jnp.where(pick, -jnp.inf, gate)
        bias_t = jnp.concatenate([bias, jnp.zeros((LANES - nbp, blk), F32)], axis=0).T
        q_pasts.append(jnp.concatenate([q_owns[hh], bias_t.astype(BF16)], axis=1))

        s = jnp.where(key_i <= qry_i, own_scores[hh], NEG)
        m0 = jnp.max(s, axis=0, keepdims=True)
        p = jnp.exp2(s - m0)
        acc0 = jnp.dot(v_rows(i, hh), p.astype(BF16), preferred_element_type=F32)
        init.append((m0, acc0))

    def stage(js, carry):
        scores = []
        for j in js:
            st = pl.multiple_of(j * blk, blk)
            onehot = jnp.broadcast_to(jnp.where(lane == j, 1.0, 0.0).astype(BF16), (blk, LANES))
            for hh in range(nheads):
                kj = k_ref[pl.ds(st, blk), (hh // 2) * LANES:(hh // 2 + 1) * LANES]
                scores.append(lax.dot_general(jnp.concatenate([kj, onehot], axis=1), q_pasts[hh],
                                              _NT, preferred_element_type=F32))
        carry = list(carry)
        for n, j in enumerate(js):
            for hh in range(nheads):
                m, acc = carry[hh]
                sj = scores[n * nheads + hh]
                m_new = jnp.maximum(m, jnp.max(sj, axis=0, keepdims=True))
                alpha = jnp.exp2(m - m_new)
                pj = jnp.exp2(sj - m_new)
                acc = alpha * acc + jnp.dot(v_rows(j, hh), pj.astype(BF16),
                                            preferred_element_type=F32)
                carry[hh] = (m_new, acc)
        return tuple(carry)

    fin = tuple(init)
    base = 0
    size = MOBA_UNROLL
    while size >= 1:
        trips = (i - base) // size if size == MOBA_UNROLL else ((i - base) // size) % 2
        fin = lax.fori_loop(
            0, trips,
            lambda t, c, base=base, size=size: stage([base + t * size + u for u in range(size)], c),
            fin)
        base = base + trips * size
        size //= 2
    out_t = jnp.concatenate([acc[:HEAD_DIM] / acc[HEAD_DIM:HEAD_DIM + 1] for _, acc in fin],
                            axis=0)
    o_ref[...] = out_t.T


def _attn_prompt_call(kernel_fn, qk, vtb, bsz, seq, scratch, wid):
    nblk = seq // MOBA_BLOCK
    ngrp = BRANCH_W // wid
    return pl.pallas_call(
        kernel_fn,
        grid=(bsz, ngrp, nblk),
        in_specs=[
            pl.BlockSpec((MOBA_BLOCK, wid), lambda b, p, i: (b * nblk + i, p)),
            pl.BlockSpec((seq, wid), lambda b, p, i: (b, ngrp + p)),
            pl.BlockSpec((None, nblk, wid, MOBA_BLOCK), lambda b, p, i: (b, 0, p, 0)),
        ],
        out_specs=pl.BlockSpec((MOBA_BLOCK, wid), lambda b, p, i: (b * nblk + i, p)),
        out_shape=jax.ShapeDtypeStruct((bsz * seq, BRANCH_W), F32),
        scratch_shapes=scratch,
        compiler_params=_cparams(("parallel", "parallel", "arbitrary")),
    )(qk, qk, vtb)


MOBA_STEP_W = 512


def _moba_prompt(qk, vtb, bsz, seq):
    nblk = seq // MOBA_BLOCK
    nbp = -(-nblk // 8) * 8
    return _attn_prompt_call(functools.partial(_moba_prompt_kernel, nblk=nblk), qk, vtb, bsz, seq,
                             [pltpu.VMEM((nbp, MOBA_STEP_W), F32)], MOBA_STEP_W)


def _sb_prompt_kernel(q_ref, k_ref, vt_ref, o_ref):
    i = pl.program_id(2)
    nheads = q_ref.shape[1] // HEAD_DIM
    tile = SB_TILE
    lane = lax.broadcasted_iota(jnp.int32, (1, LANES), 1)
    low = lane < HEAD_DIM
    key_i = lax.broadcasted_iota(jnp.int32, (tile, tile), 0)
    qry_i = lax.broadcasted_iota(jnp.int32, (tile, tile), 1)
    tri_t = jnp.where(qry_i >= key_i, 1.0, 0.0).astype(BF16)
    tri_2 = jnp.concatenate([tri_t, tri_t], axis=1)
    strict = key_i < qry_i

    q_hs = []
    for hh in range(nheads):
        q = q_ref[:, (hh // 2) * LANES:(hh // 2 + 1) * LANES]
        mine = low if hh % 2 == 0 else jnp.logical_not(low)
        q_hs.append(jnp.where(mine, q, jnp.zeros_like(q)))

    def sweep(j, state, diag):
        st = pl.multiple_of(j * tile, tile)
        zs = [lax.dot_general(k_ref[pl.ds(st, tile), (hh // 2) * LANES:(hh // 2 + 1) * LANES],
                              q_hs[hh], _NT, preferred_element_type=F32)
              for hh in range(nheads)]
        cums = []
        for hh in range(nheads):
            sp = _softplus2(zs[hh])
            if diag:
                sp = jnp.where(strict, sp, 0.0)
            hi, lo = _split_bf16(sp)
            cums.append(jnp.dot(tri_2, jnp.concatenate([hi, lo], axis=0),
                                preferred_element_type=F32) + state[hh][0])
        new = []
        for hh in range(nheads):
            w = jnp.exp2(zs[hh] - cums[hh])
            if diag:
                w = jnp.where(strict, w, 0.0)
            acc = state[hh][1] + jnp.dot(vt_ref[j, hh * HEAD_DIM:(hh + 1) * HEAD_DIM, :],
                                         w.astype(BF16), preferred_element_type=F32)
            new.append((cums[hh][0:1, :], acc))
        return tuple(new)

    def live_of(state):
        spent = jnp.min(state[0][0])
        for hh in range(1, nheads):
            spent = jnp.minimum(spent, jnp.min(state[hh][0]))
        return -spent

    def cond(st):
        j, live, _ = st
        return (j >= 0) & (live > SB_DEAD)

    def body(st):
        j, _, state = st
        new = sweep(j, state, False)
        return j - 1, live_of(new), new

    zero = tuple((jnp.zeros((1, tile), F32), jnp.zeros((HEAD_DIM, tile), F32))
                 for _ in range(nheads))
    init = sweep(i, zero, True)
    _, _, fin = lax.while_loop(cond, body, (i - 1, live_of(init), init))
    o_ref[...] = jnp.concatenate([acc for _, acc in fin], axis=0).T


SB_STEP_W = 512


def _sb_prompt(qk, vtb, bsz, seq):
    return _attn_prompt_call(_sb_prompt_kernel, qk, vtb, bsz, seq, [], SB_STEP_W)


def _query_rows(qrep_ref, nheads):
    nrow, width = qrep_ref.shape[1], qrep_ref.shape[2]
    row_w = lax.broadcasted_iota(jnp.int32, (nrow, width), 0)
    lane_w = lax.broadcasted_iota(jnp.int32, (nrow, width), 1)
    head_mask = (lane_w // HEAD_DIM) == (row_w % nheads)
    qrows = jnp.where(head_mask, qrep_ref[0], jnp.zeros_like(qrep_ref[0]))
    return qrows, head_mask


def _pick_heads(y, head_mask, nheads):
    nrow, width = y.shape
    y = jnp.where(head_mask, y, 0.0)
    return jnp.sum(y.reshape(nrow // nheads, nheads, width), axis=1)


def _moba_sample_weights(qrows, knew_ref, s_ref, p_ref, nheads):
    nblk, nrow = s_ref.shape[0], s_ref.shape[1]
    row = lax.broadcasted_iota(jnp.int32, (nrow, LANES), 0)
    lane = lax.broadcasted_iota(jnp.int32, (nrow, LANES), 1)
    qidx = row // nheads
    zn = lax.dot_general(qrows, knew_ref[0], _NT, preferred_element_type=F32)
    gates = jnp.zeros((nrow, LANES), F32)
    for bl in range(nblk):
        gates = jnp.where(lane == bl, jnp.sum(s_ref[bl], axis=1, keepdims=True), gates)
    gates = jnp.where(lane < nblk, gates, -jnp.inf)
    sel = jnp.zeros((nrow, LANES), F32)
    for _ in range(MOBA_TOPK):
        mx = jnp.max(gates, axis=1, keepdims=True)
        first = jnp.min(jnp.where(gates == mx, lane, LANES), axis=1, keepdims=True)
        pick = (lane == first) & (mx > -jnp.inf)
        sel = jnp.where(pick, 1.0, sel)
        gates = jnp.where(pick, -jnp.inf, gates)
    zn = jnp.where(lane <= qidx, zn, NEG)
    m = jnp.max(zn, axis=1, keepdims=True)
    cols = []
    for bl in range(nblk):
        col = jnp.max(jnp.where(lane == bl, sel, 0.0), axis=1, keepdims=True) > 0.5
        cols.append(col)
        m = jnp.maximum(m, jnp.max(jnp.where(col, s_ref[bl], NEG), axis=1, keepdims=True))
    pn = jnp.exp2(zn - m)
    l = jnp.sum(pn, axis=1, keepdims=True)
    for bl in range(nblk):
        p = jnp.exp2(jnp.where(cols[bl], s_ref[bl], NEG) - m)
        l = l + jnp.sum(p, axis=1, keepdims=True)
        p_ref[bl] = p.astype(BF16)
    return pn, l


def _moba_paged_kernel(pt_ref, qrep_ref, knew_ref, vnew_ref, k_hbm, v_hbm, o_ref,
                       kbuf, vbuf, ksem, vsem, s_ref, p_ref, *, npages, nheads):
    b = pl.program_id(0)
    ring = kbuf.shape[0]
    page = kbuf.shape[2]
    ppb = MOBA_BLOCK // page
    qrows, head_mask = _query_rows(qrep_ref, nheads)

    def k_copy(seq, p, slot):
        return pltpu.make_async_copy(k_hbm.at[pt_ref[seq * npages + p]], kbuf.at[slot], ksem.at[slot])

    def v_copy(p, slot):
        return pltpu.make_async_copy(v_hbm.at[pt_ref[b * npages + p]], vbuf.at[slot], vsem.at[slot])

    @pl.when(b == 0)
    def _():
        for p in range(ring):
            k_copy(0, p, p).start()

    grp = min(PAGE_GROUP, ring)

    def k_group(gi, carry):
        base = gi * grp
        slot0 = (gi % (ring // grp)) * grp
        for u in range(grp):
            k_copy(b, base + u, slot0 + u).wait()
        for u in range(grp):
            s_ref[gi * (grp // ppb) + u // ppb, :, (u % ppb) * page:(u % ppb + 1) * page] = jnp.dot(
                qrows, kbuf[slot0 + u].astype(BF16), preferred_element_type=F32)

        @pl.when(base + ring < npages)
        def _():
            for u in range(grp):
                k_copy(b, base + ring + u, slot0 + u).start()
        return carry

    lax.fori_loop(0, npages // grp, k_group, 0)
    for p in range(ring):
        v_copy(p, p).start()

    @pl.when(b + 1 < pl.num_programs(0))
    def _():
        for p in range(ring):
            k_copy(b + 1, p, p).start()

    pn, l = _moba_sample_weights(qrows, knew_ref, s_ref, p_ref, nheads)
    acc0 = jnp.dot(pn.astype(BF16), vnew_ref[0], preferred_element_type=F32)

    def v_group(gi, acc):
        base = gi * grp
        slot0 = (gi % (ring // grp)) * grp
        for u in range(grp):
            v_copy(base + u, slot0 + u).wait()
        for u in range(grp):
            pw = p_ref[gi * (grp // ppb) + u // ppb, :, (u % ppb) * page:(u % ppb + 1) * page]
            acc = acc + lax.dot_general(pw, vbuf[slot0 + u].astype(BF16), _NT,
                                        preferred_element_type=F32)

        @pl.when(base + ring < npages)
        def _():
            for u in range(grp):
                v_copy(base + ring + u, slot0 + u).start()
        return acc

    acc = lax.fori_loop(0, npages // grp, v_group, acc0)
    o_ref[0] = _pick_heads(acc / l, head_mask, nheads)


def _paged_operands(q_bf, knew_bf, vnew_bf, cache_k, cache_v, page_table):
    nb, nq, width = q_bf.shape
    nheads = width // HEAD_DIM
    nphys, page = cache_k.shape[0], cache_k.shape[1]
    ck = cache_k.transpose(0, 2, 3, 1).reshape(nphys, width, page)
    cv = cache_v.transpose(0, 2, 3, 1).reshape(nphys, width, page)
    qrep = jnp.repeat(q_bf, nheads, axis=1)
    pad = ((0, 0), (0, LANES - nq), (0, 0))
    knew = jnp.pad(knew_bf, pad)
    vnew = jnp.pad(vnew_bf, pad)
    pt = page_table.reshape(-1).astype(jnp.int32)
    return qrep, knew, vnew, ck, cv, pt, nheads, page


def _moba_paged(q_bf, knew_bf, vnew_bf, cache_k, cache_v, page_table):
    nb, nq, width = q_bf.shape
    qrep, knew, vnew, ck, cv, pt, nheads, page = _paged_operands(
        q_bf, knew_bf, vnew_bf, cache_k, cache_v, page_table)
    nrow = nq * nheads
    npages = page_table.shape[1]
    ppb = MOBA_BLOCK // page
    ring = min(PAGE_RING, npages)
    grp = min(PAGE_GROUP, ring)
    assert grp % ppb == 0 and ring % grp == 0 and npages % ring == 0
    per_b = lambda b, pt: (b, 0, 0)
    grid_spec = pltpu.PrefetchScalarGridSpec(
        num_scalar_prefetch=1,
        grid=(nb,),
        in_specs=[
            pl.BlockSpec((1, nrow, width), per_b),
            pl.BlockSpec((1, LANES, width), per_b),
            pl.BlockSpec((1, LANES, width), per_b),
            pl.BlockSpec(memory_space=pl.ANY),
            pl.BlockSpec(memory_space=pl.ANY),
        ],
        out_specs=pl.BlockSpec((1, nq, width), per_b),
        scratch_shapes=[
            pltpu.VMEM((ring, width, page), F32),
            pltpu.VMEM((ring, width, page), F32),
            pltpu.SemaphoreType.DMA((ring,)),
            pltpu.SemaphoreType.DMA((ring,)),
            pltpu.VMEM((npages // ppb, nrow, MOBA_BLOCK), F32),
            pltpu.VMEM((npages // ppb, nrow, MOBA_BLOCK), BF16),
        ],
    )
    return pl.pallas_call(
        functools.partial(_moba_paged_kernel, npages=npages, nheads=nheads),
        grid_spec=grid_spec,
        out_shape=jax.ShapeDtypeStruct((nb, nq, width), F32),
        compiler_params=_cparams(("arbitrary",)),
    )(pt, qrep, knew, vnew, ck, cv)


def _moba_fused_kernel(pt_ref, q_ref, k_ref, vt_ref, qrep_ref, knew_ref, vnew_ref, kc_hbm, vc_hbm,
                       o_ref, os_ref, km_ref, pbuf, psem, s_ref, p_ref, acc_ref, l_ref,
                       *, nblk, npages, nheads):
    i = pl.program_id(2)
    step = pl.program_id(0) * nblk + i
    seq = step // 2
    page = pbuf.shape[2]
    ppb = MOBA_BLOCK // page
    grp = min(PAGE_GROUP, npages)

    half = npages // 2
    gph = half // grp
    even = step % 2 == 0
    odd = step % 2 == 1
    has_next = step + 1 < pl.num_programs(0) * nblk

    def page_copy(cache_hbm, sq, p):
        return pltpu.make_async_copy(cache_hbm.at[pt_ref[sq * npages + p]], pbuf.at[p], psem.at[p])

    def start_half(cache_hbm, sq, h):
        for p in range(h * half, (h + 1) * half):
            page_copy(cache_hbm, sq, p).start()

    def wait_half(cache_hbm, sq, h):
        for p in range(h * half, (h + 1) * half):
            page_copy(cache_hbm, sq, p).wait()

    def scores_half(h):
        def k_group(gi, carry):
            for u in range(grp):
                s_ref[gi * (grp // ppb) + u // ppb, :, (u % ppb) * page:(u % ppb + 1) * page] = (
                    jnp.dot(qrows, pbuf[gi * grp + u].astype(BF16), preferred_element_type=F32))
            return carry
        lax.fori_loop(h * gph, (h + 1) * gph, k_group, 0)

    def values_half(h, acc):
        def v_group(gi, acc):
            for u in range(grp):
                pw = p_ref[gi * (grp // ppb) + u // ppb, :, (u % ppb) * page:(u % ppb + 1) * page]
                acc = acc + lax.dot_general(pw, pbuf[gi * grp + u].astype(BF16), _NT,
                                            preferred_element_type=F32)
            return acc
        return lax.fori_loop(h * gph, (h + 1) * gph, v_group, acc)

    qrows, head_mask = _query_rows(qrep_ref, nheads)

    @pl.when(step == 0)
    def _():
        start_half(kc_hbm, 0, 0)
        start_half(kc_hbm, 0, 1)

    @pl.when(even)
    def _():
        wait_half(kc_hbm, seq, 0)
        scores_half(0)
        start_half(vc_hbm, seq, 0)

    @pl.when(odd)
    def _():
        pn, l = _moba_sample_weights(qrows, knew_ref, s_ref, p_ref, nheads)
        l_ref[...] = jnp.broadcast_to(l, l_ref.shape)
        acc0 = jnp.dot(pn.astype(BF16), vnew_ref[0], preferred_element_type=F32)
        wait_half(vc_hbm, seq, 0)
        acc_ref[...] = values_half(0, acc0)

        @pl.when(has_next)
        def _():
            start_half(kc_hbm, seq + 1, 0)

    _moba_prompt_block(i, q_ref, k_ref, vt_ref, o_ref, km_ref, nblk)

    @pl.when(even)
    def _():
        wait_half(kc_hbm, seq, 1)
        scores_half(1)
        start_half(vc_hbm, seq, 1)

    @pl.when(odd)
    def _():
        wait_half(vc_hbm, seq, 1)
        acc = values_half(1, acc_ref[...])
        os_ref[0] = _pick_heads(acc / l_ref[:, 0:1], head_mask, nheads)

        @pl.when(has_next)
        def _():
            start_half(kc_hbm, seq + 1, 1)


def _moba_fused(qk, vtb, bsz, seq, q_bf, knew_bf, vnew_bf, cache_k, cache_v, page_table):
    nb, nq, width = q_bf.shape
    qrep, knew, vnew, ck, cv, pt, nheads, page = _paged_operands(
        q_bf, knew_bf, vnew_bf, cache_k, cache_v, page_table)
    nrow = nq * nheads
    npages = page_table.shape[1]
    nblk = seq // MOBA_BLOCK
    nbp = -(-nblk // 8) * 8
    wid = MOBA_STEP_W
    ppb = MOBA_BLOCK // page
    grp = min(PAGE_GROUP, npages)
    assert bsz * nblk == 2 * nb and wid == BRANCH_W and grp % ppb == 0 and npages % (2 * grp) == 0
    per_seq = lambda b, p, i, pt: ((b * nblk + i) // 2, 0, 0)
    grid_spec = pltpu.PrefetchScalarGridSpec(
        num_scalar_prefetch=1,
        grid=(bsz, 1, nblk),
        in_specs=[
            pl.BlockSpec((MOBA_BLOCK, wid), lambda b, p, i, pt: (b * nblk + i, 0)),
            pl.BlockSpec((seq, wid), lambda b, p, i, pt: (b, 1)),
            pl.BlockSpec((None, nblk, wid, MOBA_BLOCK), lambda b, p, i, pt: (b, 0, 0, 0)),
            pl.BlockSpec((1, nrow, width), per_seq),
            pl.BlockSpec((1, LANES, width), per_seq),
            pl.BlockSpec((1, LANES, width), per_seq),
            pl.BlockSpec(memory_space=pl.ANY),
            pl.BlockSpec(memory_space=pl.ANY),
        ],
        out_specs=[
            pl.BlockSpec((MOBA_BLOCK, wid), lambda b, p, i, pt: (b * nblk + i, 0)),
            pl.BlockSpec((1, nq, width), per_seq),
        ],
        scratch_shapes=[
            pltpu.VMEM((nbp, wid), F32),
            pltpu.VMEM((npages, width, page), F32),
            pltpu.SemaphoreType.DMA((npages,)),
            pltpu.VMEM((npages // ppb, nrow, MOBA_BLOCK), F32),
            pltpu.VMEM((npages // ppb, nrow, MOBA_BLOCK), BF16),
            pltpu.VMEM((nrow, width), F32),
            pltpu.VMEM((nrow, LANES), F32),
        ],
    )
    return pl.pallas_call(
        functools.partial(_moba_fused_kernel, nblk=nblk, npages=npages, nheads=nheads),
        grid_spec=grid_spec,
        out_shape=[jax.ShapeDtypeStruct((bsz * seq, BRANCH_W), F32),
                   jax.ShapeDtypeStruct((nb, nq, width), F32)],
        compiler_params=_cparams(("arbitrary", "arbitrary", "arbitrary")),
    )(pt, qk, qk, vtb, qrep, knew, vnew, ck, cv)


def _sb_paged_kernel(pt_ref, live_ref, qrep_ref, knew_ref, vnew_ref, cin_ref, ain_ref, *rest,
                     group, nheads, first):
    del pt_ref
    k_refs = rest[:group]
    v_refs = rest[group:2 * group]
    y_ref, acc_ref, carry_ref = rest[2 * group:2 * group + 3]
    b = pl.program_id(0)
    s = pl.program_id(1)
    nrow = qrep_ref.shape[1]
    page = k_refs[0].shape[1]
    ppt = SB_TILE // page
    qrows, head_mask = _query_rows(qrep_ref, nheads)
    tri = _tri(SB_TILE)

    @pl.when(s == 0)
    def _():
        if first:
            row = lax.broadcasted_iota(jnp.int32, (nrow, LANES), 0)
            lane = lax.broadcasted_iota(jnp.int32, (nrow, LANES), 1)
            strict = lane < row // nheads
            zn = lax.dot_general(qrows, knew_ref[0], _NT, preferred_element_type=F32)
            hi, lo = _split_bf16(jnp.where(strict, -_softplus2(zn), 0.0))
            tri_n = _tri(LANES)
            cum = (jnp.dot(hi, tri_n, preferred_element_type=F32)
                   + jnp.dot(lo, tri_n, preferred_element_type=F32))
            wn = jnp.where(strict, jnp.exp2(zn + cum), 0.0)
            acc_ref[0] = jnp.dot(wn.astype(BF16), vnew_ref[0], preferred_element_type=F32)
            carry_ref[0] = jnp.broadcast_to(cum[:, 0:1], carry_ref.shape[1:])
        else:
            acc_ref[0] = ain_ref[0]
            carry_ref[0] = cin_ref[0]

    @pl.when(live_ref[b] > 0)
    def _():
        carry = carry_ref[0][:, 0:1]
        acc = acc_ref[0]
        for t in reversed(range(group // ppt)):
            kt = jnp.concatenate([k_refs[t * ppt + u][...] for u in range(ppt)], axis=1).astype(BF16)
            vt = jnp.concatenate([v_refs[t * ppt + u][...] for u in range(ppt)], axis=1).astype(BF16)
            z = jnp.dot(qrows, kt, preferred_element_type=F32)
            hi, lo = _split_bf16(-_softplus2(z))
            cum = (jnp.dot(hi, tri, preferred_element_type=F32)
                   + jnp.dot(lo, tri, preferred_element_type=F32)) + carry
            w = jnp.exp2(z + cum).astype(BF16)
            acc = acc + lax.dot_general(w, vt, _NT, preferred_element_type=F32)
            carry = cum[:, 0:1]
        acc_ref[0] = acc
        carry_ref[0] = jnp.broadcast_to(carry, carry_ref.shape[1:])

    @pl.when(s == pl.num_programs(1) - 1)
    def _():
        y_ref[0] = _pick_heads(acc_ref[0], head_mask, nheads)


def _sb_paged_call(first, pt, live, qrep, knew, vnew, carry_in, acc_in, ck, cv,
                   nq, npages, group, chunk_lo, nsteps, nheads):
    nb, nrow, width = qrep.shape
    page = ck.shape[2]

    def page_map(g):
        def index(b, s, pt, live):
            chunk_page = (chunk_lo + nsteps - 1 - s) * group + g
            return (jnp.where(live[b] > 0, pt[b * npages + chunk_page], 0), 0, 0)
        return index

    per_b = lambda b, s, pt, live: (b, 0, 0)
    in_specs = [
        pl.BlockSpec((1, nrow, width), per_b),
        pl.BlockSpec((1, LANES, width), per_b),
        pl.BlockSpec((1, LANES, width), per_b),
        pl.BlockSpec((1, nrow, LANES), per_b),
        pl.BlockSpec((1, nrow, width), per_b),
    ]
    in_specs += [pl.BlockSpec((None, width, page), page_map(g), pipeline_mode=pl.Buffered(PAGE_BUFFERS))
                 for g in range(group)] * 2
    grid_spec = pltpu.PrefetchScalarGridSpec(
        num_scalar_prefetch=2,
        grid=(nb, nsteps),
        in_specs=in_specs,
        out_specs=[pl.BlockSpec((1, nq, width), per_b),
                   pl.BlockSpec((1, nrow, width), per_b),
                   pl.BlockSpec((1, nrow, LANES), per_b)],
    )
    return pl.pallas_call(
        functools.partial(_sb_paged_kernel, group=group, nheads=nheads, first=first),
        grid_spec=grid_spec,
        out_shape=[jax.ShapeDtypeStruct((nb, nq, width), F32),
                   jax.ShapeDtypeStruct((nb, nrow, width), F32),
                   jax.ShapeDtypeStruct((nb, nrow, LANES), F32)],
        compiler_params=_cparams(("parallel", "arbitrary")),
    )(pt, live, qrep, knew, vnew, carry_in, acc_in, *([ck] * group), *([cv] * group))


def _sb_paged(q_bf, knew_bf, vnew_bf, cache_k, cache_v, page_table, group):
    nb, nq, width = q_bf.shape
    qrep, knew, vnew, ck, cv, pt, nheads, page = _paged_operands(
        q_bf, knew_bf, vnew_bf, cache_k, cache_v, page_table)
    nrow = nq * nheads
    npages = page_table.shape[1]
    nchunk = npages // group
    zc = jnp.zeros((nb, nrow, LANES), F32)
    za = jnp.zeros((nb, nrow, width), F32)
    all_live = jnp.ones((nb,), jnp.int32)
    y, acc, carry = _sb_paged_call(True, pt, all_live, qrep, knew, vnew, zc, za, ck, cv,
                                   nq, npages, group, nchunk - 1, 1, nheads)
    if nchunk == 1:
        return y
    live = (jnp.max(carry, axis=(1, 2)) > SB_DEAD).astype(jnp.int32)

    def older(_):
        return _sb_paged_call(False, pt, live, qrep, knew, vnew, carry, acc, ck, cv,
                              nq, npages, group, 0, nchunk - 1, nheads)[0]

    return lax.cond(jnp.any(live > 0), older, lambda _: y, None)


def _finish(y_first, y_second, w_ref, g_ref, x_ref, o_ref):
    half = y_first.shape[1]
    out = (jnp.dot(y_first.astype(BF16), w_ref[0:half, :], preferred_element_type=F32)
           + jnp.dot(y_second.astype(BF16), w_ref[half:2 * half, :], preferred_element_type=F32))
    h = x_ref[...] + _rms(out, g_ref[...])
    o_ref[...] = h
    return h


def _even_out_prompt_kernel(ab_ref, ac_ref, ah_ref, az_ref, bz_ref, hc_ref, hh_ref, yb_ref, x_ref,
                            w_ref, cw_ref, g_ref, *rest, next_proj):
    if next_proj is None:
        o_ref, tail_ref, ue_ref = rest
    else:
        g2_ref, w2_ref, o_ref, tail_ref, *proj_out, ue_ref, xn_ref = rest
    tm = ab_ref.shape[0]
    hrows = hc_ref.shape[0]
    u = ac_ref[...].astype(F32) * ah_ref[...].astype(F32)
    first = pl.program_id(1) == 0
    halo = hc_ref[...].astype(F32) * hh_ref[...].astype(F32)
    ue_ref[0:8, :] = jnp.where(first, 0.0, halo[hrows - 8:hrows, :])
    ue_ref[8:8 + tm, :] = u
    conv = (ue_ref[6:6 + tm, :] * cw_ref[0:1, :] + ue_ref[7:7 + tm, :] * cw_ref[1:2, :]
            + u * cw_ref[2:3, :])
    y_a = ab_ref[...].astype(F32) * conv * _silu(az_ref[...].astype(F32))
    y_b = yb_ref[...] * _silu(bz_ref[...].astype(F32))
    tail_ref[...] = u[tm - 8:tm, :]
    h = _finish(y_a, y_b, w_ref, g_ref, x_ref, o_ref)
    if next_proj is not None:
        qkv_first, blk, keep = next_proj
        xn_ref[...] = _rms(h, g2_ref[...]).astype(BF16)
        _proj_tile(xn_ref, w2_ref, *proj_out, qkv_first, blk, keep)


def _even_out_prompt(rest, y_b, x, w_out_bf, conv_w, g_post, bsz, seq, tm, nxt=None):
    m, d = x.shape
    tps = seq // tm
    w = BRANCH_W
    row = lambda b, t: b * tps + t
    col = lambda c: pl.BlockSpec((tm, w), lambda b, t, c=c: (row(b, t), c))
    hrows = 16
    halo = lambda c: pl.BlockSpec(
        (hrows, w), lambda b, t, c=c: (jnp.maximum(row(b, t) * (tm // hrows) - 1, 0), c))
    const = lambda shape: pl.BlockSpec(shape, lambda b, t: (0,) * len(shape),
                                       pipeline_mode=pl.Buffered(1))
    in_specs = [col(0), col(1), col(2), col(3), col(4), halo(1), halo(2),
                pl.BlockSpec((tm, w), lambda b, t: (row(b, t), 0)),
                pl.BlockSpec((tm, d), lambda b, t: (row(b, t), 0)),
                const(w_out_bf.shape), const(conv_w.shape), const((1, d))]
    args = [rest, rest, rest, rest, rest, rest, rest, y_b, x, w_out_bf, conv_w, g_post.reshape(1, d)]
    out_specs = [pl.BlockSpec((tm, d), lambda b, t: (row(b, t), 0)),
                 pl.BlockSpec((8, w), lambda b, t: (row(b, t), 0))]
    out_shape = [jax.ShapeDtypeStruct((m, d), F32),
                 jax.ShapeDtypeStruct((bsz * tps * 8, w), F32)]
    scratch = [pltpu.VMEM((tm + 8, w), F32)]
    next_proj = None
    if nxt is not None:
        g2, w2_bf, qkv_first, keep_rows = nxt
        specs2, shapes2, blk, keep = _proj_outputs(w2_bf.shape[1], bsz, seq, tm, keep_rows)
        in_specs += [const((1, d)), const(w2_bf.shape)]
        args += [g2.reshape(1, d), w2_bf]
        out_specs += specs2
        out_shape += shapes2
        scratch.append(pltpu.VMEM((tm, d), BF16))
        next_proj = (qkv_first, blk, keep)
    outs = pl.pallas_call(
        functools.partial(_even_out_prompt_kernel, next_proj=next_proj),
        grid=(bsz, tps),
        in_specs=in_specs,
        out_specs=out_specs,
        out_shape=out_shape,
        scratch_shapes=scratch,
        compiler_params=_cparams(("parallel", "arbitrary")),
    )(*args)
    return outs[0], outs[1], (tuple(outs[2:]) if nxt is not None else None)


def _even_out_sample_kernel(rest_ref, st_ref, yb_ref, x_ref, w_ref, cw_ref, g_ref,
                            o_ref, tail_ref, *, nb):
    w = BRANCH_W
    rows = rest_ref.shape[0]
    rest = rest_ref[...].astype(F32)
    u = rest[:, w:2 * w] * rest[:, 2 * w:3 * w]
    ue = jnp.concatenate([st_ref[...], u], axis=0)
    conv = (ue[0:rows] * cw_ref[0:1, :] + ue[nb:nb + rows] * cw_ref[1:2, :]
            + ue[2 * nb:2 * nb + rows] * cw_ref[2:3, :])
    y_a = rest[:, 0:w] * conv * _silu(rest[:, 3 * w:4 * w])
    y_b = yb_ref[...] * _silu(rest[:, 4 * w:5 * w])
    tail_ref[...] = ue[rows:rows + 2 * nb]
    _finish(y_a, y_b, w_ref, g_ref, x_ref, o_ref)


def _even_out_sample(rest, state_tm, y_b, x, w_out_bf, conv_w, g_post, nb):
    m, d = x.shape
    return pl.pallas_call(
        functools.partial(_even_out_sample_kernel, nb=nb),
        out_shape=[jax.ShapeDtypeStruct((m, d), F32),
                   jax.ShapeDtypeStruct(((CONV_W - 1) * nb, BRANCH_W), F32)],
        compiler_params=pltpu.CompilerParams(vmem_limit_bytes=VMEM_LIMIT),
    )(rest, state_tm, y_b, x, w_out_bf, conv_w, g_post.reshape(1, d))


def _gmlp_mix(v, gw_ref, low):
    r_i = lax.broadcasted_iota(jnp.int32, (GMLP_CHUNK, GMLP_CHUNK), 0)
    c_i = lax.broadcasted_iota(jnp.int32, (GMLP_CHUNK, GMLP_CHUNK), 1)
    tril = c_i <= r_i
    parts = []
    for p in range(v.shape[1] // LANES):
        vp = v[:, p * LANES:(p + 1) * LANES]
        v_lo = jnp.where(low, vp, jnp.zeros_like(vp)).astype(BF16)
        v_hi = jnp.where(low, jnp.zeros_like(vp), vp).astype(BF16)
        w_lo = jnp.where(tril, gw_ref[2 * p], 0.0).astype(BF16)
        w_hi = jnp.where(tril, gw_ref[2 * p + 1], 0.0).astype(BF16)
        parts.append(jnp.dot(w_lo, v_lo, preferred_element_type=F32)
                     + jnp.dot(w_hi, v_hi, preferred_element_type=F32))
    return jnp.concatenate(parts, axis=1)


def _odd_out_prompt_kernel(cu_ref, cv_ref, cz_ref, dz_ref, yd_ref, x_ref, w_ref, gw_ref, gb_ref,
                           g_ref, o_ref, yc_ref):
    tm = cu_ref.shape[0]
    low = lax.broadcasted_iota(jnp.int32, (1, LANES), 1) < HEAD_DIM
    for c in range(tm // GMLP_CHUNK):
        rows = slice(c * GMLP_CHUNK, (c + 1) * GMLP_CHUNK)
        mixed = _gmlp_mix(cv_ref[rows, :], gw_ref, low) + gb_ref[...]
        yc_ref[rows, :] = (cu_ref[rows, :].astype(F32) * mixed
                           * _silu(cz_ref[rows, :].astype(F32)))
    y_d = yd_ref[...] * _silu(dz_ref[...].astype(F32))
    _finish(yc_ref[...], y_d, w_ref, g_ref, x_ref, o_ref)


def _odd_out_prompt(rest, y_d, x, w_out_bf, gmlp_w, gb_full, g_post, tm):
    m, d = x.shape
    col = lambda c: pl.BlockSpec((tm, BRANCH_W), lambda i, c=c: (i, c))
    return pl.pallas_call(
        _odd_out_prompt_kernel,
        grid=(m // tm,),
        in_specs=[col(0), col(1), col(2), col(3),
                  pl.BlockSpec((tm, BRANCH_W), lambda i: (i, 0)),
                  pl.BlockSpec((tm, d), lambda i: (i, 0)),
                  pl.BlockSpec(w_out_bf.shape, lambda i: (0, 0)),
                  pl.BlockSpec(gmlp_w.shape, lambda i: (0, 0, 0)),
                  pl.BlockSpec(gb_full.shape, lambda i: (0, 0)),
                  pl.BlockSpec((1, d), lambda i: (0, 0))],
        out_specs=pl.BlockSpec((tm, d), lambda i: (i, 0)),
        out_shape=jax.ShapeDtypeStruct((m, d), F32),
        scratch_shapes=[pltpu.VMEM((tm, BRANCH_W), F32)],
        compiler_params=_cparams(("parallel",)),
    )(rest, rest, rest, rest, y_d, x, w_out_bf, gmlp_w, gb_full, g_post.reshape(1, d))


def _odd_out_sample_kernel(rest_ref, yd_ref, x_ref, w_ref, w4_ref, b4_ref, g_ref, o_ref, *, nb, nq):
    w = BRANCH_W
    rest = rest_ref[...].astype(F32)
    parts = []
    for t in range(nq):
        mixed = jnp.broadcast_to(b4_ref[t:t + 1, :], (nb, w))
        for s in range(t + 1):
            mixed = mixed + w4_ref[t * nq + s:t * nq + s + 1, :] * rest[s * nb:(s + 1) * nb, w:2 * w]
        parts.append(mixed)
    mixed = jnp.concatenate(parts, axis=0)
    y_c = rest[:, 0:w] * mixed * _silu(rest[:, 2 * w:3 * w])
    y_d = yd_ref[...] * _silu(rest[:, 3 * w:4 * w])
    _finish(y_c, y_d, w_ref, g_ref, x_ref, o_ref)


def _odd_out_sample(rest, y_d, x, w_out_bf, w4, b4, g_post, nb, nq):
    m, d = x.shape
    return pl.pallas_call(
        functools.partial(_odd_out_sample_kernel, nb=nb, nq=nq),
        out_shape=jax.ShapeDtypeStruct((m, d), F32),
        compiler_params=pltpu.CompilerParams(vmem_limit_bytes=VMEM_LIMIT),
    )(rest, y_d, x, w_out_bf, w4, b4, g_post.reshape(1, d))


def _to_token_major(a):
    nb, nq, w = a.shape
    return a.transpose(1, 0, 2).reshape(nq * nb, w)


def _to_batch_major(a, nb):
    w = a.shape[1]
    return a.reshape(-1, nb, w).transpose(1, 0, 2)


def _heads_prompt(t, heads):
    bsz, w, seq = t.shape
    return t.reshape(bsz, heads, w // heads, seq).transpose(0, 3, 1, 2)


def _open_rows(seq):
    open_start = ((seq - 1) // GMLP_CHUNK) * GMLP_CHUNK
    return open_start, seq - open_start


def _project_prompt(hp, dims, g_pre, w_in_bf, qkv_first, tm):
    bsz, seq, _, _ = dims
    return _proj(hp, g_pre, w_in_bf, qkv_first, bsz, seq, tm, _open_rows(seq))


def _project_sample(hs, dims, g_pre, w_in_bf, qkv_first, tm):
    _, _, nb, nq = dims
    ns = nq * nb
    rows = max(ns, tm)
    hs_pad = jnp.pad(hs, ((0, rows - ns), (0, 0)))
    rest_s, qk_s, kt_s, vt_s, _, keep_s = _proj(hs_pad, g_pre, w_in_bf, qkv_first, 1, rows, rows,
                                                (0, ns))
    k_s = _to_batch_major(kt_s[0, :, :ns].T, nb)
    v_s = _to_batch_major(vt_s[0, :, :ns].T, nb)
    q_s = _to_batch_major(qk_s[:ns, :BRANCH_W], nb)
    return rest_s[:ns], q_s, k_s, v_s, keep_s[0]


def _even_layer(hp, hs, dims, state_conv, cache_k, cache_v, page_table,
                g_pre, g_post, w_in, conv_w, w_out, tm, proj_prompt, nxt):
    bsz, seq, nb, nq = dims
    w = BRANCH_W
    heads = w // HEAD_DIM
    w_in_bf = w_in.astype(BF16)
    w_out_bf = w_out.astype(BF16)
    if proj_prompt is None:
        proj_prompt = _project_prompt(hp, dims, g_pre, w_in_bf, 4, tm)
    rest_p, qk_p, kt_p, vt_p, vtb_p, _ = proj_prompt
    rest_s, q_s, k_s, v_s, _ = _project_sample(hs, dims, g_pre, w_in_bf, 4, tm)
    sample_qkv = (q_s, k_s.astype(BF16), v_s.astype(BF16))
    if bsz * (seq // MOBA_BLOCK) == 2 * nb:
        yb_p, yb_s = _moba_fused(qk_p, vtb_p, bsz, seq, *sample_qkv, cache_k, cache_v, page_table)
    else:
        yb_p = _moba_prompt(qk_p, vtb_p, bsz, seq)
        yb_s = _moba_paged(*sample_qkv, cache_k, cache_v, page_table)
    yb_s = _to_token_major(yb_s)
    if nxt is not None:
        nxt = (*nxt, _open_rows(seq))
    hp, tail_p, proj_next = _even_out_prompt(rest_p, yb_p, hp, w_out_bf, conv_w, g_post,
                                             bsz, seq, tm, nxt)
    state_tm = _to_token_major(state_conv)
    hs, tail_s = _even_out_sample(rest_s, state_tm, yb_s, hs, w_out_bf, conv_w, g_post, nb)
    conv_p = tail_p.reshape(bsz, seq // tm, 8, w)[:, -1, 8 - (CONV_W - 1):, :]
    conv_s = _to_batch_major(tail_s, nb)
    outs = (conv_p, conv_s, _heads_prompt(kt_p, heads), _heads_prompt(vt_p, heads),
            k_s.reshape(nb, nq, heads, HEAD_DIM), v_s.reshape(nb, nq, heads, HEAD_DIM))
    return hp, hs, proj_next, outs


def _odd_layer(hp, hs, dims, cache_k, cache_v, page_table,
               g_pre, g_post, w_in_bf, gmlp_w, gmlp_b, w_out, tm, group, proj_prompt):
    bsz, seq, nb, nq = dims
    w = BRANCH_W
    heads = w // HEAD_DIM
    w_out_bf = w_out.astype(BF16)
    ngroups = gmlp_w.shape[0]
    cpg = w // ngroups
    if proj_prompt is None:
        proj_prompt = _project_prompt(hp, dims, g_pre, w_in_bf, 3, tm)
    rest_p, qk_p, kt_p, vt_p, vtb_p, gv_p = proj_prompt
    rest_s, q_s, k_s, v_s, keep_s = _project_sample(hs, dims, g_pre, w_in_bf, 3, tm)
    yd_p = _sb_prompt(qk_p, vtb_p, bsz, seq)
    yd_s = _to_token_major(_sb_paged(q_s, k_s.astype(BF16), v_s.astype(BF16),
                                     cache_k, cache_v, page_table, group))
    gb_full = jnp.repeat(gmlp_b.T, cpg, axis=1)
    hp = _odd_out_prompt(rest_p, yd_p, hp, w_out_bf, gmlp_w, gb_full, g_post, tm)
    w4 = jnp.repeat(gmlp_w[:, :nq, :nq].transpose(1, 2, 0), cpg, axis=2).reshape(nq * nq, w)
    b4 = jnp.repeat(gmlp_b[:, :nq].T, cpg, axis=1)
    hs = _odd_out_sample(rest_s, yd_s, hs, w_out_bf, w4, b4, g_post, nb, nq)
    gv_s = _to_batch_major(keep_s, nb)
    outs = (_heads_prompt(kt_p, heads), _heads_prompt(vt_p, heads),
            k_s.reshape(nb, nq, heads, HEAD_DIM), v_s.reshape(nb, nq, heads, HEAD_DIM), gv_p, gv_s)
    return hp, hs, outs


def kernel(x_prompt, x_sample, state_conv, cache_k_moba, cache_v_moba, cache_k_sb, cache_v_sb,
           page_table, norm_pre_e, norm_post_e, w_in_e, conv_w, w_out_e,
           norm_pre_o, norm_post_o, w_in_o, gmlp_w, gmlp_b, w_out_o):
    bsz, seq, d = x_prompt.shape
    nb, nq, _ = x_sample.shape
    depth = norm_pre_e.shape[0] + norm_pre_o.shape[0]
    npages, page = page_table.shape[1], cache_k_moba.shape[2]
    assert seq % MOBA_BLOCK == 0 and (npages * page) % MOBA_BLOCK == 0 and nq <= GMLP_CHUNK
    assert (nq - 1) // MOBA_BLOCK == 0 and w_in_e.shape[2] == 8 * BRANCH_W and w_in_o.shape[2] == 7 * BRANCH_W
    dims = (bsz, seq, nb, nq)
    tm = min(512, seq)
    group_sb = min(4, npages)
    hp = x_prompt.reshape(bsz * seq, d)
    hs = _to_token_major(x_sample)
    ev, od = [], []
    pending = None
    for i in range(depth):
        j = i // 2
        if i % 2 == 0:
            nxt = (norm_pre_o[j], w_in_o[j].astype(BF16), 3) if i + 1 < depth else None
            hp, hs, pending, outs = _even_layer(
                hp, hs, dims, state_conv[j], cache_k_moba[j], cache_v_moba[j], page_table,
                norm_pre_e[j], norm_post_e[j], w_in_e[j], conv_w[j], w_out_e[j], tm, pending, nxt)
            ev.append(outs)
        else:
            hp, hs, outs = _odd_layer(
                hp, hs, dims, cache_k_sb[j], cache_v_sb[j], page_table,
                norm_pre_o[j], norm_post_o[j], w_in_o[j].astype(BF16), gmlp_w[j], gmlp_b[j],
                w_out_o[j], tm, group_sb, pending)
            od.append(outs)
            pending = None
    y_prompt = hp.reshape(bsz, seq, d)
    y_sample = _to_batch_major(hs, nb)
    ev_out = [jnp.stack([r[k] for r in ev]) for k in range(6)]
    od_out = [jnp.stack([r[k] for r in od]) for k in range(6)]
    conv_p, conv_s, kmp, vmp, kms, vms = ev_out
    ksp, vsp, kss, vss, gvp, gvs = od_out
    return (y_prompt, y_sample, conv_p, conv_s, kmp, vmp, kms, vms, ksp, vsp, kss, vss, gvp, gvs)
```

```python
import functools

import jax
import jax.numpy as jnp
from jax import lax
from jax.experimental import pallas as pl
from jax.experimental.pallas import tpu as pltpu

F32 = jnp.float32
BF16 = jnp.bfloat16

HEAD_DIM = 64
LANES = 128
BRANCH_W = 512
MOBA_BLOCK = 256
MOBA_TOPK = 3
MOBA_UNROLL = 4
GMLP_CHUNK = 128
CONV_W = 3
RMS_EPS = 1e-6
NEG = -1e30
LOG2E = 1.4426950408889634
Q_SCALE = HEAD_DIM ** -0.5 * LOG2E
SB_TILE = 256
SB_DEAD = -160.0
VMEM_LIMIT = 56 * 1024 * 1024
PAGE_BUFFERS = 2
PAGE_RING = 32
PAGE_GROUP = 8

_NT = (((1,), (1,)), ((), ()))


def _cparams(sem):
    return pltpu.CompilerParams(dimension_semantics=sem, vmem_limit_bytes=VMEM_LIMIT)


def _rms(x, g):
    ms = jnp.mean(x * x, axis=-1, keepdims=True)
    return x * lax.rsqrt(ms + RMS_EPS) * g


def _silu(x):
    return x / (1.0 + jnp.exp(-x))


def _softplus2(z2):
    return jnp.maximum(z2, 0.0) + jnp.log2(1.0 + jnp.exp2(-jnp.abs(z2)))


def _split_bf16(x):
    hi = x.astype(BF16)
    lo = (x - hi.astype(F32)).astype(BF16)
    return hi, lo


def _tri(n):
    r = lax.broadcasted_iota(jnp.int32, (n, n), 0)
    c = lax.broadcasted_iota(jnp.int32, (n, n), 1)
    return jnp.where(r >= c, 1.0, 0.0).astype(BF16)


def _proj_kernel(x_ref, g_ref, w_ref, rest_ref, qk_ref, kt_ref, vt_ref, vtb_ref, keep_ref, xn_ref,
                 *, qkv_first, blk, keep):
    xn_ref[...] = _rms(x_ref[...], g_ref[...]).astype(BF16)
    _proj_tile(xn_ref, w_ref, rest_ref, qk_ref, kt_ref, vt_ref, vtb_ref, keep_ref, qkv_first, blk, keep)


def _proj_tile(xn_ref, w_ref, rest_ref, qk_ref, kt_ref, vt_ref, vtb_ref, keep_ref, qkv_first, blk, keep):
    tm = xn_ref.shape[0]
    w = BRANCH_W
    keep_tile, keep_row = keep
    for j in range(w_ref.shape[1] // w):
        r = jnp.dot(xn_ref[...], w_ref[:, j * w:(j + 1) * w], preferred_element_type=F32)
        if j == qkv_first:
            qk_ref[:, 0:w] = (r * Q_SCALE).astype(BF16)
        elif j == qkv_first + 1:
            qk_ref[:, w:2 * w] = r.astype(BF16)
            for c in range(tm // blk):
                kt_ref[:, c * blk:(c + 1) * blk] = r[c * blk:(c + 1) * blk, :].T
        elif j == qkv_first + 2:
            for c in range(tm // blk):
                t = r[c * blk:(c + 1) * blk, :].T
                vt_ref[:, c * blk:(c + 1) * blk] = t
                vtb_ref[c] = t.astype(BF16)
        else:
            jr = j if j < qkv_first else j - 3
            rest_ref[:, jr * w:(jr + 1) * w] = r.astype(BF16)
            if jr == 1:
                @pl.when(pl.program_id(1) == keep_tile)
                def _():
                    keep_ref[...] = r[keep_row:keep_row + keep_ref.shape[0], :]


def _proj_outputs(n, bsz, seq, tm, keep_rows):
    tps = seq // tm
    blk = min(MOBA_BLOCK, tm)
    w = BRANCH_W
    m = bsz * seq
    keep_first, keep_n = keep_rows
    keep = (keep_first // tm, keep_first % tm)
    assert keep[1] + keep_n <= tm and keep_n % 8 == 0
    specs = [
        pl.BlockSpec((tm, n - 3 * w), lambda b, t: (b * tps + t, 0)),
        pl.BlockSpec((tm, 2 * w), lambda b, t: (b * tps + t, 0)),
        pl.BlockSpec((None, w, tm), lambda b, t: (b, 0, t)),
        pl.BlockSpec((None, w, tm), lambda b, t: (b, 0, t)),
        pl.BlockSpec((None, tm // blk, w, blk), lambda b, t: (b, t, 0, 0)),
        pl.BlockSpec((None, keep_n, w), lambda b, t: (b, 0, 0)),
    ]
    shapes = [
        jax.ShapeDtypeStruct((m, n - 3 * w), BF16),
        jax.ShapeDtypeStruct((m, 2 * w), BF16),
        jax.ShapeDtypeStruct((bsz, w, seq), F32),
        jax.ShapeDtypeStruct((bsz, w, seq), F32),
        jax.ShapeDtypeStruct((bsz, seq // blk, w, blk), BF16),
        jax.ShapeDtypeStruct((bsz, keep_n, w), F32),
    ]
    return specs, shapes, blk, keep


def _proj(x, g, w_bf, qkv_first, bsz, seq, tm, keep_rows):
    m, d = x.shape
    n = w_bf.shape[1]
    tps = seq // tm
    out_specs, out_shape, blk, keep = _proj_outputs(n, bsz, seq, tm, keep_rows)
    return pl.pallas_call(
        functools.partial(_proj_kernel, qkv_first=qkv_first, blk=blk, keep=keep),
        grid=(bsz, tps),
        in_specs=[
            pl.BlockSpec((tm, d), lambda b, t: (b * tps + t, 0)),
            pl.BlockSpec((1, d), lambda b, t: (0, 0)),
            pl.BlockSpec((d, n), lambda b, t: (0, 0), pipeline_mode=pl.Buffered(1)),
        ],
        out_specs=out_specs,
        out_shape=out_shape,
        scratch_shapes=[pltpu.VMEM((tm, d), BF16)],
        compiler_params=_cparams(("parallel", "arbitrary")),
    )(x, g.reshape(1, d), w_bf)


def _moba_prompt_kernel(q_ref, k_ref, vt_ref, o_ref, km_ref, *, nblk):
    _moba_prompt_block(pl.program_id(2), q_ref, k_ref, vt_ref, o_ref, km_ref, nblk)


def _moba_prompt_block(i, q_ref, k_ref, vt_ref, o_ref, km_ref, nblk):
    lane = lax.broadcasted_iota(jnp.int32, (1, LANES), 1)
    low = lane < HEAD_DIM
    nbp = km_ref.shape[0]
    blk = MOBA_BLOCK
    nheads = q_ref.shape[1] // HEAD_DIM

    @pl.when(i == 0)
    def _():
        km_ref[...] = jnp.zeros_like(km_ref)
        for j in range(nblk):
            rows = k_ref[j * blk:(j + 1) * blk, :].astype(F32)
            km_ref[j:j + 1, :] = jnp.sum(rows, axis=0, keepdims=True) * (1.0 / blk)

    brow = lax.broadcasted_iota(jnp.int32, (nbp, blk), 0)
    key_i = lax.broadcasted_iota(jnp.int32, (blk, blk), 0)
    qry_i = lax.broadcasted_iota(jnp.int32, (blk, blk), 1)
    start = pl.multiple_of(i * blk, blk)

    ones_rows = jnp.ones((2 * 8, blk), BF16)

    def v_rows(j, hh):
        return jnp.concatenate([vt_ref[j, hh * HEAD_DIM:(hh + 1) * HEAD_DIM, :], ones_rows], axis=0)

    q_owns, gates, own_scores = [], [], []
    for hh in range(nheads):
        cols = slice((hh // 2) * LANES, (hh // 2 + 1) * LANES)
        q = q_ref[:, cols]
        km = km_ref[:, cols]
        mine = low if hh % 2 == 0 else jnp.logical_not(low)
        km_hi, km_lo = _split_bf16(jnp.where(mine, km, 0.0))
        gates.append(lax.dot_general(km_hi, q, _NT, preferred_element_type=F32)
                     + lax.dot_general(km_lo, q, _NT, preferred_element_type=F32))
        q_own = jnp.where(mine, q, jnp.zeros_like(q))
        q_owns.append(q_own)
        own_scores.append(lax.dot_general(k_ref[pl.ds(start, blk), cols], q_own, _NT,
                                          preferred_element_type=F32))

    q_pasts, init = [], []
    for hh in range(nheads):
        gate = jnp.where(brow < i, gates[hh], -jnp.inf)
        bias = jnp.full((nbp, blk), NEG, F32)
        for _ in range(MOBA_TOPK):
            mx = jnp.max(gate, axis=0, keepdims=True)
            first = jnp.min(jnp.where(gate == mx, brow, nbp), axis=0, keepdims=True)
            pick = (brow == first) & (mx > -jnp.inf)
            bias = jnp.where(pick, 0.0, bias)
            gate = jnp.where(pick, -jnp.inf, gate)
        bias_t = jnp.concatenate([bias, jnp.zeros((LANES - nbp, blk), F32)], axis=0).T
        q_pasts.append(jnp.concatenate([q_owns[hh], bias_t.astype(BF16)], axis=1))

        s = jnp.where(key_i <= qry_i, own_scores[hh], NEG)
        m0 = jnp.max(s, axis=0, keepdims=True)
        p = jnp.exp2(s - m0)
        acc0 = jnp.dot(v_rows(i, hh), p.astype(BF16), preferred_element_type=F32)
        init.append((m0, acc0))

    def stage(js, carry):
        scores = []
        for j in js:
            st = pl.multiple_of(j * blk, blk)
            onehot = jnp.broadcast_to(jnp.where(lane == j, 1.0, 0.0).astype(BF16), (blk, LANES))
            for hh in range(nheads):
                kj = k_ref[pl.ds(st, blk), (hh // 2) * LANES:(hh // 2 + 1) * LANES]
                scores.append(lax.dot_general(jnp.concatenate([kj, onehot], axis=1), q_pasts[hh],
                                              _NT, preferred_element_type=F32))
        carry = list(carry)
        for n, j in enumerate(js):
            for hh in range(nheads):
                m, acc = carry[hh]
                sj = scores[n * nheads + hh]
                m_new = jnp.maximum(m, jnp.max(sj, axis=0, keepdims=True))
                alpha = jnp.exp2(m - m_new)
                pj = jnp.exp2(sj - m_new)
                acc = alpha * acc + jnp.dot(v_rows(j, hh), pj.astype(BF16),
                                            preferred_element_type=F32)
                carry[hh] = (m_new, acc)
        return tuple(carry)

    fin = tuple(init)
    base = 0
    size = MOBA_UNROLL
    while size >= 1:
        trips = (i - base) // size if size == MOBA_UNROLL else ((i - base) // size) % 2
        fin = lax.fori_loop(
            0, trips,
            lambda t, c, base=base, size=size: stage([base + t * size + u for u in range(size)], c),
            fin)
        base = base + trips * size
        size //= 2
    out_t = jnp.concatenate([acc[:HEAD_DIM] / acc[HEAD_DIM:HEAD_DIM + 1] for _, acc in fin],
                            axis=0)
    o_ref[...] = out_t.T.astype(o_ref.dtype)


def _attn_prompt_call(kernel_fn, qk, vtb, bsz, seq, scratch, wid):
    nblk = seq // MOBA_BLOCK
    ngrp = BRANCH_W // wid
    return pl.pallas_call(
        kernel_fn,
        grid=(bsz, ngrp, nblk),
        in_specs=[
            pl.BlockSpec((MOBA_BLOCK, wid), lambda b, p, i: (b * nblk + i, p)),
            pl.BlockSpec((seq, wid), lambda b, p, i: (b, ngrp + p)),
            pl.BlockSpec((None, nblk, wid, MOBA_BLOCK), lambda b, p, i: (b, 0, p, 0)),
        ],
        out_specs=pl.BlockSpec((MOBA_BLOCK, wid), lambda b, p, i: (b * nblk + i, p)),
        out_shape=jax.ShapeDtypeStruct((bsz * seq, BRANCH_W), BF16),
        scratch_shapes=scratch,
        compiler_params=_cparams(("parallel", "parallel", "arbitrary")),
    )(qk, qk, vtb)


MOBA_STEP_W = 512


def _moba_prompt(qk, vtb, bsz, seq):
    nblk = seq // MOBA_BLOCK
    nbp = -(-nblk // 8) * 8
    return _attn_prompt_call(functools.partial(_moba_prompt_kernel, nblk=nblk), qk, vtb, bsz, seq,
                             [pltpu.VMEM((nbp, MOBA_STEP_W), F32)], MOBA_STEP_W)


def _sb_prompt_kernel(q_ref, k_ref, vt_ref, o_ref, sums_ref):
    i = pl.program_id(2)
    nheads = q_ref.shape[1] // HEAD_DIM
    tile = SB_TILE
    lane = lax.broadcasted_iota(jnp.int32, (1, LANES), 1)
    low = lane < HEAD_DIM
    key_i = lax.broadcasted_iota(jnp.int32, (tile, tile), 0)
    qry_i = lax.broadcasted_iota(jnp.int32, (tile, tile), 1)
    tri_t = jnp.where(qry_i >= key_i, 1.0, 0.0).astype(BF16)
    tri_2 = jnp.concatenate([tri_t, tri_t], axis=1)
    strict = key_i < qry_i

    q_hs = []
    for hh in range(nheads):
        q = q_ref[:, (hh // 2) * LANES:(hh // 2 + 1) * LANES]
        mine = low if hh % 2 == 0 else jnp.logical_not(low)
        q_hs.append(jnp.where(mine, q, jnp.zeros_like(q)))

    def sweep(tiles, state):
        zs = []
        for j, _ in tiles:
            st = pl.multiple_of(j * tile, tile)
            zs.append([lax.dot_general(
                k_ref[pl.ds(st, tile), (hh // 2) * LANES:(hh // 2 + 1) * LANES],
                q_hs[hh], _NT, preferred_element_type=F32) for hh in range(nheads)])
        for n, (_, mask) in enumerate(tiles):
            for hh in range(nheads):
                sp = _softplus2(zs[n][hh])
                if mask is not None:
                    sp = jnp.where(mask, sp, 0.0)
                hi, lo = _split_bf16(sp)
                sums_ref[n, hh] = jnp.dot(tri_2, jnp.concatenate([hi, lo], axis=0),
                                          preferred_element_type=F32)
        state = list(state)
        for n, (j, mask) in enumerate(tiles):
            for hh in range(nheads):
                spent, acc = state[hh]
                cum = sums_ref[n, hh] + spent
                w = jnp.exp2(zs[n][hh] - cum)
                if mask is not None:
                    w = jnp.where(mask, w, 0.0)
                acc = acc + jnp.dot(vt_ref[j, hh * HEAD_DIM:(hh + 1) * HEAD_DIM, :],
                                    w.astype(BF16), preferred_element_type=F32)
                state[hh] = (cum[0:1, :], acc)
        return tuple(state)

    def live_of(state):
        spent = jnp.min(state[0][0])
        for hh in range(1, nheads):
            spent = jnp.minimum(spent, jnp.min(state[hh][0]))
        return -spent

    def cond(st):
        j, live, _ = st
        return (j >= 0) & (live > SB_DEAD)

    def body(st):
        j, _, state = st
        new = sweep([(j, None)], state)
        return j - 1, live_of(new), new

    zero = tuple((jnp.zeros((1, tile), F32), jnp.zeros((HEAD_DIM, tile), F32))
                 for _ in range(nheads))
    has_prev = jnp.broadcast_to(i > 0, (tile, tile))
    init = sweep([(i, strict), (jnp.maximum(i - 1, 0), has_prev)], zero)
    _, _, fin = lax.while_loop(cond, body, (i - 2, live_of(init), init))
    o_ref[...] = jnp.concatenate([acc for _, acc in fin], axis=0).T.astype(o_ref.dtype)


SB_STEP_W = 512


def _sb_prompt(qk, vtb, bsz, seq):
    scratch = [pltpu.VMEM((2, SB_STEP_W // HEAD_DIM, SB_TILE, SB_TILE), F32)]
    return _attn_prompt_call(_sb_prompt_kernel, qk, vtb, bsz, seq, scratch, SB_STEP_W)


def _query_rows(qrep_ref, nheads):
    nrow, width = qrep_ref.shape[1], qrep_ref.shape[2]
    row_w = lax.broadcasted_iota(jnp.int32, (nrow, width), 0)
    lane_w = lax.broadcasted_iota(jnp.int32, (nrow, width), 1)
    head_mask = (lane_w // HEAD_DIM) == (row_w % nheads)
    qrows = jnp.where(head_mask, qrep_ref[0], jnp.zeros_like(qrep_ref[0]))
    return qrows, head_mask


def _pick_heads(y, head_mask, nheads):
    nrow, width = y.shape
    y = jnp.where(head_mask, y, 0.0)
    return jnp.sum(y.reshape(nrow // nheads, nheads, width), axis=1)


def _moba_sample_weights(qrows, knew_ref, s_ref, p_ref, nheads):
    nblk, nrow = s_ref.shape[0], s_ref.shape[1]
    row = lax.broadcasted_iota(jnp.int32, (nrow, LANES), 0)
    lane = lax.broadcasted_iota(jnp.int32, (nrow, LANES), 1)
    qidx = row // nheads
    zn = lax.dot_general(qrows, knew_ref[0], _NT, preferred_element_type=F32)
    gates = jnp.zeros((nrow, LANES), F32)
    for bl in range(nblk):
        gates = jnp.where(lane == bl, jnp.sum(s_ref[bl], axis=1, keepdims=True), gates)
    gates = jnp.where(lane < nblk, gates, -jnp.inf)
    sel = jnp.zeros((nrow, LANES), F32)
    for _ in range(MOBA_TOPK):
        mx = jnp.max(gates, axis=1, keepdims=True)
        first = jnp.min(jnp.where(gates == mx, lane, LANES), axis=1, keepdims=True)
        pick = (lane == first) & (mx > -jnp.inf)
        sel = jnp.where(pick, 1.0, sel)
        gates = jnp.where(pick, -jnp.inf, gates)
    zn = jnp.where(lane <= qidx, zn, NEG)
    m = jnp.max(zn, axis=1, keepdims=True)
    cols = []
    for bl in range(nblk):
        col = jnp.max(jnp.where(lane == bl, sel, 0.0), axis=1, keepdims=True) > 0.5
        cols.append(col)
        m = jnp.maximum(m, jnp.max(jnp.where(col, s_ref[bl], NEG), axis=1, keepdims=True))
    pn = jnp.exp2(zn - m)
    l = jnp.sum(pn, axis=1, keepdims=True)
    for bl in range(nblk):
        p = jnp.exp2(jnp.where(cols[bl], s_ref[bl], NEG) - m)
        l = l + jnp.sum(p, axis=1, keepdims=True)
        p_ref[bl] = p.astype(BF16)
    return pn, l


def _moba_paged_kernel(pt_ref, qrep_ref, knew_ref, vnew_ref, k_hbm, v_hbm, o_ref,
                       kbuf, vbuf, ksem, vsem, s_ref, p_ref, *, npages, nheads):
    b = pl.program_id(0)
    ring = kbuf.shape[0]
    page = kbuf.shape[2]
    ppb = MOBA_BLOCK // page
    qrows, head_mask = _query_rows(qrep_ref, nheads)

    def k_copy(seq, p, slot):
        return pltpu.make_async_copy(k_hbm.at[pt_ref[seq * npages + p]], kbuf.at[slot], ksem.at[slot])

    def v_copy(p, slot):
        return pltpu.make_async_copy(v_hbm.at[pt_ref[b * npages + p]], vbuf.at[slot], vsem.at[slot])

    @pl.when(b == 0)
    def _():
        for p in range(ring):
            k_copy(0, p, p).start()

    grp = min(PAGE_GROUP, ring)

    def k_group(gi, carry):
        base = gi * grp
        slot0 = (gi % (ring // grp)) * grp
        for u in range(grp):
            k_copy(b, base + u, slot0 + u).wait()
        for u in range(grp):
            s_ref[gi * (grp // ppb) + u // ppb, :, (u % ppb) * page:(u % ppb + 1) * page] = jnp.dot(
                qrows, kbuf[slot0 + u].astype(BF16), preferred_element_type=F32)

        @pl.when(base + ring < npages)
        def _():
            for u in range(grp):
                k_copy(b, base + ring + u, slot0 + u).start()
        return carry

    lax.fori_loop(0, npages // grp, k_group, 0)
    for p in range(ring):
        v_copy(p, p).start()

    @pl.when(b + 1 < pl.num_programs(0))
    def _():
        for p in range(ring):
            k_copy(b + 1, p, p).start()

    pn, l = _moba_sample_weights(qrows, knew_ref, s_ref, p_ref, nheads)
    acc0 = jnp.dot(pn.astype(BF16), vnew_ref[0], preferred_element_type=F32)

    def v_group(gi, acc):
        base = gi * grp
        slot0 = (gi % (ring // grp)) * grp
        for u in range(grp):
            v_copy(base + u, slot0 + u).wait()
        for u in range(grp):
            pw = p_ref[gi * (grp // ppb) + u // ppb, :, (u % ppb) * page:(u % ppb + 1) * page]
            acc = acc + lax.dot_general(pw, vbuf[slot0 + u].astype(BF16), _NT,
                                        preferred_element_type=F32)

        @pl.when(base + ring < npages)
        def _():
            for u in range(grp):
                v_copy(base + ring + u, slot0 + u).start()
        return acc

    acc = lax.fori_loop(0, npages // grp, v_group, acc0)
    o_ref[0] = _pick_heads(acc / l, head_mask, nheads)


def _paged_operands(q_bf, knew_bf, vnew_bf, cache_k, cache_v, page_table):
    nb, nq, width = q_bf.shape
    nheads = width // HEAD_DIM
    nphys, page = cache_k.shape[0], cache_k.shape[1]
    ck = cache_k.transpose(0, 2, 3, 1).reshape(nphys, width, page)
    cv = cache_v.transpose(0, 2, 3, 1).reshape(nphys, width, page)
    qrep = jnp.repeat(q_bf, nheads, axis=1)
    pad = ((0, 0), (0, LANES - nq), (0, 0))
    knew = jnp.pad(knew_bf, pad)
    vnew = jnp.pad(vnew_bf, pad)
    pt = page_table.reshape(-1).astype(jnp.int32)
    return qrep, knew, vnew, ck, cv, pt, nheads, page


def _moba_paged(q_bf, knew_bf, vnew_bf, cache_k, cache_v, page_table):
    nb, nq, width = q_bf.shape
    qrep, knew, vnew, ck, cv, pt, nheads, page = _paged_operands(
        q_bf, knew_bf, vnew_bf, cache_k, cache_v, page_table)
    nrow = nq * nheads
    npages = page_table.shape[1]
    ppb = MOBA_BLOCK // page
    ring = min(PAGE_RING, npages)
    grp = min(PAGE_GROUP, ring)
    assert grp % ppb == 0 and ring % grp == 0 and npages % ring == 0
    per_b = lambda b, pt: (b, 0, 0)
    grid_spec = pltpu.PrefetchScalarGridSpec(
        num_scalar_prefetch=1,
        grid=(nb,),
        in_specs=[
            pl.BlockSpec((1, nrow, width), per_b),
            pl.BlockSpec((1, LANES, width), per_b),
            pl.BlockSpec((1, LANES, width), per_b),
            pl.BlockSpec(memory_space=pl.ANY),
            pl.BlockSpec(memory_space=pl.ANY),
        ],
        out_specs=pl.BlockSpec((1, nq, width), per_b),
        scratch_shapes=[
            pltpu.VMEM((ring, width, page), F32),
            pltpu.VMEM((ring, width, page), F32),
            pltpu.SemaphoreType.DMA((ring,)),
            pltpu.SemaphoreType.DMA((ring,)),
            pltpu.VMEM((npages // ppb, nrow, MOBA_BLOCK), F32),
            pltpu.VMEM((npages // ppb, nrow, MOBA_BLOCK), BF16),
        ],
    )
    return pl.pallas_call(
        functools.partial(_moba_paged_kernel, npages=npages, nheads=nheads),
        grid_spec=grid_spec,
        out_shape=jax.ShapeDtypeStruct((nb, nq, width), F32),
        compiler_params=_cparams(("arbitrary",)),
    )(pt, qrep, knew, vnew, ck, cv)


def _moba_fused_kernel(pt_ref, q_ref, k_ref, vt_ref, qrep_ref, knew_ref, vnew_ref, kc_hbm, vc_hbm,
                       o_ref, os_ref, km_ref, pbuf, psem, s_ref, p_ref, acc_ref, l_ref,
                       *, nblk, npages, nheads):
    i = pl.program_id(2)
    step = pl.program_id(0) * nblk + i
    seq = step // 2
    page = pbuf.shape[2]
    ppb = MOBA_BLOCK // page
    grp = min(PAGE_GROUP, npages)

    half = npages // 2
    gph = half // grp
    even = step % 2 == 0
    odd = step % 2 == 1
    has_next = step + 1 < pl.num_programs(0) * nblk

    def page_copy(cache_hbm, sq, p):
        return pltpu.make_async_copy(cache_hbm.at[pt_ref[sq * npages + p]], pbuf.at[p], psem.at[p])

    def start_half(cache_hbm, sq, h):
        for p in range(h * half, (h + 1) * half):
            page_copy(cache_hbm, sq, p).start()

    def wait_half(cache_hbm, sq, h):
        for p in range(h * half, (h + 1) * half):
            page_copy(cache_hbm, sq, p).wait()

    def scores_half(h):
        def k_group(gi, carry):
            for u in range(grp):
                s_ref[gi * (grp // ppb) + u // ppb, :, (u % ppb) * page:(u % ppb + 1) * page] = (
                    jnp.dot(qrows, pbuf[gi * grp + u].astype(BF16), preferred_element_type=F32))
            return carry
        lax.fori_loop(h * gph, (h + 1) * gph, k_group, 0)

    def values_half(h, acc):
        def v_group(gi, acc):
            for u in range(grp):
                pw = p_ref[gi * (grp // ppb) + u // ppb, :, (u % ppb) * page:(u % ppb + 1) * page]
                acc = acc + lax.dot_general(pw, pbuf[gi * grp + u].astype(BF16), _NT,
                                            preferred_element_type=F32)
            return acc
        return lax.fori_loop(h * gph, (h + 1) * gph, v_group, acc)

    qrows, head_mask = _query_rows(qrep_ref, nheads)

    @pl.when(step == 0)
    def _():
        start_half(kc_hbm, 0, 0)
        start_half(kc_hbm, 0, 1)

    @pl.when(even)
    def _():
        wait_half(kc_hbm, seq, 0)
        scores_half(0)
        start_half(vc_hbm, seq, 0)

    @pl.when(odd)
    def _():
        pn, l = _moba_sample_weights(qrows, knew_ref, s_ref, p_ref, nheads)
        l_ref[...] = jnp.broadcast_to(l, l_ref.shape)
        acc0 = jnp.dot(pn.astype(BF16), vnew_ref[0], preferred_element_type=F32)
        wait_half(vc_hbm, seq, 0)
        acc_ref[...] = values_half(0, acc0)

        @pl.when(has_next)
        def _():
            start_half(kc_hbm, seq + 1, 0)

    _moba_prompt_block(i, q_ref, k_ref, vt_ref, o_ref, km_ref, nblk)

    @pl.when(even)
    def _():
        wait_half(kc_hbm, seq, 1)
        scores_half(1)
        start_half(vc_hbm, seq, 1)

    @pl.when(odd)
    def _():
        wait_half(vc_hbm, seq, 1)
        acc = values_half(1, acc_ref[...])
        os_ref[0] = _pick_heads(acc / l_ref[:, 0:1], head_mask, nheads)

        @pl.when(has_next)
        def _():
            start_half(kc_hbm, seq + 1, 1)


def _moba_fused(qk, vtb, bsz, seq, q_bf, knew_bf, vnew_bf, cache_k, cache_v, page_table):
    nb, nq, width = q_bf.shape
    qrep, knew, vnew, ck, cv, pt, nheads, page = _paged_operands(
        q_bf, knew_bf, vnew_bf, cache_k, cache_v, page_table)
    nrow = nq * nheads
    npages = page_table.shape[1]
    nblk = seq // MOBA_BLOCK
    nbp = -(-nblk // 8) * 8
    wid = MOBA_STEP_W
    ppb = MOBA_BLOCK // page
    grp = min(PAGE_GROUP, npages)
    assert bsz * nblk == 2 * nb and wid == BRANCH_W and grp % ppb == 0 and npages % (2 * grp) == 0
    per_seq = lambda b, p, i, pt: ((b * nblk + i) // 2, 0, 0)
    grid_spec = pltpu.PrefetchScalarGridSpec(
        num_scalar_prefetch=1,
        grid=(bsz, 1, nblk),
        in_specs=[
            pl.BlockSpec((MOBA_BLOCK, wid), lambda b, p, i, pt: (b * nblk + i, 0)),
            pl.BlockSpec((seq, wid), lambda b, p, i, pt: (b, 1)),
            pl.BlockSpec((None, nblk, wid, MOBA_BLOCK), lambda b, p, i, pt: (b, 0, 0, 0)),
            pl.BlockSpec((1, nrow, width), per_seq),
            pl.BlockSpec((1, LANES, width), per_seq),
            pl.BlockSpec((1, LANES, width), per_seq),
            pl.BlockSpec(memory_space=pl.ANY),
            pl.BlockSpec(memory_space=pl.ANY),
        ],
        out_specs=[
            pl.BlockSpec((MOBA_BLOCK, wid), lambda b, p, i, pt: (b * nblk + i, 0)),
            pl.BlockSpec((1, nq, width), per_seq),
        ],
        scratch_shapes=[
            pltpu.VMEM((nbp, wid), F32),
            pltpu.VMEM((npages, width, page), F32),
            pltpu.SemaphoreType.DMA((npages,)),
            pltpu.VMEM((npages // ppb, nrow, MOBA_BLOCK), F32),
            pltpu.VMEM((npages // ppb, nrow, MOBA_BLOCK), BF16),
            pltpu.VMEM((nrow, width), F32),
            pltpu.VMEM((nrow, LANES), F32),
        ],
    )
    return pl.pallas_call(
        functools.partial(_moba_fused_kernel, nblk=nblk, npages=npages, nheads=nheads),
        grid_spec=grid_spec,
        out_shape=[jax.ShapeDtypeStruct((bsz * seq, BRANCH_W), BF16),
                   jax.ShapeDtypeStruct((nb, nq, width), F32)],
        compiler_params=_cparams(("arbitrary", "arbitrary", "arbitrary")),
    )(pt, qk, qk, vtb, qrep, knew, vnew, ck, cv)


def _sb_paged_kernel(pt_ref, live_ref, qrep_ref, knew_ref, vnew_ref, cin_ref, ain_ref, *rest,
                     group, nheads, first):
    del pt_ref
    k_refs = rest[:group]
    v_refs = rest[group:2 * group]
    y_ref, acc_ref, carry_ref = rest[2 * group:2 * group + 3]
    b = pl.program_id(0)
    s = pl.program_id(1)
    nrow = qrep_ref.shape[1]
    page = k_refs[0].shape[1]
    ppt = SB_TILE // page
    qrows, head_mask = _query_rows(qrep_ref, nheads)
    tri = _tri(SB_TILE)

    @pl.when(s == 0)
    def _():
        if first:
            row = lax.broadcasted_iota(jnp.int32, (nrow, LANES), 0)
            lane = lax.broadcasted_iota(jnp.int32, (nrow, LANES), 1)
            strict = lane < row // nheads
            zn = lax.dot_general(qrows, knew_ref[0], _NT, preferred_element_type=F32)
            hi, lo = _split_bf16(jnp.where(strict, -_softplus2(zn), 0.0))
            tri_n = _tri(LANES)
            cum = (jnp.dot(hi, tri_n, preferred_element_type=F32)
                   + jnp.dot(lo, tri_n, preferred_element_type=F32))
            wn = jnp.where(strict, jnp.exp2(zn + cum), 0.0)
            acc_ref[0] = jnp.dot(wn.astype(BF16), vnew_ref[0], preferred_element_type=F32)
            carry_ref[0] = jnp.broadcast_to(cum[:, 0:1], carry_ref.shape[1:])
        else:
            acc_ref[0] = ain_ref[0]
            carry_ref[0] = cin_ref[0]

    @pl.when(live_ref[b] > 0)
    def _():
        carry = carry_ref[0][:, 0:1]
        acc = acc_ref[0]
        order = list(reversed(range(group // ppt)))
        tri_2 = jnp.concatenate([tri, tri], axis=0)
        zs = [jnp.dot(qrows,
                      jnp.concatenate([k_refs[t * ppt + u][...] for u in range(ppt)],
                                      axis=1).astype(BF16),
                      preferred_element_type=F32) for t in order]
        sums = []
        for z in zs:
            hi, lo = _split_bf16(-_softplus2(z))
            sums.append(jnp.dot(jnp.concatenate([hi, lo], axis=1), tri_2,
                                preferred_element_type=F32))
        ws = []
        for z, within in zip(zs, sums):
            cum = within + carry
            ws.append(jnp.exp2(z + cum).astype(BF16))
            carry = cum[:, 0:1]
        for t, w in zip(order, ws):
            vt = jnp.concatenate([v_refs[t * ppt + u][...] for u in range(ppt)], axis=1).astype(BF16)
            acc = acc + lax.dot_general(w, vt, _NT, preferred_element_type=F32)
        acc_ref[0] = acc
        carry_ref[0] = jnp.broadcast_to(carry, carry_ref.shape[1:])

    @pl.when(s == pl.num_programs(1) - 1)
    def _():
        y_ref[0] = _pick_heads(acc_ref[0], head_mask, nheads)


def _sb_paged_call(first, pt, live, qrep, knew, vnew, carry_in, acc_in, ck, cv,
                   nq, npages, group, chunk_lo, nsteps, nheads):
    nb, nrow, width = qrep.shape
    page = ck.shape[2]

    def page_map(g):
        def index(b, s, pt, live):
            chunk_page = (chunk_lo + nsteps - 1 - s) * group + g
            return (jnp.where(live[b] > 0, pt[b * npages + chunk_page], 0), 0, 0)
        return index

    per_b = lambda b, s, pt, live: (b, 0, 0)
    in_specs = [
        pl.BlockSpec((1, nrow, width), per_b),
        pl.BlockSpec((1, LANES, width), per_b),
        pl.BlockSpec((1, LANES, width), per_b),
        pl.BlockSpec((1, nrow, LANES), per_b),
        pl.BlockSpec((1, nrow, width), per_b),
    ]
    in_specs += [pl.BlockSpec((None, width, page), page_map(g), pipeline_mode=pl.Buffered(PAGE_BUFFERS))
                 for g in range(group)] * 2
    grid_spec = pltpu.PrefetchScalarGridSpec(
        num_scalar_prefetch=2,
        grid=(nb, nsteps),
        in_specs=in_specs,
        out_specs=[pl.BlockSpec((1, nq, width), per_b),
                   pl.BlockSpec((1, nrow, width), per_b),
                   pl.BlockSpec((1, nrow, LANES), per_b)],
    )
    return pl.pallas_call(
        functools.partial(_sb_paged_kernel, group=group, nheads=nheads, first=first),
        grid_spec=grid_spec,
        out_shape=[jax.ShapeDtypeStruct((nb, nq, width), F32),
                   jax.ShapeDtypeStruct((nb, nrow, width), F32),
                   jax.ShapeDtypeStruct((nb, nrow, LANES), F32)],
        compiler_params=_cparams(("parallel", "arbitrary")),
    )(pt, live, qrep, knew, vnew, carry_in, acc_in, *([ck] * group), *([cv] * group))


def _sb_paged(q_bf, knew_bf, vnew_bf, cache_k, cache_v, page_table, group):
    nb, nq, width = q_bf.shape
    qrep, knew, vnew, ck, cv, pt, nheads, page = _paged_operands(
        q_bf, knew_bf, vnew_bf, cache_k, cache_v, page_table)
    nrow = nq * nheads
    npages = page_table.shape[1]
    nchunk = npages // group
    zc = jnp.zeros((nb, nrow, LANES), F32)
    za = jnp.zeros((nb, nrow, width), F32)
    all_live = jnp.ones((nb,), jnp.int32)
    y, acc, carry = _sb_paged_call(True, pt, all_live, qrep, knew, vnew, zc, za, ck, cv,
                                   nq, npages, group, nchunk - 1, 1, nheads)
    if nchunk == 1:
        return y
    live = (jnp.max(carry, axis=(1, 2)) > SB_DEAD).astype(jnp.int32)

    def older(_):
        return _sb_paged_call(False, pt, live, qrep, knew, vnew, carry, acc, ck, cv,
                              nq, npages, group, 0, nchunk - 1, nheads)[0]

    return lax.cond(jnp.any(live > 0), older, lambda _: y, None)


def _finish(y_first, y_second, w_ref, g_ref, x_ref, o_ref):
    half = y_first.shape[1]
    out = (jnp.dot(y_first.astype(BF16), w_ref[0:half, :], preferred_element_type=F32)
           + jnp.dot(y_second.astype(BF16), w_ref[half:2 * half, :], preferred_element_type=F32))
    h = x_ref[...] + _rms(out, g_ref[...])
    o_ref[...] = h
    return h


def _even_out_prompt_kernel(ab_ref, ac_ref, ah_ref, az_ref, bz_ref, hc_ref, hh_ref, yb_ref, x_ref,
                            w_ref, cw_ref, g_ref, *rest, next_proj):
    if next_proj is None:
        o_ref, tail_ref, ue_ref = rest
    else:
        g2_ref, w2_ref, o_ref, tail_ref, *proj_out, ue_ref, xn_ref = rest
    tm = ab_ref.shape[0]
    hrows = hc_ref.shape[0]
    u = ac_ref[...].astype(F32) * ah_ref[...].astype(F32)
    first = pl.program_id(1) == 0
    halo = hc_ref[...].astype(F32) * hh_ref[...].astype(F32)
    ue_ref[0:8, :] = jnp.where(first, 0.0, halo[hrows - 8:hrows, :])
    ue_ref[8:8 + tm, :] = u
    conv = (ue_ref[6:6 + tm, :] * cw_ref[0:1, :] + ue_ref[7:7 + tm, :] * cw_ref[1:2, :]
            + u * cw_ref[2:3, :])
    y_a = ab_ref[...].astype(F32) * conv * _silu(az_ref[...].astype(F32))
    y_b = yb_ref[...].astype(F32) * _silu(bz_ref[...].astype(F32))
    tail_ref[...] = u[tm - 8:tm, :]
    h = _finish(y_a, y_b, w_ref, g_ref, x_ref, o_ref)
    if next_proj is not None:
        qkv_first, blk, keep = next_proj
        xn_ref[...] = _rms(h, g2_ref[...]).astype(BF16)
        _proj_tile(xn_ref, w2_ref, *proj_out, qkv_first, blk, keep)


def _even_out_prompt(rest, y_b, x, w_out_bf, conv_w, g_post, bsz, seq, tm, nxt=None):
    m, d = x.shape
    tps = seq // tm
    w = BRANCH_W
    row = lambda b, t: b * tps + t
    col = lambda c: pl.BlockSpec((tm, w), lambda b, t, c=c: (row(b, t), c))
    hrows = 16
    halo = lambda c: pl.BlockSpec(
        (hrows, w), lambda b, t, c=c: (jnp.maximum(row(b, t) * (tm // hrows) - 1, 0), c))
    const = lambda shape: pl.BlockSpec(shape, lambda b, t: (0,) * len(shape),
                                       pipeline_mode=pl.Buffered(1))
    in_specs = [col(0), col(1), col(2), col(3), col(4), halo(1), halo(2),
                pl.BlockSpec((tm, w), lambda b, t: (row(b, t), 0)),
                pl.BlockSpec((tm, d), lambda b, t: (row(b, t), 0)),
                const(w_out_bf.shape), const(conv_w.shape), const((1, d))]
    args = [rest, rest, rest, rest, rest, rest, rest, y_b, x, w_out_bf, conv_w, g_post.reshape(1, d)]
    out_specs = [pl.BlockSpec((tm, d), lambda b, t: (row(b, t), 0)),
                 pl.BlockSpec((8, w), lambda b, t: (row(b, t), 0))]
    out_shape = [jax.ShapeDtypeStruct((m, d), F32),
                 jax.ShapeDtypeStruct((bsz * tps * 8, w), F32)]
    scratch = [pltpu.VMEM((tm + 8, w), F32)]
    next_proj = None
    if nxt is not None:
        g2, w2_bf, qkv_first, keep_rows = nxt
        specs2, shapes2, blk, keep = _proj_outputs(w2_bf.shape[1], bsz, seq, tm, keep_rows)
        in_specs += [const((1, d)), const(w2_bf.shape)]
        args += [g2.reshape(1, d), w2_bf]
        out_specs += specs2
        out_shape += shapes2
        scratch.append(pltpu.VMEM((tm, d), BF16))
        next_proj = (qkv_first, blk, keep)
    outs = pl.pallas_call(
        functools.partial(_even_out_prompt_kernel, next_proj=next_proj),
        grid=(bsz, tps),
        in_specs=in_specs,
        out_specs=out_specs,
        out_shape=out_shape,
        scratch_shapes=scratch,
        compiler_params=_cparams(("parallel", "arbitrary")),
    )(*args)
    return outs[0], outs[1], (tuple(outs[2:]) if nxt is not None else None)


def _even_out_sample_kernel(rest_ref, st_ref, yb_ref, x_ref, w_ref, cw_ref, g_ref,
                            o_ref, tail_ref, *, nb):
    w = BRANCH_W
    rows = rest_ref.shape[0]
    rest = rest_ref[...].astype(F32)
    u = rest[:, w:2 * w] * rest[:, 2 * w:3 * w]
    ue = jnp.concatenate([st_ref[...], u], axis=0)
    conv = (ue[0:rows] * cw_ref[0:1, :] + ue[nb:nb + rows] * cw_ref[1:2, :]
            + ue[2 * nb:2 * nb + rows] * cw_ref[2:3, :])
    y_a = rest[:, 0:w] * conv * _silu(rest[:, 3 * w:4 * w])
    y_b = yb_ref[...] * _silu(rest[:, 4 * w:5 * w])
    tail_ref[...] = ue[rows:rows + 2 * nb]
    _finish(y_a, y_b, w_ref, g_ref, x_ref, o_ref)


def _even_out_sample(rest, state_tm, y_b, x, w_out_bf, conv_w, g_post, nb):
    m, d = x.shape
    return pl.pallas_call(
        functools.partial(_even_out_sample_kernel, nb=nb),
        out_shape=[jax.ShapeDtypeStruct((m, d), F32),
                   jax.ShapeDtypeStruct(((CONV_W - 1) * nb, BRANCH_W), F32)],
        compiler_params=pltpu.CompilerParams(vmem_limit_bytes=VMEM_LIMIT),
    )(rest, state_tm, y_b, x, w_out_bf, conv_w, g_post.reshape(1, d))


def _gmlp_mix(v, gw_ref, low):
    r_i = lax.broadcasted_iota(jnp.int32, (GMLP_CHUNK, GMLP_CHUNK), 0)
    c_i = lax.broadcasted_iota(jnp.int32, (GMLP_CHUNK, GMLP_CHUNK), 1)
    tril = c_i <= r_i
    parts = []
    for p in range(v.shape[1] // LANES):
        vp = v[:, p * LANES:(p + 1) * LANES]
        v_lo = jnp.where(low, vp, jnp.zeros_like(vp)).astype(BF16)
        v_hi = jnp.where(low, jnp.zeros_like(vp), vp).astype(BF16)
        w_lo = jnp.where(tril, gw_ref[2 * p], 0.0).astype(BF16)
        w_hi = jnp.where(tril, gw_ref[2 * p + 1], 0.0).astype(BF16)
        parts.append(jnp.dot(w_lo, v_lo, preferred_element_type=F32)
                     + jnp.dot(w_hi, v_hi, preferred_element_type=F32))
    return jnp.concatenate(parts, axis=1)


def _odd_out_prompt_kernel(cu_ref, cv_ref, cz_ref, dz_ref, yd_ref, x_ref, w_ref, gw_ref, gb_ref,
                           g_ref, o_ref, yc_ref):
    tm = cu_ref.shape[0]
    low = lax.broadcasted_iota(jnp.int32, (1, LANES), 1) < HEAD_DIM
    for c in range(tm // GMLP_CHUNK):
        rows = slice(c * GMLP_CHUNK, (c + 1) * GMLP_CHUNK)
        mixed = _gmlp_mix(cv_ref[rows, :], gw_ref, low) + gb_ref[...]
        yc_ref[rows, :] = (cu_ref[rows, :].astype(F32) * mixed
                           * _silu(cz_ref[rows, :].astype(F32)))
    y_d = yd_ref[...].astype(F32) * _silu(dz_ref[...].astype(F32))
    _finish(yc_ref[...], y_d, w_ref, g_ref, x_ref, o_ref)


def _odd_out_prompt(rest, y_d, x, w_out_bf, gmlp_w, gb_full, g_post, tm):
    m, d = x.shape
    col = lambda c: pl.BlockSpec((tm, BRANCH_W), lambda i, c=c: (i, c))
    return pl.pallas_call(
        _odd_out_prompt_kernel,
        grid=(m // tm,),
        in_specs=[col(0), col(1), col(2), col(3),
                  pl.BlockSpec((tm, BRANCH_W), lambda i: (i, 0)),
                  pl.BlockSpec((tm, d), lambda i: (i, 0)),
                  pl.BlockSpec(w_out_bf.shape, lambda i: (0, 0)),
                  pl.BlockSpec(gmlp_w.shape, lambda i: (0, 0, 0)),
                  pl.BlockSpec(gb_full.shape, lambda i: (0, 0)),
                  pl.BlockSpec((1, d), lambda i: (0, 0))],
        out_specs=pl.BlockSpec((tm, d), lambda i: (i, 0)),
        out_shape=jax.ShapeDtypeStruct((m, d), F32),
        scratch_shapes=[pltpu.VMEM((tm, BRANCH_W), F32)],
        compiler_params=_cparams(("parallel",)),
    )(rest, rest, rest, rest, y_d, x, w_out_bf, gmlp_w, gb_full, g_post.reshape(1, d))


def _odd_out_sample_kernel(rest_ref, yd_ref, x_ref, w_ref, w4_ref, b4_ref, g_ref, o_ref, *, nb, nq):
    w = BRANCH_W
    rest = rest_ref[...].astype(F32)
    parts = []
    for t in range(nq):
        mixed = jnp.broadcast_to(b4_ref[t:t + 1, :], (nb, w))
        for s in range(t + 1):
            mixed = mixed + w4_ref[t * nq + s:t * nq + s + 1, :] * rest[s * nb:(s + 1) * nb, w:2 * w]
        parts.append(mixed)
    mixed = jnp.concatenate(parts, axis=0)
    y_c = rest[:, 0:w] * mixed * _silu(rest[:, 2 * w:3 * w])
    y_d = yd_ref[...] * _silu(rest[:, 3 * w:4 * w])
    _finish(y_c, y_d, w_ref, g_ref, x_ref, o_ref)


def _odd_out_sample(rest, y_d, x, w_out_bf, w4, b4, g_post, nb, nq):
    m, d = x.shape
    return pl.pallas_call(
        functools.partial(_odd_out_sample_kernel, nb=nb, nq=nq),
        out_shape=jax.ShapeDtypeStruct((m, d), F32),
        compiler_params=pltpu.CompilerParams(vmem_limit_bytes=VMEM_LIMIT),
    )(rest, y_d, x, w_out_bf, w4, b4, g_post.reshape(1, d))


def _to_token_major(a):
    nb, nq, w = a.shape
    return a.transpose(1, 0, 2).reshape(nq * nb, w)


def _to_batch_major(a, nb):
    w = a.shape[1]
    return a.reshape(-1, nb, w).transpose(1, 0, 2)


def _heads_prompt(t, heads):
    bsz, w, seq = t.shape
    return t.reshape(bsz, heads, w // heads, seq).transpose(0, 3, 1, 2)


def _open_rows(seq):
    open_start = ((seq - 1) // GMLP_CHUNK) * GMLP_CHUNK
    return open_start, seq - open_start


def _project_prompt(hp, dims, g_pre, w_in_bf, qkv_first, tm):
    bsz, seq, _, _ = dims
    return _proj(hp, g_pre, w_in_bf, qkv_first, bsz, seq, tm, _open_rows(seq))


def _project_sample(hs, dims, g_pre, w_in_bf, qkv_first, tm):
    _, _, nb, nq = dims
    ns = nq * nb
    rows = max(ns, tm)
    hs_pad = jnp.pad(hs, ((0, rows - ns), (0, 0)))
    rest_s, qk_s, kt_s, vt_s, _, keep_s = _proj(hs_pad, g_pre, w_in_bf, qkv_first, 1, rows, rows,
                                                (0, ns))
    k_s = _to_batch_major(kt_s[0, :, :ns].T, nb)
    v_s = _to_batch_major(vt_s[0, :, :ns].T, nb)
    q_s = _to_batch_major(qk_s[:ns, :BRANCH_W], nb)
    return rest_s[:ns], q_s, k_s, v_s, keep_s[0]


def _even_layer(hp, hs, dims, state_conv, cache_k, cache_v, page_table,
                g_pre, g_post, w_in, conv_w, w_out, tm, proj_prompt, nxt):
    bsz, seq, nb, nq = dims
    w = BRANCH_W
    heads = w // HEAD_DIM
    w_in_bf = w_in.astype(BF16)
    w_out_bf = w_out.astype(BF16)
    if proj_prompt is None:
        proj_prompt = _project_prompt(hp, dims, g_pre, w_in_bf, 4, tm)
    rest_p, qk_p, kt_p, vt_p, vtb_p, _ = proj_prompt
    rest_s, q_s, k_s, v_s, _ = _project_sample(hs, dims, g_pre, w_in_bf, 4, tm)
    sample_qkv = (q_s, k_s.astype(BF16), v_s.astype(BF16))
    if bsz * (seq // MOBA_BLOCK) == 2 * nb:
        yb_p, yb_s = _moba_fused(qk_p, vtb_p, bsz, seq, *sample_qkv, cache_k, cache_v, page_table)
    else:
        yb_p = _moba_prompt(qk_p, vtb_p, bsz, seq)
        yb_s = _moba_paged(*sample_qkv, cache_k, cache_v, page_table)
    yb_s = _to_token_major(yb_s)
    if nxt is not None:
        nxt = (*nxt, _open_rows(seq))
    hp, tail_p, proj_next = _even_out_prompt(rest_p, yb_p, hp, w_out_bf, conv_w, g_post,
                                             bsz, seq, tm, nxt)
    state_tm = _to_token_major(state_conv)
    hs, tail_s = _even_out_sample(rest_s, state_tm, yb_s, hs, w_out_bf, conv_w, g_post, nb)
    conv_p = tail_p.reshape(bsz, seq // tm, 8, w)[:, -1, 8 - (CONV_W - 1):, :]
    conv_s = _to_batch_major(tail_s, nb)
    outs = (conv_p, conv_s, _heads_prompt(kt_p, heads), _heads_prompt(vt_p, heads),
            k_s.reshape(nb, nq, heads, HEAD_DIM), v_s.reshape(nb, nq, heads, HEAD_DIM))
    return hp, hs, proj_next, outs


def _odd_layer(hp, hs, dims, cache_k, cache_v, page_table,
               g_pre, g_post, w_in_bf, gmlp_w, gmlp_b, w_out, tm, group, proj_prompt):
    bsz, seq, nb, nq = dims
    w = BRANCH_W
    heads = w // HEAD_DIM
    w_out_bf = w_out.astype(BF16)
    ngroups = gmlp_w.shape[0]
    cpg = w // ngroups
    if proj_prompt is None:
        proj_prompt = _project_prompt(hp, dims, g_pre, w_in_bf, 3, tm)
    rest_p, qk_p, kt_p, vt_p, vtb_p, gv_p = proj_prompt
    rest_s, q_s, k_s, v_s, keep_s = _project_sample(hs, dims, g_pre, w_in_bf, 3, tm)
    yd_p = _sb_prompt(qk_p, vtb_p, bsz, seq)
    yd_s = _to_token_major(_sb_paged(q_s, k_s.astype(BF16), v_s.astype(BF16),
                                     cache_k, cache_v, page_table, group))
    gb_full = jnp.repeat(gmlp_b.T, cpg, axis=1)
    hp = _odd_out_prompt(rest_p, yd_p, hp, w_out_bf, gmlp_w, gb_full, g_post, tm)
    w4 = jnp.repeat(gmlp_w[:, :nq, :nq].transpose(1, 2, 0), cpg, axis=2).reshape(nq * nq, w)
    b4 = jnp.repeat(gmlp_b[:, :nq].T, cpg, axis=1)
    hs = _odd_out_sample(rest_s, yd_s, hs, w_out_bf, w4, b4, g_post, nb, nq)
    gv_s = _to_batch_major(keep_s, nb)
    outs = (_heads_prompt(kt_p, heads), _heads_prompt(vt_p, heads),
            k_s.reshape(nb, nq, heads, HEAD_DIM), v_s.reshape(nb, nq, heads, HEAD_DIM), gv_p, gv_s)
    return hp, hs, outs


def kernel(x_prompt, x_sample, state_conv, cache_k_moba, cache_v_moba, cache_k_sb, cache_v_sb,
           page_table, norm_pre_e, norm_post_e, w_in_e, conv_w, w_out_e,
           norm_pre_o, norm_post_o, w_in_o, gmlp_w, gmlp_b, w_out_o):
    bsz, seq, d = x_prompt.shape
    nb, nq, _ = x_sample.shape
    depth = norm_pre_e.shape[0] + norm_pre_o.shape[0]
    npages, page = page_table.shape[1], cache_k_moba.shape[2]
    assert seq % MOBA_BLOCK == 0 and (npages * page) % MOBA_BLOCK == 0 and nq <= GMLP_CHUNK
    assert (nq - 1) // MOBA_BLOCK == 0 and w_in_e.shape[2] == 8 * BRANCH_W and w_in_o.shape[2] == 7 * BRANCH_W
    dims = (bsz, seq, nb, nq)
    tm = min(512, seq)
    group_sb = min(4, npages)
    hp = x_prompt.reshape(bsz * seq, d)
    hs = _to_token_major(x_sample)
    ev, od = [], []
    pending = None
    for i in range(depth):
        j = i // 2
        if i % 2 == 0:
            nxt = (norm_pre_o[j], w_in_o[j].astype(BF16), 3) if i + 1 < depth else None
            hp, hs, pending, outs = _even_layer(
                hp, hs, dims, state_conv[j], cache_k_moba[j], cache_v_moba[j], page_table,
                norm_pre_e[j], norm_post_e[j], w_in_e[j], conv_w[j], w_out_e[j], tm, pending, nxt)
            ev.append(outs)
        else:
            hp, hs, outs = _odd_layer(
                hp, hs, dims, cache_k_sb[j], cache_v_sb[j], page_table,
                norm_pre_o[j], norm_post_o[j], w_in_o[j].astype(BF16), gmlp_w[j], gmlp_b[j],
                w_out_o[j], tm, group_sb, pending)
            od.append(outs)
            pending = None
    y_prompt = hp.reshape(bsz, seq, d)
    y_sample = _to_batch_major(hs, nb)
    ev_out = [jnp.stack([r[k] for r in ev]) for k in range(6)]
    od_out = [jnp.stack([r[k] for r in od]) for k in range(6)]
    conv_p, conv_s, kmp, vmp, kms, vms = ev_out
    ksp, vsp, kss, vss, gvp, gvs = od_out
    return (y_prompt, y_sample, conv_p, conv_s, kmp, vmp, kms, vms, ksp, vsp, kss, vss, gvp, gvs)
```

```python
import functools

import jax
import jax.numpy as jnp
from jax import lax
from jax.experimental import pallas as pl
from jax.experimental.pallas import tpu as pltpu

F32 = jnp.float32
BF16 = jnp.bfloat16

HEAD_DIM = 64
LANES = 128
BRANCH_W = 512
MOBA_BLOCK = 256
MOBA_TOPK = 3
MOBA_UNROLL = 4
GMLP_CHUNK = 128
CONV_W = 3
RMS_EPS = 1e-6
NEG = -1e30
LOG2E = 1.4426950408889634
Q_SCALE = HEAD_DIM ** -0.5 * LOG2E
SB_TILE = 256
SB_DEAD = -160.0
VMEM_LIMIT = 56 * 1024 * 1024
PAGE_BUFFERS = 2
PAGE_RING = 32
PAGE_GROUP = 8

_NT = (((1,), (1,)), ((), ()))


def _cparams(sem):
    return pltpu.CompilerParams(dimension_semantics=sem, vmem_limit_bytes=VMEM_LIMIT)


def _rms(x, g):
    ms = jnp.mean(x * x, axis=-1, keepdims=True)
    return x * lax.rsqrt(ms + RMS_EPS) * g


def _silu(x):
    return x / (1.0 + jnp.exp(-x))


def _softplus2(z2):
    return jnp.maximum(z2, 0.0) + jnp.log2(1.0 + jnp.exp2(-jnp.abs(z2)))


def _split_bf16(x):
    hi = x.astype(BF16)
    lo = (x - hi.astype(F32)).astype(BF16)
    return hi, lo


def _tri(n):
    r = lax.broadcasted_iota(jnp.int32, (n, n), 0)
    c = lax.broadcasted_iota(jnp.int32, (n, n), 1)
    return jnp.where(r >= c, 1.0, 0.0).astype(BF16)


def _proj_kernel(x_ref, g_ref, w_ref, rest_ref, qk_ref, kt_ref, vt_ref, vtb_ref, keep_ref, xn_ref,
                 *, qkv_first, blk, keep):
    xn_ref[...] = _rms(x_ref[...], g_ref[...]).astype(BF16)
    _proj_tile(xn_ref, w_ref, rest_ref, qk_ref, kt_ref, vt_ref, vtb_ref, keep_ref, qkv_first, blk, keep)


def _proj_tile(xn_ref, w_ref, rest_ref, qk_ref, kt_ref, vt_ref, vtb_ref, keep_ref, qkv_first, blk, keep):
    tm = xn_ref.shape[0]
    w = BRANCH_W
    keep_tile, keep_row = keep
    for j in range(w_ref.shape[1] // w):
        r = jnp.dot(xn_ref[...], w_ref[:, j * w:(j + 1) * w], preferred_element_type=F32)
        if j == qkv_first:
            qk_ref[:, 0:w] = (r * Q_SCALE).astype(BF16)
        elif j == qkv_first + 1:
            qk_ref[:, w:2 * w] = r.astype(BF16)
            for c in range(tm // blk):
                kt_ref[:, c * blk:(c + 1) * blk] = r[c * blk:(c + 1) * blk, :].T
        elif j == qkv_first + 2:
            for c in range(tm // blk):
                t = r[c * blk:(c + 1) * blk, :].T
                vt_ref[:, c * blk:(c + 1) * blk] = t
                vtb_ref[c] = t.astype(BF16)
        else:
            jr = j if j < qkv_first else j - 3
            rest_ref[:, jr * w:(jr + 1) * w] = r.astype(BF16)
            if jr == 1:
                @pl.when(pl.program_id(1) == keep_tile)
                def _():
                    keep_ref[...] = r[keep_row:keep_row + keep_ref.shape[0], :]


def _proj_outputs(n, bsz, seq, tm, keep_rows):
    tps = seq // tm
    blk = min(MOBA_BLOCK, tm)
    w = BRANCH_W
    m = bsz * seq
    keep_first, keep_n = keep_rows
    keep = (keep_first // tm, keep_first % tm)
    assert keep[1] + keep_n <= tm and keep_n % 8 == 0
    specs = [
        pl.BlockSpec((tm, n - 3 * w), lambda b, t: (b * tps + t, 0)),
        pl.BlockSpec((tm, 2 * w), lambda b, t: (b * tps + t, 0)),
        pl.BlockSpec((None, w, tm), lambda b, t: (b, 0, t)),
        pl.BlockSpec((None, w, tm), lambda b, t: (b, 0, t)),
        pl.BlockSpec((None, tm // blk, w, blk), lambda b, t: (b, t, 0, 0)),
        pl.BlockSpec((None, keep_n, w), lambda b, t: (b, 0, 0)),
    ]
    shapes = [
        jax.ShapeDtypeStruct((m, n - 3 * w), BF16),
        jax.ShapeDtypeStruct((m, 2 * w), BF16),
        jax.ShapeDtypeStruct((bsz, w, seq), F32),
        jax.ShapeDtypeStruct((bsz, w, seq), F32),
        jax.ShapeDtypeStruct((bsz, seq // blk, w, blk), BF16),
        jax.ShapeDtypeStruct((bsz, keep_n, w), F32),
    ]
    return specs, shapes, blk, keep


def _proj(x, g, w_bf, qkv_first, bsz, seq, tm, keep_rows):
    m, d = x.shape
    n = w_bf.shape[1]
    tps = seq // tm
    out_specs, out_shape, blk, keep = _proj_outputs(n, bsz, seq, tm, keep_rows)
    return pl.pallas_call(
        functools.partial(_proj_kernel, qkv_first=qkv_first, blk=blk, keep=keep),
        grid=(bsz, tps),
        in_specs=[
            pl.BlockSpec((tm, d), lambda b, t: (b * tps + t, 0)),
            pl.BlockSpec((1, d), lambda b, t: (0, 0)),
            pl.BlockSpec((d, n), lambda b, t: (0, 0), pipeline_mode=pl.Buffered(1)),
        ],
        out_specs=out_specs,
        out_shape=out_shape,
        scratch_shapes=[pltpu.VMEM((tm, d), BF16)],
        compiler_params=_cparams(("parallel", "arbitrary")),
    )(x, g.reshape(1, d), w_bf)


def _moba_prompt_kernel(q_ref, k_ref, vt_ref, o_ref, km_ref, *, nblk):
    _moba_prompt_block(pl.program_id(2), q_ref, k_ref, vt_ref, o_ref, km_ref, nblk)


def _moba_prompt_block(i, q_ref, k_ref, vt_ref, o_ref, km_ref, nblk):
    lane = lax.broadcasted_iota(jnp.int32, (1, LANES), 1)
    low = lane < HEAD_DIM
    nbp = km_ref.shape[0]
    blk = MOBA_BLOCK
    nheads = q_ref.shape[1] // HEAD_DIM

    @pl.when(i == 0)
    def _():
        km_ref[...] = jnp.zeros_like(km_ref)
        for j in range(nblk):
            rows = k_ref[j * blk:(j + 1) * blk, :].astype(F32)
            km_ref[j:j + 1, :] = jnp.sum(rows, axis=0, keepdims=True) * (1.0 / blk)

    brow = lax.broadcasted_iota(jnp.int32, (nbp, blk), 0)
    key_i = lax.broadcasted_iota(jnp.int32, (blk, blk), 0)
    qry_i = lax.broadcasted_iota(jnp.int32, (blk, blk), 1)
    start = pl.multiple_of(i * blk, blk)

    ones_rows = jnp.ones((2 * 8, blk), BF16)

    def v_rows(j, hh):
        return jnp.concatenate([vt_ref[j, hh * HEAD_DIM:(hh + 1) * HEAD_DIM, :], ones_rows], axis=0)

    q_owns, gates, own_scores = [], [], []
    for hh in range(nheads):
        cols = slice((hh // 2) * LANES, (hh // 2 + 1) * LANES)
        q = q_ref[:, cols]
        km = km_ref[:, cols]
        mine = low if hh % 2 == 0 else jnp.logical_not(low)
        km_hi, km_lo = _split_bf16(jnp.where(mine, km, 0.0))
        gates.append(lax.dot_general(km_hi, q, _NT, preferred_element_type=F32)
                     + lax.dot_general(km_lo, q, _NT, preferred_element_type=F32))
        q_own = jnp.where(mine, q, jnp.zeros_like(q))
        q_owns.append(q_own)
        own_scores.append(lax.dot_general(k_ref[pl.ds(start, blk), cols], q_own, _NT,
                                          preferred_element_type=F32))

    q_pasts, init = [], []
    for hh in range(nheads):
        gate = jnp.where(brow < i, gates[hh], -jnp.inf)
        bias = jnp.full((nbp, blk), NEG, F32)
        for _ in range(MOBA_TOPK):
            mx = jnp.max(gate, axis=0, keepdims=True)
            first = jnp.min(jnp.where(gate == mx, brow, nbp), axis=0, keepdims=True)
            pick = (brow == first) & (mx > -jnp.inf)
            bias = jnp.where(pick, 0.0, bias)
            gate = jnp.where(pick, -jnp.inf, gate)
        bias_t = jnp.concatenate([bias, jnp.zeros((LANES - nbp, blk), F32)], axis=0).T
        q_pasts.append(jnp.concatenate([q_owns[hh], bias_t.astype(BF16)], axis=1))

        s = jnp.where(key_i <= qry_i, own_scores[hh], NEG)
        m0 = jnp.max(s, axis=0, keepdims=True)
        p = jnp.exp2(s - m0)
        acc0 = jnp.dot(v_rows(i, hh), p.astype(BF16), preferred_element_type=F32)
        init.append((m0, acc0))

    def stage(js, carry):
        scores = []
        for j in js:
            st = pl.multiple_of(j * blk, blk)
            onehot = jnp.broadcast_to(jnp.where(lane == j, 1.0, 0.0).astype(BF16), (blk, LANES))
            for hh in range(nheads):
                kj = k_ref[pl.ds(st, blk), (hh // 2) * LANES:(hh // 2 + 1) * LANES]
                scores.append(lax.dot_general(jnp.concatenate([kj, onehot], axis=1), q_pasts[hh],
                                              _NT, preferred_element_type=F32))
        carry = list(carry)
        for n, j in enumerate(js):
            for hh in range(nheads):
                m, acc = carry[hh]
                sj = scores[n * nheads + hh]
                m_new = jnp.maximum(m, jnp.max(sj, axis=0, keepdims=True))
                alpha = jnp.exp2(m - m_new)
                pj = jnp.exp2(sj - m_new)
                acc = alpha * acc + jnp.dot(v_rows(j, hh), pj.astype(BF16),
                                            preferred_element_type=F32)
                carry[hh] = (m_new, acc)
        return tuple(carry)

    fin = tuple(init)
    base = 0
    size = MOBA_UNROLL
    while size >= 1:
        trips = (i - base) // size if size == MOBA_UNROLL else ((i - base) // size) % 2
        fin = lax.fori_loop(
            0, trips,
            lambda t, c, base=base, size=size: stage([base + t * size + u for u in range(size)], c),
            fin)
        base = base + trips * size
        size //= 2
    out_t = jnp.concatenate([acc[:HEAD_DIM] / acc[HEAD_DIM:HEAD_DIM + 1] for _, acc in fin],
                            axis=0)
    o_ref[...] = out_t.T.astype(o_ref.dtype)


def _attn_prompt_call(kernel_fn, qk, vtb, bsz, seq, scratch, wid):
    nblk = seq // MOBA_BLOCK
    ngrp = BRANCH_W // wid
    return pl.pallas_call(
        kernel_fn,
        grid=(bsz, ngrp, nblk),
        in_specs=[
            pl.BlockSpec((MOBA_BLOCK, wid), lambda b, p, i: (b * nblk + i, p)),
            pl.BlockSpec((seq, wid), lambda b, p, i: (b, ngrp + p)),
            pl.BlockSpec((None, nblk, wid, MOBA_BLOCK), lambda b, p, i: (b, 0, p, 0)),
        ],
        out_specs=pl.BlockSpec((MOBA_BLOCK, wid), lambda b, p, i: (b * nblk + i, p)),
        out_shape=jax.ShapeDtypeStruct((bsz * seq, BRANCH_W), BF16),
        scratch_shapes=scratch,
        compiler_params=_cparams(("parallel", "parallel", "arbitrary")),
    )(qk, qk, vtb)


MOBA_STEP_W = 512


def _moba_prompt(qk, vtb, bsz, seq):
    nblk = seq // MOBA_BLOCK
    nbp = -(-nblk // 8) * 8
    return _attn_prompt_call(functools.partial(_moba_prompt_kernel, nblk=nblk), qk, vtb, bsz, seq,
                             [pltpu.VMEM((nbp, MOBA_STEP_W), F32)], MOBA_STEP_W)


def _sb_prompt_kernel(q_ref, k_ref, vt_ref, o_ref, sums_ref):
    i = pl.program_id(2)
    nheads = q_ref.shape[1] // HEAD_DIM
    tile = SB_TILE
    lane = lax.broadcasted_iota(jnp.int32, (1, LANES), 1)
    low = lane < HEAD_DIM
    key_i = lax.broadcasted_iota(jnp.int32, (tile, tile), 0)
    qry_i = lax.broadcasted_iota(jnp.int32, (tile, tile), 1)
    tri_t = jnp.where(qry_i >= key_i, 1.0, 0.0).astype(BF16)
    tri_2 = jnp.concatenate([tri_t, tri_t], axis=1)
    strict = key_i < qry_i

    q_hs = []
    for hh in range(nheads):
        q = q_ref[:, (hh // 2) * LANES:(hh // 2 + 1) * LANES]
        mine = low if hh % 2 == 0 else jnp.logical_not(low)
        q_hs.append(jnp.where(mine, q, jnp.zeros_like(q)))

    def sweep(tiles, state):
        zs = []
        for j, _ in tiles:
            st = pl.multiple_of(j * tile, tile)
            zs.append([lax.dot_general(
                k_ref[pl.ds(st, tile), (hh // 2) * LANES:(hh // 2 + 1) * LANES],
                q_hs[hh], _NT, preferred_element_type=F32) for hh in range(nheads)])
        for n, (_, mask) in enumerate(tiles):
            for hh in range(nheads):
                sp = _softplus2(zs[n][hh])
                if mask is not None:
                    sp = jnp.where(mask, sp, 0.0)
                hi, lo = _split_bf16(sp)
                sums_ref[n, hh] = jnp.dot(tri_2, jnp.concatenate([hi, lo], axis=0),
                                          preferred_element_type=F32)
        state = list(state)
        for n, (j, mask) in enumerate(tiles):
            for hh in range(nheads):
                spent, acc = state[hh]
                cum = sums_ref[n, hh] + spent
                w = jnp.exp2(zs[n][hh] - cum)
                if mask is not None:
                    w = jnp.where(mask, w, 0.0)
                acc = acc + jnp.dot(vt_ref[j, hh * HEAD_DIM:(hh + 1) * HEAD_DIM, :],
                                    w.astype(BF16), preferred_element_type=F32)
                state[hh] = (cum[0:1, :], acc)
        return tuple(state)

    def live_of(state):
        spent = jnp.min(state[0][0])
        for hh in range(1, nheads):
            spent = jnp.minimum(spent, jnp.min(state[hh][0]))
        return -spent

    def cond(st):
        j, live, _ = st
        return (j >= 0) & (live > SB_DEAD)

    def body(st):
        j, _, state = st
        new = sweep([(j, None)], state)
        return j - 1, live_of(new), new

    zero = tuple((jnp.zeros((1, tile), F32), jnp.zeros((HEAD_DIM, tile), F32))
                 for _ in range(nheads))
    has_prev = jnp.broadcast_to(i > 0, (tile, tile))
    init = sweep([(i, strict), (jnp.maximum(i - 1, 0), has_prev)], zero)
    _, _, fin = lax.while_loop(cond, body, (i - 2, live_of(init), init))
    o_ref[...] = jnp.concatenate([acc for _, acc in fin], axis=0).T.astype(o_ref.dtype)


SB_STEP_W = 512


def _sb_prompt(qk, vtb, bsz, seq):
    scratch = [pltpu.VMEM((2, SB_STEP_W // HEAD_DIM, SB_TILE, SB_TILE), F32)]
    return _attn_prompt_call(_sb_prompt_kernel, qk, vtb, bsz, seq, scratch, SB_STEP_W)


def _query_rows(qrep_ref, nheads):
    nrow, width = qrep_ref.shape[1], qrep_ref.shape[2]
    row_w = lax.broadcasted_iota(jnp.int32, (nrow, width), 0)
    lane_w = lax.broadcasted_iota(jnp.int32, (nrow, width), 1)
    head_mask = (lane_w // HEAD_DIM) == (row_w % nheads)
    qrows = jnp.where(head_mask, qrep_ref[0], jnp.zeros_like(qrep_ref[0]))
    return qrows, head_mask


def _pick_heads(y, head_mask, nheads):
    nrow, width = y.shape
    y = jnp.where(head_mask, y, 0.0)
    return jnp.sum(y.reshape(nrow // nheads, nheads, width), axis=1)


def _moba_sample_weights(qrows, knew_ref, s_ref, p_ref, nheads):
    nblk, nrow = s_ref.shape[0], s_ref.shape[1]
    row = lax.broadcasted_iota(jnp.int32, (nrow, LANES), 0)
    lane = lax.broadcasted_iota(jnp.int32, (nrow, LANES), 1)
    qidx = row // nheads
    zn = lax.dot_general(qrows, knew_ref[0], _NT, preferred_element_type=F32)
    gates = jnp.zeros((nrow, LANES), F32)
    for bl in range(nblk):
        gates = jnp.where(lane == bl, jnp.sum(s_ref[bl], axis=1, keepdims=True), gates)
    gates = jnp.where(lane < nblk, gates, -jnp.inf)
    sel = jnp.zeros((nrow, LANES), F32)
    for _ in range(MOBA_TOPK):
        mx = jnp.max(gates, axis=1, keepdims=True)
        first = jnp.min(jnp.where(gates == mx, lane, LANES), axis=1, keepdims=True)
        pick = (lane == first) & (mx > -jnp.inf)
        sel = jnp.where(pick, 1.0, sel)
        gates = jnp.where(pick, -jnp.inf, gates)
    zn = jnp.where(lane <= qidx, zn, NEG)
    m = jnp.max(zn, axis=1, keepdims=True)
    cols = []
    for bl in range(nblk):
        col = jnp.max(jnp.where(lane == bl, sel, 0.0), axis=1, keepdims=True) > 0.5
        cols.append(col)
        m = jnp.maximum(m, jnp.max(jnp.where(col, s_ref[bl], NEG), axis=1, keepdims=True))
    pn = jnp.exp2(zn - m)
    l = jnp.sum(pn, axis=1, keepdims=True)
    for bl in range(nblk):
        p = jnp.exp2(jnp.where(cols[bl], s_ref[bl], NEG) - m)
        l = l + jnp.sum(p, axis=1, keepdims=True)
        p_ref[bl] = p.astype(BF16)
    return pn, l


def _moba_paged_kernel(pt_ref, qrep_ref, knew_ref, vnew_ref, k_hbm, v_hbm, o_ref,
                       kbuf, vbuf, ksem, vsem, s_ref, p_ref, *, npages, nheads):
    b = pl.program_id(0)
    ring = kbuf.shape[0]
    page = kbuf.shape[2]
    ppb = MOBA_BLOCK // page
    qrows, head_mask = _query_rows(qrep_ref, nheads)

    def k_copy(seq, p, slot):
        return pltpu.make_async_copy(k_hbm.at[pt_ref[seq * npages + p]], kbuf.at[slot], ksem.at[slot])

    def v_copy(p, slot):
        return pltpu.make_async_copy(v_hbm.at[pt_ref[b * npages + p]], vbuf.at[slot], vsem.at[slot])

    @pl.when(b == 0)
    def _():
        for p in range(ring):
            k_copy(0, p, p).start()

    grp = min(PAGE_GROUP, ring)

    def k_group(gi, carry):
        base = gi * grp
        slot0 = (gi % (ring // grp)) * grp
        for u in range(grp):
            k_copy(b, base + u, slot0 + u).wait()
        for u in range(grp):
            s_ref[gi * (grp // ppb) + u // ppb, :, (u % ppb) * page:(u % ppb + 1) * page] = jnp.dot(
                qrows, kbuf[slot0 + u].astype(BF16), preferred_element_type=F32)

        @pl.when(base + ring < npages)
        def _():
            for u in range(grp):
                k_copy(b, base + ring + u, slot0 + u).start()
        return carry

    lax.fori_loop(0, npages // grp, k_group, 0)
    for p in range(ring):
        v_copy(p, p).start()

    @pl.when(b + 1 < pl.num_programs(0))
    def _():
        for p in range(ring):
            k_copy(b + 1, p, p).start()

    pn, l = _moba_sample_weights(qrows, knew_ref, s_ref, p_ref, nheads)
    acc0 = jnp.dot(pn.astype(BF16), vnew_ref[0], preferred_element_type=F32)

    def v_group(gi, acc):
        base = gi * grp
        slot0 = (gi % (ring // grp)) * grp
        for u in range(grp):
            v_copy(base + u, slot0 + u).wait()
        for u in range(grp):
            pw = p_ref[gi * (grp // ppb) + u // ppb, :, (u % ppb) * page:(u % ppb + 1) * page]
            acc = acc + lax.dot_general(pw, vbuf[slot0 + u].astype(BF16), _NT,
                                        preferred_element_type=F32)

        @pl.when(base + ring < npages)
        def _():
            for u in range(grp):
                v_copy(base + ring + u, slot0 + u).start()
        return acc

    acc = lax.fori_loop(0, npages // grp, v_group, acc0)
    o_ref[0] = _pick_heads(acc / l, head_mask, nheads)


def _paged_operands(q_bf, knew_bf, vnew_bf, cache_k, cache_v, page_table):
    nb, nq, width = q_bf.shape
    nheads = width // HEAD_DIM
    nphys, page = cache_k.shape[0], cache_k.shape[1]
    ck = cache_k.transpose(0, 2, 3, 1).reshape(nphys, width, page)
    cv = cache_v.transpose(0, 2, 3, 1).reshape(nphys, width, page)
    qrep = jnp.repeat(q_bf, nheads, axis=1)
    pad = ((0, 0), (0, LANES - nq), (0, 0))
    knew = jnp.pad(knew_bf, pad)
    vnew = jnp.pad(vnew_bf, pad)
    pt = page_table.reshape(-1).astype(jnp.int32)
    return qrep, knew, vnew, ck, cv, pt, nheads, page


def _moba_paged(q_bf, knew_bf, vnew_bf, cache_k, cache_v, page_table):
    nb, nq, width = q_bf.shape
    qrep, knew, vnew, ck, cv, pt, nheads, page = _paged_operands(
        q_bf, knew_bf, vnew_bf, cache_k, cache_v, page_table)
    nrow = nq * nheads
    npages = page_table.shape[1]
    ppb = MOBA_BLOCK // page
    ring = min(PAGE_RING, npages)
    grp = min(PAGE_GROUP, ring)
    assert grp % ppb == 0 and ring % grp == 0 and npages % ring == 0
    per_b = lambda b, pt: (b, 0, 0)
    grid_spec = pltpu.PrefetchScalarGridSpec(
        num_scalar_prefetch=1,
        grid=(nb,),
        in_specs=[
            pl.BlockSpec((1, nrow, width), per_b),
            pl.BlockSpec((1, LANES, width), per_b),
            pl.BlockSpec((1, LANES, width), per_b),
            pl.BlockSpec(memory_space=pl.ANY),
            pl.BlockSpec(memory_space=pl.ANY),
        ],
        out_specs=pl.BlockSpec((1, nq, width), per_b),
        scratch_shapes=[
            pltpu.VMEM((ring, width, page), F32),
            pltpu.VMEM((ring, width, page), F32),
            pltpu.SemaphoreType.DMA((ring,)),
            pltpu.SemaphoreType.DMA((ring,)),
            pltpu.VMEM((npages // ppb, nrow, MOBA_BLOCK), F32),
            pltpu.VMEM((npages // ppb, nrow, MOBA_BLOCK), BF16),
        ],
    )
    return pl.pallas_call(
        functools.partial(_moba_paged_kernel, npages=npages, nheads=nheads),
        grid_spec=grid_spec,
        out_shape=jax.ShapeDtypeStruct((nb, nq, width), F32),
        compiler_params=_cparams(("arbitrary",)),
    )(pt, qrep, knew, vnew, ck, cv)


def _moba_fused_kernel(pt_ref, q_ref, k_ref, vt_ref, qrep_ref, knew_ref, vnew_ref, kc_hbm, vc_hbm,
                       o_ref, os_ref, km_ref, pbuf, psem, s_ref, p_ref, acc_ref, l_ref,
                       *, nblk, npages, nheads):
    i = pl.program_id(2)
    step = pl.program_id(0) * nblk + i
    seq = step // 2
    page = pbuf.shape[2]
    ppb = MOBA_BLOCK // page
    grp = min(PAGE_GROUP, npages)

    half = npages // 2
    gph = half // grp
    even = step % 2 == 0
    odd = step % 2 == 1
    has_next = step + 1 < pl.num_programs(0) * nblk

    def page_copy(cache_hbm, sq, p):
        return pltpu.make_async_copy(cache_hbm.at[pt_ref[sq * npages + p]], pbuf.at[p], psem.at[p])

    def start_half(cache_hbm, sq, h):
        for p in range(h * half, (h + 1) * half):
            page_copy(cache_hbm, sq, p).start()

    def wait_half(cache_hbm, sq, h):
        for p in range(h * half, (h + 1) * half):
            page_copy(cache_hbm, sq, p).wait()

    def scores_half(h):
        def k_group(gi, carry):
            for u in range(grp):
                s_ref[gi * (grp // ppb) + u // ppb, :, (u % ppb) * page:(u % ppb + 1) * page] = (
                    jnp.dot(qrows, pbuf[gi * grp + u].astype(BF16), preferred_element_type=F32))
            return carry
        lax.fori_loop(h * gph, (h + 1) * gph, k_group, 0)

    def values_half(h, acc):
        def v_group(gi, acc):
            for u in range(grp):
                pw = p_ref[gi * (grp // ppb) + u // ppb, :, (u % ppb) * page:(u % ppb + 1) * page]
                acc = acc + lax.dot_general(pw, pbuf[gi * grp + u].astype(BF16), _NT,
                                            preferred_element_type=F32)
            return acc
        return lax.fori_loop(h * gph, (h + 1) * gph, v_group, acc)

    qrows, head_mask = _query_rows(qrep_ref, nheads)

    @pl.when(step == 0)
    def _():
        start_half(kc_hbm, 0, 0)
        start_half(kc_hbm, 0, 1)

    @pl.when(even)
    def _():
        wait_half(kc_hbm, seq, 0)
        scores_half(0)
        start_half(vc_hbm, seq, 0)

    @pl.when(odd)
    def _():
        pn, l = _moba_sample_weights(qrows, knew_ref, s_ref, p_ref, nheads)
        l_ref[...] = jnp.broadcast_to(l, l_ref.shape)
        acc0 = jnp.dot(pn.astype(BF16), vnew_ref[0], preferred_element_type=F32)
        wait_half(vc_hbm, seq, 0)
        acc_ref[...] = values_half(0, acc0)

        @pl.when(has_next)
        def _():
            start_half(kc_hbm, seq + 1, 0)

    _moba_prompt_block(i, q_ref, k_ref, vt_ref, o_ref, km_ref, nblk)

    @pl.when(even)
    def _():
        wait_half(kc_hbm, seq, 1)
        scores_half(1)
        start_half(vc_hbm, seq, 1)

    @pl.when(odd)
    def _():
        wait_half(vc_hbm, seq, 1)
        acc = values_half(1, acc_ref[...])
        os_ref[0] = _pick_heads(acc / l_ref[:, 0:1], head_mask, nheads)

        @pl.when(has_next)
        def _():
            start_half(kc_hbm, seq + 1, 1)


def _moba_fused(qk, vtb, bsz, seq, q_bf, knew_bf, vnew_bf, cache_k, cache_v, page_table):
    nb, nq, width = q_bf.shape
    qrep, knew, vnew, ck, cv, pt, nheads, page = _paged_operands(
        q_bf, knew_bf, vnew_bf, cache_k, cache_v, page_table)
    nrow = nq * nheads
    npages = page_table.shape[1]
    nblk = seq // MOBA_BLOCK
    nbp = -(-nblk // 8) * 8
    wid = MOBA_STEP_W
    ppb = MOBA_BLOCK // page
    grp = min(PAGE_GROUP, npages)
    assert bsz * nblk == 2 * nb and wid == BRANCH_W and grp % ppb == 0 and npages % (2 * grp) == 0
    per_seq = lambda b, p, i, pt: ((b * nblk + i) // 2, 0, 0)
    grid_spec = pltpu.PrefetchScalarGridSpec(
        num_scalar_prefetch=1,
        grid=(bsz, 1, nblk),
        in_specs=[
            pl.BlockSpec((MOBA_BLOCK, wid), lambda b, p, i, pt: (b * nblk + i, 0)),
            pl.BlockSpec((seq, wid), lambda b, p, i, pt: (b, 1)),
            pl.BlockSpec((None, nblk, wid, MOBA_BLOCK), lambda b, p, i, pt: (b, 0, 0, 0)),
            pl.BlockSpec((1, nrow, width), per_seq),
            pl.BlockSpec((1, LANES, width), per_seq),
            pl.BlockSpec((1, LANES, width), per_seq),
            pl.BlockSpec(memory_space=pl.ANY),
            pl.BlockSpec(memory_space=pl.ANY),
        ],
        out_specs=[
            pl.BlockSpec((MOBA_BLOCK, wid), lambda b, p, i, pt: (b * nblk + i, 0)),
            pl.BlockSpec((1, nq, width), per_seq),
        ],
        scratch_shapes=[
            pltpu.VMEM((nbp, wid), F32),
            pltpu.VMEM((npages, width, page), F32),
            pltpu.SemaphoreType.DMA((npages,)),
            pltpu.VMEM((npages // ppb, nrow, MOBA_BLOCK), F32),
            pltpu.VMEM((npages // ppb, nrow, MOBA_BLOCK), BF16),
            pltpu.VMEM((nrow, width), F32),
            pltpu.VMEM((nrow, LANES), F32),
        ],
    )
    return pl.pallas_call(
        functools.partial(_moba_fused_kernel, nblk=nblk, npages=npages, nheads=nheads),
        grid_spec=grid_spec,
        out_shape=[jax.ShapeDtypeStruct((bsz * seq, BRANCH_W), BF16),
                   jax.ShapeDtypeStruct((nb, nq, width), F32)],
        compiler_params=_cparams(("arbitrary", "arbitrary", "arbitrary")),
    )(pt, qk, qk, vtb, qrep, knew, vnew, ck, cv)


def _sb_paged_kernel(pt_ref, live_ref, qrep_ref, knew_ref, vnew_ref, cin_ref, ain_ref, *rest,
                     group, nheads, first):
    del pt_ref
    k_refs = rest[:group]
    v_refs = rest[group:2 * group]
    y_ref, acc_ref, carry_ref = rest[2 * group:2 * group + 3]
    b = pl.program_id(0)
    s = pl.program_id(1)
    nrow = qrep_ref.shape[1]
    page = k_refs[0].shape[1]
    ppt = SB_TILE // page
    qrows, head_mask = _query_rows(qrep_ref, nheads)
    tri = _tri(SB_TILE)

    @pl.when(s == 0)
    def _():
        if first:
            row = lax.broadcasted_iota(jnp.int32, (nrow, LANES), 0)
            lane = lax.broadcasted_iota(jnp.int32, (nrow, LANES), 1)
            strict = lane < row // nheads
            zn = lax.dot_general(qrows, knew_ref[0], _NT, preferred_element_type=F32)
            hi, lo = _split_bf16(jnp.where(strict, -_softplus2(zn), 0.0))
            tri_n = _tri(LANES)
            cum = (jnp.dot(hi, tri_n, preferred_element_type=F32)
                   + jnp.dot(lo, tri_n, preferred_element_type=F32))
            wn = jnp.where(strict, jnp.exp2(zn + cum), 0.0)
            acc_ref[0] = jnp.dot(wn.astype(BF16), vnew_ref[0], preferred_element_type=F32)
            carry_ref[0] = jnp.broadcast_to(cum[:, 0:1], carry_ref.shape[1:])
        else:
            acc_ref[0] = ain_ref[0]
            carry_ref[0] = cin_ref[0]

    @pl.when(live_ref[b] > 0)
    def _():
        carry = carry_ref[0][:, 0:1]
        acc = acc_ref[0]
        order = list(reversed(range(group // ppt)))
        tri_2 = jnp.concatenate([tri, tri], axis=0)
        zs = [jnp.dot(qrows,
                      jnp.concatenate([k_refs[t * ppt + u][...] for u in range(ppt)],
                                      axis=1).astype(BF16),
                      preferred_element_type=F32) for t in order]
        sums = []
        for z in zs:
            hi, lo = _split_bf16(-_softplus2(z))
            sums.append(jnp.dot(jnp.concatenate([hi, lo], axis=1), tri_2,
                                preferred_element_type=F32))
        ws = []
        for z, within in zip(zs, sums):
            cum = within + carry
            ws.append(jnp.exp2(z + cum).astype(BF16))
            carry = cum[:, 0:1]
        for t, w in zip(order, ws):
            vt = jnp.concatenate([v_refs[t * ppt + u][...] for u in range(ppt)], axis=1).astype(BF16)
            acc = acc + lax.dot_general(w, vt, _NT, preferred_element_type=F32)
        acc_ref[0] = acc
        carry_ref[0] = jnp.broadcast_to(carry, carry_ref.shape[1:])

    @pl.when(s == pl.num_programs(1) - 1)
    def _():
        y_ref[0] = _pick_heads(acc_ref[0], head_mask, nheads)


def _sb_paged_call(first, pt, live, qrep, knew, vnew, carry_in, acc_in, ck, cv,
                   nq, npages, group, chunk_lo, nsteps, nheads):
    nb, nrow, width = qrep.shape
    page = ck.shape[2]

    def page_map(g):
        def index(b, s, pt, live):
            chunk_page = (chunk_lo + nsteps - 1 - s) * group + g
            return (jnp.where(live[b] > 0, pt[b * npages + chunk_page], 0), 0, 0)
        return index

    per_b = lambda b, s, pt, live: (b, 0, 0)
    in_specs = [
        pl.BlockSpec((1, nrow, width), per_b),
        pl.BlockSpec((1, LANES, width), per_b),
        pl.BlockSpec((1, LANES, width), per_b),
        pl.BlockSpec((1, nrow, LANES), per_b),
        pl.BlockSpec((1, nrow, width), per_b),
    ]
    in_specs += [pl.BlockSpec((None, width, page), page_map(g), pipeline_mode=pl.Buffered(PAGE_BUFFERS))
                 for g in range(group)] * 2
    grid_spec = pltpu.PrefetchScalarGridSpec(
        num_scalar_prefetch=2,
        grid=(nb, nsteps),
        in_specs=in_specs,
        out_specs=[pl.BlockSpec((1, nq, width), per_b),
                   pl.BlockSpec((1, nrow, width), per_b),
                   pl.BlockSpec((1, nrow, LANES), per_b)],
    )
    return pl.pallas_call(
        functools.partial(_sb_paged_kernel, group=group, nheads=nheads, first=first),
        grid_spec=grid_spec,
        out_shape=[jax.ShapeDtypeStruct((nb, nq, width), F32),
                   jax.ShapeDtypeStruct((nb, nrow, width), F32),
                   jax.ShapeDtypeStruct((nb, nrow, LANES), F32)],
        compiler_params=_cparams(("parallel", "arbitrary")),
    )(pt, live, qrep, knew, vnew, carry_in, acc_in, *([ck] * group), *([cv] * group))


def _sb_paged(q_bf, knew_bf, vnew_bf, cache_k, cache_v, page_table, group):
    nb, nq, width = q_bf.shape
    qrep, knew, vnew, ck, cv, pt, nheads, page = _paged_operands(
        q_bf, knew_bf, vnew_bf, cache_k, cache_v, page_table)
    nrow = nq * nheads
    npages = page_table.shape[1]
    nchunk = npages // group
    zc = jnp.zeros((nb, nrow, LANES), F32)
    za = jnp.zeros((nb, nrow, width), F32)
    all_live = jnp.ones((nb,), jnp.int32)
    y, acc, carry = _sb_paged_call(True, pt, all_live, qrep, knew, vnew, zc, za, ck, cv,
                                   nq, npages, group, nchunk - 1, 1, nheads)
    if nchunk == 1:
        return y
    live = (jnp.max(carry, axis=(1, 2)) > SB_DEAD).astype(jnp.int32)

    def older(_):
        return _sb_paged_call(False, pt, live, qrep, knew, vnew, carry, acc, ck, cv,
                              nq, npages, group, 0, nchunk - 1, nheads)[0]

    return lax.cond(jnp.any(live > 0), older, lambda _: y, None)


def _finish(y_first, y_second, w_ref, g_ref, x_ref, o_ref):
    half = y_first.shape[1]
    out = (jnp.dot(y_first.astype(BF16), w_ref[0:half, :], preferred_element_type=F32)
           + jnp.dot(y_second.astype(BF16), w_ref[half:2 * half, :], preferred_element_type=F32))
    h = x_ref[...] + _rms(out, g_ref[...])
    o_ref[...] = h
    return h


def _even_out_prompt_kernel(r_ref, halo_ref, yb_ref, x_ref, w_ref, cw_ref, g_ref, *rest, next_proj):
    if next_proj is None:
        o_ref, tail_ref, ue_ref = rest
    else:
        g2_ref, w2_ref, o_ref, tail_ref, *proj_out, ue_ref, xn_ref = rest
    tm = r_ref.shape[0]
    hrows = halo_ref.shape[0]
    w = BRANCH_W
    group = lambda ref, c: ref[:, c * w:(c + 1) * w].astype(F32)
    u = group(r_ref, 1) * group(r_ref, 2)
    first = pl.program_id(1) == 0
    halo = group(halo_ref, 1) * group(halo_ref, 2)
    ue_ref[0:8, :] = jnp.where(first, 0.0, halo[hrows - 8:hrows, :])
    ue_ref[8:8 + tm, :] = u
    conv = (ue_ref[6:6 + tm, :] * cw_ref[0:1, :] + ue_ref[7:7 + tm, :] * cw_ref[1:2, :]
            + u * cw_ref[2:3, :])
    y_a = group(r_ref, 0) * conv * _silu(group(r_ref, 3))
    y_b = yb_ref[...].astype(F32) * _silu(group(r_ref, 4))
    tail_ref[...] = u[tm - 8:tm, :]
    h = _finish(y_a, y_b, w_ref, g_ref, x_ref, o_ref)
    if next_proj is not None:
        qkv_first, blk, keep = next_proj
        xn_ref[...] = _rms(h, g2_ref[...]).astype(BF16)
        _proj_tile(xn_ref, w2_ref, *proj_out, qkv_first, blk, keep)


def _even_out_prompt(rest, y_b, x, w_out_bf, conv_w, g_post, bsz, seq, tm, nxt=None):
    m, d = x.shape
    tps = seq // tm
    w = BRANCH_W
    row = lambda b, t: b * tps + t
    hrows = 16
    const = lambda shape: pl.BlockSpec(shape, lambda b, t: (0,) * len(shape),
                                       pipeline_mode=pl.Buffered(1))
    in_specs = [pl.BlockSpec((tm, rest.shape[1]), lambda b, t: (row(b, t), 0)),
                pl.BlockSpec((hrows, rest.shape[1]),
                             lambda b, t: (jnp.maximum(row(b, t) * (tm // hrows) - 1, 0), 0)),
                pl.BlockSpec((tm, w), lambda b, t: (row(b, t), 0)),
                pl.BlockSpec((tm, d), lambda b, t: (row(b, t), 0)),
                const(w_out_bf.shape), const(conv_w.shape), const((1, d))]
    args = [rest, rest, y_b, x, w_out_bf, conv_w, g_post.reshape(1, d)]
    out_specs = [pl.BlockSpec((tm, d), lambda b, t: (row(b, t), 0)),
                 pl.BlockSpec((8, w), lambda b, t: (row(b, t), 0))]
    out_shape = [jax.ShapeDtypeStruct((m, d), F32),
                 jax.ShapeDtypeStruct((bsz * tps * 8, w), F32)]
    scratch = [pltpu.VMEM((tm + 8, w), F32)]
    next_proj = None
    if nxt is not None:
        g2, w2_bf, qkv_first, keep_rows = nxt
        specs2, shapes2, blk, keep = _proj_outputs(w2_bf.shape[1], bsz, seq, tm, keep_rows)
        in_specs += [const((1, d)), const(w2_bf.shape)]
        args += [g2.reshape(1, d), w2_bf]
        out_specs += specs2
        out_shape += shapes2
        scratch.append(pltpu.VMEM((tm, d), BF16))
        next_proj = (qkv_first, blk, keep)
    outs = pl.pallas_call(
        functools.partial(_even_out_prompt_kernel, next_proj=next_proj),
        grid=(bsz, tps),
        in_specs=in_specs,
        out_specs=out_specs,
        out_shape=out_shape,
        scratch_shapes=scratch,
        compiler_params=_cparams(("parallel", "arbitrary")),
    )(*args)
    return outs[0], outs[1], (tuple(outs[2:]) if nxt is not None else None)


def _even_out_sample_kernel(rest_ref, st_ref, yb_ref, x_ref, w_ref, cw_ref, g_ref,
                            o_ref, tail_ref, *, nb):
    w = BRANCH_W
    rows = rest_ref.shape[0]
    rest = rest_ref[...].astype(F32)
    u = rest[:, w:2 * w] * rest[:, 2 * w:3 * w]
    ue = jnp.concatenate([st_ref[...], u], axis=0)
    conv = (ue[0:rows] * cw_ref[0:1, :] + ue[nb:nb + rows] * cw_ref[1:2, :]
            + ue[2 * nb:2 * nb + rows] * cw_ref[2:3, :])
    y_a = rest[:, 0:w] * conv * _silu(rest[:, 3 * w:4 * w])
    y_b = yb_ref[...] * _silu(rest[:, 4 * w:5 * w])
    tail_ref[...] = ue[rows:rows + 2 * nb]
    _finish(y_a, y_b, w_ref, g_ref, x_ref, o_ref)


def _even_out_sample(rest, state_tm, y_b, x, w_out_bf, conv_w, g_post, nb):
    m, d = x.shape
    return pl.pallas_call(
        functools.partial(_even_out_sample_kernel, nb=nb),
        out_shape=[jax.ShapeDtypeStruct((m, d), F32),
                   jax.ShapeDtypeStruct(((CONV_W - 1) * nb, BRANCH_W), F32)],
        compiler_params=pltpu.CompilerParams(vmem_limit_bytes=VMEM_LIMIT),
    )(rest, state_tm, y_b, x, w_out_bf, conv_w, g_post.reshape(1, d))


def _gmlp_mix(v, gw_ref, low):
    r_i = lax.broadcasted_iota(jnp.int32, (GMLP_CHUNK, GMLP_CHUNK), 0)
    c_i = lax.broadcasted_iota(jnp.int32, (GMLP_CHUNK, GMLP_CHUNK), 1)
    tril = c_i <= r_i
    parts = []
    for p in range(v.shape[1] // LANES):
        vp = v[:, p * LANES:(p + 1) * LANES]
        v_lo = jnp.where(low, vp, jnp.zeros_like(vp)).astype(BF16)
        v_hi = jnp.where(low, jnp.zeros_like(vp), vp).astype(BF16)
        w_lo = jnp.where(tril, gw_ref[2 * p], 0.0).astype(BF16)
        w_hi = jnp.where(tril, gw_ref[2 * p + 1], 0.0).astype(BF16)
        parts.append(jnp.dot(w_lo, v_lo, preferred_element_type=F32)
                     + jnp.dot(w_hi, v_hi, preferred_element_type=F32))
    return jnp.concatenate(parts, axis=1)


def _odd_out_prompt_kernel(r_ref, yd_ref, x_ref, w_ref, gw_ref, gb_ref, g_ref, o_ref, yc_ref):
    tm = r_ref.shape[0]
    w = BRANCH_W
    low = lax.broadcasted_iota(jnp.int32, (1, LANES), 1) < HEAD_DIM
    for c in range(tm // GMLP_CHUNK):
        rows = slice(c * GMLP_CHUNK, (c + 1) * GMLP_CHUNK)
        mixed = _gmlp_mix(r_ref[rows, w:2 * w], gw_ref, low) + gb_ref[...]
        yc_ref[rows, :] = (r_ref[rows, 0:w].astype(F32) * mixed
                           * _silu(r_ref[rows, 2 * w:3 * w].astype(F32)))
    y_d = yd_ref[...].astype(F32) * _silu(r_ref[:, 3 * w:4 * w].astype(F32))
    _finish(yc_ref[...], y_d, w_ref, g_ref, x_ref, o_ref)


def _odd_out_prompt(rest, y_d, x, w_out_bf, gmlp_w, gb_full, g_post, tm):
    m, d = x.shape
    return pl.pallas_call(
        _odd_out_prompt_kernel,
        grid=(m // tm,),
        in_specs=[pl.BlockSpec((tm, rest.shape[1]), lambda i: (i, 0)),
                  pl.BlockSpec((tm, BRANCH_W), lambda i: (i, 0)),
                  pl.BlockSpec((tm, d), lambda i: (i, 0)),
                  pl.BlockSpec(w_out_bf.shape, lambda i: (0, 0)),
                  pl.BlockSpec(gmlp_w.shape, lambda i: (0, 0, 0)),
                  pl.BlockSpec(gb_full.shape, lambda i: (0, 0)),
                  pl.BlockSpec((1, d), lambda i: (0, 0))],
        out_specs=pl.BlockSpec((tm, d), lambda i: (i, 0)),
        out_shape=jax.ShapeDtypeStruct((m, d), F32),
        scratch_shapes=[pltpu.VMEM((tm, BRANCH_W), F32)],
        compiler_params=_cparams(("parallel",)),
    )(rest, y_d, x, w_out_bf, gmlp_w, gb_full, g_post.reshape(1, d))


def _odd_out_sample_kernel(rest_ref, yd_ref, x_ref, w_ref, w4_ref, b4_ref, g_ref, o_ref, *, nb, nq):
    w = BRANCH_W
    rest = rest_ref[...].astype(F32)
    parts = []
    for t in range(nq):
        mixed = jnp.broadcast_to(b4_ref[t:t + 1, :], (nb, w))
        for s in range(t + 1):
            mixed = mixed + w4_ref[t * nq + s:t * nq + s + 1, :] * rest[s * nb:(s + 1) * nb, w:2 * w]
        parts.append(mixed)
    mixed = jnp.concatenate(parts, axis=0)
    y_c = rest[:, 0:w] * mixed * _silu(rest[:, 2 * w:3 * w])
    y_d = yd_ref[...] * _silu(rest[:, 3 * w:4 * w])
    _finish(y_c, y_d, w_ref, g_ref, x_ref, o_ref)


def _odd_out_sample(rest, y_d, x, w_out_bf, w4, b4, g_post, nb, nq):
    m, d = x.shape
    return pl.pallas_call(
        functools.partial(_odd_out_sample_kernel, nb=nb, nq=nq),
        out_shape=jax.ShapeDtypeStruct((m, d), F32),
        compiler_params=pltpu.CompilerParams(vmem_limit_bytes=VMEM_LIMIT),
    )(rest, y_d, x, w_out_bf, w4, b4, g_post.reshape(1, d))


def _to_token_major(a):
    nb, nq, w = a.shape
    return a.transpose(1, 0, 2).reshape(nq * nb, w)


def _to_batch_major(a, nb):
    w = a.shape[1]
    return a.reshape(-1, nb, w).transpose(1, 0, 2)


def _heads_prompt(t, heads):
    bsz, w, seq = t.shape
    return t.reshape(bsz, heads, w // heads, seq).transpose(0, 3, 1, 2)


def _open_rows(seq):
    open_start = ((seq - 1) // GMLP_CHUNK) * GMLP_CHUNK
    return open_start, seq - open_start


def _project_prompt(hp, dims, g_pre, w_in_bf, qkv_first, tm):
    bsz, seq, _, _ = dims
    return _proj(hp, g_pre, w_in_bf, qkv_first, bsz, seq, tm, _open_rows(seq))


def _project_sample(hs, dims, g_pre, w_in_bf, qkv_first, tm):
    _, _, nb, nq = dims
    ns = nq * nb
    rows = max(ns, tm)
    hs_pad = jnp.pad(hs, ((0, rows - ns), (0, 0)))
    rest_s, qk_s, kt_s, vt_s, _, keep_s = _proj(hs_pad, g_pre, w_in_bf, qkv_first, 1, rows, rows,
                                                (0, ns))
    k_s = _to_batch_major(kt_s[0, :, :ns].T, nb)
    v_s = _to_batch_major(vt_s[0, :, :ns].T, nb)
    q_s = _to_batch_major(qk_s[:ns, :BRANCH_W], nb)
    return rest_s[:ns], q_s, k_s, v_s, keep_s[0]


def _even_layer(hp, hs, dims, state_conv, cache_k, cache_v, page_table,
                g_pre, g_post, w_in, conv_w, w_out, tm, proj_prompt, nxt):
    bsz, seq, nb, nq = dims
    w = BRANCH_W
    heads = w // HEAD_DIM
    w_in_bf = w_in.astype(BF16)
    w_out_bf = w_out.astype(BF16)
    if proj_prompt is None:
        proj_prompt = _project_prompt(hp, dims, g_pre, w_in_bf, 4, tm)
    rest_p, qk_p, kt_p, vt_p, vtb_p, _ = proj_prompt
    rest_s, q_s, k_s, v_s, _ = _project_sample(hs, dims, g_pre, w_in_bf, 4, tm)
    sample_qkv = (q_s, k_s.astype(BF16), v_s.astype(BF16))
    if bsz * (seq // MOBA_BLOCK) == 2 * nb:
        yb_p, yb_s = _moba_fused(qk_p, vtb_p, bsz, seq, *sample_qkv, cache_k, cache_v, page_table)
    else:
        yb_p = _moba_prompt(qk_p, vtb_p, bsz, seq)
        yb_s = _moba_paged(*sample_qkv, cache_k, cache_v, page_table)
    yb_s = _to_token_major(yb_s)
    if nxt is not None:
        nxt = (*nxt, _open_rows(seq))
    hp, tail_p, proj_next = _even_out_prompt(rest_p, yb_p, hp, w_out_bf, conv_w, g_post,
                                             bsz, seq, tm, nxt)
    state_tm = _to_token_major(state_conv)
    hs, tail_s = _even_out_sample(rest_s, state_tm, yb_s, hs, w_out_bf, conv_w, g_post, nb)
    conv_p = tail_p.reshape(bsz, seq // tm, 8, w)[:, -1, 8 - (CONV_W - 1):, :]
    conv_s = _to_batch_major(tail_s, nb)
    outs = (conv_p, conv_s, _heads_prompt(kt_p, heads), _heads_prompt(vt_p, heads),
            k_s.reshape(nb, nq, heads, HEAD_DIM), v_s.reshape(nb, nq, heads, HEAD_DIM))
    return hp, hs, proj_next, outs


def _odd_layer(hp, hs, dims, cache_k, cache_v, page_table,
               g_pre, g_post, w_in_bf, gmlp_w, gmlp_b, w_out, tm, group, proj_prompt):
    bsz, seq, nb, nq = dims
    w = BRANCH_W
    heads = w // HEAD_DIM
    w_out_bf = w_out.astype(BF16)
    ngroups = gmlp_w.shape[0]
    cpg = w // ngroups
    if proj_prompt is None:
        proj_prompt = _project_prompt(hp, dims, g_pre, w_in_bf, 3, tm)
    rest_p, qk_p, kt_p, vt_p, vtb_p, gv_p = proj_prompt
    rest_s, q_s, k_s, v_s, keep_s = _project_sample(hs, dims, g_pre, w_in_bf, 3, tm)
    yd_p = _sb_prompt(qk_p, vtb_p, bsz, seq)
    yd_s = _to_token_major(_sb_paged(q_s, k_s.astype(BF16), v_s.astype(BF16),
                                     cache_k, cache_v, page_table, group))
    gb_full = jnp.repeat(gmlp_b.T, cpg, axis=1)
    hp = _odd_out_prompt(rest_p, yd_p, hp, w_out_bf, gmlp_w, gb_full, g_post, tm)
    w4 = jnp.repeat(gmlp_w[:, :nq, :nq].transpose(1, 2, 0), cpg, axis=2).reshape(nq * nq, w)
    b4 = jnp.repeat(gmlp_b[:, :nq].T, cpg, axis=1)
    hs = _odd_out_sample(rest_s, yd_s, hs, w_out_bf, w4, b4, g_post, nb, nq)
    gv_s = _to_batch_major(keep_s, nb)
    outs = (_heads_prompt(kt_p, heads), _heads_prompt(vt_p, heads),
            k_s.reshape(nb, nq, heads, HEAD_DIM), v_s.reshape(nb, nq, heads, HEAD_DIM), gv_p, gv_s)
    return hp, hs, outs


def kernel(x_prompt, x_sample, state_conv, cache_k_moba, cache_v_moba, cache_k_sb, cache_v_sb,
           page_table, norm_pre_e, norm_post_e, w_in_e, conv_w, w_out_e,
           norm_pre_o, norm_post_o, w_in_o, gmlp_w, gmlp_b, w_out_o):
    bsz, seq, d = x_prompt.shape
    nb, nq, _ = x_sample.shape
    depth = norm_pre_e.shape[0] + norm_pre_o.shape[0]
    npages, page = page_table.shape[1], cache_k_moba.shape[2]
    assert seq % MOBA_BLOCK == 0 and (npages * page) % MOBA_BLOCK == 0 and nq <= GMLP_CHUNK
    assert (nq - 1) // MOBA_BLOCK == 0 and w_in_e.shape[2] == 8 * BRANCH_W and w_in_o.shape[2] == 7 * BRANCH_W
    dims = (bsz, seq, nb, nq)
    tm = min(512, seq)
    group_sb = min(4, npages)
    hp = x_prompt.reshape(bsz * seq, d)
    hs = _to_token_major(x_sample)
    ev, od = [], []
    pending = None
    for i in range(depth):
        j = i // 2
        if i % 2 == 0:
            nxt = (norm_pre_o[j], w_in_o[j].astype(BF16), 3) if i + 1 < depth else None
            hp, hs, pending, outs = _even_layer(
                hp, hs, dims, state_conv[j], cache_k_moba[j], cache_v_moba[j], page_table,
                norm_pre_e[j], norm_post_e[j], w_in_e[j], conv_w[j], w_out_e[j], tm, pending, nxt)
            ev.append(outs)
        else:
            hp, hs, outs = _odd_layer(
                hp, hs, dims, cache_k_sb[j], cache_v_sb[j], page_table,
                norm_pre_o[j], norm_post_o[j], w_in_o[j].astype(BF16), gmlp_w[j], gmlp_b[j],
                w_out_o[j], tm, group_sb, pending)
            od.append(outs)
            pending = None
    y_prompt = hp.reshape(bsz, seq, d)
    y_sample = _to_batch_major(hs, nb)
    ev_out = [jnp.stack([r[k] for r in ev]) for k in range(6)]
    od_out = [jnp.stack([r[k] for r in od]) for k in range(6)]
    conv_p, conv_s, kmp, vmp, kms, vms = ev_out
    ksp, vsp, kss, vss, gvp, gvs = od_out
    return (y_prompt, y_sample, conv_p, conv_s, kmp, vmp, kms, vms, ksp, vsp, kss, vss, gvp, gvs)
```

```python
import functools

import jax
import jax.numpy as jnp
from jax import lax
from jax.experimental import pallas as pl
from jax.experimental.pallas import tpu as pltpu

F32 = jnp.float32
BF16 = jnp.bfloat16

HEAD_DIM = 64
LANES = 128
BRANCH_W = 512
MOBA_BLOCK = 256
MOBA_TOPK = 3
MOBA_HEAD_GROUP = 8
MOBA_UNROLL = 4
GMLP_CHUNK = 128
CONV_W = 3
RMS_EPS = 1e-6
NEG = -1e30
LOG2E = 1.4426950408889634
Q_SCALE = HEAD_DIM ** -0.5 * LOG2E
SB_TILE = 256
SB_DEAD = -160.0
VMEM_LIMIT = 56 * 1024 * 1024
PAGE_BUFFERS = 2
PAGE_RING = 32
PAGE_GROUP = 32

_NT = (((1,), (1,)), ((), ()))


def _cparams(sem):
    return pltpu.CompilerParams(dimension_semantics=sem, vmem_limit_bytes=VMEM_LIMIT)


def _rms(x, g):
    ms = jnp.mean(x * x, axis=-1, keepdims=True)
    return x * lax.rsqrt(ms + RMS_EPS) * g


def _silu(x):
    return x / (1.0 + jnp.exp(-x))


def _softplus2(z2):
    return jnp.maximum(z2, 0.0) + jnp.log2(1.0 + jnp.exp2(-jnp.abs(z2)))


def _split_bf16(x):
    hi = x.astype(BF16)
    lo = (x - hi.astype(F32)).astype(BF16)
    return hi, lo


def _tri(n):
    r = lax.broadcasted_iota(jnp.int32, (n, n), 0)
    c = lax.broadcasted_iota(jnp.int32, (n, n), 1)
    return jnp.where(r >= c, 1.0, 0.0).astype(BF16)


def _proj_kernel(x_ref, g_ref, w_ref, rest_ref, qk_ref, kt_ref, vt_ref, vtb_ref, keep_ref, xn_ref,
                 *, qkv_first, blk, keep):
    xn_ref[...] = _rms(x_ref[...], g_ref[...]).astype(BF16)
    _proj_tile(xn_ref, w_ref, rest_ref, qk_ref, kt_ref, vt_ref, vtb_ref, keep_ref, qkv_first, blk, keep)


def _proj_tile(xn_ref, w_ref, rest_ref, qk_ref, kt_ref, vt_ref, vtb_ref, keep_ref, qkv_first, blk, keep):
    tm = xn_ref.shape[0]
    w = BRANCH_W
    keep_tile, keep_row = keep
    for j in range(w_ref.shape[1] // w):
        r = jnp.dot(xn_ref[...], w_ref[:, j * w:(j + 1) * w], preferred_element_type=F32)
        if j == qkv_first:
            qk_ref[:, 0:w] = (r * Q_SCALE).astype(BF16)
        elif j == qkv_first + 1:
            qk_ref[:, w:2 * w] = r.astype(BF16)
            for c in range(tm // blk):
                kt_ref[:, c * blk:(c + 1) * blk] = r[c * blk:(c + 1) * blk, :].T
        elif j == qkv_first + 2:
            for c in range(tm // blk):
                t = r[c * blk:(c + 1) * blk, :].T
                vt_ref[:, c * blk:(c + 1) * blk] = t
                vtb_ref[c] = t.astype(BF16)
        else:
            jr = j if j < qkv_first else j - 3
            rest_ref[:, jr * w:(jr + 1) * w] = r.astype(BF16)
            if jr == 1:
                @pl.when(pl.program_id(1) == keep_tile)
                def _():
                    keep_ref[...] = r[keep_row:keep_row + keep_ref.shape[0], :]


def _proj_outputs(n, bsz, seq, tm, keep_rows):
    tps = seq // tm
    blk = min(MOBA_BLOCK, tm)
    w = BRANCH_W
    m = bsz * seq
    keep_first, keep_n = keep_rows
    keep = (keep_first // tm, keep_first % tm)
    assert keep[1] + keep_n <= tm and keep_n % 8 == 0
    specs = [
        pl.BlockSpec((tm, n - 3 * w), lambda b, t: (b * tps + t, 0)),
        pl.BlockSpec((tm, 2 * w), lambda b, t: (b * tps + t, 0)),
        pl.BlockSpec((None, w, tm), lambda b, t: (b, 0, t)),
        pl.BlockSpec((None, w, tm), lambda b, t: (b, 0, t)),
        pl.BlockSpec((None, tm // blk, w, blk), lambda b, t: (b, t, 0, 0)),
        pl.BlockSpec((None, keep_n, w), lambda b, t: (b, 0, 0)),
    ]
    shapes = [
        jax.ShapeDtypeStruct((m, n - 3 * w), BF16),
        jax.ShapeDtypeStruct((m, 2 * w), BF16),
        jax.ShapeDtypeStruct((bsz, w, seq), F32),
        jax.ShapeDtypeStruct((bsz, w, seq), F32),
        jax.ShapeDtypeStruct((bsz, seq // blk, w, blk), BF16),
        jax.ShapeDtypeStruct((bsz, keep_n, w), F32),
    ]
    return specs, shapes, blk, keep


def _proj(x, g, w_bf, qkv_first, bsz, seq, tm, keep_rows):
    m, d = x.shape
    n = w_bf.shape[1]
    tps = seq // tm
    out_specs, out_shape, blk, keep = _proj_outputs(n, bsz, seq, tm, keep_rows)
    return pl.pallas_call(
        functools.partial(_proj_kernel, qkv_first=qkv_first, blk=blk, keep=keep),
        grid=(bsz, tps),
        in_specs=[
            pl.BlockSpec((tm, d), lambda b, t: (b * tps + t, 0)),
            pl.BlockSpec((1, d), lambda b, t: (0, 0)),
            pl.BlockSpec((d, n), lambda b, t: (0, 0), pipeline_mode=pl.Buffered(1)),
        ],
        out_specs=out_specs,
        out_shape=out_shape,
        scratch_shapes=[pltpu.VMEM((tm, d), BF16)],
        compiler_params=_cparams(("parallel", "arbitrary")),
    )(x, g.reshape(1, d), w_bf)


def _moba_prompt_kernel(q_ref, k_ref, vt_ref, o_ref, km_ref, *, nblk):
    _moba_prompt_block(pl.program_id(2), q_ref, k_ref, vt_ref, o_ref, km_ref, nblk)


def _moba_prompt_block(i, q_ref, k_ref, vt_ref, o_ref, km_ref, nblk):
    lane = lax.broadcasted_iota(jnp.int32, (1, LANES), 1)
    low = lane < HEAD_DIM
    nbp = km_ref.shape[0]
    blk = MOBA_BLOCK
    nheads = q_ref.shape[1] // HEAD_DIM

    @pl.when(i == 0)
    def _():
        km_ref[...] = jnp.zeros_like(km_ref)
        for j in range(nblk):
            rows = k_ref[j * blk:(j + 1) * blk, :].astype(F32)
            km_ref[j:j + 1, :] = jnp.sum(rows, axis=0, keepdims=True) * (1.0 / blk)

    brow = lax.broadcasted_iota(jnp.int32, (nbp, blk), 0)
    key_i = lax.broadcasted_iota(jnp.int32, (blk, blk), 0)
    qry_i = lax.broadcasted_iota(jnp.int32, (blk, blk), 1)
    start = pl.multiple_of(i * blk, blk)

    ones_rows = jnp.ones((2 * 8, blk), BF16)

    def v_rows(j, hh):
        return jnp.concatenate([vt_ref[j, hh * HEAD_DIM:(hh + 1) * HEAD_DIM, :], ones_rows], axis=0)

    q_owns, gates, own_scores = [], [], []
    for hh in range(nheads):
        cols = slice((hh // 2) * LANES, (hh // 2 + 1) * LANES)
        q = q_ref[:, cols]
        km = km_ref[:, cols]
        mine = low if hh % 2 == 0 else jnp.logical_not(low)
        km_hi, km_lo = _split_bf16(jnp.where(mine, km, 0.0))
        gates.append(lax.dot_general(km_hi, q, _NT, preferred_element_type=F32)
                     + lax.dot_general(km_lo, q, _NT, preferred_element_type=F32))
        q_own = jnp.where(mine, q, jnp.zeros_like(q))
        q_owns.append(q_own)
        own_scores.append(lax.dot_general(k_ref[pl.ds(start, blk), cols], q_own, _NT,
                                          preferred_element_type=F32))

    q_pasts, init = [], []
    for hh in range(nheads):
        gate = jnp.where(brow < i, gates[hh], -jnp.inf)
        bias = jnp.full((nbp, blk), NEG, F32)
        for _ in range(MOBA_TOPK):
            mx = jnp.max(gate, axis=0, keepdims=True)
            first = jnp.min(jnp.where(gate == mx, brow, nbp), axis=0, keepdims=True)
            pick = (brow == first) & (mx > -jnp.inf)
            bias = jnp.where(pick, 0.0, bias)
            gate = jnp.where(pick, -jnp.inf, gate)
        bias_t = jnp.concatenate([bias, jnp.zeros((LANES - nbp, blk), F32)], axis=0).T
        q_pasts.append(jnp.concatenate([q_owns[hh], bias_t.astype(BF16)], axis=1))

        s = jnp.where(key_i <= qry_i, own_scores[hh], NEG)
        m0 = jnp.max(s, axis=0, keepdims=True)
        p = jnp.exp2(s - m0)
        acc0 = jnp.dot(v_rows(i, hh), p.astype(BF16), preferred_element_type=F32)
        init.append((m0, acc0))

    def stage(js, heads, carry):
        scores = []
        for j in js:
            st = pl.multiple_of(j * blk, blk)
            onehot = jnp.broadcast_to(jnp.where(lane == j, 1.0, 0.0).astype(BF16), (blk, LANES))
            for hh in heads:
                kj = k_ref[pl.ds(st, blk), (hh // 2) * LANES:(hh // 2 + 1) * LANES]
                scores.append(lax.dot_general(jnp.concatenate([kj, onehot], axis=1), q_pasts[hh],
                                              _NT, preferred_element_type=F32))
        carry = list(carry)
        for n, j in enumerate(js):
            for c, hh in enumerate(heads):
                m, acc = carry[c]
                sj = scores[n * len(heads) + c]
                m_new = jnp.maximum(m, jnp.max(sj, axis=0, keepdims=True))
                alpha = jnp.exp2(m - m_new)
                pj = jnp.exp2(sj - m_new)
                acc = alpha * acc + jnp.dot(v_rows(j, hh), pj.astype(BF16),
                                            preferred_element_type=F32)
                carry[c] = (m_new, acc)
        return tuple(carry)

    fin = []
    for g0 in range(0, nheads, MOBA_HEAD_GROUP):
        heads = list(range(g0, min(g0 + MOBA_HEAD_GROUP, nheads)))
        part = tuple(init[hh] for hh in heads)
        base = 0
        size = MOBA_UNROLL
        while size >= 1:
            trips = (i - base) // size if size == MOBA_UNROLL else ((i - base) // size) % 2
            part = lax.fori_loop(
                0, trips,
                lambda t, c, base=base, size=size, heads=heads: stage(
                    [base + t * size + u for u in range(size)], heads, c),
                part)
            base = base + trips * size
            size //= 2
        fin.extend(part)
    out_t = jnp.concatenate([acc[:HEAD_DIM] / acc[HEAD_DIM:HEAD_DIM + 1] for _, acc in fin],
                            axis=0)
    o_ref[...] = out_t.T.astype(o_ref.dtype)


def _attn_prompt_call(kernel_fn, qk, vtb, bsz, seq, scratch, wid):
    nblk = seq // MOBA_BLOCK
    ngrp = BRANCH_W // wid
    return pl.pallas_call(
        kernel_fn,
        grid=(bsz, ngrp, nblk),
        in_specs=[
            pl.BlockSpec((MOBA_BLOCK, wid), lambda b, p, i: (b * nblk + i, p)),
            pl.BlockSpec((seq, wid), lambda b, p, i: (b, ngrp + p)),
            pl.BlockSpec((None, nblk, wid, MOBA_BLOCK), lambda b, p, i: (b, 0, p, 0)),
        ],
        out_specs=pl.BlockSpec((MOBA_BLOCK, wid), lambda b, p, i: (b * nblk + i, p)),
        out_shape=jax.ShapeDtypeStruct((bsz * seq, BRANCH_W), BF16),
        scratch_shapes=scratch,
        compiler_params=_cparams(("parallel", "parallel", "arbitrary")),
    )(qk, qk, vtb)


MOBA_STEP_W = 512


def _moba_prompt(qk, vtb, bsz, seq):
    nblk = seq // MOBA_BLOCK
    nbp = -(-nblk // 8) * 8
    return _attn_prompt_call(functools.partial(_moba_prompt_kernel, nblk=nblk), qk, vtb, bsz, seq,
                             [pltpu.VMEM((nbp, MOBA_STEP_W), F32)], MOBA_STEP_W)


def _sb_prompt_kernel(q_ref, k_ref, vt_ref, o_ref, sums_ref):
    i = pl.program_id(2)
    nheads = q_ref.shape[1] // HEAD_DIM
    tile = SB_TILE
    lane = lax.broadcasted_iota(jnp.int32, (1, LANES), 1)
    low = lane < HEAD_DIM
    key_i = lax.broadcasted_iota(jnp.int32, (tile, tile), 0)
    qry_i = lax.broadcasted_iota(jnp.int32, (tile, tile), 1)
    tri_t = jnp.where(qry_i >= key_i, 1.0, 0.0).astype(BF16)
    tri_2 = jnp.concatenate([tri_t, tri_t], axis=1)
    strict = key_i < qry_i

    q_hs = []
    for hh in range(nheads):
        q = q_ref[:, (hh // 2) * LANES:(hh // 2 + 1) * LANES]
        mine = low if hh % 2 == 0 else jnp.logical_not(low)
        q_hs.append(jnp.where(mine, q, jnp.zeros_like(q)))

    def sweep(tiles, state):
        zs = []
        for j, _ in tiles:
            st = pl.multiple_of(j * tile, tile)
            zs.append([lax.dot_general(
                k_ref[pl.ds(st, tile), (hh // 2) * LANES:(hh // 2 + 1) * LANES],
                q_hs[hh], _NT, preferred_element_type=F32) for hh in range(nheads)])
        for n, (_, mask) in enumerate(tiles):
            for hh in range(nheads):
                sp = _softplus2(zs[n][hh])
                if mask is not None:
                    sp = jnp.where(mask, sp, 0.0)
                hi, lo = _split_bf16(sp)
                sums_ref[n, hh] = jnp.dot(tri_2, jnp.concatenate([hi, lo], axis=0),
                                          preferred_element_type=F32)
        state = list(state)
        for n, (j, mask) in enumerate(tiles):
            for hh in range(nheads):
                spent, acc = state[hh]
                cum = sums_ref[n, hh] + spent
                w = jnp.exp2(zs[n][hh] - cum)
                if mask is not None:
                    w = jnp.where(mask, w, 0.0)
                acc = acc + jnp.dot(vt_ref[j, hh * HEAD_DIM:(hh + 1) * HEAD_DIM, :],
                                    w.astype(BF16), preferred_element_type=F32)
                state[hh] = (cum[0:1, :], acc)
        return tuple(state)

    def live_of(state):
        spent = jnp.min(state[0][0])
        for hh in range(1, nheads):
            spent = jnp.minimum(spent, jnp.min(state[hh][0]))
        return -spent

    def cond(st):
        j, live, _ = st
        return (j >= 0) & (live > SB_DEAD)

    def body(st):
        j, _, state = st
        new = sweep([(j, None)], state)
        return j - 1, live_of(new), new

    zero = tuple((jnp.zeros((1, tile), F32), jnp.zeros((HEAD_DIM, tile), F32))
                 for _ in range(nheads))
    has_prev = jnp.broadcast_to(i > 0, (tile, tile))
    init = sweep([(i, strict), (jnp.maximum(i - 1, 0), has_prev)], zero)
    _, _, fin = lax.while_loop(cond, body, (i - 2, live_of(init), init))
    o_ref[...] = jnp.concatenate([acc for _, acc in fin], axis=0).T.astype(o_ref.dtype)


SB_STEP_W = 512


def _sb_prompt(qk, vtb, bsz, seq):
    scratch = [pltpu.VMEM((2, SB_STEP_W // HEAD_DIM, SB_TILE, SB_TILE), F32)]
    return _attn_prompt_call(_sb_prompt_kernel, qk, vtb, bsz, seq, scratch, SB_STEP_W)


def _query_rows(qrep_ref, nheads):
    nrow, width = qrep_ref.shape[1], qrep_ref.shape[2]
    row_w = lax.broadcasted_iota(jnp.int32, (nrow, width), 0)
    lane_w = lax.broadcasted_iota(jnp.int32, (nrow, width), 1)
    head_mask = (lane_w // HEAD_DIM) == (row_w % nheads)
    qrows = jnp.where(head_mask, qrep_ref[0], jnp.zeros_like(qrep_ref[0]))
    return qrows, head_mask


def _pick_heads(y, head_mask, nheads):
    nrow, width = y.shape
    y = jnp.where(head_mask, y, 0.0)
    return jnp.sum(y.reshape(nrow // nheads, nheads, width), axis=1)


def _moba_sample_weights(qrows, knew_ref, s_ref, p_ref, nheads):
    nblk, nrow = s_ref.shape[0], s_ref.shape[1]
    row = lax.broadcasted_iota(jnp.int32, (nrow, LANES), 0)
    lane = lax.broadcasted_iota(jnp.int32, (nrow, LANES), 1)
    qidx = row // nheads
    zn = lax.dot_general(qrows, knew_ref[0], _NT, preferred_element_type=F32)
    gates = jnp.zeros((nrow, LANES), F32)
    for bl in range(nblk):
        gates = jnp.where(lane == bl, jnp.sum(s_ref[bl], axis=1, keepdims=True), gates)
    gates = jnp.where(lane < nblk, gates, -jnp.inf)
    sel = jnp.zeros((nrow, LANES), F32)
    for _ in range(MOBA_TOPK):
        mx = jnp.max(gates, axis=1, keepdims=True)
        first = jnp.min(jnp.where(gates == mx, lane, LANES), axis=1, keepdims=True)
        pick = (lane == first) & (mx > -jnp.inf)
        sel = jnp.where(pick, 1.0, sel)
        gates = jnp.where(pick, -jnp.inf, gates)
    zn = jnp.where(lane <= qidx, zn, NEG)
    m = jnp.max(zn, axis=1, keepdims=True)
    cols = []
    for bl in range(nblk):
        col = jnp.max(jnp.where(lane == bl, sel, 0.0), axis=1, keepdims=True) > 0.5
        cols.append(col)
        m = jnp.maximum(m, jnp.max(jnp.where(col, s_ref[bl], NEG), axis=1, keepdims=True))
    pn = jnp.exp2(zn - m)
    l = jnp.sum(pn, axis=1, keepdims=True)
    for bl in range(nblk):
        p = jnp.exp2(jnp.where(cols[bl], s_ref[bl], NEG) - m)
        l = l + jnp.sum(p, axis=1, keepdims=True)
        p_ref[bl] = p.astype(BF16)
    return pn, l


def _moba_paged_kernel(pt_ref, qrep_ref, knew_ref, vnew_ref, k_hbm, v_hbm, o_ref,
                       kbuf, vbuf, ksem, vsem, s_ref, p_ref, *, npages, nheads):
    b = pl.program_id(0)
    ring = kbuf.shape[0]
    page = kbuf.shape[2]
    ppb = MOBA_BLOCK // page
    qrows, head_mask = _query_rows(qrep_ref, nheads)

    def k_copy(seq, p, slot):
        return pltpu.make_async_copy(k_hbm.at[pt_ref[seq * npages + p]], kbuf.at[slot], ksem.at[slot])

    def v_copy(p, slot):
        return pltpu.make_async_copy(v_hbm.at[pt_ref[b * npages + p]], vbuf.at[slot], vsem.at[slot])

    @pl.when(b == 0)
    def _():
        for p in range(ring):
            k_copy(0, p, p).start()

    grp = min(PAGE_GROUP, ring)

    def k_group(gi, carry):
        base = gi * grp
        slot0 = (gi % (ring // grp)) * grp
        for u in range(grp):
            k_copy(b, base + u, slot0 + u).wait()
        for u in range(grp):
            s_ref[gi * (grp // ppb) + u // ppb, :, (u % ppb) * page:(u % ppb + 1) * page] = jnp.dot(
                qrows, kbuf[slot0 + u].astype(BF16), preferred_element_type=F32)

        @pl.when(base + ring < npages)
        def _():
            for u in range(grp):
                k_copy(b, base + ring + u, slot0 + u).start()
        return carry

    lax.fori_loop(0, npages // grp, k_group, 0)
    for p in range(ring):
        v_copy(p, p).start()

    @pl.when(b + 1 < pl.num_programs(0))
    def _():
        for p in range(ring):
            k_copy(b + 1, p, p).start()

    pn, l = _moba_sample_weights(qrows, knew_ref, s_ref, p_ref, nheads)
    acc0 = jnp.dot(pn.astype(BF16), vnew_ref[0], preferred_element_type=F32)

    def v_group(gi, acc):
        base = gi * grp
        slot0 = (gi % (ring // grp)) * grp
        for u in range(grp):
            v_copy(base + u, slot0 + u).wait()
        for u in range(grp):
            pw = p_ref[gi * (grp // ppb) + u // ppb, :, (u % ppb) * page:(u % ppb + 1) * page]
            acc = acc + lax.dot_general(pw, vbuf[slot0 + u].astype(BF16), _NT,
                                        preferred_element_type=F32)

        @pl.when(base + ring < npages)
        def _():
            for u in range(grp):
                v_copy(base + ring + u, slot0 + u).start()
        return acc

    acc = lax.fori_loop(0, npages // grp, v_group, acc0)
    o_ref[0] = _pick_heads(acc / l, head_mask, nheads)


def _paged_operands(q_bf, knew_bf, vnew_bf, cache_k, cache_v, page_table):
    nb, nq, width = q_bf.shape
    nheads = width // HEAD_DIM
    nphys, page = cache_k.shape[0], cache_k.shape[1]
    ck = cache_k.transpose(0, 2, 3, 1).reshape(nphys, width, page)
    cv = cache_v.transpose(0, 2, 3, 1).reshape(nphys, width, page)
    qrep = jnp.repeat(q_bf, nheads, axis=1)
    pad = ((0, 0), (0, LANES - nq), (0, 0))
    knew = jnp.pad(knew_bf, pad)
    vnew = jnp.pad(vnew_bf, pad)
    pt = page_table.reshape(-1).astype(jnp.int32)
    return qrep, knew, vnew, ck, cv, pt, nheads, page


def _moba_paged(q_bf, knew_bf, vnew_bf, cache_k, cache_v, page_table):
    nb, nq, width = q_bf.shape
    qrep, knew, vnew, ck, cv, pt, nheads, page = _paged_operands(
        q_bf, knew_bf, vnew_bf, cache_k, cache_v, page_table)
    nrow = nq * nheads
    npages = page_table.shape[1]
    ppb = MOBA_BLOCK // page
    ring = min(PAGE_RING, npages)
    grp = min(PAGE_GROUP, ring)
    assert grp % ppb == 0 and ring % grp == 0 and npages % ring == 0
    per_b = lambda b, pt: (b, 0, 0)
    grid_spec = pltpu.PrefetchScalarGridSpec(
        num_scalar_prefetch=1,
        grid=(nb,),
        in_specs=[
            pl.BlockSpec((1, nrow, width), per_b),
            pl.BlockSpec((1, LANES, width), per_b),
            pl.BlockSpec((1, LANES, width), per_b),
            pl.BlockSpec(memory_space=pl.ANY),
            pl.BlockSpec(memory_space=pl.ANY),
        ],
        out_specs=pl.BlockSpec((1, nq, width), per_b),
        scratch_shapes=[
            pltpu.VMEM((ring, width, page), F32),
            pltpu.VMEM((ring, width, page), F32),
            pltpu.SemaphoreType.DMA((ring,)),
            pltpu.SemaphoreType.DMA((ring,)),
            pltpu.VMEM((npages // ppb, nrow, MOBA_BLOCK), F32),
            pltpu.VMEM((npages // ppb, nrow, MOBA_BLOCK), BF16),
        ],
    )
    return pl.pallas_call(
        functools.partial(_moba_paged_kernel, npages=npages, nheads=nheads),
        grid_spec=grid_spec,
        out_shape=jax.ShapeDtypeStruct((nb, nq, width), F32),
        compiler_params=_cparams(("arbitrary",)),
    )(pt, qrep, knew, vnew, ck, cv)


def _moba_fused_kernel(pt_ref, q_ref, k_ref, vt_ref, qrep_ref, knew_ref, vnew_ref, kc_hbm, vc_hbm,
                       o_ref, os_ref, km_ref, pbuf, psem, s_ref, p_ref, acc_ref, l_ref,
                       *, nblk, npages, nheads):
    i = pl.program_id(2)
    step = pl.program_id(0) * nblk + i
    seq = step // 2
    page = pbuf.shape[2]
    ppb = MOBA_BLOCK // page
    grp = min(PAGE_GROUP, npages)

    half = npages // 2
    gph = half // grp
    even = step % 2 == 0
    odd = step % 2 == 1
    has_next = step + 1 < pl.num_programs(0) * nblk

    def page_copy(cache_hbm, sq, p):
        return pltpu.make_async_copy(cache_hbm.at[pt_ref[sq * npages + p]], pbuf.at[p], psem.at[p])

    def start_half(cache_hbm, sq, h):
        for p in range(h * half, (h + 1) * half):
            page_copy(cache_hbm, sq, p).start()

    def wait_half(cache_hbm, sq, h):
        for p in range(h * half, (h + 1) * half):
            page_copy(cache_hbm, sq, p).wait()

    def scores_half(h):
        def k_group(gi, carry):
            for u in range(grp):
                s_ref[gi * (grp // ppb) + u // ppb, :, (u % ppb) * page:(u % ppb + 1) * page] = (
                    jnp.dot(qrows, pbuf[gi * grp + u].astype(BF16), preferred_element_type=F32))
            return carry
        lax.fori_loop(h * gph, (h + 1) * gph, k_group, 0)

    def values_half(h, acc):
        def v_group(gi, acc):
            for u in range(grp):
                pw = p_ref[gi * (grp // ppb) + u // ppb, :, (u % ppb) * page:(u % ppb + 1) * page]
                acc = acc + lax.dot_general(pw, pbuf[gi * grp + u].astype(BF16), _NT,
                                            preferred_element_type=F32)
            return acc
        return lax.fori_loop(h * gph, (h + 1) * gph, v_group, acc)

    qrows, head_mask = _query_rows(qrep_ref, nheads)

    @pl.when(step == 0)
    def _():
        start_half(kc_hbm, 0, 0)
        start_half(kc_hbm, 0, 1)

    @pl.when(even)
    def _():
        wait_half(kc_hbm, seq, 0)
        scores_half(0)
        start_half(vc_hbm, seq, 0)

    @pl.when(odd)
    def _():
        pn, l = _moba_sample_weights(qrows, knew_ref, s_ref, p_ref, nheads)
        l_ref[...] = jnp.broadcast_to(l, l_ref.shape)
        acc0 = jnp.dot(pn.astype(BF16), vnew_ref[0], preferred_element_type=F32)
        wait_half(vc_hbm, seq, 0)
        acc_ref[...] = values_half(0, acc0)

        @pl.when(has_next)
        def _():
            start_half(kc_hbm, seq + 1, 0)

    _moba_prompt_block(i, q_ref, k_ref, vt_ref, o_ref, km_ref, nblk)

    @pl.when(even)
    def _():
        wait_half(kc_hbm, seq, 1)
        scores_half(1)
        start_half(vc_hbm, seq, 1)

    @pl.when(odd)
    def _():
        wait_half(vc_hbm, seq, 1)
        acc = values_half(1, acc_ref[...])
        os_ref[0] = _pick_heads(acc / l_ref[:, 0:1], head_mask, nheads)

        @pl.when(has_next)
        def _():
            start_half(kc_hbm, seq + 1, 1)


def _moba_fused(qk, vtb, bsz, seq, q_bf, knew_bf, vnew_bf, cache_k, cache_v, page_table):
    nb, nq, width = q_bf.shape
    qrep, knew, vnew, ck, cv, pt, nheads, page = _paged_operands(
        q_bf, knew_bf, vnew_bf, cache_k, cache_v, page_table)
    nrow = nq * nheads
    npages = page_table.shape[1]
    nblk = seq // MOBA_BLOCK
    nbp = -(-nblk // 8) * 8
    wid = MOBA_STEP_W
    ppb = MOBA_BLOCK // page
    grp = min(PAGE_GROUP, npages)
    assert bsz * nblk == 2 * nb and wid == BRANCH_W and grp % ppb == 0 and npages % (2 * grp) == 0
    per_seq = lambda b, p, i, pt: ((b * nblk + i) // 2, 0, 0)
    grid_spec = pltpu.PrefetchScalarGridSpec(
        num_scalar_prefetch=1,
        grid=(bsz, 1, nblk),
        in_specs=[
            pl.BlockSpec((MOBA_BLOCK, wid), lambda b, p, i, pt: (b * nblk + i, 0)),
            pl.BlockSpec((seq, wid), lambda b, p, i, pt: (b, 1)),
            pl.BlockSpec((None, nblk, wid, MOBA_BLOCK), lambda b, p, i, pt: (b, 0, 0, 0)),
            pl.BlockSpec((1, nrow, width), per_seq),
            pl.BlockSpec((1, LANES, width), per_seq),
            pl.BlockSpec((1, LANES, width), per_seq),
            pl.BlockSpec(memory_space=pl.ANY),
            pl.BlockSpec(memory_space=pl.ANY),
        ],
        out_specs=[
            pl.BlockSpec((MOBA_BLOCK, wid), lambda b, p, i, pt: (b * nblk + i, 0)),
            pl.BlockSpec((1, nq, width), per_seq),
        ],
        scratch_shapes=[
            pltpu.VMEM((nbp, wid), F32),
            pltpu.VMEM((npages, width, page), F32),
            pltpu.SemaphoreType.DMA((npages,)),
            pltpu.VMEM((npages // ppb, nrow, MOBA_BLOCK), F32),
            pltpu.VMEM((npages // ppb, nrow, MOBA_BLOCK), BF16),
            pltpu.VMEM((nrow, width), F32),
            pltpu.VMEM((nrow, LANES), F32),
        ],
    )
    return pl.pallas_call(
        functools.partial(_moba_fused_kernel, nblk=nblk, npages=npages, nheads=nheads),
        grid_spec=grid_spec,
        out_shape=[jax.ShapeDtypeStruct((bsz * seq, BRANCH_W), BF16),
                   jax.ShapeDtypeStruct((nb, nq, width), F32)],
        compiler_params=_cparams(("arbitrary", "arbitrary", "arbitrary")),
    )(pt, qk, qk, vtb, qrep, knew, vnew, ck, cv)


def _sb_paged_kernel(pt_ref, live_ref, qrep_ref, knew_ref, vnew_ref, cin_ref, ain_ref, *rest,
                     group, nheads, first):
    del pt_ref
    k_refs = rest[:group]
    v_refs = rest[group:2 * group]
    y_ref, acc_ref, carry_ref = rest[2 * group:2 * group + 3]
    b = pl.program_id(0)
    s = pl.program_id(1)
    nrow = qrep_ref.shape[1]
    page = k_refs[0].shape[1]
    ppt = SB_TILE // page
    qrows, head_mask = _query_rows(qrep_ref, nheads)
    tri = _tri(SB_TILE)

    @pl.when(s == 0)
    def _():
        if first:
            row = lax.broadcasted_iota(jnp.int32, (nrow, LANES), 0)
            lane = lax.broadcasted_iota(jnp.int32, (nrow, LANES), 1)
            strict = lane < row // nheads
            zn = lax.dot_general(qrows, knew_ref[0], _NT, preferred_element_type=F32)
            hi, lo = _split_bf16(jnp.where(strict, -_softplus2(zn), 0.0))
            tri_n = _tri(LANES)
            cum = (jnp.dot(hi, tri_n, preferred_element_type=F32)
                   + jnp.dot(lo, tri_n, preferred_element_type=F32))
            wn = jnp.where(strict, jnp.exp2(zn + cum), 0.0)
            acc_ref[0] = jnp.dot(wn.astype(BF16), vnew_ref[0], preferred_element_type=F32)
            carry_ref[0] = jnp.broadcast_to(cum[:, 0:1], carry_ref.shape[1:])
        else:
            acc_ref[0] = ain_ref[0]
            carry_ref[0] = cin_ref[0]

    @pl.when(live_ref[b] > 0)
    def _():
        carry = carry_ref[0][:, 0:1]
        acc = acc_ref[0]
        order = list(reversed(range(group // ppt)))
        tri_2 = jnp.concatenate([tri, tri], axis=0)
        zs = [jnp.dot(qrows,
                      jnp.concatenate([k_refs[t * ppt + u][...] for u in range(ppt)],
                                      axis=1).astype(BF16),
                      preferred_element_type=F32) for t in order]
        sums = []
        for z in zs:
            hi, lo = _split_bf16(-_softplus2(z))
            sums.append(jnp.dot(jnp.concatenate([hi, lo], axis=1), tri_2,
                                preferred_element_type=F32))
        ws = []
        for z, within in zip(zs, sums):
            cum = within + carry
            ws.append(jnp.exp2(z + cum).astype(BF16))
            carry = cum[:, 0:1]
        for t, w in zip(order, ws):
            vt = jnp.concatenate([v_refs[t * ppt + u][...] for u in range(ppt)], axis=1).astype(BF16)
            acc = acc + lax.dot_general(w, vt, _NT, preferred_element_type=F32)
        acc_ref[0] = acc
        carry_ref[0] = jnp.broadcast_to(carry, carry_ref.shape[1:])

    @pl.when(s == pl.num_programs(1) - 1)
    def _():
        y_ref[0] = _pick_heads(acc_ref[0], head_mask, nheads)


def _sb_paged_call(first, pt, live, qrep, knew, vnew, carry_in, acc_in, ck, cv,
                   nq, npages, group, chunk_lo, nsteps, nheads):
    nb, nrow, width = qrep.shape
    page = ck.shape[2]

    def page_map(g):
        def index(b, s, pt, live):
            chunk_page = (chunk_lo + nsteps - 1 - s) * group + g
            return (jnp.where(live[b] > 0, pt[b * npages + chunk_page], 0), 0, 0)
        return index

    per_b = lambda b, s, pt, live: (b, 0, 0)
    in_specs = [
        pl.BlockSpec((1, nrow, width), per_b),
        pl.BlockSpec((1, LANES, width), per_b),
        pl.BlockSpec((1, LANES, width), per_b),
        pl.BlockSpec((1, nrow, LANES), per_b),
        pl.BlockSpec((1, nrow, width), per_b),
    ]
    in_specs += [pl.BlockSpec((None, width, page), page_map(g), pipeline_mode=pl.Buffered(PAGE_BUFFERS))
                 for g in range(group)] * 2
    grid_spec = pltpu.PrefetchScalarGridSpec(
        num_scalar_prefetch=2,
        grid=(nb, nsteps),
        in_specs=in_specs,
        out_specs=[pl.BlockSpec((1, nq, width), per_b),
                   pl.BlockSpec((1, nrow, width), per_b),
                   pl.BlockSpec((1, nrow, LANES), per_b)],
    )
    return pl.pallas_call(
        functools.partial(_sb_paged_kernel, group=group, nheads=nheads, first=first),
        grid_spec=grid_spec,
        out_shape=[jax.ShapeDtypeStruct((nb, nq, width), F32),
                   jax.ShapeDtypeStruct((nb, nrow, width), F32),
                   jax.ShapeDtypeStruct((nb, nrow, LANES), F32)],
        compiler_params=_cparams(("parallel", "arbitrary")),
    )(pt, live, qrep, knew, vnew, carry_in, acc_in, *([ck] * group), *([cv] * group))


def _sb_paged(q_bf, knew_bf, vnew_bf, cache_k, cache_v, page_table, group):
    nb, nq, width = q_bf.shape
    qrep, knew, vnew, ck, cv, pt, nheads, page = _paged_operands(
        q_bf, knew_bf, vnew_bf, cache_k, cache_v, page_table)
    nrow = nq * nheads
    npages = page_table.shape[1]
    nchunk = npages // group
    zc = jnp.zeros((nb, nrow, LANES), F32)
    za = jnp.zeros((nb, nrow, width), F32)
    all_live = jnp.ones((nb,), jnp.int32)
    y, acc, carry = _sb_paged_call(True, pt, all_live, qrep, knew, vnew, zc, za, ck, cv,
                                   nq, npages, group, nchunk - 1, 1, nheads)
    if nchunk == 1:
        return y
    live = (jnp.max(carry, axis=(1, 2)) > SB_DEAD).astype(jnp.int32)

    def older(_):
        return _sb_paged_call(False, pt, live, qrep, knew, vnew, carry, acc, ck, cv,
                              nq, npages, group, 0, nchunk - 1, nheads)[0]

    return lax.cond(jnp.any(live > 0), older, lambda _: y, None)


def _finish(y_first, y_second, w_ref, g_ref, x_ref, o_ref):
    half = y_first.shape[1]
    out = (jnp.dot(y_first.astype(BF16), w_ref[0:half, :], preferred_element_type=F32)
           + jnp.dot(y_second.astype(BF16), w_ref[half:2 * half, :], preferred_element_type=F32))
    h = x_ref[...] + _rms(out, g_ref[...])
    o_ref[...] = h
    return h


def _even_out_prompt_kernel(r_ref, halo_ref, yb_ref, x_ref, w_ref, cw_ref, g_ref, *rest, next_proj):
    if next_proj is None:
        o_ref, tail_ref, ue_ref = rest
    else:
        g2_ref, w2_ref, o_ref, tail_ref, *proj_out, ue_ref, xn_ref = rest
    tm = r_ref.shape[0]
    hrows = halo_ref.shape[0]
    w = BRANCH_W
    group = lambda ref, c: ref[:, c * w:(c + 1) * w].astype(F32)
    u = group(r_ref, 1) * group(r_ref, 2)
    first = pl.program_id(1) == 0
    halo = group(halo_ref, 1) * group(halo_ref, 2)
    ue_ref[0:8, :] = jnp.where(first, 0.0, halo[hrows - 8:hrows, :])
    ue_ref[8:8 + tm, :] = u
    conv = (ue_ref[6:6 + tm, :] * cw_ref[0:1, :] + ue_ref[7:7 + tm, :] * cw_ref[1:2, :]
            + u * cw_ref[2:3, :])
    y_a = group(r_ref, 0) * conv * _silu(group(r_ref, 3))
    y_b = yb_ref[...].astype(F32) * _silu(group(r_ref, 4))
    tail_ref[...] = u[tm - 8:tm, :]
    h = _finish(y_a, y_b, w_ref, g_ref, x_ref, o_ref)
    if next_proj is not None:
        qkv_first, blk, keep = next_proj
        xn_ref[...] = _rms(h, g2_ref[...]).astype(BF16)
        _proj_tile(xn_ref, w2_ref, *proj_out, qkv_first, blk, keep)


def _even_out_prompt(rest, y_b, x, w_out_bf, conv_w, g_post, bsz, seq, tm, nxt=None):
    m, d = x.shape
    tps = seq // tm
    w = BRANCH_W
    row = lambda b, t: b * tps + t
    hrows = 16
    const = lambda shape: pl.BlockSpec(shape, lambda b, t: (0,) * len(shape),
                                       pipeline_mode=pl.Buffered(1))
    in_specs = [pl.BlockSpec((tm, rest.shape[1]), lambda b, t: (row(b, t), 0)),
                pl.BlockSpec((hrows, rest.shape[1]),
                             lambda b, t: (jnp.maximum(row(b, t) * (tm // hrows) - 1, 0), 0)),
                pl.BlockSpec((tm, w), lambda b, t: (row(b, t), 0)),
                pl.BlockSpec((tm, d), lambda b, t: (row(b, t), 0)),
                const(w_out_bf.shape), const(conv_w.shape), const((1, d))]
    args = [rest, rest, y_b, x, w_out_bf, conv_w, g_post.reshape(1, d)]
    out_specs = [pl.BlockSpec((tm, d), lambda b, t: (row(b, t), 0)),
                 pl.BlockSpec((8, w), lambda b, t: (row(b, t), 0))]
    out_shape = [jax.ShapeDtypeStruct((m, d), F32),
                 jax.ShapeDtypeStruct((bsz * tps * 8, w), F32)]
    scratch = [pltpu.VMEM((tm + 8, w), F32)]
    next_proj = None
    if nxt is not None:
        g2, w2_bf, qkv_first, keep_rows = nxt
        specs2, shapes2, blk, keep = _proj_outputs(w2_bf.shape[1], bsz, seq, tm, keep_rows)
        in_specs += [const((1, d)), const(w2_bf.shape)]
        args += [g2.reshape(1, d), w2_bf]
        out_specs += specs2
        out_shape += shapes2
        scratch.append(pltpu.VMEM((tm, d), BF16))
        next_proj = (qkv_first, blk, keep)
    outs = pl.pallas_call(
        functools.partial(_even_out_prompt_kernel, next_proj=next_proj),
        grid=(bsz, tps),
        in_specs=in_specs,
        out_specs=out_specs,
        out_shape=out_shape,
        scratch_shapes=scratch,
        compiler_params=_cparams(("parallel", "arbitrary")),
    )(*args)
    return outs[0], outs[1], (tuple(outs[2:]) if nxt is not None else None)


def _even_out_sample_kernel(rest_ref, st_ref, yb_ref, x_ref, w_ref, cw_ref, g_ref,
                            o_ref, tail_ref, *, nb):
    w = BRANCH_W
    rows = rest_ref.shape[0]
    rest = rest_ref[...].astype(F32)
    u = rest[:, w:2 * w] * rest[:, 2 * w:3 * w]
    ue = jnp.concatenate([st_ref[...], u], axis=0)
    conv = (ue[0:rows] * cw_ref[0:1, :] + ue[nb:nb + rows] * cw_ref[1:2, :]
            + ue[2 * nb:2 * nb + rows] * cw_ref[2:3, :])
    y_a = rest[:, 0:w] * conv * _silu(rest[:, 3 * w:4 * w])
    y_b = yb_ref[...] * _silu(rest[:, 4 * w:5 * w])
    tail_ref[...] = ue[rows:rows + 2 * nb]
    _finish(y_a, y_b, w_ref, g_ref, x_ref, o_ref)


def _even_out_sample(rest, state_tm, y_b, x, w_out_bf, conv_w, g_post, nb):
    m, d = x.shape
    return pl.pallas_call(
        functools.partial(_even_out_sample_kernel, nb=nb),
        out_shape=[jax.ShapeDtypeStruct((m, d), F32),
                   jax.ShapeDtypeStruct(((CONV_W - 1) * nb, BRANCH_W), F32)],
        compiler_params=pltpu.CompilerParams(vmem_limit_bytes=VMEM_LIMIT),
    )(rest, state_tm, y_b, x, w_out_bf, conv_w, g_post.reshape(1, d))


def _gmlp_mix(v, gw_ref, low):
    r_i = lax.broadcasted_iota(jnp.int32, (GMLP_CHUNK, GMLP_CHUNK), 0)
    c_i = lax.broadcasted_iota(jnp.int32, (GMLP_CHUNK, GMLP_CHUNK), 1)
    tril = c_i <= r_i
    parts = []
    for p in range(v.shape[1] // LANES):
        vp = v[:, p * LANES:(p + 1) * LANES]
        v_lo = jnp.where(low, vp, jnp.zeros_like(vp)).astype(BF16)
        v_hi = jnp.where(low, jnp.zeros_like(vp), vp).astype(BF16)
        w_lo = jnp.where(tril, gw_ref[2 * p], 0.0).astype(BF16)
        w_hi = jnp.where(tril, gw_ref[2 * p + 1], 0.0).astype(BF16)
        parts.append(jnp.dot(w_lo, v_lo, preferred_element_type=F32)
                     + jnp.dot(w_hi, v_hi, preferred_element_type=F32))
    return jnp.concatenate(parts, axis=1)


def _odd_out_prompt_kernel(r_ref, yd_ref, x_ref, w_ref, gw_ref, gb_ref, g_ref, o_ref, yc_ref):
    tm = r_ref.shape[0]
    w = BRANCH_W
    low = lax.broadcasted_iota(jnp.int32, (1, LANES), 1) < HEAD_DIM
    for c in range(tm // GMLP_CHUNK):
        rows = slice(c * GMLP_CHUNK, (c + 1) * GMLP_CHUNK)
        mixed = _gmlp_mix(r_ref[rows, w:2 * w], gw_ref, low) + gb_ref[...]
        yc_ref[rows, :] = (r_ref[rows, 0:w].astype(F32) * mixed
                           * _silu(r_ref[rows, 2 * w:3 * w].astype(F32)))
    y_d = yd_ref[...].astype(F32) * _silu(r_ref[:, 3 * w:4 * w].astype(F32))
    _finish(yc_ref[...], y_d, w_ref, g_ref, x_ref, o_ref)


def _odd_out_prompt(rest, y_d, x, w_out_bf, gmlp_w, gb_full, g_post, tm):
    m, d = x.shape
    return pl.pallas_call(
        _odd_out_prompt_kernel,
        grid=(m // tm,),
        in_specs=[pl.BlockSpec((tm, rest.shape[1]), lambda i: (i, 0)),
                  pl.BlockSpec((tm, BRANCH_W), lambda i: (i, 0)),
                  pl.BlockSpec((tm, d), lambda i: (i, 0)),
                  pl.BlockSpec(w_out_bf.shape, lambda i: (0, 0)),
                  pl.BlockSpec(gmlp_w.shape, lambda i: (0, 0, 0)),
                  pl.BlockSpec(gb_full.shape, lambda i: (0, 0)),
                  pl.BlockSpec((1, d), lambda i: (0, 0))],
        out_specs=pl.BlockSpec((tm, d), lambda i: (i, 0)),
        out_shape=jax.ShapeDtypeStruct((m, d), F32),
        scratch_shapes=[pltpu.VMEM((tm, BRANCH_W), F32)],
        compiler_params=_cparams(("parallel",)),
    )(rest, y_d, x, w_out_bf, gmlp_w, gb_full, g_post.reshape(1, d))


def _odd_out_sample_kernel(rest_ref, yd_ref, x_ref, w_ref, w4_ref, b4_ref, g_ref, o_ref, *, nb, nq):
    w = BRANCH_W
    rest = rest_ref[...].astype(F32)
    parts = []
    for t in range(nq):
        mixed = jnp.broadcast_to(b4_ref[t:t + 1, :], (nb, w))
        for s in range(t + 1):
            mixed = mixed + w4_ref[t * nq + s:t * nq + s + 1, :] * rest[s * nb:(s + 1) * nb, w:2 * w]
        parts.append(mixed)
    mixed = jnp.concatenate(parts, axis=0)
    y_c = rest[:, 0:w] * mixed * _silu(rest[:, 2 * w:3 * w])
    y_d = yd_ref[...] * _silu(rest[:, 3 * w:4 * w])
    _finish(y_c, y_d, w_ref, g_ref, x_ref, o_ref)


def _odd_out_sample(rest, y_d, x, w_out_bf, w4, b4, g_post, nb, nq):
    m, d = x.shape
    return pl.pallas_call(
        functools.partial(_odd_out_sample_kernel, nb=nb, nq=nq),
        out_shape=jax.ShapeDtypeStruct((m, d), F32),
        compiler_params=pltpu.CompilerParams(vmem_limit_bytes=VMEM_LIMIT),
    )(rest, y_d, x, w_out_bf, w4, b4, g_post.reshape(1, d))


def _to_token_major(a):
    nb, nq, w = a.shape
    return a.transpose(1, 0, 2).reshape(nq * nb, w)


def _to_batch_major(a, nb):
    w = a.shape[1]
    return a.reshape(-1, nb, w).transpose(1, 0, 2)


def _heads_prompt(t, heads):
    bsz, w, seq = t.shape
    return t.reshape(bsz, heads, w // heads, seq).transpose(0, 3, 1, 2)


def _open_rows(seq):
    open_start = ((seq - 1) // GMLP_CHUNK) * GMLP_CHUNK
    return open_start, seq - open_start


def _project_prompt(hp, dims, g_pre, w_in_bf, qkv_first, tm):
    bsz, seq, _, _ = dims
    return _proj(hp, g_pre, w_in_bf, qkv_first, bsz, seq, tm, _open_rows(seq))


def _project_sample(hs, dims, g_pre, w_in_bf, qkv_first, tm):
    _, _, nb, nq = dims
    ns = nq * nb
    rows = max(ns, tm)
    hs_pad = jnp.pad(hs, ((0, rows - ns), (0, 0)))
    rest_s, qk_s, kt_s, vt_s, _, keep_s = _proj(hs_pad, g_pre, w_in_bf, qkv_first, 1, rows, rows,
                                                (0, ns))
    k_s = _to_batch_major(kt_s[0, :, :ns].T, nb)
    v_s = _to_batch_major(vt_s[0, :, :ns].T, nb)
    q_s = _to_batch_major(qk_s[:ns, :BRANCH_W], nb)
    return rest_s[:ns], q_s, k_s, v_s, keep_s[0]


def _even_layer(hp, hs, dims, state_conv, cache_k, cache_v, page_table,
                g_pre, g_post, w_in, conv_w, w_out, tm, proj_prompt, nxt):
    bsz, seq, nb, nq = dims
    w = BRANCH_W
    heads = w // HEAD_DIM
    w_in_bf = w_in.astype(BF16)
    w_out_bf = w_out.astype(BF16)
    if proj_prompt is None:
        proj_prompt = _project_prompt(hp, dims, g_pre, w_in_bf, 4, tm)
    rest_p, qk_p, kt_p, vt_p, vtb_p, _ = proj_prompt
    rest_s, q_s, k_s, v_s, _ = _project_sample(hs, dims, g_pre, w_in_bf, 4, tm)
    sample_qkv = (q_s, k_s.astype(BF16), v_s.astype(BF16))
    if bsz * (seq // MOBA_BLOCK) == 2 * nb:
        yb_p, yb_s = _moba_fused(qk_p, vtb_p, bsz, seq, *sample_qkv, cache_k, cache_v, page_table)
    else:
        yb_p = _moba_prompt(qk_p, vtb_p, bsz, seq)
        yb_s = _moba_paged(*sample_qkv, cache_k, cache_v, page_table)
    yb_s = _to_token_major(yb_s)
    if nxt is not None:
        nxt = (*nxt, _open_rows(seq))
    hp, tail_p, proj_next = _even_out_prompt(rest_p, yb_p, hp, w_out_bf, conv_w, g_post,
                                             bsz, seq, tm, nxt)
    state_tm = _to_token_major(state_conv)
    hs, tail_s = _even_out_sample(rest_s, state_tm, yb_s, hs, w_out_bf, conv_w, g_post, nb)
    conv_p = tail_p.reshape(bsz, seq // tm, 8, w)[:, -1, 8 - (CONV_W - 1):, :]
    conv_s = _to_batch_major(tail_s, nb)
    outs = (conv_p, conv_s, _heads_prompt(kt_p, heads), _heads_prompt(vt_p, heads),
            k_s.reshape(nb, nq, heads, HEAD_DIM), v_s.reshape(nb, nq, heads, HEAD_DIM))
    return hp, hs, proj_next, outs


def _odd_layer(hp, hs, dims, cache_k, cache_v, page_table,
               g_pre, g_post, w_in_bf, gmlp_w, gmlp_b, w_out, tm, group, proj_prompt):
    bsz, seq, nb, nq = dims
    w = BRANCH_W
    heads = w // HEAD_DIM
    w_out_bf = w_out.astype(BF16)
    ngroups = gmlp_w.shape[0]
    cpg = w // ngroups
    if proj_prompt is None:
        proj_prompt = _project_prompt(hp, dims, g_pre, w_in_bf, 3, tm)
    rest_p, qk_p, kt_p, vt_p, vtb_p, gv_p = proj_prompt
    rest_s, q_s, k_s, v_s, keep_s = _project_sample(hs, dims, g_pre, w_in_bf, 3, tm)
    yd_p = _sb_prompt(qk_p, vtb_p, bsz, seq)
    yd_s = _to_token_major(_sb_paged(q_s, k_s.astype(BF16), v_s.astype(BF16),
                                     cache_k, cache_v, page_table, group))
    gb_full = jnp.repeat(gmlp_b.T, cpg, axis=1)
    hp = _odd_out_prompt(rest_p, yd_p, hp, w_out_bf, gmlp_w, gb_full, g_post, tm)
    w4 = jnp.repeat(gmlp_w[:, :nq, :nq].transpose(1, 2, 0), cpg, axis=2).reshape(nq * nq, w)
    b4 = jnp.repeat(gmlp_b[:, :nq].T, cpg, axis=1)
    hs = _odd_out_sample(rest_s, yd_s, hs, w_out_bf, w4, b4, g_post, nb, nq)
    gv_s = _to_batch_major(keep_s, nb)
    outs = (_heads_prompt(kt_p, heads), _heads_prompt(vt_p, heads),
            k_s.reshape(nb, nq, heads, HEAD_DIM), v_s.reshape(nb, nq, heads, HEAD_DIM), gv_p, gv_s)
    return hp, hs, outs


def kernel(x_prompt, x_sample, state_conv, cache_k_moba, cache_v_moba, cache_k_sb, cache_v_sb,
           page_table, norm_pre_e, norm_post_e, w_in_e, conv_w, w_out_e,
           norm_pre_o, norm_post_o, w_in_o, gmlp_w, gmlp_b, w_out_o):
    bsz, seq, d = x_prompt.shape
    nb, nq, _ = x_sample.shape
    depth = norm_pre_e.shape[0] + norm_pre_o.shape[0]
    npages, page = page_table.shape[1], cache_k_moba.shape[2]
    assert seq % MOBA_BLOCK == 0 and (npages * page) % MOBA_BLOCK == 0 and nq <= GMLP_CHUNK
    assert (nq - 1) // MOBA_BLOCK == 0 and w_in_e.shape[2] == 8 * BRANCH_W and w_in_o.shape[2] == 7 * BRANCH_W
    dims = (bsz, seq, nb, nq)
    tm = min(512, seq)
    group_sb = min(4, npages)
    hp = x_prompt.reshape(bsz * seq, d)
    hs = _to_token_major(x_sample)
    ev, od = [], []
    pending = None
    for i in range(depth):
        j = i // 2
        if i % 2 == 0:
            nxt = (norm_pre_o[j], w_in_o[j].astype(BF16), 3) if i + 1 < depth else None
            hp, hs, pending, outs = _even_layer(
                hp, hs, dims, state_conv[j], cache_k_moba[j], cache_v_moba[j], page_table,
                norm_pre_e[j], norm_post_e[j], w_in_e[j], conv_w[j], w_out_e[j], tm, pending, nxt)
            ev.append(outs)
        else:
            hp, hs, outs = _odd_layer(
                hp, hs, dims, cache_k_sb[j], cache_v_sb[j], page_table,
                norm_pre_o[j], norm_post_o[j], w_in_o[j].astype(BF16), gmlp_w[j], gmlp_b[j],
                w_out_o[j], tm, group_sb, pending)
            od.append(outs)
            pending = None
    y_prompt = hp.reshape(bsz, seq, d)
    y_sample = _to_batch_major(hs, nb)
    ev_out = [jnp.stack([r[k] for r in ev]) for k in range(6)]
    od_out = [jnp.stack([r[k] for r in od]) for k in range(6)]
    conv_p, conv_s, kmp, vmp, kms, vms = ev_out
    ksp, vsp, kss, vss, gvp, gvs = od_out
    return (y_prompt, y_sample, conv_p, conv_s, kmp, vmp, kms, vms, ksp, vsp, kss, vss, gvp, gvs)
```

```python
import functools

import jax
import jax.numpy as jnp
from jax import lax
from jax.experimental import pallas as pl
from jax.experimental.pallas import tpu as pltpu

F32 = jnp.float32
BF16 = jnp.bfloat16

HEAD_DIM = 64
LANES = 128
BRANCH_W = 512
MOBA_BLOCK = 256
MOBA_TOPK = 3
MOBA_HEAD_GROUP = 8
MOBA_UNROLL = 4
GMLP_CHUNK = 128
CONV_W = 3
RMS_EPS = 1e-6
NEG = -1e30
LOG2E = 1.4426950408889634
Q_SCALE = HEAD_DIM ** -0.5 * LOG2E
SB_TILE = 256
SB_DEAD = -160.0
VMEM_LIMIT = 56 * 1024 * 1024
PAGE_BUFFERS = 2
PAGE_RING = 32
PAGE_GROUP = 32

_NT = (((1,), (1,)), ((), ()))


def _cparams(sem):
    return pltpu.CompilerParams(dimension_semantics=sem, vmem_limit_bytes=VMEM_LIMIT)


def _rms(x, g):
    ms = jnp.mean(x * x, axis=-1, keepdims=True)
    return x * lax.rsqrt(ms + RMS_EPS) * g


def _silu(x):
    return x / (1.0 + jnp.exp(-x))


def _softplus2(z2):
    return jnp.maximum(z2, 0.0) + jnp.log2(1.0 + jnp.exp2(-jnp.abs(z2)))


def _split_bf16(x):
    hi = x.astype(BF16)
    lo = (x - hi.astype(F32)).astype(BF16)
    return hi, lo


def _tri(n):
    r = lax.broadcasted_iota(jnp.int32, (n, n), 0)
    c = lax.broadcasted_iota(jnp.int32, (n, n), 1)
    return jnp.where(r >= c, 1.0, 0.0).astype(BF16)


def _proj_kernel(x_ref, g_ref, w_ref, rest_ref, qk_ref, kt_ref, vt_ref, vtb_ref, keep_ref, xn_ref,
                 *, qkv_first, blk, keep):
    xn_ref[...] = _rms(x_ref[...], g_ref[...]).astype(BF16)
    _proj_tile(xn_ref, w_ref, rest_ref, qk_ref, kt_ref, vt_ref, vtb_ref, keep_ref, qkv_first, blk, keep)


def _proj_tile(xn_ref, w_ref, rest_ref, qk_ref, kt_ref, vt_ref, vtb_ref, keep_ref, qkv_first, blk, keep):
    tm = xn_ref.shape[0]
    w = BRANCH_W
    keep_tile, keep_row = keep
    for j in range(w_ref.shape[1] // w):
        r = jnp.dot(xn_ref[...], w_ref[:, j * w:(j + 1) * w], preferred_element_type=F32)
        if j == qkv_first:
            qk_ref[:, 0:w] = (r * Q_SCALE).astype(BF16)
        elif j == qkv_first + 1:
            qk_ref[:, w:2 * w] = r.astype(BF16)
            for c in range(tm // blk):
                kt_ref[:, c * blk:(c + 1) * blk] = r[c * blk:(c + 1) * blk, :].T
        elif j == qkv_first + 2:
            for c in range(tm // blk):
                t = r[c * blk:(c + 1) * blk, :].T
                vt_ref[:, c * blk:(c + 1) * blk] = t
                vtb_ref[c] = t.astype(BF16)
        else:
            jr = j if j < qkv_first else j - 3
            rest_ref[:, jr * w:(jr + 1) * w] = r.astype(BF16)
            if jr == 1:
                @pl.when(pl.program_id(1) == keep_tile)
                def _():
                    keep_ref[...] = r[keep_row:keep_row + keep_ref.shape[0], :]


def _proj_outputs(n, bsz, seq, tm, keep_rows):
    tps = seq // tm
    blk = min(MOBA_BLOCK, tm)
    w = BRANCH_W
    m = bsz * seq
    keep_first, keep_n = keep_rows
    keep = (keep_first // tm, keep_first % tm)
    assert keep[1] + keep_n <= tm and keep_n % 8 == 0
    specs = [
        pl.BlockSpec((tm, n - 3 * w), lambda b, t: (b * tps + t, 0)),
        pl.BlockSpec((tm, 2 * w), lambda b, t: (b * tps + t, 0)),
        pl.BlockSpec((None, w, tm), lambda b, t: (b, 0, t)),
        pl.BlockSpec((None, w, tm), lambda b, t: (b, 0, t)),
        pl.BlockSpec((None, tm // blk, w, blk), lambda b, t: (b, t, 0, 0)),
        pl.BlockSpec((None, keep_n, w), lambda b, t: (b, 0, 0)),
    ]
    shapes = [
        jax.ShapeDtypeStruct((m, n - 3 * w), BF16),
        jax.ShapeDtypeStruct((m, 2 * w), BF16),
        jax.ShapeDtypeStruct((bsz, w, seq), F32),
        jax.ShapeDtypeStruct((bsz, w, seq), F32),
        jax.ShapeDtypeStruct((bsz, seq // blk, w, blk), BF16),
        jax.ShapeDtypeStruct((bsz, keep_n, w), F32),
    ]
    return specs, shapes, blk, keep


def _proj(x, g, w_bf, qkv_first, bsz, seq, tm, keep_rows):
    m, d = x.shape
    n = w_bf.shape[1]
    tps = seq // tm
    out_specs, out_shape, blk, keep = _proj_outputs(n, bsz, seq, tm, keep_rows)
    return pl.pallas_call(
        functools.partial(_proj_kernel, qkv_first=qkv_first, blk=blk, keep=keep),
        grid=(bsz, tps),
        in_specs=[
            pl.BlockSpec((tm, d), lambda b, t: (b * tps + t, 0)),
            pl.BlockSpec((1, d), lambda b, t: (0, 0)),
            pl.BlockSpec((d, n), lambda b, t: (0, 0), pipeline_mode=pl.Buffered(1)),
        ],
        out_specs=out_specs,
        out_shape=out_shape,
        scratch_shapes=[pltpu.VMEM((tm, d), BF16)],
        compiler_params=_cparams(("parallel", "arbitrary")),
    )(x, g.reshape(1, d), w_bf)


def _moba_prompt_kernel(q_ref, k_ref, vt_ref, o_ref, km_ref, *, nblk):
    _moba_prompt_block(pl.program_id(2), q_ref, k_ref, vt_ref, o_ref, km_ref, nblk)


def _moba_prompt_block(i, q_ref, k_ref, vt_ref, o_ref, km_ref, nblk):
    lane = lax.broadcasted_iota(jnp.int32, (1, LANES), 1)
    low = lane < HEAD_DIM
    nbp = km_ref.shape[0]
    blk = MOBA_BLOCK
    nheads = q_ref.shape[1] // HEAD_DIM

    @pl.when(i == 0)
    def _():
        km_ref[...] = jnp.zeros_like(km_ref)
        for j in range(nblk):
            rows = k_ref[j * blk:(j + 1) * blk, :].astype(F32)
            km_ref[j:j + 1, :] = jnp.sum(rows, axis=0, keepdims=True) * (1.0 / blk)

    brow = lax.broadcasted_iota(jnp.int32, (nbp, blk), 0)
    key_i = lax.broadcasted_iota(jnp.int32, (blk, blk), 0)
    qry_i = lax.broadcasted_iota(jnp.int32, (blk, blk), 1)
    start = pl.multiple_of(i * blk, blk)

    ones_rows = jnp.ones((2 * 8, blk), BF16)

    def v_rows(j, hh):
        return jnp.concatenate([vt_ref[j, hh * HEAD_DIM:(hh + 1) * HEAD_DIM, :], ones_rows], axis=0)

    q_owns, gates, own_scores = [], [], []
    for hh in range(nheads):
        cols = slice((hh // 2) * LANES, (hh // 2 + 1) * LANES)
        q = q_ref[:, cols]
        km = km_ref[:, cols]
        mine = low if hh % 2 == 0 else jnp.logical_not(low)
        km_hi, km_lo = _split_bf16(jnp.where(mine, km, 0.0))
        gates.append(lax.dot_general(km_hi, q, _NT, preferred_element_type=F32)
                     + lax.dot_general(km_lo, q, _NT, preferred_element_type=F32))
        q_own = jnp.where(mine, q, jnp.zeros_like(q))
        q_owns.append(q_own)
        own_scores.append(lax.dot_general(k_ref[pl.ds(start, blk), cols], q_own, _NT,
                                          preferred_element_type=F32))

    q_pasts, init = [], []
    for hh in range(nheads):
        gate = jnp.where(brow < i, gates[hh], -jnp.inf)
        bias = jnp.full((nbp, blk), NEG, F32)
        for _ in range(MOBA_TOPK):
            mx = jnp.max(gate, axis=0, keepdims=True)
            first = jnp.min(jnp.where(gate == mx, brow, nbp), axis=0, keepdims=True)
            pick = (brow == first) & (mx > -jnp.inf)
            bias = jnp.where(pick, 0.0, bias)
            gate = jnp.where(pick, -jnp.inf, gate)
        bias_t = jnp.concatenate([bias, jnp.zeros((LANES - nbp, blk), F32)], axis=0).T
        q_pasts.append(jnp.concatenate([q_owns[hh], bias_t.astype(BF16)], axis=1))

        s = jnp.where(key_i <= qry_i, own_scores[hh], NEG)
        m0 = jnp.max(s, axis=0, keepdims=True)
        p = jnp.exp2(s - m0)
        acc0 = jnp.dot(v_rows(i, hh), p.astype(BF16), preferred_element_type=F32)
        init.append((m0, acc0))

    def stage(js, heads, carry):
        scores = []
        for j in js:
            st = pl.multiple_of(j * blk, blk)
            onehot = jnp.broadcast_to(jnp.where(lane == j, 1.0, 0.0).astype(BF16), (blk, LANES))
            for hh in heads:
                kj = k_ref[pl.ds(st, blk), (hh // 2) * LANES:(hh // 2 + 1) * LANES]
                scores.append(lax.dot_general(jnp.concatenate([kj, onehot], axis=1), q_pasts[hh],
                                              _NT, preferred_element_type=F32))
        carry = list(carry)
        for n, j in enumerate(js):
            for c, hh in enumerate(heads):
                m, acc = carry[c]
                sj = scores[n * len(heads) + c]
                m_new = jnp.maximum(m, jnp.max(sj, axis=0, keepdims=True))
                alpha = jnp.exp2(m - m_new)
                pj = jnp.exp2(sj - m_new)
                acc = alpha * acc + jnp.dot(v_rows(j, hh), pj.astype(BF16),
                                            preferred_element_type=F32)
                carry[c] = (m_new, acc)
        return tuple(carry)

    fin = []
    for g0 in range(0, nheads, MOBA_HEAD_GROUP):
        heads = list(range(g0, min(g0 + MOBA_HEAD_GROUP, nheads)))
        part = tuple(init[hh] for hh in heads)
        base = 0
        size = MOBA_UNROLL
        while size >= 1:
            trips = (i - base) // size if size == MOBA_UNROLL else ((i - base) // size) % 2
            part = lax.fori_loop(
                0, trips,
                lambda t, c, base=base, size=size, heads=heads: stage(
                    [base + t * size + u for u in range(size)], heads, c),
                part)
            base = base + trips * size
            size //= 2
        fin.extend(part)
    out_t = jnp.concatenate([acc[:HEAD_DIM] / acc[HEAD_DIM:HEAD_DIM + 1] for _, acc in fin],
                            axis=0)
    o_ref[...] = out_t.T.astype(o_ref.dtype)


def _attn_prompt_call(kernel_fn, qk, vtb, bsz, seq, scratch, wid):
    nblk = seq // MOBA_BLOCK
    ngrp = BRANCH_W // wid
    return pl.pallas_call(
        kernel_fn,
        grid=(bsz, ngrp, nblk),
        in_specs=[
            pl.BlockSpec((MOBA_BLOCK, wid), lambda b, p, i: (b * nblk + i, p)),
            pl.BlockSpec((seq, wid), lambda b, p, i: (b, ngrp + p)),
            pl.BlockSpec((None, nblk, wid, MOBA_BLOCK), lambda b, p, i: (b, 0, p, 0)),
        ],
        out_specs=pl.BlockSpec((MOBA_BLOCK, wid), lambda b, p, i: (b * nblk + i, p)),
        out_shape=jax.ShapeDtypeStruct((bsz * seq, BRANCH_W), BF16),
        scratch_shapes=scratch,
        compiler_params=_cparams(("parallel", "parallel", "arbitrary")),
    )(qk, qk, vtb)


MOBA_STEP_W = 512


def _moba_prompt(qk, vtb, bsz, seq):
    nblk = seq // MOBA_BLOCK
    nbp = -(-nblk // 8) * 8
    return _attn_prompt_call(functools.partial(_moba_prompt_kernel, nblk=nblk), qk, vtb, bsz, seq,
                             [pltpu.VMEM((nbp, MOBA_STEP_W), F32)], MOBA_STEP_W)


def _sb_prompt_kernel(q_ref, k_ref, vt_ref, o_ref, sums_ref):
    i = pl.program_id(2)
    nheads = q_ref.shape[1] // HEAD_DIM
    tile = SB_TILE
    lane = lax.broadcasted_iota(jnp.int32, (1, LANES), 1)
    low = lane < HEAD_DIM
    key_i = lax.broadcasted_iota(jnp.int32, (tile, tile), 0)
    qry_i = lax.broadcasted_iota(jnp.int32, (tile, tile), 1)
    tri_t = jnp.where(qry_i >= key_i, 1.0, 0.0).astype(BF16)
    tri_2 = jnp.concatenate([tri_t, tri_t], axis=1)
    strict = key_i < qry_i

    q_hs = []
    for hh in range(nheads):
        q = q_ref[:, (hh // 2) * LANES:(hh // 2 + 1) * LANES]
        mine = low if hh % 2 == 0 else jnp.logical_not(low)
        q_hs.append(jnp.where(mine, q, jnp.zeros_like(q)))

    def sweep(tiles, state):
        zs = []
        for j, _ in tiles:
            st = pl.multiple_of(j * tile, tile)
            zs.append([lax.dot_general(
                k_ref[pl.ds(st, tile), (hh // 2) * LANES:(hh // 2 + 1) * LANES],
                q_hs[hh], _NT, preferred_element_type=F32) for hh in range(nheads)])
        for n, (_, mask) in enumerate(tiles):
            for hh in range(nheads):
                sp = _softplus2(zs[n][hh])
                if mask is not None:
                    sp = jnp.where(mask, sp, 0.0)
                hi, lo = _split_bf16(sp)
                sums_ref[n, hh] = jnp.dot(tri_2, jnp.concatenate([hi, lo], axis=0),
                                          preferred_element_type=F32)
        state = list(state)
        for n, (j, mask) in enumerate(tiles):
            for hh in range(nheads):
                spent, acc = state[hh]
                cum = sums_ref[n, hh] + spent
                w = jnp.exp2(zs[n][hh] - cum)
                if mask is not None:
                    w = jnp.where(mask, w, 0.0)
                acc = acc + jnp.dot(vt_ref[j, hh * HEAD_DIM:(hh + 1) * HEAD_DIM, :],
                                    w.astype(BF16), preferred_element_type=F32)
                state[hh] = (cum[0:1, :], acc)
        return tuple(state)

    def live_of(state):
        spent = jnp.min(state[0][0])
        for hh in range(1, nheads):
            spent = jnp.minimum(spent, jnp.min(state[hh][0]))
        return -spent

    def cond(st):
        j, live, _ = st
        return (j >= 0) & (live > SB_DEAD)

    def body(st):
        j, _, state = st
        new = sweep([(j, None)], state)
        return j - 1, live_of(new), new

    zero = tuple((jnp.zeros((1, tile), F32), jnp.zeros((HEAD_DIM, tile), F32))
                 for _ in range(nheads))
    has_prev = jnp.broadcast_to(i > 0, (tile, tile))
    init = sweep([(i, strict), (jnp.maximum(i - 1, 0), has_prev)], zero)
    _, _, fin = lax.while_loop(cond, body, (i - 2, live_of(init), init))
    o_ref[...] = jnp.concatenate([acc for _, acc in fin], axis=0).T.astype(o_ref.dtype)


SB_STEP_W = 512


def _sb_prompt(qk, vtb, bsz, seq):
    scratch = [pltpu.VMEM((2, SB_STEP_W // HEAD_DIM, SB_TILE, SB_TILE), F32)]
    return _attn_prompt_call(_sb_prompt_kernel, qk, vtb, bsz, seq, scratch, SB_STEP_W)


def _query_rows(qrep_ref, nheads):
    nrow, width = qrep_ref.shape[1], qrep_ref.shape[2]
    row_w = lax.broadcasted_iota(jnp.int32, (nrow, width), 0)
    lane_w = lax.broadcasted_iota(jnp.int32, (nrow, width), 1)
    head_mask = (lane_w // HEAD_DIM) == (row_w % nheads)
    qrows = jnp.where(head_mask, qrep_ref[0], jnp.zeros_like(qrep_ref[0]))
    return qrows, head_mask


def _pick_heads(y, head_mask, nheads):
    nrow, width = y.shape
    y = jnp.where(head_mask, y, 0.0)
    return jnp.sum(y.reshape(nrow // nheads, nheads, width), axis=1)


def _moba_sample_weights(qrows, knew_ref, s_ref, p_ref, nheads):
    nblk, nrow = s_ref.shape[0], s_ref.shape[1]
    row = lax.broadcasted_iota(jnp.int32, (nrow, LANES), 0)
    lane = lax.broadcasted_iota(jnp.int32, (nrow, LANES), 1)
    qidx = row // nheads
    zn = lax.dot_general(qrows, knew_ref[0], _NT, preferred_element_type=F32)
    gates = jnp.zeros((nrow, LANES), F32)
    for bl in range(nblk):
        gates = jnp.where(lane == bl, jnp.sum(s_ref[bl], axis=1, keepdims=True), gates)
    gates = jnp.where(lane < nblk, gates, -jnp.inf)
    sel = jnp.zeros((nrow, LANES), F32)
    for _ in range(MOBA_TOPK):
        mx = jnp.max(gates, axis=1, keepdims=True)
        first = jnp.min(jnp.where(gates == mx, lane, LANES), axis=1, keepdims=True)
        pick = (lane == first) & (mx > -jnp.inf)
        sel = jnp.where(pick, 1.0, sel)
        gates = jnp.where(pick, -jnp.inf, gates)
    zn = jnp.where(lane <= qidx, zn, NEG)
    m = jnp.max(zn, axis=1, keepdims=True)
    cols = []
    for bl in range(nblk):
        col = jnp.max(jnp.where(lane == bl, sel, 0.0), axis=1, keepdims=True) > 0.5
        cols.append(col)
        m = jnp.maximum(m, jnp.max(jnp.where(col, s_ref[bl], NEG), axis=1, keepdims=True))
    pn = jnp.exp2(zn - m)
    l = jnp.sum(pn, axis=1, keepdims=True)
    for bl in range(nblk):
        p = jnp.exp2(jnp.where(cols[bl], s_ref[bl], NEG) - m)
        l = l + jnp.sum(p, axis=1, keepdims=True)
        p_ref[bl] = p.astype(BF16)
    return pn, l


def _moba_paged_kernel(pt_ref, qrep_ref, knew_ref, vnew_ref, k_hbm, v_hbm, o_ref,
                       kbuf, vbuf, ksem, vsem, s_ref, p_ref, *, npages, nheads):
    b = pl.program_id(0)
    ring = kbuf.shape[0]
    page = kbuf.shape[2]
    ppb = MOBA_BLOCK // page
    qrows, head_mask = _query_rows(qrep_ref, nheads)

    def k_copy(seq, p, slot):
        return pltpu.make_async_copy(k_hbm.at[pt_ref[seq * npages + p]], kbuf.at[slot], ksem.at[slot])

    def v_copy(p, slot):
        return pltpu.make_async_copy(v_hbm.at[pt_ref[b * npages + p]], vbuf.at[slot], vsem.at[slot])

    @pl.when(b == 0)
    def _():
        for p in range(ring):
            k_copy(0, p, p).start()

    grp = min(PAGE_GROUP, ring)

    def k_group(gi, carry):
        base = gi * grp
        slot0 = (gi % (ring // grp)) * grp
        for u in range(grp):
            k_copy(b, base + u, slot0 + u).wait()
        for u in range(grp):
            s_ref[gi * (grp // ppb) + u // ppb, :, (u % ppb) * page:(u % ppb + 1) * page] = jnp.dot(
                qrows, kbuf[slot0 + u].astype(BF16), preferred_element_type=F32)

        @pl.when(base + ring < npages)
        def _():
            for u in range(grp):
                k_copy(b, base + ring + u, slot0 + u).start()
        return carry

    lax.fori_loop(0, npages // grp, k_group, 0)
    for p in range(ring):
        v_copy(p, p).start()

    @pl.when(b + 1 < pl.num_programs(0))
    def _():
        for p in range(ring):
            k_copy(b + 1, p, p).start()

    pn, l = _moba_sample_weights(qrows, knew_ref, s_ref, p_ref, nheads)
    acc0 = jnp.dot(pn.astype(BF16), vnew_ref[0], preferred_element_type=F32)

    def v_group(gi, acc):
        base = gi * grp
        slot0 = (gi % (ring // grp)) * grp
        for u in range(grp):
            v_copy(base + u, slot0 + u).wait()
        for u in range(grp):
            pw = p_ref[gi * (grp // ppb) + u // ppb, :, (u % ppb) * page:(u % ppb + 1) * page]
            acc = acc + lax.dot_general(pw, vbuf[slot0 + u].astype(BF16), _NT,
                                        preferred_element_type=F32)

        @pl.when(base + ring < npages)
        def _():
            for u in range(grp):
                v_copy(base + ring + u, slot0 + u).start()
        return acc

    acc = lax.fori_loop(0, npages // grp, v_group, acc0)
    o_ref[0] = _pick_heads(acc / l, head_mask, nheads)


def _paged_operands(q_bf, knew_bf, vnew_bf, cache_k, cache_v, page_table):
    nb, nq, width = q_bf.shape
    nheads = width // HEAD_DIM
    nphys, page = cache_k.shape[0], cache_k.shape[1]
    ck = cache_k.transpose(0, 2, 3, 1).reshape(nphys, width, page)
    cv = cache_v.transpose(0, 2, 3, 1).reshape(nphys, width, page)
    qrep = jnp.repeat(q_bf, nheads, axis=1)
    pad = ((0, 0), (0, LANES - nq), (0, 0))
    knew = jnp.pad(knew_bf, pad)
    vnew = jnp.pad(vnew_bf, pad)
    pt = page_table.reshape(-1).astype(jnp.int32)
    return qrep, knew, vnew, ck, cv, pt, nheads, page


def _moba_paged(q_bf, knew_bf, vnew_bf, cache_k, cache_v, page_table):
    nb, nq, width = q_bf.shape
    qrep, knew, vnew, ck, cv, pt, nheads, page = _paged_operands(
        q_bf, knew_bf, vnew_bf, cache_k, cache_v, page_table)
    nrow = nq * nheads
    npages = page_table.shape[1]
    ppb = MOBA_BLOCK // page
    ring = min(PAGE_RING, npages)
    grp = min(PAGE_GROUP, ring)
    assert grp % ppb == 0 and ring % grp == 0 and npages % ring == 0
    per_b = lambda b, pt: (b, 0, 0)
    grid_spec = pltpu.PrefetchScalarGridSpec(
        num_scalar_prefetch=1,
        grid=(nb,),
        in_specs=[
            pl.BlockSpec((1, nrow, width), per_b),
            pl.BlockSpec((1, LANES, width), per_b),
            pl.BlockSpec((1, LANES, width), per_b),
            pl.BlockSpec(memory_space=pl.ANY),
            pl.BlockSpec(memory_space=pl.ANY),
        ],
        out_specs=pl.BlockSpec((1, nq, width), per_b),
        scratch_shapes=[
            pltpu.VMEM((ring, width, page), F32),
            pltpu.VMEM((ring, width, page), F32),
            pltpu.SemaphoreType.DMA((ring,)),
            pltpu.SemaphoreType.DMA((ring,)),
            pltpu.VMEM((npages // ppb, nrow, MOBA_BLOCK), F32),
            pltpu.VMEM((npages // ppb, nrow, MOBA_BLOCK), BF16),
        ],
    )
    return pl.pallas_call(
        functools.partial(_moba_paged_kernel, npages=npages, nheads=nheads),
        grid_spec=grid_spec,
        out_shape=jax.ShapeDtypeStruct((nb, nq, width), F32),
        compiler_params=_cparams(("arbitrary",)),
    )(pt, qrep, knew, vnew, ck, cv)


def _moba_fused_kernel(pt_ref, q_ref, k_ref, vt_ref, qrep_ref, knew_ref, vnew_ref, kc_hbm, vc_hbm,
                       o_ref, os_ref, km_ref, pbuf, psem, s_ref, p_ref, acc_ref, l_ref,
                       *, nblk, npages, nheads):
    i = pl.program_id(2)
    step = pl.program_id(0) * nblk + i
    seq = step // 2
    page = pbuf.shape[2]
    ppb = MOBA_BLOCK // page
    grp = min(PAGE_GROUP, npages)

    half = npages // 2
    gph = half // grp
    even = step % 2 == 0
    odd = step % 2 == 1
    has_next = step + 1 < pl.num_programs(0) * nblk

    def page_copy(cache_hbm, sq, p):
        return pltpu.make_async_copy(cache_hbm.at[pt_ref[sq * npages + p]], pbuf.at[p], psem.at[p])

    def start_half(cache_hbm, sq, h):
        for p in range(h * half, (h + 1) * half):
            page_copy(cache_hbm, sq, p).start()

    def wait_half(cache_hbm, sq, h):
        for p in range(h * half, (h + 1) * half):
            page_copy(cache_hbm, sq, p).wait()

    def scores_half(h):
        def k_group(gi, carry):
            for u in range(grp):
                s_ref[gi * (grp // ppb) + u // ppb, :, (u % ppb) * page:(u % ppb + 1) * page] = (
                    jnp.dot(qrows, pbuf[gi * grp + u].astype(BF16), preferred_element_type=F32))
            return carry
        lax.fori_loop(h * gph, (h + 1) * gph, k_group, 0)

    def values_half(h, acc):
        def v_group(gi, acc):
            for u in range(grp):
                pw = p_ref[gi * (grp // ppb) + u // ppb, :, (u % ppb) * page:(u % ppb + 1) * page]
                acc = acc + lax.dot_general(pw, pbuf[gi * grp + u].astype(BF16), _NT,
                                            preferred_element_type=F32)
            return acc
        return lax.fori_loop(h * gph, (h + 1) * gph, v_group, acc)

    qrows, head_mask = _query_rows(qrep_ref, nheads)

    @pl.when(step == 0)
    def _():
        start_half(kc_hbm, 0, 0)
        start_half(kc_hbm, 0, 1)

    @pl.when(even)
    def _():
        wait_half(kc_hbm, seq, 0)
        scores_half(0)
        start_half(vc_hbm, seq, 0)

    @pl.when(odd)
    def _():
        pn, l = _moba_sample_weights(qrows, knew_ref, s_ref, p_ref, nheads)
        l_ref[...] = jnp.broadcast_to(l, l_ref.shape)
        acc0 = jnp.dot(pn.astype(BF16), vnew_ref[0], preferred_element_type=F32)
        wait_half(vc_hbm, seq, 0)
        acc_ref[...] = values_half(0, acc0)

        @pl.when(has_next)
        def _():
            start_half(kc_hbm, seq + 1, 0)

    _moba_prompt_block(i, q_ref, k_ref, vt_ref, o_ref, km_ref, nblk)

    @pl.when(even)
    def _():
        wait_half(kc_hbm, seq, 1)
        scores_half(1)
        start_half(vc_hbm, seq, 1)

    @pl.when(odd)
    def _():
        wait_half(vc_hbm, seq, 1)
        acc = values_half(1, acc_ref[...])
        os_ref[0] = _pick_heads(acc / l_ref[:, 0:1], head_mask, nheads)

        @pl.when(has_next)
        def _():
            start_half(kc_hbm, seq + 1, 1)


def _moba_fused(qk, vtb, bsz, seq, q_bf, knew_bf, vnew_bf, cache_k, cache_v, page_table):
    nb, nq, width = q_bf.shape
    qrep, knew, vnew, ck, cv, pt, nheads, page = _paged_operands(
        q_bf, knew_bf, vnew_bf, cache_k, cache_v, page_table)
    nrow = nq * nheads
    npages = page_table.shape[1]
    nblk = seq // MOBA_BLOCK
    nbp = -(-nblk // 8) * 8
    wid = MOBA_STEP_W
    ppb = MOBA_BLOCK // page
    grp = min(PAGE_GROUP, npages)
    assert bsz * nblk == 2 * nb and wid == BRANCH_W and grp % ppb == 0 and npages % (2 * grp) == 0
    per_seq = lambda b, p, i, pt: ((b * nblk + i) // 2, 0, 0)
    grid_spec = pltpu.PrefetchScalarGridSpec(
        num_scalar_prefetch=1,
        grid=(bsz, 1, nblk),
        in_specs=[
            pl.BlockSpec((MOBA_BLOCK, wid), lambda b, p, i, pt: (b * nblk + i, 0)),
            pl.BlockSpec((seq, wid), lambda b, p, i, pt: (b, 1)),
            pl.BlockSpec((None, nblk, wid, MOBA_BLOCK), lambda b, p, i, pt: (b, 0, 0, 0)),
            pl.BlockSpec((1, nrow, width), per_seq),
            pl.BlockSpec((1, LANES, width), per_seq),
            pl.BlockSpec((1, LANES, width), per_seq),
            pl.BlockSpec(memory_space=pl.ANY),
            pl.BlockSpec(memory_space=pl.ANY),
        ],
        out_specs=[
            pl.BlockSpec((MOBA_BLOCK, wid), lambda b, p, i, pt: (b * nblk + i, 0)),
            pl.BlockSpec((1, nq, width), per_seq),
        ],
        scratch_shapes=[
            pltpu.VMEM((nbp, wid), F32),
            pltpu.VMEM((npages, width, page), F32),
            pltpu.SemaphoreType.DMA((npages,)),
            pltpu.VMEM((npages // ppb, nrow, MOBA_BLOCK), F32),
            pltpu.VMEM((npages // ppb, nrow, MOBA_BLOCK), BF16),
            pltpu.VMEM((nrow, width), F32),
            pltpu.VMEM((nrow, LANES), F32),
        ],
    )
    return pl.pallas_call(
        functools.partial(_moba_fused_kernel, nblk=nblk, npages=npages, nheads=nheads),
        grid_spec=grid_spec,
        out_shape=[jax.ShapeDtypeStruct((bsz * seq, BRANCH_W), BF16),
                   jax.ShapeDtypeStruct((nb, nq, width), F32)],
        compiler_params=_cparams(("arbitrary", "arbitrary", "arbitrary")),
    )(pt, qk, qk, vtb, qrep, knew, vnew, ck, cv)


def _sb_paged_kernel(pt_ref, live_ref, qrep_ref, knew_ref, vnew_ref, cin_ref, ain_ref, *rest,
                     group, nheads, first):
    del pt_ref
    k_refs = rest[:group]
    v_refs = rest[group:2 * group]
    y_ref, acc_ref, carry_ref = rest[2 * group:2 * group + 3]
    b = pl.program_id(0)
    s = pl.program_id(1)
    nrow = qrep_ref.shape[1]
    page = k_refs[0].shape[1]
    ppt = SB_TILE // page
    qrows, head_mask = _query_rows(qrep_ref, nheads)
    tri = _tri(SB_TILE)

    @pl.when(s == 0)
    def _():
        if first:
            row = lax.broadcasted_iota(jnp.int32, (nrow, LANES), 0)
            lane = lax.broadcasted_iota(jnp.int32, (nrow, LANES), 1)
            strict = lane < row // nheads
            zn = lax.dot_general(qrows, knew_ref[0], _NT, preferred_element_type=F32)
            hi, lo = _split_bf16(jnp.where(strict, -_softplus2(zn), 0.0))
            tri_n = _tri(LANES)
            cum = (jnp.dot(hi, tri_n, preferred_element_type=F32)
                   + jnp.dot(lo, tri_n, preferred_element_type=F32))
            wn = jnp.where(strict, jnp.exp2(zn + cum), 0.0)
            acc_ref[0] = jnp.dot(wn.astype(BF16), vnew_ref[0], preferred_element_type=F32)
            carry_ref[0] = jnp.broadcast_to(cum[:, 0:1], carry_ref.shape[1:])
        else:
            acc_ref[0] = ain_ref[0]
            carry_ref[0] = cin_ref[0]

    @pl.when(live_ref[b] > 0)
    def _():
        carry = carry_ref[0][:, 0:1]
        acc = acc_ref[0]
        order = list(reversed(range(group // ppt)))
        tri_2 = jnp.concatenate([tri, tri], axis=0)
        zs = [jnp.dot(qrows,
                      jnp.concatenate([k_refs[t * ppt + u][...] for u in range(ppt)],
                                      axis=1).astype(BF16),
                      preferred_element_type=F32) for t in order]
        sums = []
        for z in zs:
            hi, lo = _split_bf16(-_softplus2(z))
            sums.append(jnp.dot(jnp.concatenate([hi, lo], axis=1), tri_2,
                                preferred_element_type=F32))
        ws = []
        for z, within in zip(zs, sums):
            cum = within + carry
            ws.append(jnp.exp2(z + cum).astype(BF16))
            carry = cum[:, 0:1]
        for t, w in zip(order, ws):
            vt = jnp.concatenate([v_refs[t * ppt + u][...] for u in range(ppt)], axis=1).astype(BF16)
            acc = acc + lax.dot_general(w, vt, _NT, preferred_element_type=F32)
        acc_ref[0] = acc
        carry_ref[0] = jnp.broadcast_to(carry, carry_ref.shape[1:])

    @pl.when(s == pl.num_programs(1) - 1)
    def _():
        y_ref[0] = _pick_heads(acc_ref[0], head_mask, nheads)


def _sb_paged_call(first, pt, live, qrep, knew, vnew, carry_in, acc_in, ck, cv,
                   nq, npages, group, chunk_lo, nsteps, nheads):
    nb, nrow, width = qrep.shape
    page = ck.shape[2]

    def page_map(g):
        def index(b, s, pt, live):
            chunk_page = (chunk_lo + nsteps - 1 - s) * group + g
            return (jnp.where(live[b] > 0, pt[b * npages + chunk_page], 0), 0, 0)
        return index

    per_b = lambda b, s, pt, live: (b, 0, 0)
    in_specs = [
        pl.BlockSpec((1, nrow, width), per_b),
        pl.BlockSpec((1, LANES, width), per_b),
        pl.BlockSpec((1, LANES, width), per_b),
        pl.BlockSpec((1, nrow, LANES), per_b),
        pl.BlockSpec((1, nrow, width), per_b),
    ]
    in_specs += [pl.BlockSpec((None, width, page), page_map(g), pipeline_mode=pl.Buffered(PAGE_BUFFERS))
                 for g in range(group)] * 2
    grid_spec = pltpu.PrefetchScalarGridSpec(
        num_scalar_prefetch=2,
        grid=(nb, nsteps),
        in_specs=in_specs,
        out_specs=[pl.BlockSpec((1, nq, width), per_b),
                   pl.BlockSpec((1, nrow, width), per_b),
                   pl.BlockSpec((1, nrow, LANES), per_b)],
    )
    return pl.pallas_call(
        functools.partial(_sb_paged_kernel, group=group, nheads=nheads, first=first),
        grid_spec=grid_spec,
        out_shape=[jax.ShapeDtypeStruct((nb, nq, width), F32),
                   jax.ShapeDtypeStruct((nb, nrow, width), F32),
                   jax.ShapeDtypeStruct((nb, nrow, LANES), F32)],
        compiler_params=_cparams(("parallel", "arbitrary")),
    )(pt, live, qrep, knew, vnew, carry_in, acc_in, *([ck] * group), *([cv] * group))


def _sb_paged(q_bf, knew_bf, vnew_bf, cache_k, cache_v, page_table, group):
    nb, nq, width = q_bf.shape
    qrep, knew, vnew, ck, cv, pt, nheads, page = _paged_operands(
        q_bf, knew_bf, vnew_bf, cache_k, cache_v, page_table)
    nrow = nq * nheads
    npages = page_table.shape[1]
    nchunk = npages // group
    zc = jnp.zeros((nb, nrow, LANES), F32)
    za = jnp.zeros((nb, nrow, width), F32)
    all_live = jnp.ones((nb,), jnp.int32)
    y, acc, carry = _sb_paged_call(True, pt, all_live, qrep, knew, vnew, zc, za, ck, cv,
                                   nq, npages, group, nchunk - 1, 1, nheads)
    if nchunk == 1:
        return y
    live = (jnp.max(carry, axis=(1, 2)) > SB_DEAD).astype(jnp.int32)

    def older(_):
        return _sb_paged_call(False, pt, live, qrep, knew, vnew, carry, acc, ck, cv,
                              nq, npages, group, 0, nchunk - 1, nheads)[0]

    return lax.cond(jnp.any(live > 0), older, lambda _: y, None)


def _finish(y_first, y_second, w_ref, g_ref, x_ref, o_ref):
    half = y_first.shape[1]
    out = (jnp.dot(y_first.astype(BF16), w_ref[0:half, :], preferred_element_type=F32)
           + jnp.dot(y_second.astype(BF16), w_ref[half:2 * half, :], preferred_element_type=F32))
    h = x_ref[...] + _rms(out, g_ref[...])
    o_ref[...] = h
    return h


def _even_out_prompt_kernel(r_ref, halo_ref, yb_ref, x_ref, w_ref, cw_ref, g_ref, *rest, next_proj):
    if next_proj is None:
        o_ref, tail_ref, ue_ref = rest
    else:
        g2_ref, w2_ref, o_ref, tail_ref, *proj_out, ue_ref, xn_ref = rest
    tm = r_ref.shape[0]
    hrows = halo_ref.shape[0]
    w = BRANCH_W
    group = lambda ref, c: ref[:, c * w:(c + 1) * w].astype(F32)
    u = group(r_ref, 1) * group(r_ref, 2)
    first = pl.program_id(1) == 0
    halo = group(halo_ref, 1) * group(halo_ref, 2)
    ue_ref[0:8, :] = jnp.where(first, 0.0, halo[hrows - 8:hrows, :])
    ue_ref[8:8 + tm, :] = u
    conv = (ue_ref[6:6 + tm, :] * cw_ref[0:1, :] + ue_ref[7:7 + tm, :] * cw_ref[1:2, :]
            + u * cw_ref[2:3, :])
    y_a = group(r_ref, 0) * conv * _silu(group(r_ref, 3))
    y_b = yb_ref[...].astype(F32) * _silu(group(r_ref, 4))
    tail_ref[...] = u[tm - 8:tm, :]
    h = _finish(y_a, y_b, w_ref, g_ref, x_ref, o_ref)
    if next_proj is not None:
        qkv_first, blk, keep = next_proj
        xn_ref[...] = _rms(h, g2_ref[...]).astype(BF16)
        _proj_tile(xn_ref, w2_ref, *proj_out, qkv_first, blk, keep)


def _even_out_prompt(rest, y_b, x, w_out_bf, conv_w, g_post, bsz, seq, tm, nxt=None):
    m, d = x.shape
    tps = seq // tm
    w = BRANCH_W
    row = lambda b, t: b * tps + t
    hrows = 16
    const = lambda shape: pl.BlockSpec(shape, lambda b, t: (0,) * len(shape),
                                       pipeline_mode=pl.Buffered(1))
    in_specs = [pl.BlockSpec((tm, rest.shape[1]), lambda b, t: (row(b, t), 0)),
                pl.BlockSpec((hrows, rest.shape[1]),
                             lambda b, t: (jnp.maximum(row(b, t) * (tm // hrows) - 1, 0), 0)),
                pl.BlockSpec((tm, w), lambda b, t: (row(b, t), 0)),
                pl.BlockSpec((tm, d), lambda b, t: (row(b, t), 0)),
                const(w_out_bf.shape), const(conv_w.shape), const((1, d))]
    args = [rest, rest, y_b, x, w_out_bf, conv_w, g_post.reshape(1, d)]
    out_specs = [pl.BlockSpec((tm, d), lambda b, t: (row(b, t), 0)),
                 pl.BlockSpec((8, w), lambda b, t: (row(b, t), 0))]
    out_shape = [jax.ShapeDtypeStruct((m, d), F32),
                 jax.ShapeDtypeStruct((bsz * tps * 8, w), F32)]
    scratch = [pltpu.VMEM((tm + 8, w), F32)]
    next_proj = None
    if nxt is not None:
        g2, w2_bf, qkv_first, keep_rows = nxt
        specs2, shapes2, blk, keep = _proj_outputs(w2_bf.shape[1], bsz, seq, tm, keep_rows)
        in_specs += [const((1, d)), const(w2_bf.shape)]
        args += [g2.reshape(1, d), w2_bf]
        out_specs += specs2
        out_shape += shapes2
        scratch.append(pltpu.VMEM((tm, d), BF16))
        next_proj = (qkv_first, blk, keep)
    outs = pl.pallas_call(
        functools.partial(_even_out_prompt_kernel, next_proj=next_proj),
        grid=(bsz, tps),
        in_specs=in_specs,
        out_specs=out_specs,
        out_shape=out_shape,
        scratch_shapes=scratch,
        compiler_params=_cparams(("parallel", "arbitrary")),
    )(*args)
    return outs[0], outs[1], (tuple(outs[2:]) if nxt is not None else None)


def _even_out_sample_kernel(rest_ref, st_ref, yb_ref, x_ref, w_ref, cw_ref, g_ref,
                            o_ref, tail_ref, *, nb):
    w = BRANCH_W
    rows = rest_ref.shape[0]
    rest = rest_ref[...].astype(F32)
    u = rest[:, w:2 * w] * rest[:, 2 * w:3 * w]
    ue = jnp.concatenate([st_ref[...], u], axis=0)
    conv = (ue[0:rows] * cw_ref[0:1, :] + ue[nb:nb + rows] * cw_ref[1:2, :]
            + ue[2 * nb:2 * nb + rows] * cw_ref[2:3, :])
    y_a = rest[:, 0:w] * conv * _silu(rest[:, 3 * w:4 * w])
    y_b = yb_ref[...] * _silu(rest[:, 4 * w:5 * w])
    tail_ref[...] = ue[rows:rows + 2 * nb]
    _finish(y_a, y_b, w_ref, g_ref, x_ref, o_ref)


def _even_out_sample(rest, state_tm, y_b, x, w_out_bf, conv_w, g_post, nb):
    m, d = x.shape
    return pl.pallas_call(
        functools.partial(_even_out_sample_kernel, nb=nb),
        out_shape=[jax.ShapeDtypeStruct((m, d), F32),
                   jax.ShapeDtypeStruct(((CONV_W - 1) * nb, BRANCH_W), F32)],
        compiler_params=pltpu.CompilerParams(vmem_limit_bytes=VMEM_LIMIT),
    )(rest, state_tm, y_b, x, w_out_bf, conv_w, g_post.reshape(1, d))


def _gmlp_mix(v, gw_ref, low):
    r_i = lax.broadcasted_iota(jnp.int32, (GMLP_CHUNK, GMLP_CHUNK), 0)
    c_i = lax.broadcasted_iota(jnp.int32, (GMLP_CHUNK, GMLP_CHUNK), 1)
    tril = c_i <= r_i
    parts = []
    for p in range(v.shape[1] // LANES):
        vp = v[:, p * LANES:(p + 1) * LANES]
        v_lo = jnp.where(low, vp, jnp.zeros_like(vp)).astype(BF16)
        v_hi = jnp.where(low, jnp.zeros_like(vp), vp).astype(BF16)
        w_lo = jnp.where(tril, gw_ref[2 * p], 0.0).astype(BF16)
        w_hi = jnp.where(tril, gw_ref[2 * p + 1], 0.0).astype(BF16)
        parts.append(jnp.dot(w_lo, v_lo, preferred_element_type=F32)
                     + jnp.dot(w_hi, v_hi, preferred_element_type=F32))
    return jnp.concatenate(parts, axis=1)


def _odd_out_prompt_kernel(r_ref, yd_ref, x_ref, w_ref, gw_ref, gb_ref, g_ref, o_ref, yc_ref):
    tm = r_ref.shape[0]
    w = BRANCH_W
    low = lax.broadcasted_iota(jnp.int32, (1, LANES), 1) < HEAD_DIM
    for c in range(tm // GMLP_CHUNK):
        rows = slice(c * GMLP_CHUNK, (c + 1) * GMLP_CHUNK)
        mixed = _gmlp_mix(r_ref[rows, w:2 * w], gw_ref, low) + gb_ref[...]
        yc_ref[rows, :] = (r_ref[rows, 0:w].astype(F32) * mixed
                           * _silu(r_ref[rows, 2 * w:3 * w].astype(F32)))
    y_d = yd_ref[...].astype(F32) * _silu(r_ref[:, 3 * w:4 * w].astype(F32))
    _finish(yc_ref[...], y_d, w_ref, g_ref, x_ref, o_ref)


def _odd_out_prompt(rest, y_d, x, w_out_bf, gmlp_w, gb_full, g_post, tm):
    m, d = x.shape
    return pl.pallas_call(
        _odd_out_prompt_kernel,
        grid=(m // tm,),
        in_specs=[pl.BlockSpec((tm, rest.shape[1]), lambda i: (i, 0)),
                  pl.BlockSpec((tm, BRANCH_W), lambda i: (i, 0)),
                  pl.BlockSpec((tm, d), lambda i: (i, 0)),
                  pl.BlockSpec(w_out_bf.shape, lambda i: (0, 0)),
                  pl.BlockSpec(gmlp_w.shape, lambda i: (0, 0, 0)),
                  pl.BlockSpec(gb_full.shape, lambda i: (0, 0)),
                  pl.BlockSpec((1, d), lambda i: (0, 0))],
        out_specs=pl.BlockSpec((tm, d), lambda i: (i, 0)),
        out_shape=jax.ShapeDtypeStruct((m, d), F32),
        scratch_shapes=[pltpu.VMEM((tm, BRANCH_W), F32)],
        compiler_params=_cparams(("parallel",)),
    )(rest, y_d, x, w_out_bf, gmlp_w, gb_full, g_post.reshape(1, d))


def _odd_out_sample_kernel(rest_ref, yd_ref, x_ref, w_ref, w4_ref, b4_ref, g_ref, o_ref, *, nb, nq):
    w = BRANCH_W
    rest = rest_ref[...].astype(F32)
    parts = []
    for t in range(nq):
        mixed = jnp.broadcast_to(b4_ref[t:t + 1, :], (nb, w))
        for s in range(t + 1):
            mixed = mixed + w4_ref[t * nq + s:t * nq + s + 1, :] * rest[s * nb:(s + 1) * nb, w:2 * w]
        parts.append(mixed)
    mixed = jnp.concatenate(parts, axis=0)
    y_c = rest[:, 0:w] * mixed * _silu(rest[:, 2 * w:3 * w])
    y_d = yd_ref[...] * _silu(rest[:, 3 * w:4 * w])
    _finish(y_c, y_d, w_ref, g_ref, x_ref, o_ref)


def _odd_out_sample(rest, y_d, x, w_out_bf, w4, b4, g_post, nb, nq):
    m, d = x.shape
    return pl.pallas_call(
        functools.partial(_odd_out_sample_kernel, nb=nb, nq=nq),
        out_shape=jax.ShapeDtypeStruct((m, d), F32),
        compiler_params=pltpu.CompilerParams(vmem_limit_bytes=VMEM_LIMIT),
    )(rest, y_d, x, w_out_bf, w4, b4, g_post.reshape(1, d))


def _to_token_major(a):
    nb, nq, w = a.shape
    return a.transpose(1, 0, 2).reshape(nq * nb, w)


def _to_batch_major(a, nb):
    w = a.shape[1]
    return a.reshape(-1, nb, w).transpose(1, 0, 2)


def _heads_prompt(t, heads):
    bsz, w, seq = t.shape
    return t.reshape(bsz, heads, w // heads, seq).transpose(0, 3, 1, 2)


def _open_rows(seq):
    open_start = ((seq - 1) // GMLP_CHUNK) * GMLP_CHUNK
    return open_start, seq - open_start


def _project_prompt(hp, dims, g_pre, w_in_bf, qkv_first, tm):
    bsz, seq, _, _ = dims
    return _proj(hp, g_pre, w_in_bf, qkv_first, bsz, seq, tm, _open_rows(seq))


def _project_sample(hs, dims, g_pre, w_in_bf, qkv_first, tm):
    _, _, nb, nq = dims
    ns = nq * nb
    rows = max(ns, tm)
    hs_pad = jnp.pad(hs, ((0, rows - ns), (0, 0)))
    rest_s, qk_s, kt_s, vt_s, _, keep_s = _proj(hs_pad, g_pre, w_in_bf, qkv_first, 1, rows, rows,
                                                (0, ns))
    k_s = _to_batch_major(kt_s[0, :, :ns].T, nb)
    v_s = _to_batch_major(vt_s[0, :, :ns].T, nb)
    q_s = _to_batch_major(qk_s[:ns, :BRANCH_W], nb)
    return rest_s[:ns], q_s, k_s, v_s, keep_s[0]


def _even_layer(hp, hs, dims, state_conv, cache_k, cache_v, page_table,
                g_pre, g_post, w_in, conv_w, w_out, tm, proj_prompt, nxt):
    bsz, seq, nb, nq = dims
    w = BRANCH_W
    heads = w // HEAD_DIM
    w_in_bf = w_in.astype(BF16)
    w_out_bf = w_out.astype(BF16)
    if proj_prompt is None:
        proj_prompt = _project_prompt(hp, dims, g_pre, w_in_bf, 4, tm)
    rest_p, qk_p, kt_p, vt_p, vtb_p, _ = proj_prompt
    rest_s, q_s, k_s, v_s, _ = _project_sample(hs, dims, g_pre, w_in_bf, 4, tm)
    sample_qkv = (q_s, k_s.astype(BF16), v_s.astype(BF16))
    if bsz * (seq // MOBA_BLOCK) == 2 * nb:
        yb_p, yb_s = _moba_fused(qk_p, vtb_p, bsz, seq, *sample_qkv, cache_k, cache_v, page_table)
    else:
        yb_p = _moba_prompt(qk_p, vtb_p, bsz, seq)
        yb_s = _moba_paged(*sample_qkv, cache_k, cache_v, page_table)
    yb_s = _to_token_major(yb_s)
    if nxt is not None:
        nxt = (*nxt, _open_rows(seq))
    hp, tail_p, proj_next = _even_out_prompt(rest_p, yb_p, hp, w_out_bf, conv_w, g_post,
                                             bsz, seq, tm, nxt)
    state_tm = _to_token_major(state_conv)
    hs, tail_s = _even_out_sample(rest_s, state_tm, yb_s, hs, w_out_bf, conv_w, g_post, nb)
    conv_p = tail_p.reshape(bsz, seq // tm, 8, w)[:, -1, 8 - (CONV_W - 1):, :]
    conv_s = _to_batch_major(tail_s, nb)
    outs = (conv_p, conv_s, _heads_prompt(kt_p, heads), _heads_prompt(vt_p, heads),
            k_s.reshape(nb, nq, heads, HEAD_DIM), v_s.reshape(nb, nq, heads, HEAD_DIM))
    return hp, hs, proj_next, outs


def _odd_layer(hp, hs, dims, cache_k, cache_v, page_table,
               g_pre, g_post, w_in_bf, gmlp_w, gmlp_b, w_out, tm, group, proj_prompt):
    bsz, seq, nb, nq = dims
    w = BRANCH_W
    heads = w // HEAD_DIM
    w_out_bf = w_out.astype(BF16)
    ngroups = gmlp_w.shape[0]
    cpg = w // ngroups
    if proj_prompt is None:
        proj_prompt = _project_prompt(hp, dims, g_pre, w_in_bf, 3, tm)
    rest_p, qk_p, kt_p, vt_p, vtb_p, gv_p = proj_prompt
    rest_s, q_s, k_s, v_s, keep_s = _project_sample(hs, dims, g_pre, w_in_bf, 3, tm)
    yd_p = _sb_prompt(qk_p, vtb_p, bsz, seq)
    yd_s = _to_token_major(_sb_paged(q_s, k_s.astype(BF16), v_s.astype(BF16),
                                     cache_k, cache_v, page_table, group))
    gb_full = jnp.repeat(gmlp_b.T, cpg, axis=1)
    hp = _odd_out_prompt(rest_p, yd_p, hp, w_out_bf, gmlp_w, gb_full, g_post, tm)
    w4 = jnp.repeat(gmlp_w[:, :nq, :nq].transpose(1, 2, 0), cpg, axis=2).reshape(nq * nq, w)
    b4 = jnp.repeat(gmlp_b[:, :nq].T, cpg, axis=1)
    hs = _odd_out_sample(rest_s, yd_s, hs, w_out_bf, w4, b4, g_post, nb, nq)
    gv_s = _to_batch_major(keep_s, nb)
    outs = (_heads_prompt(kt_p, heads), _heads_prompt(vt_p, heads),
            k_s.reshape(nb, nq, heads, HEAD_DIM), v_s.reshape(nb, nq, heads, HEAD_DIM), gv_p, gv_s)
    return hp, hs, outs


def kernel(x_prompt, x_sample, state_conv, cache_k_moba, cache_v_moba, cache_k_sb, cache_v_sb,
           page_table, norm_pre_e, norm_post_e, w_in_e, conv_w, w_out_e,
           norm_pre_o, norm_post_o, w_in_o, gmlp_w, gmlp_b, w_out_o):
    bsz, seq, d = x_prompt.shape
    nb, nq, _ = x_sample.shape
    depth = norm_pre_e.shape[0] + norm_pre_o.shape[0]
    npages, page = page_table.shape[1], cache_k_moba.shape[2]
    assert seq % MOBA_BLOCK == 0 and (npages * page) % MOBA_BLOCK == 0 and nq <= GMLP_CHUNK
    assert (nq - 1) // MOBA_BLOCK == 0 and w_in_e.shape[2] == 8 * BRANCH_W and w_in_o.shape[2] == 7 * BRANCH_W
    dims = (bsz, seq, nb, nq)
    tm = min(512, seq)
    group_sb = min(SB_TILE // page, npages)
    hp = x_prompt.reshape(bsz * seq, d)
    hs = _to_token_major(x_sample)
    ev, od = [], []
    pending = None
    for i in range(depth):
        j = i // 2
        if i % 2 == 0:
            nxt = (norm_pre_o[j], w_in_o[j].astype(BF16), 3) if i + 1 < depth else None
            hp, hs, pending, outs = _even_layer(
                hp, hs, dims, state_conv[j], cache_k_moba[j], cache_v_moba[j], page_table,
                norm_pre_e[j], norm_post_e[j], w_in_e[j], conv_w[j], w_out_e[j], tm, pending, nxt)
            ev.append(outs)
        else:
            hp, hs, outs = _odd_layer(
                hp, hs, dims, cache_k_sb[j], cache_v_sb[j], page_table,
                norm_pre_o[j], norm_post_o[j], w_in_o[j].astype(BF16), gmlp_w[j], gmlp_b[j],
                w_out_o[j], tm, group_sb, pending)
            od.append(outs)
            pending = None
    y_prompt = hp.reshape(bsz, seq, d)
    y_sample = _to_batch_major(hs, nb)
    ev_out = [jnp.stack([r[k] for r in ev]) for k in range(6)]
    od_out = [jnp.stack([r[k] for r in od]) for k in range(6)]
    conv_p, conv_s, kmp, vmp, kms, vms = ev_out
    ksp, vsp, kss, vss, gvp, gvs = od_out
    return (y_prompt, y_sample, conv_p, conv_s, kmp, vmp, kms, vms, ksp, vsp, kss, vss, gvp, gvs)
```

```python
import functools

import jax
import jax.numpy as jnp
from jax import lax
from jax.experimental import pallas as pl
from jax.experimental.pallas import tpu as pltpu

F32 = jnp.float32
BF16 = jnp.bfloat16

HEAD_DIM = 64
LANES = 128
BRANCH_W = 512
MOBA_BLOCK = 256
MOBA_TOPK = 3
MOBA_HEAD_GROUP = 8
MOBA_UNROLL = 4
GMLP_CHUNK = 128
CONV_W = 3
RMS_EPS = 1e-6
NEG = -1e30
LOG2E = 1.4426950408889634
Q_SCALE = HEAD_DIM ** -0.5 * LOG2E
SB_TILE = 256
SB_DEAD = -160.0
VMEM_LIMIT = 56 * 1024 * 1024
PAGE_BUFFERS = 2
PAGE_RING = 32
PAGE_GROUP = 32

_NT = (((1,), (1,)), ((), ()))


def _cparams(sem):
    return pltpu.CompilerParams(dimension_semantics=sem, vmem_limit_bytes=VMEM_LIMIT)


def _rms(x, g):
    ms = jnp.mean(x * x, axis=-1, keepdims=True)
    return x * lax.rsqrt(ms + RMS_EPS) * g


def _silu(x):
    return x / (1.0 + jnp.exp(-x))


def _softplus2(z2):
    return jnp.maximum(z2, 0.0) + jnp.log2(1.0 + jnp.exp2(-jnp.abs(z2)))


def _split_bf16(x):
    hi = x.astype(BF16)
    lo = (x - hi.astype(F32)).astype(BF16)
    return hi, lo


def _tri(n):
    r = lax.broadcasted_iota(jnp.int32, (n, n), 0)
    c = lax.broadcasted_iota(jnp.int32, (n, n), 1)
    return jnp.where(r >= c, 1.0, 0.0).astype(BF16)


def _proj_kernel(x_ref, g_ref, w_ref, rest_ref, qk_ref, kt_ref, vt_ref, vtb_ref, keep_ref, xn_ref,
                 *, qkv_first, blk, keep):
    xn_ref[...] = _rms(x_ref[...], g_ref[...]).astype(BF16)
    _proj_tile(xn_ref, w_ref, rest_ref, qk_ref, kt_ref, vt_ref, vtb_ref, keep_ref, qkv_first, blk, keep)


def _proj_tile(xn_ref, w_ref, rest_ref, qk_ref, kt_ref, vt_ref, vtb_ref, keep_ref, qkv_first, blk, keep):
    tm = xn_ref.shape[0]
    w = BRANCH_W
    keep_tile, keep_row = keep
    for j in range(w_ref.shape[1] // w):
        r = jnp.dot(xn_ref[...], w_ref[:, j * w:(j + 1) * w], preferred_element_type=F32)
        if j == qkv_first:
            qk_ref[:, 0:w] = (r * Q_SCALE).astype(BF16)
        elif j == qkv_first + 1:
            qk_ref[:, w:2 * w] = r.astype(BF16)
            for c in range(tm // blk):
                kt_ref[:, c * blk:(c + 1) * blk] = r[c * blk:(c + 1) * blk, :].T
        elif j == qkv_first + 2:
            for c in range(tm // blk):
                t = r[c * blk:(c + 1) * blk, :].T
                vt_ref[:, c * blk:(c + 1) * blk] = t
                vtb_ref[c] = t.astype(BF16)
        else:
            jr = j if j < qkv_first else j - 3
            rest_ref[:, jr * w:(jr + 1) * w] = r.astype(BF16)
            if jr == 1:
                @pl.when(pl.program_id(1) == keep_tile)
                def _():
                    keep_ref[...] = r[keep_row:keep_row + keep_ref.shape[0], :]


def _proj_outputs(n, bsz, seq, tm, keep_rows):
    tps = seq // tm
    blk = min(MOBA_BLOCK, tm)
    w = BRANCH_W
    m = bsz * seq
    keep_first, keep_n = keep_rows
    keep = (keep_first // tm, keep_first % tm)
    assert keep[1] + keep_n <= tm and keep_n % 8 == 0
    specs = [
        pl.BlockSpec((tm, n - 3 * w), lambda b, t: (b * tps + t, 0)),
        pl.BlockSpec((tm, 2 * w), lambda b, t: (b * tps + t, 0)),
        pl.BlockSpec((None, w, tm), lambda b, t: (b, 0, t)),
        pl.BlockSpec((None, w, tm), lambda b, t: (b, 0, t)),
        pl.BlockSpec((None, tm // blk, w, blk), lambda b, t: (b, t, 0, 0)),
        pl.BlockSpec((None, keep_n, w), lambda b, t: (b, 0, 0)),
    ]
    shapes = [
        jax.ShapeDtypeStruct((m, n - 3 * w), BF16),
        jax.ShapeDtypeStruct((m, 2 * w), BF16),
        jax.ShapeDtypeStruct((bsz, w, seq), F32),
        jax.ShapeDtypeStruct((bsz, w, seq), F32),
        jax.ShapeDtypeStruct((bsz, seq // blk, w, blk), BF16),
        jax.ShapeDtypeStruct((bsz, keep_n, w), F32),
    ]
    return specs, shapes, blk, keep


def _proj(x, g, w_bf, qkv_first, bsz, seq, tm, keep_rows):
    m, d = x.shape
    n = w_bf.shape[1]
    tps = seq // tm
    out_specs, out_shape, blk, keep = _proj_outputs(n, bsz, seq, tm, keep_rows)
    return pl.pallas_call(
        functools.partial(_proj_kernel, qkv_first=qkv_first, blk=blk, keep=keep),
        grid=(bsz, tps),
        in_specs=[
            pl.BlockSpec((tm, d), lambda b, t: (b * tps + t, 0)),
            pl.BlockSpec((1, d), lambda b, t: (0, 0)),
            pl.BlockSpec((d, n), lambda b, t: (0, 0), pipeline_mode=pl.Buffered(1)),
        ],
        out_specs=out_specs,
        out_shape=out_shape,
        scratch_shapes=[pltpu.VMEM((tm, d), BF16)],
        compiler_params=_cparams(("parallel", "arbitrary")),
    )(x, g.reshape(1, d), w_bf)


def _moba_prompt_kernel(q_ref, k_ref, vt_ref, o_ref, km_ref, *, nblk):
    _moba_prompt_block(pl.program_id(2), q_ref, k_ref, vt_ref, o_ref, km_ref, nblk)


def _moba_prompt_block(i, q_ref, k_ref, vt_ref, o_ref, km_ref, nblk):
    lane = lax.broadcasted_iota(jnp.int32, (1, LANES), 1)
    low = lane < HEAD_DIM
    nbp = km_ref.shape[0]
    blk = MOBA_BLOCK
    nheads = q_ref.shape[1] // HEAD_DIM

    @pl.when(i == 0)
    def _():
        km_ref[...] = jnp.zeros_like(km_ref)
        for j in range(nblk):
            rows = k_ref[j * blk:(j + 1) * blk, :].astype(F32)
            km_ref[j:j + 1, :] = jnp.sum(rows, axis=0, keepdims=True) * (1.0 / blk)

    brow = lax.broadcasted_iota(jnp.int32, (nbp, blk), 0)
    key_i = lax.broadcasted_iota(jnp.int32, (blk, blk), 0)
    qry_i = lax.broadcasted_iota(jnp.int32, (blk, blk), 1)
    start = pl.multiple_of(i * blk, blk)

    ones_rows = jnp.ones((2 * 8, blk), BF16)

    def v_rows(j, hh):
        return jnp.concatenate([vt_ref[j, hh * HEAD_DIM:(hh + 1) * HEAD_DIM, :], ones_rows], axis=0)

    q_owns, gates, own_scores = [], [], []
    for hh in range(nheads):
        cols = slice((hh // 2) * LANES, (hh // 2 + 1) * LANES)
        q = q_ref[:, cols]
        km = km_ref[:, cols]
        mine = low if hh % 2 == 0 else jnp.logical_not(low)
        km_hi, km_lo = _split_bf16(jnp.where(mine, km, 0.0))
        gates.append(lax.dot_general(km_hi, q, _NT, preferred_element_type=F32)
                     + lax.dot_general(km_lo, q, _NT, preferred_element_type=F32))
        q_own = jnp.where(mine, q, jnp.zeros_like(q))
        q_owns.append(q_own)
        own_scores.append(lax.dot_general(k_ref[pl.ds(start, blk), cols], q_own, _NT,
                                          preferred_element_type=F32))

    q_pasts, init = [], []
    for hh in range(nheads):
        gate = jnp.where(brow < i, gates[hh], -jnp.inf)
        bias = jnp.full((nbp, blk), NEG, F32)
        for _ in range(MOBA_TOPK):
            mx = jnp.max(gate, axis=0, keepdims=True)
            first = jnp.min(jnp.where(gate == mx, brow, nbp), axis=0, keepdims=True)
            pick = (brow == first) & (mx > -jnp.inf)
            bias = jnp.where(pick, 0.0, bias)
            gate = jnp.where(pick, -jnp.inf, gate)
        bias_t = jnp.concatenate([bias, jnp.zeros((LANES - nbp, blk), F32)], axis=0).T
        q_pasts.append(jnp.concatenate([q_owns[hh], bias_t.astype(BF16)], axis=1))

        s = jnp.where(key_i <= qry_i, own_scores[hh], NEG)
        m0 = jnp.max(s, axis=0, keepdims=True)
        p = jnp.exp2(s - m0)
        acc0 = jnp.dot(v_rows(i, hh), p.astype(BF16), preferred_element_type=F32)
        init.append((m0, acc0))

    def stage(js, heads, carry):
        scores = []
        for j in js:
            st = pl.multiple_of(j * blk, blk)
            onehot = jnp.broadcast_to(jnp.where(lane == j, 1.0, 0.0).astype(BF16), (blk, LANES))
            for hh in heads:
                kj = k_ref[pl.ds(st, blk), (hh // 2) * LANES:(hh // 2 + 1) * LANES]
                scores.append(lax.dot_general(jnp.concatenate([kj, onehot], axis=1), q_pasts[hh],
                                              _NT, preferred_element_type=F32))
        carry = list(carry)
        for n, j in enumerate(js):
            for c, hh in enumerate(heads):
                m, acc = carry[c]
                sj = scores[n * len(heads) + c]
                m_new = jnp.maximum(m, jnp.max(sj, axis=0, keepdims=True))
                alpha = jnp.exp2(m - m_new)
                pj = jnp.exp2(sj - m_new)
                acc = alpha * acc + jnp.dot(v_rows(j, hh), pj.astype(BF16),
                                            preferred_element_type=F32)
                carry[c] = (m_new, acc)
        return tuple(carry)

    fin = []
    for g0 in range(0, nheads, MOBA_HEAD_GROUP):
        heads = list(range(g0, min(g0 + MOBA_HEAD_GROUP, nheads)))
        part = tuple(init[hh] for hh in heads)
        base = 0
        size = MOBA_UNROLL
        while size >= 1:
            trips = (i - base) // size if size == MOBA_UNROLL else ((i - base) // size) % 2
            part = lax.fori_loop(
                0, trips,
                lambda t, c, base=base, size=size, heads=heads: stage(
                    [base + t * size + u for u in range(size)], heads, c),
                part)
            base = base + trips * size
            size //= 2
        fin.extend(part)
    out_t = jnp.concatenate([acc[:HEAD_DIM] / acc[HEAD_DIM:HEAD_DIM + 1] for _, acc in fin],
                            axis=0)
    o_ref[...] = out_t.T.astype(o_ref.dtype)


def _attn_prompt_call(kernel_fn, qk, vtb, bsz, seq, scratch, wid):
    nblk = seq // MOBA_BLOCK
    ngrp = BRANCH_W // wid
    return pl.pallas_call(
        kernel_fn,
        grid=(bsz, ngrp, nblk),
        in_specs=[
            pl.BlockSpec((MOBA_BLOCK, wid), lambda b, p, i: (b * nblk + i, p)),
            pl.BlockSpec((seq, wid), lambda b, p, i: (b, ngrp + p)),
            pl.BlockSpec((None, nblk, wid, MOBA_BLOCK), lambda b, p, i: (b, 0, p, 0)),
        ],
        out_specs=pl.BlockSpec((MOBA_BLOCK, wid), lambda b, p, i: (b * nblk + i, p)),
        out_shape=jax.ShapeDtypeStruct((bsz * seq, BRANCH_W), BF16),
        scratch_shapes=scratch,
        compiler_params=_cparams(("parallel", "parallel", "arbitrary")),
    )(qk, qk, vtb)


MOBA_STEP_W = 512


def _moba_prompt(qk, vtb, bsz, seq):
    nblk = seq // MOBA_BLOCK
    nbp = -(-nblk // 8) * 8
    return _attn_prompt_call(functools.partial(_moba_prompt_kernel, nblk=nblk), qk, vtb, bsz, seq,
                             [pltpu.VMEM((nbp, MOBA_STEP_W), F32)], MOBA_STEP_W)


def _sb_prompt_kernel(q_ref, k_ref, vt_ref, o_ref, sums_ref):
    i = pl.program_id(2)
    nheads = q_ref.shape[1] // HEAD_DIM
    tile = SB_TILE
    lane = lax.broadcasted_iota(jnp.int32, (1, LANES), 1)
    low = lane < HEAD_DIM
    key_i = lax.broadcasted_iota(jnp.int32, (tile, tile), 0)
    qry_i = lax.broadcasted_iota(jnp.int32, (tile, tile), 1)
    tri_t = jnp.where(qry_i >= key_i, 1.0, 0.0).astype(BF16)
    tri_2 = jnp.concatenate([tri_t, tri_t], axis=1)
    strict = key_i < qry_i

    q_hs = []
    for hh in range(nheads):
        q = q_ref[:, (hh // 2) * LANES:(hh // 2 + 1) * LANES]
        mine = low if hh % 2 == 0 else jnp.logical_not(low)
        q_hs.append(jnp.where(mine, q, jnp.zeros_like(q)))

    def sweep(tiles, state):
        zs = []
        for j, _ in tiles:
            st = pl.multiple_of(j * tile, tile)
            zs.append([lax.dot_general(
                k_ref[pl.ds(st, tile), (hh // 2) * LANES:(hh // 2 + 1) * LANES],
                q_hs[hh], _NT, preferred_element_type=F32) for hh in range(nheads)])
        for n, (_, mask) in enumerate(tiles):
            for hh in range(nheads):
                sp = _softplus2(zs[n][hh])
                if mask is not None:
                    sp = jnp.where(mask, sp, 0.0)
                hi, lo = _split_bf16(sp)
                sums_ref[n, hh] = jnp.dot(tri_2, jnp.concatenate([hi, lo], axis=0),
                                          preferred_element_type=F32)
        state = list(state)
        for n, (j, mask) in enumerate(tiles):
            for hh in range(nheads):
                spent, acc = state[hh]
                cum = sums_ref[n, hh] + spent
                w = jnp.exp2(zs[n][hh] - cum)
                if mask is not None:
                    w = jnp.where(mask, w, 0.0)
                acc = acc + jnp.dot(vt_ref[j, hh * HEAD_DIM:(hh + 1) * HEAD_DIM, :],
                                    w.astype(BF16), preferred_element_type=F32)
                state[hh] = (cum[0:1, :], acc)
        return tuple(state)

    def live_of(state):
        spent = jnp.min(state[0][0])
        for hh in range(1, nheads):
            spent = jnp.minimum(spent, jnp.min(state[hh][0]))
        return -spent

    def cond(st):
        j, live, _ = st
        return (j >= 0) & (live > SB_DEAD)

    def body(st):
        j, _, state = st
        new = sweep([(j, None)], state)
        return j - 1, live_of(new), new

    zero = tuple((jnp.zeros((1, tile), F32), jnp.zeros((HEAD_DIM, tile), F32))
                 for _ in range(nheads))
    has_prev = jnp.broadcast_to(i > 0, (tile, tile))
    init = sweep([(i, strict), (jnp.maximum(i - 1, 0), has_prev)], zero)
    _, _, fin = lax.while_loop(cond, body, (i - 2, live_of(init), init))
    o_ref[...] = jnp.concatenate([acc for _, acc in fin], axis=0).T.astype(o_ref.dtype)


SB_STEP_W = 512


def _sb_prompt(qk, vtb, bsz, seq):
    scratch = [pltpu.VMEM((2, SB_STEP_W // HEAD_DIM, SB_TILE, SB_TILE), F32)]
    return _attn_prompt_call(_sb_prompt_kernel, qk, vtb, bsz, seq, scratch, SB_STEP_W)


def _query_rows(qrep_ref, nheads):
    nrow, width = qrep_ref.shape[1], qrep_ref.shape[2]
    row_w = lax.broadcasted_iota(jnp.int32, (nrow, width), 0)
    lane_w = lax.broadcasted_iota(jnp.int32, (nrow, width), 1)
    head_mask = (lane_w // HEAD_DIM) == (row_w % nheads)
    qrows = jnp.where(head_mask, qrep_ref[0], jnp.zeros_like(qrep_ref[0]))
    return qrows, head_mask


def _pick_heads(y, head_mask, nheads):
    nrow, width = y.shape
    y = jnp.where(head_mask, y, 0.0)
    return jnp.sum(y.reshape(nrow // nheads, nheads, width), axis=1)


def _moba_sample_weights(qrows, knew_ref, s_ref, p_ref, nheads):
    nblk, nrow = s_ref.shape[0], s_ref.shape[1]
    row = lax.broadcasted_iota(jnp.int32, (nrow, LANES), 0)
    lane = lax.broadcasted_iota(jnp.int32, (nrow, LANES), 1)
    qidx = row // nheads
    zn = lax.dot_general(qrows, knew_ref[0], _NT, preferred_element_type=F32)
    gates = jnp.zeros((nrow, LANES), F32)
    for bl in range(nblk):
        gates = jnp.where(lane == bl, jnp.sum(s_ref[bl], axis=1, keepdims=True), gates)
    gates = jnp.where(lane < nblk, gates, -jnp.inf)
    sel = jnp.zeros((nrow, LANES), F32)
    for _ in range(MOBA_TOPK):
        mx = jnp.max(gates, axis=1, keepdims=True)
        first = jnp.min(jnp.where(gates == mx, lane, LANES), axis=1, keepdims=True)
        pick = (lane == first) & (mx > -jnp.inf)
        sel = jnp.where(pick, 1.0, sel)
        gates = jnp.where(pick, -jnp.inf, gates)
    zn = jnp.where(lane <= qidx, zn, NEG)
    m = jnp.max(zn, axis=1, keepdims=True)
    cols = []
    for bl in range(nblk):
        col = jnp.max(jnp.where(lane == bl, sel, 0.0), axis=1, keepdims=True) > 0.5
        cols.append(col)
        m = jnp.maximum(m, jnp.max(jnp.where(col, s_ref[bl], NEG), axis=1, keepdims=True))
    pn = jnp.exp2(zn - m)
    l = jnp.sum(pn, axis=1, keepdims=True)
    for bl in range(nblk):
        p = jnp.exp2(jnp.where(cols[bl], s_ref[bl], NEG) - m)
        l = l + jnp.sum(p, axis=1, keepdims=True)
        p_ref[bl] = p.astype(BF16)
    return pn, l


def _moba_paged_kernel(pt_ref, qrep_ref, knew_ref, vnew_ref, k_hbm, v_hbm, o_ref,
                       kbuf, vbuf, ksem, vsem, s_ref, p_ref, *, npages, nheads):
    b = pl.program_id(0)
    ring = kbuf.shape[0]
    page = kbuf.shape[2]
    ppb = MOBA_BLOCK // page
    qrows, head_mask = _query_rows(qrep_ref, nheads)

    def k_copy(seq, p, slot):
        return pltpu.make_async_copy(k_hbm.at[pt_ref[seq * npages + p]], kbuf.at[slot], ksem.at[slot])

    def v_copy(p, slot):
        return pltpu.make_async_copy(v_hbm.at[pt_ref[b * npages + p]], vbuf.at[slot], vsem.at[slot])

    @pl.when(b == 0)
    def _():
        for p in range(ring):
            k_copy(0, p, p).start()

    grp = min(PAGE_GROUP, ring)

    def k_group(gi, carry):
        base = gi * grp
        slot0 = (gi % (ring // grp)) * grp
        for u in range(grp):
            k_copy(b, base + u, slot0 + u).wait()
        for u in range(grp):
            s_ref[gi * (grp // ppb) + u // ppb, :, (u % ppb) * page:(u % ppb + 1) * page] = jnp.dot(
                qrows, kbuf[slot0 + u].astype(BF16), preferred_element_type=F32)

        @pl.when(base + ring < npages)
        def _():
            for u in range(grp):
                k_copy(b, base + ring + u, slot0 + u).start()
        return carry

    lax.fori_loop(0, npages // grp, k_group, 0)
    for p in range(ring):
        v_copy(p, p).start()

    @pl.when(b + 1 < pl.num_programs(0))
    def _():
        for p in range(ring):
            k_copy(b + 1, p, p).start()

    pn, l = _moba_sample_weights(qrows, knew_ref, s_ref, p_ref, nheads)
    acc0 = jnp.dot(pn.astype(BF16), vnew_ref[0], preferred_element_type=F32)

    def v_group(gi, acc):
        base = gi * grp
        slot0 = (gi % (ring // grp)) * grp
        for u in range(grp):
            v_copy(base + u, slot0 + u).wait()
        for u in range(grp):
            pw = p_ref[gi * (grp // ppb) + u // ppb, :, (u % ppb) * page:(u % ppb + 1) * page]
            acc = acc + lax.dot_general(pw, vbuf[slot0 + u].astype(BF16), _NT,
                                        preferred_element_type=F32)

        @pl.when(base + ring < npages)
        def _():
            for u in range(grp):
                v_copy(base + ring + u, slot0 + u).start()
        return acc

    acc = lax.fori_loop(0, npages // grp, v_group, acc0)
    o_ref[0] = _pick_heads(acc / l, head_mask, nheads)


def _paged_operands(q_bf, knew_bf, vnew_bf, cache_k, cache_v, page_table):
    nb, nq, width = q_bf.shape
    nheads = width // HEAD_DIM
    nphys, page = cache_k.shape[0], cache_k.shape[1]
    ck = cache_k.transpose(0, 2, 3, 1).reshape(nphys, width, page)
    cv = cache_v.transpose(0, 2, 3, 1).reshape(nphys, width, page)
    qrep = jnp.repeat(q_bf, nheads, axis=1)
    pad = ((0, 0), (0, LANES - nq), (0, 0))
    knew = jnp.pad(knew_bf, pad)
    vnew = jnp.pad(vnew_bf, pad)
    pt = page_table.reshape(-1).astype(jnp.int32)
    return qrep, knew, vnew, ck, cv, pt, nheads, page


def _moba_paged(q_bf, knew_bf, vnew_bf, cache_k, cache_v, page_table):
    nb, nq, width = q_bf.shape
    qrep, knew, vnew, ck, cv, pt, nheads, page = _paged_operands(
        q_bf, knew_bf, vnew_bf, cache_k, cache_v, page_table)
    nrow = nq * nheads
    npages = page_table.shape[1]
    ppb = MOBA_BLOCK // page
    ring = min(PAGE_RING, npages)
    grp = min(PAGE_GROUP, ring)
    assert grp % ppb == 0 and ring % grp == 0 and npages % ring == 0
    per_b = lambda b, pt: (b, 0, 0)
    grid_spec = pltpu.PrefetchScalarGridSpec(
        num_scalar_prefetch=1,
        grid=(nb,),
        in_specs=[
            pl.BlockSpec((1, nrow, width), per_b),
            pl.BlockSpec((1, LANES, width), per_b),
            pl.BlockSpec((1, LANES, width), per_b),
            pl.BlockSpec(memory_space=pl.ANY),
            pl.BlockSpec(memory_space=pl.ANY),
        ],
        out_specs=pl.BlockSpec((1, nq, width), per_b),
        scratch_shapes=[
            pltpu.VMEM((ring, width, page), F32),
            pltpu.VMEM((ring, width, page), F32),
            pltpu.SemaphoreType.DMA((ring,)),
            pltpu.SemaphoreType.DMA((ring,)),
            pltpu.VMEM((npages // ppb, nrow, MOBA_BLOCK), F32),
            pltpu.VMEM((npages // ppb, nrow, MOBA_BLOCK), BF16),
        ],
    )
    return pl.pallas_call(
        functools.partial(_moba_paged_kernel, npages=npages, nheads=nheads),
        grid_spec=grid_spec,
        out_shape=jax.ShapeDtypeStruct((nb, nq, width), F32),
        compiler_params=_cparams(("arbitrary",)),
    )(pt, qrep, knew, vnew, ck, cv)


def _moba_fused_kernel(pt_ref, q_ref, k_ref, vt_ref, qrep_ref, knew_ref, vnew_ref, kc_hbm, vc_hbm,
                       o_ref, os_ref, km_ref, pbuf, psem, s_ref, p_ref, acc_ref, l_ref,
                       *, nblk, npages, nheads):
    i = pl.program_id(2)
    step = pl.program_id(0) * nblk + i
    seq = step // 2
    page = pbuf.shape[2]
    ppb = MOBA_BLOCK // page
    grp = min(PAGE_GROUP, npages)

    half = npages // 2
    gph = half // grp
    even = step % 2 == 0
    odd = step % 2 == 1
    has_next = step + 1 < pl.num_programs(0) * nblk

    def page_copy(cache_hbm, sq, p):
        return pltpu.make_async_copy(cache_hbm.at[pt_ref[sq * npages + p]], pbuf.at[p], psem.at[p])

    def start_half(cache_hbm, sq, h):
        for p in range(h * half, (h + 1) * half):
            page_copy(cache_hbm, sq, p).start(priority=p % 2)

    def wait_half(cache_hbm, sq, h):
        for p in range(h * half, (h + 1) * half):
            page_copy(cache_hbm, sq, p).wait()

    def scores_half(h):
        def k_group(gi, carry):
            for u in range(grp):
                s_ref[gi * (grp // ppb) + u // ppb, :, (u % ppb) * page:(u % ppb + 1) * page] = (
                    jnp.dot(qrows, pbuf[gi * grp + u].astype(BF16), preferred_element_type=F32))
            return carry
        lax.fori_loop(h * gph, (h + 1) * gph, k_group, 0)

    def values_half(h, acc):
        def v_group(gi, acc):
            for u in range(grp):
                pw = p_ref[gi * (grp // ppb) + u // ppb, :, (u % ppb) * page:(u % ppb + 1) * page]
                acc = acc + lax.dot_general(pw, pbuf[gi * grp + u].astype(BF16), _NT,
                                            preferred_element_type=F32)
            return acc
        return lax.fori_loop(h * gph, (h + 1) * gph, v_group, acc)

    qrows, head_mask = _query_rows(qrep_ref, nheads)

    @pl.when(step == 0)
    def _():
        start_half(kc_hbm, 0, 0)
        start_half(kc_hbm, 0, 1)

    @pl.when(even)
    def _():
        wait_half(kc_hbm, seq, 0)
        scores_half(0)
        start_half(vc_hbm, seq, 0)

    @pl.when(odd)
    def _():
        pn, l = _moba_sample_weights(qrows, knew_ref, s_ref, p_ref, nheads)
        l_ref[...] = jnp.broadcast_to(l, l_ref.shape)
        acc0 = jnp.dot(pn.astype(BF16), vnew_ref[0], preferred_element_type=F32)
        wait_half(vc_hbm, seq, 0)
        acc_ref[...] = values_half(0, acc0)

        @pl.when(has_next)
        def _():
            start_half(kc_hbm, seq + 1, 0)

    _moba_prompt_block(i, q_ref, k_ref, vt_ref, o_ref, km_ref, nblk)

    @pl.when(even)
    def _():
        wait_half(kc_hbm, seq, 1)
        scores_half(1)
        start_half(vc_hbm, seq, 1)

    @pl.when(odd)
    def _():
        wait_half(vc_hbm, seq, 1)
        acc = values_half(1, acc_ref[...])
        os_ref[0] = _pick_heads(acc / l_ref[:, 0:1], head_mask, nheads)

        @pl.when(has_next)
        def _():
            start_half(kc_hbm, seq + 1, 1)


def _moba_fused(qk, vtb, bsz, seq, q_bf, knew_bf, vnew_bf, cache_k, cache_v, page_table):
    nb, nq, width = q_bf.shape
    qrep, knew, vnew, ck, cv, pt, nheads, page = _paged_operands(
        q_bf, knew_bf, vnew_bf, cache_k, cache_v, page_table)
    nrow = nq * nheads
    npages = page_table.shape[1]
    nblk = seq // MOBA_BLOCK
    nbp = -(-nblk // 8) * 8
    wid = MOBA_STEP_W
    ppb = MOBA_BLOCK // page
    grp = min(PAGE_GROUP, npages)
    assert bsz * nblk == 2 * nb and wid == BRANCH_W and grp % ppb == 0 and npages % (2 * grp) == 0
    per_seq = lambda b, p, i, pt: ((b * nblk + i) // 2, 0, 0)
    grid_spec = pltpu.PrefetchScalarGridSpec(
        num_scalar_prefetch=1,
        grid=(bsz, 1, nblk),
        in_specs=[
            pl.BlockSpec((MOBA_BLOCK, wid), lambda b, p, i, pt: (b * nblk + i, 0)),
            pl.BlockSpec((seq, wid), lambda b, p, i, pt: (b, 1)),
            pl.BlockSpec((None, nblk, wid, MOBA_BLOCK), lambda b, p, i, pt: (b, 0, 0, 0)),
            pl.BlockSpec((1, nrow, width), per_seq),
            pl.BlockSpec((1, LANES, width), per_seq),
            pl.BlockSpec((1, LANES, width), per_seq),
            pl.BlockSpec(memory_space=pl.ANY),
            pl.BlockSpec(memory_space=pl.ANY),
        ],
        out_specs=[
            pl.BlockSpec((MOBA_BLOCK, wid), lambda b, p, i, pt: (b * nblk + i, 0)),
            pl.BlockSpec((1, nq, width), per_seq),
        ],
        scratch_shapes=[
            pltpu.VMEM((nbp, wid), F32),
            pltpu.VMEM((npages, width, page), F32),
            pltpu.SemaphoreType.DMA((npages,)),
            pltpu.VMEM((npages // ppb, nrow, MOBA_BLOCK), F32),
            pltpu.VMEM((npages // ppb, nrow, MOBA_BLOCK), BF16),
            pltpu.VMEM((nrow, width), F32),
            pltpu.VMEM((nrow, LANES), F32),
        ],
    )
    return pl.pallas_call(
        functools.partial(_moba_fused_kernel, nblk=nblk, npages=npages, nheads=nheads),
        grid_spec=grid_spec,
        out_shape=[jax.ShapeDtypeStruct((bsz * seq, BRANCH_W), BF16),
                   jax.ShapeDtypeStruct((nb, nq, width), F32)],
        compiler_params=_cparams(("arbitrary", "arbitrary", "arbitrary")),
    )(pt, qk, qk, vtb, qrep, knew, vnew, ck, cv)


def _sb_paged_kernel(pt_ref, live_ref, qrep_ref, knew_ref, vnew_ref, cin_ref, ain_ref, *rest,
                     group, nheads, first):
    del pt_ref
    k_refs = rest[:group]
    v_refs = rest[group:2 * group]
    y_ref, acc_ref, carry_ref = rest[2 * group:2 * group + 3]
    b = pl.program_id(0)
    s = pl.program_id(1)
    nrow = qrep_ref.shape[1]
    page = k_refs[0].shape[1]
    ppt = SB_TILE // page
    qrows, head_mask = _query_rows(qrep_ref, nheads)
    tri = _tri(SB_TILE)

    @pl.when(s == 0)
    def _():
        if first:
            row = lax.broadcasted_iota(jnp.int32, (nrow, LANES), 0)
            lane = lax.broadcasted_iota(jnp.int32, (nrow, LANES), 1)
            strict = lane < row // nheads
            zn = lax.dot_general(qrows, knew_ref[0], _NT, preferred_element_type=F32)
            hi, lo = _split_bf16(jnp.where(strict, -_softplus2(zn), 0.0))
            tri_n = _tri(LANES)
            cum = (jnp.dot(hi, tri_n, preferred_element_type=F32)
                   + jnp.dot(lo, tri_n, preferred_element_type=F32))
            wn = jnp.where(strict, jnp.exp2(zn + cum), 0.0)
            acc_ref[0] = jnp.dot(wn.astype(BF16), vnew_ref[0], preferred_element_type=F32)
            carry_ref[0] = jnp.broadcast_to(cum[:, 0:1], carry_ref.shape[1:])
        else:
            acc_ref[0] = ain_ref[0]
            carry_ref[0] = cin_ref[0]

    @pl.when(live_ref[b] > 0)
    def _():
        carry = carry_ref[0][:, 0:1]
        acc = acc_ref[0]
        order = list(reversed(range(group // ppt)))
        tri_2 = jnp.concatenate([tri, tri], axis=0)
        zs = [jnp.dot(qrows,
                      jnp.concatenate([k_refs[t * ppt + u][...] for u in range(ppt)],
                                      axis=1).astype(BF16),
                      preferred_element_type=F32) for t in order]
        sums = []
        for z in zs:
            hi, lo = _split_bf16(-_softplus2(z))
            sums.append(jnp.dot(jnp.concatenate([hi, lo], axis=1), tri_2,
                                preferred_element_type=F32))
        ws = []
        for z, within in zip(zs, sums):
            cum = within + carry
            ws.append(jnp.exp2(z + cum).astype(BF16))
            carry = cum[:, 0:1]
        for t, w in zip(order, ws):
            vt = jnp.concatenate([v_refs[t * ppt + u][...] for u in range(ppt)], axis=1).astype(BF16)
            acc = acc + lax.dot_general(w, vt, _NT, preferred_element_type=F32)
        acc_ref[0] = acc
        carry_ref[0] = jnp.broadcast_to(carry, carry_ref.shape[1:])

    @pl.when(s == pl.num_programs(1) - 1)
    def _():
        y_ref[0] = _pick_heads(acc_ref[0], head_mask, nheads)


def _sb_paged_call(first, pt, live, qrep, knew, vnew, carry_in, acc_in, ck, cv,
                   nq, npages, group, chunk_lo, nsteps, nheads):
    nb, nrow, width = qrep.shape
    page = ck.shape[2]

    def page_map(g):
        def index(b, s, pt, live):
            chunk_page = (chunk_lo + nsteps - 1 - s) * group + g
            return (jnp.where(live[b] > 0, pt[b * npages + chunk_page], 0), 0, 0)
        return index

    per_b = lambda b, s, pt, live: (b, 0, 0)
    in_specs = [
        pl.BlockSpec((1, nrow, width), per_b),
        pl.BlockSpec((1, LANES, width), per_b),
        pl.BlockSpec((1, LANES, width), per_b),
        pl.BlockSpec((1, nrow, LANES), per_b),
        pl.BlockSpec((1, nrow, width), per_b),
    ]
    in_specs += [pl.BlockSpec((None, width, page), page_map(g), pipeline_mode=pl.Buffered(PAGE_BUFFERS))
                 for g in range(group)] * 2
    grid_spec = pltpu.PrefetchScalarGridSpec(
        num_scalar_prefetch=2,
        grid=(nb, nsteps),
        in_specs=in_specs,
        out_specs=[pl.BlockSpec((1, nq, width), per_b),
                   pl.BlockSpec((1, nrow, width), per_b),
                   pl.BlockSpec((1, nrow, LANES), per_b)],
    )
    return pl.pallas_call(
        functools.partial(_sb_paged_kernel, group=group, nheads=nheads, first=first),
        grid_spec=grid_spec,
        out_shape=[jax.ShapeDtypeStruct((nb, nq, width), F32),
                   jax.ShapeDtypeStruct((nb, nrow, width), F32),
                   jax.ShapeDtypeStruct((nb, nrow, LANES), F32)],
        compiler_params=_cparams(("parallel", "arbitrary")),
    )(pt, live, qrep, knew, vnew, carry_in, acc_in, *([ck] * group), *([cv] * group))


def _sb_paged(q_bf, knew_bf, vnew_bf, cache_k, cache_v, page_table, group):
    nb, nq, width = q_bf.shape
    qrep, knew, vnew, ck, cv, pt, nheads, page = _paged_operands(
        q_bf, knew_bf, vnew_bf, cache_k, cache_v, page_table)
    nrow = nq * nheads
    npages = page_table.shape[1]
    nchunk = npages // group
    zc = jnp.zeros((nb, nrow, LANES), F32)
    za = jnp.zeros((nb, nrow, width), F32)
    all_live = jnp.ones((nb,), jnp.int32)
    y, acc, carry = _sb_paged_call(True, pt, all_live, qrep, knew, vnew, zc, za, ck, cv,
                                   nq, npages, group, nchunk - 1, 1, nheads)
    if nchunk == 1:
        return y
    live = (jnp.max(carry, axis=(1, 2)) > SB_DEAD).astype(jnp.int32)

    def older(_):
        return _sb_paged_call(False, pt, live, qrep, knew, vnew, carry, acc, ck, cv,
                              nq, npages, group, 0, nchunk - 1, nheads)[0]

    return lax.cond(jnp.any(live > 0), older, lambda _: y, None)


def _finish(y_first, y_second, w_ref, g_ref, x_ref, o_ref):
    half = y_first.shape[1]
    out = (jnp.dot(y_first.astype(BF16), w_ref[0:half, :], preferred_element_type=F32)
           + jnp.dot(y_second.astype(BF16), w_ref[half:2 * half, :], preferred_element_type=F32))
    h = x_ref[...] + _rms(out, g_ref[...])
    o_ref[...] = h
    return h


def _even_out_prompt_kernel(r_ref, halo_ref, yb_ref, x_ref, w_ref, cw_ref, g_ref, *rest, next_proj):
    if next_proj is None:
        o_ref, tail_ref, ue_ref = rest
    else:
        g2_ref, w2_ref, o_ref, tail_ref, *proj_out, ue_ref, xn_ref = rest
    tm = r_ref.shape[0]
    hrows = halo_ref.shape[0]
    w = BRANCH_W
    group = lambda ref, c: ref[:, c * w:(c + 1) * w].astype(F32)
    u = group(r_ref, 1) * group(r_ref, 2)
    first = pl.program_id(1) == 0
    halo = group(halo_ref, 1) * group(halo_ref, 2)
    ue_ref[0:8, :] = jnp.where(first, 0.0, halo[hrows - 8:hrows, :])
    ue_ref[8:8 + tm, :] = u
    conv = (ue_ref[6:6 + tm, :] * cw_ref[0:1, :] + ue_ref[7:7 + tm, :] * cw_ref[1:2, :]
            + u * cw_ref[2:3, :])
    y_a = group(r_ref, 0) * conv * _silu(group(r_ref, 3))
    y_b = yb_ref[...].astype(F32) * _silu(group(r_ref, 4))
    tail_ref[...] = u[tm - 8:tm, :]
    h = _finish(y_a, y_b, w_ref, g_ref, x_ref, o_ref)
    if next_proj is not None:
        qkv_first, blk, keep = next_proj
        xn_ref[...] = _rms(h, g2_ref[...]).astype(BF16)
        _proj_tile(xn_ref, w2_ref, *proj_out, qkv_first, blk, keep)


def _even_out_prompt(rest, y_b, x, w_out_bf, conv_w, g_post, bsz, seq, tm, nxt=None):
    m, d = x.shape
    tps = seq // tm
    w = BRANCH_W
    row = lambda b, t: b * tps + t
    hrows = 16
    const = lambda shape: pl.BlockSpec(shape, lambda b, t: (0,) * len(shape),
                                       pipeline_mode=pl.Buffered(1))
    in_specs = [pl.BlockSpec((tm, rest.shape[1]), lambda b, t: (row(b, t), 0)),
                pl.BlockSpec((hrows, rest.shape[1]),
                             lambda b, t: (jnp.maximum(row(b, t) * (tm // hrows) - 1, 0), 0)),
                pl.BlockSpec((tm, w), lambda b, t: (row(b, t), 0)),
                pl.BlockSpec((tm, d), lambda b, t: (row(b, t), 0)),
                const(w_out_bf.shape), const(conv_w.shape), const((1, d))]
    args = [rest, rest, y_b, x, w_out_bf, conv_w, g_post.reshape(1, d)]
    out_specs = [pl.BlockSpec((tm, d), lambda b, t: (row(b, t), 0)),
                 pl.BlockSpec((8, w), lambda b, t: (row(b, t), 0))]
    out_shape = [jax.ShapeDtypeStruct((m, d), F32),
                 jax.ShapeDtypeStruct((bsz * tps * 8, w), F32)]
    scratch = [pltpu.VMEM((tm + 8, w), F32)]
    next_proj = None
    if nxt is not None:
        g2, w2_bf, qkv_first, keep_rows = nxt
        specs2, shapes2, blk, keep = _proj_outputs(w2_bf.shape[1], bsz, seq, tm, keep_rows)
        in_specs += [const((1, d)), const(w2_bf.shape)]
        args += [g2.reshape(1, d), w2_bf]
        out_specs += specs2
        out_shape += shapes2
        scratch.append(pltpu.VMEM((tm, d), BF16))
        next_proj = (qkv_first, blk, keep)
    outs = pl.pallas_call(
        functools.partial(_even_out_prompt_kernel, next_proj=next_proj),
        grid=(bsz, tps),
        in_specs=in_specs,
        out_specs=out_specs,
        out_shape=out_shape,
        scratch_shapes=scratch,
        compiler_params=_cparams(("parallel", "arbitrary")),
    )(*args)
    return outs[0], outs[1], (tuple(outs[2:]) if nxt is not None else None)


def _even_out_sample_kernel(rest_ref, st_ref, yb_ref, x_ref, w_ref, cw_ref, g_ref,
                            o_ref, tail_ref, *, nb):
    w = BRANCH_W
    rows = rest_ref.shape[0]
    rest = rest_ref[...].astype(F32)
    u = rest[:, w:2 * w] * rest[:, 2 * w:3 * w]
    ue = jnp.concatenate([st_ref[...], u], axis=0)
    conv = (ue[0:rows] * cw_ref[0:1, :] + ue[nb:nb + rows] * cw_ref[1:2, :]
            + ue[2 * nb:2 * nb + rows] * cw_ref[2:3, :])
    y_a = rest[:, 0:w] * conv * _silu(rest[:, 3 * w:4 * w])
    y_b = yb_ref[...] * _silu(rest[:, 4 * w:5 * w])
    tail_ref[...] = ue[rows:rows + 2 * nb]
    _finish(y_a, y_b, w_ref, g_ref, x_ref, o_ref)


def _even_out_sample(rest, state_tm, y_b, x, w_out_bf, conv_w, g_post, nb):
    m, d = x.shape
    return pl.pallas_call(
        functools.partial(_even_out_sample_kernel, nb=nb),
        out_shape=[jax.ShapeDtypeStruct((m, d), F32),
                   jax.ShapeDtypeStruct(((CONV_W - 1) * nb, BRANCH_W), F32)],
        compiler_params=pltpu.CompilerParams(vmem_limit_bytes=VMEM_LIMIT),
    )(rest, state_tm, y_b, x, w_out_bf, conv_w, g_post.reshape(1, d))


def _gmlp_mix(v, gw_ref, low):
    r_i = lax.broadcasted_iota(jnp.int32, (GMLP_CHUNK, GMLP_CHUNK), 0)
    c_i = lax.broadcasted_iota(jnp.int32, (GMLP_CHUNK, GMLP_CHUNK), 1)
    tril = c_i <= r_i
    parts = []
    for p in range(v.shape[1] // LANES):
        vp = v[:, p * LANES:(p + 1) * LANES]
        v_lo = jnp.where(low, vp, jnp.zeros_like(vp)).astype(BF16)
        v_hi = jnp.where(low, jnp.zeros_like(vp), vp).astype(BF16)
        w_lo = jnp.where(tril, gw_ref[2 * p], 0.0).astype(BF16)
        w_hi = jnp.where(tril, gw_ref[2 * p + 1], 0.0).astype(BF16)
        parts.append(jnp.dot(w_lo, v_lo, preferred_element_type=F32)
                     + jnp.dot(w_hi, v_hi, preferred_element_type=F32))
    return jnp.concatenate(parts, axis=1)


def _odd_out_prompt_kernel(r_ref, yd_ref, x_ref, w_ref, gw_ref, gb_ref, g_ref, o_ref, yc_ref):
    tm = r_ref.shape[0]
    w = BRANCH_W
    low = lax.broadcasted_iota(jnp.int32, (1, LANES), 1) < HEAD_DIM
    for c in range(tm // GMLP_CHUNK):
        rows = slice(c * GMLP_CHUNK, (c + 1) * GMLP_CHUNK)
        mixed = _gmlp_mix(r_ref[rows, w:2 * w], gw_ref, low) + gb_ref[...]
        yc_ref[rows, :] = (r_ref[rows, 0:w].astype(F32) * mixed
                           * _silu(r_ref[rows, 2 * w:3 * w].astype(F32)))
    y_d = yd_ref[...].astype(F32) * _silu(r_ref[:, 3 * w:4 * w].astype(F32))
    _finish(yc_ref[...], y_d, w_ref, g_ref, x_ref, o_ref)


def _odd_out_prompt(rest, y_d, x, w_out_bf, gmlp_w, gb_full, g_post, tm):
    m, d = x.shape
    return pl.pallas_call(
        _odd_out_prompt_kernel,
        grid=(m // tm,),
        in_specs=[pl.BlockSpec((tm, rest.shape[1]), lambda i: (i, 0)),
                  pl.BlockSpec((tm, BRANCH_W), lambda i: (i, 0)),
                  pl.BlockSpec((tm, d), lambda i: (i, 0)),
                  pl.BlockSpec(w_out_bf.shape, lambda i: (0, 0)),
                  pl.BlockSpec(gmlp_w.shape, lambda i: (0, 0, 0)),
                  pl.BlockSpec(gb_full.shape, lambda i: (0, 0)),
                  pl.BlockSpec((1, d), lambda i: (0, 0))],
        out_specs=pl.BlockSpec((tm, d), lambda i: (i, 0)),
        out_shape=jax.ShapeDtypeStruct((m, d), F32),
        scratch_shapes=[pltpu.VMEM((tm, BRANCH_W), F32)],
        compiler_params=_cparams(("parallel",)),
    )(rest, y_d, x, w_out_bf, gmlp_w, gb_full, g_post.reshape(1, d))


def _odd_out_sample_kernel(rest_ref, yd_ref, x_ref, w_ref, w4_ref, b4_ref, g_ref, o_ref, *, nb, nq):
    w = BRANCH_W
    rest = rest_ref[...].astype(F32)
    parts = []
    for t in range(nq):
        mixed = jnp.broadcast_to(b4_ref[t:t + 1, :], (nb, w))
        for s in range(t + 1):
            mixed = mixed + w4_ref[t * nq + s:t * nq + s + 1, :] * rest[s * nb:(s + 1) * nb, w:2 * w]
        parts.append(mixed)
    mixed = jnp.concatenate(parts, axis=0)
    y_c = rest[:, 0:w] * mixed * _silu(rest[:, 2 * w:3 * w])
    y_d = yd_ref[...] * _silu(rest[:, 3 * w:4 * w])
    _finish(y_c, y_d, w_ref, g_ref, x_ref, o_ref)


def _odd_out_sample(rest, y_d, x, w_out_bf, w4, b4, g_post, nb, nq):
    m, d = x.shape
    return pl.pallas_call(
        functools.partial(_odd_out_sample_kernel, nb=nb, nq=nq),
        out_shape=jax.ShapeDtypeStruct((m, d), F32),
        compiler_params=pltpu.CompilerParams(vmem_limit_bytes=VMEM_LIMIT),
    )(rest, y_d, x, w_out_bf, w4, b4, g_post.reshape(1, d))


def _to_token_major(a):
    nb, nq, w = a.shape
    return a.transpose(1, 0, 2).reshape(nq * nb, w)


def _to_batch_major(a, nb):
    w = a.shape[1]
    return a.reshape(-1, nb, w).transpose(1, 0, 2)


def _heads_prompt(t, heads):
    bsz, w, seq = t.shape
    return t.reshape(bsz, heads, w // heads, seq).transpose(0, 3, 1, 2)


def _open_rows(seq):
    open_start = ((seq - 1) // GMLP_CHUNK) * GMLP_CHUNK
    return open_start, seq - open_start


def _project_prompt(hp, dims, g_pre, w_in_bf, qkv_first, tm):
    bsz, seq, _, _ = dims
    return _proj(hp, g_pre, w_in_bf, qkv_first, bsz, seq, tm, _open_rows(seq))


def _project_sample(hs, dims, g_pre, w_in_bf, qkv_first, tm):
    _, _, nb, nq = dims
    ns = nq * nb
    rows = max(ns, tm)
    hs_pad = jnp.pad(hs, ((0, rows - ns), (0, 0)))
    rest_s, qk_s, kt_s, vt_s, _, keep_s = _proj(hs_pad, g_pre, w_in_bf, qkv_first, 1, rows, rows,
                                                (0, ns))
    k_s = _to_batch_major(kt_s[0, :, :ns].T, nb)
    v_s = _to_batch_major(vt_s[0, :, :ns].T, nb)
    q_s = _to_batch_major(qk_s[:ns, :BRANCH_W], nb)
    return rest_s[:ns], q_s, k_s, v_s, keep_s[0]


def _even_layer(hp, hs, dims, state_conv, cache_k, cache_v, page_table,
                g_pre, g_post, w_in, conv_w, w_out, tm, proj_prompt, nxt):
    bsz, seq, nb, nq = dims
    w = BRANCH_W
    heads = w // HEAD_DIM
    w_in_bf = w_in.astype(BF16)
    w_out_bf = w_out.astype(BF16)
    if proj_prompt is None:
        proj_prompt = _project_prompt(hp, dims, g_pre, w_in_bf, 4, tm)
    rest_p, qk_p, kt_p, vt_p, vtb_p, _ = proj_prompt
    rest_s, q_s, k_s, v_s, _ = _project_sample(hs, dims, g_pre, w_in_bf, 4, tm)
    sample_qkv = (q_s, k_s.astype(BF16), v_s.astype(BF16))
    if bsz * (seq // MOBA_BLOCK) == 2 * nb:
        yb_p, yb_s = _moba_fused(qk_p, vtb_p, bsz, seq, *sample_qkv, cache_k, cache_v, page_table)
    else:
        yb_p = _moba_prompt(qk_p, vtb_p, bsz, seq)
        yb_s = _moba_paged(*sample_qkv, cache_k, cache_v, page_table)
    yb_s = _to_token_major(yb_s)
    if nxt is not None:
        nxt = (*nxt, _open_rows(seq))
    hp, tail_p, proj_next = _even_out_prompt(rest_p, yb_p, hp, w_out_bf, conv_w, g_post,
                                             bsz, seq, tm, nxt)
    state_tm = _to_token_major(state_conv)
    hs, tail_s = _even_out_sample(rest_s, state_tm, yb_s, hs, w_out_bf, conv_w, g_post, nb)
    conv_p = tail_p.reshape(bsz, seq // tm, 8, w)[:, -1, 8 - (CONV_W - 1):, :]
    conv_s = _to_batch_major(tail_s, nb)
    outs = (conv_p, conv_s, _heads_prompt(kt_p, heads), _heads_prompt(vt_p, heads),
            k_s.reshape(nb, nq, heads, HEAD_DIM), v_s.reshape(nb, nq, heads, HEAD_DIM))
    return hp, hs, proj_next, outs


def _odd_layer(hp, hs, dims, cache_k, cache_v, page_table,
               g_pre, g_post, w_in_bf, gmlp_w, gmlp_b, w_out, tm, group, proj_prompt):
    bsz, seq, nb, nq = dims
    w = BRANCH_W
    heads = w // HEAD_DIM
    w_out_bf = w_out.astype(BF16)
    ngroups = gmlp_w.shape[0]
    cpg = w // ngroups
    if proj_prompt is None:
        proj_prompt = _project_prompt(hp, dims, g_pre, w_in_bf, 3, tm)
    rest_p, qk_p, kt_p, vt_p, vtb_p, gv_p = proj_prompt
    rest_s, q_s, k_s, v_s, keep_s = _project_sample(hs, dims, g_pre, w_in_bf, 3, tm)
    yd_p = _sb_prompt(qk_p, vtb_p, bsz, seq)
    yd_s = _to_token_major(_sb_paged(q_s, k_s.astype(BF16), v_s.astype(BF16),
                                     cache_k, cache_v, page_table, group))
    gb_full = jnp.repeat(gmlp_b.T, cpg, axis=1)
    hp = _odd_out_prompt(rest_p, yd_p, hp, w_out_bf, gmlp_w, gb_full, g_post, tm)
    w4 = jnp.repeat(gmlp_w[:, :nq, :nq].transpose(1, 2, 0), cpg, axis=2).reshape(nq * nq, w)
    b4 = jnp.repeat(gmlp_b[:, :nq].T, cpg, axis=1)
    hs = _odd_out_sample(rest_s, yd_s, hs, w_out_bf, w4, b4, g_post, nb, nq)
    gv_s = _to_batch_major(keep_s, nb)
    outs = (_heads_prompt(kt_p, heads), _heads_prompt(vt_p, heads),
            k_s.reshape(nb, nq, heads, HEAD_DIM), v_s.reshape(nb, nq, heads, HEAD_DIM), gv_p, gv_s)
    return hp, hs, outs


def kernel(x_prompt, x_sample, state_conv, cache_k_moba, cache_v_moba, cache_k_sb, cache_v_sb,
           page_table, norm_pre_e, norm_post_e, w_in_e, conv_w, w_out_e,
           norm_pre_o, norm_post_o, w_in_o, gmlp_w, gmlp_b, w_out_o):
    bsz, seq, d = x_prompt.shape
    nb, nq, _ = x_sample.shape
    depth = norm_pre_e.shape[0] + norm_pre_o.shape[0]
    npages, page = page_table.shape[1], cache_k_moba.shape[2]
    assert seq % MOBA_BLOCK == 0 and (npages * page) % MOBA_BLOCK == 0 and nq <= GMLP_CHUNK
    assert (nq - 1) // MOBA_BLOCK == 0 and w_in_e.shape[2] == 8 * BRANCH_W and w_in_o.shape[2] == 7 * BRANCH_W
    dims = (bsz, seq, nb, nq)
    tm = min(512, seq)
    group_sb = min(SB_TILE // page, npages)
    hp = x_prompt.reshape(bsz * seq, d)
    hs = _to_token_major(x_sample)
    ev, od = [], []
    pending = None
    for i in range(depth):
        j = i // 2
        if i % 2 == 0:
            nxt = (norm_pre_o[j], w_in_o[j].astype(BF16), 3) if i + 1 < depth else None
            hp, hs, pending, outs = _even_layer(
                hp, hs, dims, state_conv[j], cache_k_moba[j], cache_v_moba[j], page_table,
                norm_pre_e[j], norm_post_e[j], w_in_e[j], conv_w[j], w_out_e[j], tm, pending, nxt)
            ev.append(outs)
        else:
            hp, hs, outs = _odd_layer(
                hp, hs, dims, cache_k_sb[j], cache_v_sb[j], page_table,
                norm_pre_o[j], norm_post_o[j], w_in_o[j].astype(BF16), gmlp_w[j], gmlp_b[j],
                w_out_o[j], tm, group_sb, pending)
            od.append(outs)
            pending = None
    y_prompt = hp.reshape(bsz, seq, d)
    y_sample = _to_batch_major(hs, nb)
    ev_out = [jnp.stack([r[k] for r in ev]) for k in range(6)]
    od_out = [jnp.stack([r[k] for r in od]) for k in range(6)]
    conv_p, conv_s, kmp, vmp, kms, vms = ev_out
    ksp, vsp, kss, vss, gvp, gvs = od_out
    return (y_prompt, y_sample, conv_p, conv_s, kmp, vmp, kms, vms, ksp, vsp, kss, vss, gvp, gvs)
```
